```python
import math
import jax
import jax.numpy as jnp
from jax import lax
import numpy as np

D_MODEL = 1024
BATCH = 8
SEQ = 2048
DEPTH = 1

PLE_DIM = 256
POOL_WIDTH = D_MODEL // 2
POOL_GROUPS = 4
POOL_GROUP_DIM = POOL_WIDTH // POOL_GROUPS
POOL_WINDOWS = (2, 4, 8, 16)
SB_HEAD_DIM = 64
SB_HEADS = (D_MODEL // 2) // SB_HEAD_DIM
SB_WIDTH = SB_HEADS * SB_HEAD_DIM
Q_BLOCK = 128
N_BRANCHES = 2
OFF_Q = POOL_WIDTH
OFF_K = OFF_Q + SB_WIDTH
OFF_V = OFF_K + SB_WIDTH
OFF_GATE = OFF_V + SB_WIDTH
IN_WIDTH = OFF_GATE + N_BRANCHES * D_MODEL
N_GROUPS = 4
EXPERTS_PER_GROUP = 8
TOP_K_IN_GROUP = 2
D_EXPERT = D_MODEL // 2
LN_EPS = 1e-5
DEEPNORM_ALPHA = (2.0 * DEPTH) ** 0.25
DEEPNORM_BETA = (8.0 * DEPTH) ** -0.25

kernel_name = "hybrid_pool_stickbreak_hmoe_deepnorm"


def layer_norm(x, g, b):
    xf = x.astype(jnp.float32)
    mu = jnp.mean(xf, axis=-1, keepdims=True)
    var = jnp.mean(jnp.square(xf - mu), axis=-1, keepdims=True)
    y = (xf - mu) * lax.rsqrt(var + LN_EPS)
    return (y * g.astype(jnp.float32) + b.astype(jnp.float32)).astype(x.dtype)


def pool_mixer(u, w_pool, pool_scale):
    bsz, seq, _ = u.shape
    uf = u.astype(jnp.float32)
    csum = jnp.pad(jnp.cumsum(uf, axis=1), ((0, 0), (1, 0), (0, 0)))
    pos = jnp.arange(1, seq + 1, dtype=jnp.float32)
    outs = []
    for g, w in enumerate(POOL_WINDOWS):
        sl = slice(g * POOL_GROUP_DIM, (g + 1) * POOL_GROUP_DIM)
        cg = csum[..., sl]
        lo = jnp.pad(cg, ((0, 0), (w - 1, 0), (0, 0)))[:, :seq]
        cnt = jnp.minimum(pos, float(w))[None, :, None]
        outs.append((cg[:, 1:] - lo) / cnt - uf[..., sl])
    pooled = jnp.stack(outs, axis=2).astype(u.dtype)
    mixed = jnp.einsum('bsgc,gcd->bsgd', pooled, w_pool).reshape(bsz, seq, POOL_WIDTH)
    return mixed * pool_scale


def stick_breaking_attention(q, k, v):
    seq = q.shape[2]
    scale = 1.0 / math.sqrt(q.shape[-1])
    outs = []
    for qb in range(seq // Q_BLOCK):
        start = qb * Q_BLOCK
        end = start + Q_BLOCK
        qf = q[:, :, start:end].astype(jnp.float32)
        kf = k[:, :, :end].astype(jnp.float32)
        z = jnp.einsum('bhqd,bhkd->bhqk', qf, kf) * scale
        qpos = jnp.arange(start, end)
        kpos = jnp.arange(end)
        causal = kpos[None, :] < qpos[:, None]
        log_beta = jax.nn.log_sigmoid(z)
        log_rem = jnp.where(causal, jax.nn.log_sigmoid(-z), 0.0)
        later = lax.cumsum(log_rem, axis=3, reverse=True) - log_rem
        a = jnp.where(causal, jnp.exp(log_beta + later), 0.0)
        outs.append(jnp.einsum('bhqk,bhkd->bhqd', a.astype(v.dtype), v[:, :, :end]))
    return jnp.concatenate(outs, axis=2)


def hierarchical_moe(x, w_rg, b_rg, w_re, b_re, w_eg, w_eu, w_ed):
    bsz, seq, d = x.shape
    xt = x.reshape(-1, d)
    xf = xt.astype(jnp.float32)
    group_probs = jax.nn.softmax(xf @ w_rg.astype(jnp.float32) + b_rg.astype(jnp.float32), axis=-1)
    g_idx = jnp.argmax(group_probs, axis=-1)
    g_prob = jnp.take_along_axis(group_probs, g_idx[:, None], axis=1)[:, 0]
    expert_logits = jnp.einsum('td,gde->tge', xf, w_re.astype(jnp.float32)) + b_re.astype(jnp.float32)
    sel_logits = jnp.take_along_axis(expert_logits, g_idx[:, None, None], axis=1)[:, 0]
    top_vals, top_idx = lax.top_k(sel_logits, TOP_K_IN_GROUP)
    top_w = jax.nn.softmax(top_vals, axis=-1) * g_prob[:, None]
    within = jnp.sum(jax.nn.one_hot(top_idx, EXPERTS_PER_GROUP, dtype=jnp.float32) * top_w[..., None], axis=1)
    combine = (jax.nn.one_hot(g_idx, N_GROUPS, dtype=jnp.float32)[:, :, None] * within[:, None, :]).astype(x.dtype)
    y = jnp.zeros_like(xt)
    for g in range(N_GROUPS):
        for e in range(EXPERTS_PER_GROUP):
            h = jax.nn.silu(xt @ w_eg[g, e]) * (xt @ w_eu[g, e])
            y = y + (h @ w_ed[g, e]) * combine[:, g, e, None]
    return y.reshape(bsz, seq, d)


def setup_inputs(seed: int = 0) -> dict:
    key = jax.random.key(seed)
    ks = jax.random.split(key, 24)
    L, D = DEPTH, D_MODEL

    def nrm(k, shape, scale):
        return jax.random.normal(k, shape, jnp.float32) * scale

    col_scale = jnp.concatenate([
        jnp.ones((POOL_WIDTH + 2 * SB_WIDTH,), jnp.float32),
        jnp.full((SB_WIDTH,), DEEPNORM_BETA, jnp.float32),
        jnp.ones((N_BRANCHES * D,), jnp.float32)])
    return {
        "x": nrm(ks[0], (BATCH, SEQ, D), 1.0),
        "p": nrm(ks[1], (DEPTH, BATCH, SEQ, PLE_DIM), 1.0),
        "w_in": nrm(ks[2], (L, D, IN_WIDTH), D ** -0.5) * col_scale,
        "w_pool": nrm(ks[3], (L, POOL_GROUPS, POOL_GROUP_DIM, POOL_GROUP_DIM), POOL_GROUP_DIM ** -0.5),
        "pool_scale": 1.0 + nrm(ks[4], (L, POOL_WIDTH), 0.1),
        "w_pu": nrm(ks[5], (L, POOL_WIDTH, D), DEEPNORM_BETA * POOL_WIDTH ** -0.5),
        "w_au": nrm(ks[6], (L, SB_WIDTH, D), DEEPNORM_BETA * SB_WIDTH ** -0.5),
        "w_o": nrm(ks[7], (L, D, D), DEEPNORM_BETA * D ** -0.5),
        "ln1_g": 1.0 + nrm(ks[8], (L, D), 0.02),
        "ln1_b": nrm(ks[9], (L, D), 0.02),
        "w_rg": nrm(ks[10], (L, D, N_GROUPS), D ** -0.5),
        "b_rg": nrm(ks[11], (L, N_GROUPS), 0.01),
        "w_re": nrm(ks[12], (L, N_GROUPS, D, EXPERTS_PER_GROUP), D ** -0.5),
        "b_re": nrm(ks[13], (L, N_GROUPS, EXPERTS_PER_GROUP), 0.01),
        "w_eg": nrm(ks[14], (L, N_GROUPS, EXPERTS_PER_GROUP, D, D_EXPERT), DEEPNORM_BETA * D ** -0.5),
        "w_eu": nrm(ks[15], (L, N_GROUPS, EXPERTS_PER_GROUP, D, D_EXPERT), DEEPNORM_BETA * D ** -0.5),
        "w_ed": nrm(ks[16], (L, N_GROUPS, EXPERTS_PER_GROUP, D_EXPERT, D), DEEPNORM_BETA * D_EXPERT ** -0.5),
        "w_pg": nrm(ks[17], (L, D, D), D ** -0.5),
        "w_pp": nrm(ks[18], (L, PLE_DIM, D), DEEPNORM_BETA * PLE_DIM ** -0.5),
        "ln2_g": 1.0 + nrm(ks[19], (L, D), 0.02),
        "ln2_b": nrm(ks[20], (L, D), 0.02),
    }


def reference(x, p, w_in, w_pool, pool_scale, w_pu, w_au, w_o, ln1_g, ln1_b,
              w_rg, b_rg, w_re, b_re, w_eg, w_eu, w_ed, w_pg, w_pp, ln2_g, ln2_b):
    bsz, seq, d = x.shape
    for i in range(DEPTH):
        proj = x @ w_in[i]
        u_pool = proj[..., :OFF_Q]
        q = proj[..., OFF_Q:OFF_K].reshape(bsz, seq, SB_HEADS, SB_HEAD_DIM).transpose(0, 2, 1, 3)
        k = proj[..., OFF_K:OFF_V].reshape(bsz, seq, SB_HEADS, SB_HEAD_DIM).transpose(0, 2, 1, 3)
        v = proj[..., OFF_V:OFF_GATE].reshape(bsz, seq, SB_HEADS, SB_HEAD_DIM).transpose(0, 2, 1, 3)
        gates = jax.nn.sigmoid(proj[..., OFF_GATE:].astype(jnp.float32)).astype(x.dtype)
        gates = gates.reshape(bsz, seq, N_BRANCHES, d)

        pool_out = pool_mixer(u_pool, w_pool[i], pool_scale[i])
        attn_out = stick_breaking_attention(q, k, v).transpose(0, 2, 1, 3).reshape(bsz, seq, SB_WIDTH)

        merged = gates[:, :, 0] * (pool_out @ w_pu[i]) + gates[:, :, 1] * (attn_out @ w_au[i])
        x = layer_norm(DEEPNORM_ALPHA * x + merged @ w_o[i], ln1_g[i], ln1_b[i])

        moe_out = hierarchical_moe(x, w_rg[i], b_rg[i], w_re[i], b_re[i], w_eg[i], w_eu[i], w_ed[i])
        ple = jax.nn.sigmoid(x @ w_pg[i]) * (p[i] @ w_pp[i])
        x = layer_norm(DEEPNORM_ALPHA * x + moe_out + ple, ln2_g[i], ln2_b[i])
    return x
```

```python
import functools
import math

import jax
import jax.numpy as jnp
from jax import lax
from jax.experimental import pallas as pl
from jax.experimental.pallas import tpu as pltpu

F32 = jnp.float32
BF16 = jnp.bfloat16

LANES = 128
POOL_WINDOWS = (2, 4, 8, 16)
HEAD_DIM = 64
N_GROUPS = 4
EXPERTS_PER_GROUP = 8
N_EXPERTS = N_GROUPS * EXPERTS_PER_GROUP
LN_EPS = 1e-5
ROUTE_LANE0 = N_GROUPS
NEG_BIG = -1e30
ATTN_DEAD_LOG = -110.0
VMEM_CAP_BYTES = 56 * 1024 * 1024


def _params(sem, vmem_bytes):
    return pltpu.CompilerParams(
        dimension_semantics=sem, vmem_limit_bytes=min(int(vmem_bytes), VMEM_CAP_BYTES))


def _layer_norm(h, g, b):
    mu = jnp.mean(h, axis=-1, keepdims=True)
    c = h - mu
    var = jnp.mean(c * c, axis=-1, keepdims=True)
    return c * lax.rsqrt(var + LN_EPS) * g + b


def _sigmoid(z):
    return 1.0 / (1.0 + jnp.exp(-z))


def _proj_kernel(x_ref, w_ref, u_ref, q_ref, k_ref, v_ref, g_ref, *, width):
    xb = x_ref[...].astype(BF16)

    def mm(lo):
        return jnp.dot(xb, w_ref[:, lo:lo + width], preferred_element_type=F32)

    u_ref[...] = mm(0)
    q_ref[...] = mm(width).astype(BF16)
    k_ref[...] = mm(2 * width).astype(BF16)
    v_ref[...] = mm(3 * width).astype(BF16)
    for c in range(g_ref.shape[1] // width):
        g_ref[:, c * width:(c + 1) * width] = _sigmoid(mm((4 + c) * width)).astype(BF16)


def _proj(x2, w_in_b, tm):
    t, d = x2.shape
    n = w_in_b.shape[1]
    width = d // 2
    gate_w = n - 4 * width
    row = lambda i: (i, 0)
    blocks = 2 * (tm * d * 4 + d * n * 2 + tm * width * (4 + 3 * 2) + tm * gate_w * 2)
    temps = tm * d * 2 + 2 * tm * width * 4
    return pl.pallas_call(
        functools.partial(_proj_kernel, width=width),
        grid=(t // tm,),
        in_specs=[pl.BlockSpec((tm, d), row), pl.BlockSpec((d, n), lambda i: (0, 0))],
        out_specs=[pl.BlockSpec((tm, width), row)] * 4 + [pl.BlockSpec((tm, gate_w), row)],
        out_shape=[jax.ShapeDtypeStruct((t, width), F32)]
        + [jax.ShapeDtypeStruct((t, width), BF16)] * 3
        + [jax.ShapeDtypeStruct((t, gate_w), BF16)],
        compiler_params=_params(("parallel",), blocks + temps),
        name="proj",
    )(x2, w_in_b)


def _pool_kernel(u_ref, wp_ref, sc_ref, o_ref):
    seq = u_ref.shape[0]
    gd = wp_ref.shape[1]
    row = lax.broadcasted_iota(jnp.int32, (seq, gd), 0)
    for g, w in enumerate(POOL_WINDOWS):
        cols = slice(g * gd, (g + 1) * gd)
        ug = u_ref[:, cols]
        s = ug
        sh = 1
        while sh < w:
            s = s + jnp.where(row >= sh, pltpu.roll(s, sh, axis=0), 0.0)
            sh *= 2
        cnt = jnp.minimum(row + 1, w).astype(F32)
        pooled = s / cnt - ug
        mixed = jnp.dot(pooled.astype(BF16), wp_ref[g], preferred_element_type=F32)
        o_ref[:, cols] = (mixed * sc_ref[:, cols]).astype(BF16)


def _pool(u, w_pool_b, pool_scale, seq):
    t, width = u.shape
    g, gd, _ = w_pool_b.shape
    blocks = 2 * (seq * width * (4 + 2) + g * gd * gd * 2 + width * 4)
    temps = 6 * seq * gd * 4
    return pl.pallas_call(
        _pool_kernel,
        grid=(t // seq,),
        in_specs=[pl.BlockSpec((seq, width), lambda b: (b, 0)),
                  pl.BlockSpec((g, gd, gd), lambda b: (0, 0, 0)),
                  pl.BlockSpec((1, width), lambda b: (0, 0))],
        out_specs=pl.BlockSpec((seq, width), lambda b: (b, 0)),
        out_shape=jax.ShapeDtypeStruct((t, width), BF16),
        compiler_params=_params(("parallel",), blocks + temps),
        name="pool",
    )(u, w_pool_b, pool_scale)


def _attn_kernel(q_ref, k_ref, v_ref, o_ref, *, tq, tk):
    qi = pl.program_id(2)
    q = q_ref[...]
    scale = 1.0 / math.sqrt(HEAD_DIM)
    lane = lax.broadcasted_iota(jnp.int32, (tq, LANES), 1)
    qpos = qi * tq + lax.broadcasted_iota(jnp.int32, (tq, tk), 0)
    kcol = lax.broadcasted_iota(jnp.int32, (tq, tk), 1)
    tri = (lax.broadcasted_iota(jnp.int32, (tk, tk), 0)
           > lax.broadcasted_iota(jnp.int32, (tk, tk), 1)).astype(BF16)
    j_first = (qi + 1) * (tq // tk) - 1

    def head_out(h):
        qh = jnp.where((lane >= h * HEAD_DIM) & (lane < (h + 1) * HEAD_DIM), q, jnp.zeros_like(q))

        def cond(carry):
            j, _, _, rem_max = carry
            return (j >= 0) & (rem_max > ATTN_DEAD_LOG)

        def body(carry):
            j, rem, acc, _ = carry
            ks = pl.multiple_of(j * tk, tk)
            kb = k_ref[pl.ds(ks, tk), :]
            vb = v_ref[pl.ds(ks, tk), :]
            z = lax.dot_general(qh, kb, (((1,), (1,)), ((), ())),
                                preferred_element_type=F32) * scale
            causal = (j * tk + kcol) < qpos
            softplus = jnp.maximum(z, 0.0) + jnp.log(1.0 + jnp.exp(-jnp.abs(z)))
            log_beta = z - softplus
            log_rem = jnp.where(causal, -softplus, 0.0)
            later = jnp.dot(log_rem.astype(BF16), tri, preferred_element_type=F32) + rem
            a = jnp.where(causal, jnp.exp(log_beta + later), 0.0)
            acc = acc + jnp.dot(a.astype(BF16), vb, preferred_element_type=F32)
            rem = rem + jnp.sum(log_rem, axis=1, keepdims=True)
            return j - 1, rem, acc, jnp.max(rem)

        init = (j_first, jnp.zeros((tq, 1), F32), jnp.zeros((tq, LANES), F32), jnp.float32(0.0))
        return lax.while_loop(cond, body, init)[2]

    o_ref[...] = jnp.where(lane < HEAD_DIM, head_out(0), head_out(1)).astype(BF16)


def _attention(q, k, v, seq, tq, tk):
    t, width = q.shape
    nb = t // seq
    pairs = width // LANES
    blocks = 2 * (2 * tq * LANES * 2 + 2 * seq * LANES * 2)
    temps = 12 * tq * tk * 4 + tk * tk * 2
    qspec = pl.BlockSpec((tq, LANES), lambda b, p, i: (b * (seq // tq) + i, p))
    kvspec = pl.BlockSpec((seq, LANES), lambda b, p, i: (b, p))
    return pl.pallas_call(
        functools.partial(_attn_kernel, tq=tq, tk=tk),
        grid=(nb, pairs, seq // tq),
        in_specs=[qspec, kvspec, kvspec],
        out_specs=qspec,
        out_shape=jax.ShapeDtypeStruct((t, width), BF16),
        compiler_params=_params(("parallel", "parallel", "parallel"), blocks + temps),
        name="attn",
    )(q, k, v)


def _merge_kernel(x_ref, po_ref, at_ref, g_ref, wpu_ref, wau_ref, wo_ref, lg_ref, lb_ref,
                  wrh_ref, wrl_ref, br_ref, x1_ref, x1b_ref, comb_ref, *, alpha):
    d = x_ref.shape[1]
    tm = x_ref.shape[0]
    a = jnp.dot(po_ref[...], wpu_ref[...], preferred_element_type=F32)
    b = jnp.dot(at_ref[...], wau_ref[...], preferred_element_type=F32)
    merged = g_ref[:, :d].astype(F32) * a + g_ref[:, d:].astype(F32) * b
    h = alpha * x_ref[...] + jnp.dot(merged.astype(BF16), wo_ref[...], preferred_element_type=F32)
    x1 = _layer_norm(h, lg_ref[...], lb_ref[...])
    x1_ref[...] = x1
    xh = x1.astype(BF16)
    x1b_ref[...] = xh

    xl = (x1 - xh.astype(F32)).astype(BF16)
    wrh = wrh_ref[...]
    logits = (jnp.dot(xh, wrh, preferred_element_type=F32)
              + jnp.dot(xh, wrl_ref[...], preferred_element_type=F32)
              + jnp.dot(xl, wrh, preferred_element_type=F32)) + br_ref[...]

    lane = lax.broadcasted_iota(jnp.int32, (tm, LANES), 1)

    def first_max(vals):
        m = jnp.max(vals, axis=1, keepdims=True)
        idx = jnp.min(jnp.where(vals == m, lane, LANES), axis=1, keepdims=True)
        return m, idx

    is_group = lane < N_GROUPS
    gm, g_idx = first_max(jnp.where(is_group, logits, NEG_BIG))
    g_prob = 1.0 / jnp.sum(jnp.where(is_group, jnp.exp(logits - gm), 0.0), axis=1, keepdims=True)
    lo = ROUTE_LANE0 + EXPERTS_PER_GROUP * g_idx
    sel = jnp.where((lane >= lo) & (lane < lo + EXPERTS_PER_GROUP), logits, NEG_BIG)
    m1, i1 = first_max(sel)
    m2, i2 = first_max(jnp.where(lane == i1, NEG_BIG, sel))
    e21 = jnp.exp(m2 - m1)
    w1 = g_prob / (1.0 + e21)
    w2 = w1 * e21
    comb_ref[...] = jnp.where(lane == i1, w1, 0.0) + jnp.where(lane == i2, w2, 0.0)


def _merge(x2, pool_out, attn_out, gates, w_pu_b, w_au_b, w_o_b, ln_g, ln_b, wr_hi, wr_lo, b_r,
           alpha, tm):
    t, d = x2.shape
    width = pool_out.shape[1]
    row = lambda i: (i, 0)
    fixed = lambda i: (0, 0)
    blocks = 2 * (tm * d * 4 + 2 * tm * width * 2 + tm * 2 * d * 2 + 2 * width * d * 2 + d * d * 2
                  + 2 * d * 4 + 2 * d * LANES * 2 + LANES * 4
                  + tm * d * 4 + tm * d * 2 + tm * LANES * 4)
    temps = 5 * tm * d * 4
    return pl.pallas_call(
        functools.partial(_merge_kernel, alpha=alpha),
        grid=(t // tm,),
        in_specs=[pl.BlockSpec((tm, d), row), pl.BlockSpec((tm, width), row),
                  pl.BlockSpec((tm, width), row), pl.BlockSpec((tm, 2 * d), row),
                  pl.BlockSpec((width, d), fixed), pl.BlockSpec((width, d), fixed),
                  pl.BlockSpec((d, d), fixed), pl.BlockSpec((1, d), fixed), pl.BlockSpec((1, d), fixed),
                  pl.BlockSpec((d, LANES), fixed), pl.BlockSpec((d, LANES), fixed),
                  pl.BlockSpec((1, LANES), fixed)],
        out_specs=[pl.BlockSpec((tm, d), row), pl.BlockSpec((tm, d), row),
                   pl.BlockSpec((tm, LANES), row)],
        out_shape=[jax.ShapeDtypeStruct((t, d), F32), jax.ShapeDtypeStruct((t, d), BF16),
                   jax.ShapeDtypeStruct((t, LANES), F32)],
        compiler_params=_params(("parallel",), blocks + temps),
        name="merge",
    )(x2, pool_out, attn_out, gates, w_pu_b, w_au_b, w_o_b, ln_g, ln_b, wr_hi, wr_lo, b_r)


def _moe_kernel(xb_ref, comb_ref, wg_ref, wu_ref, wd_ref, o_ref, acc_ref):
    e = pl.program_id(1)

    @pl.when(e == 0)
    def _():
        acc_ref[...] = jnp.zeros_like(acc_ref)

    x = xb_ref[...]
    hg = jnp.dot(x, wg_ref[0], preferred_element_type=F32)
    hu = jnp.dot(x, wu_ref[0], preferred_element_type=F32)
    h = hg * _sigmoid(hg) * hu
    y = jnp.dot(h.astype(BF16), wd_ref[0], preferred_element_type=F32)
    lane = lax.broadcasted_iota(jnp.int32, comb_ref.shape, 1)
    c = jnp.sum(jnp.where(lane == e + ROUTE_LANE0, comb_ref[...], 0.0), axis=1, keepdims=True)
    acc_ref[...] += y * c

    @pl.when(e == pl.num_programs(1) - 1)
    def _():
        o_ref[...] = acc_ref[...]


def _moe(x1b, comb, w_eg_b, w_eu_b, w_ed_b, tm):
    t, d = x1b.shape
    ne, _, de = w_eg_b.shape
    row = lambda i, e: (i, 0)
    wsel = lambda i, e: (e, 0, 0)
    blocks = 2 * (tm * d * 2 + tm * LANES * 4 + 3 * d * de * 2 + tm * d * 4) + tm * d * 4
    temps = 3 * tm * de * 4 + tm * d * 4
    return pl.pallas_call(
        _moe_kernel,
        grid=(t // tm, ne),
        in_specs=[pl.BlockSpec((tm, d), row), pl.BlockSpec((tm, LANES), row),
                  pl.BlockSpec((1, d, de), wsel), pl.BlockSpec((1, d, de), wsel),
                  pl.BlockSpec((1, de, d), wsel)],
        out_specs=pl.BlockSpec((tm, d), row),
        out_shape=jax.ShapeDtypeStruct((t, d), F32),
        scratch_shapes=[pltpu.VMEM((tm, d), F32)],
        compiler_params=_params(("parallel", "arbitrary"), blocks + temps),
        name="moe",
    )(x1b, comb, w_eg_b, w_eu_b, w_ed_b)


def _final_kernel(x1_ref, moe_ref, p_ref, wpg_ref, wpp_ref, lg_ref, lb_ref, o_ref, *, alpha):
    x1 = x1_ref[...]
    gate = _sigmoid(jnp.dot(x1.astype(BF16), wpg_ref[...], preferred_element_type=F32))
    emb = jnp.dot(p_ref[...].astype(BF16), wpp_ref[...], preferred_element_type=F32)
    h = alpha * x1 + moe_ref[...] + gate * emb
    o_ref[...] = _layer_norm(h, lg_ref[...], lb_ref[...])


def _final(x1, moe, p2, w_pg_b, w_pp_b, ln_g, ln_b, alpha, tm):
    t, d = x1.shape
    pd = p2.shape[1]
    row = lambda i: (i, 0)
    fixed = lambda i: (0, 0)
    blocks = 2 * (3 * tm * d * 4 + tm * pd * 4 + d * d * 2 + pd * d * 2 + 2 * d * 4)
    temps = 4 * tm * d * 4
    return pl.pallas_call(
        functools.partial(_final_kernel, alpha=alpha),
        grid=(t // tm,),
        in_specs=[pl.BlockSpec((tm, d), row), pl.BlockSpec((tm, d), row), pl.BlockSpec((tm, pd), row),
                  pl.BlockSpec((d, d), fixed), pl.BlockSpec((pd, d), fixed),
                  pl.BlockSpec((1, d), fixed), pl.BlockSpec((1, d), fixed)],
        out_specs=pl.BlockSpec((tm, d), row),
        out_shape=jax.ShapeDtypeStruct((t, d), F32),
        compiler_params=_params(("parallel",), blocks + temps),
        name="final",
    )(x1, moe, p2, w_pg_b, w_pp_b, ln_g, ln_b)


def kernel(x, p, w_in, w_pool, pool_scale, w_pu, w_au, w_o, ln1_g, ln1_b, w_rg, b_rg, w_re, b_re,
           w_eg, w_eu, w_ed, w_pg, w_pp, ln2_g, ln2_b):
    bsz, seq, d = x.shape
    depth = w_in.shape[0]
    t = bsz * seq
    alpha = (2.0 * depth) ** 0.25
    assert w_rg.shape[2] == N_GROUPS and w_re.shape[1:] == (N_GROUPS, d, EXPERTS_PER_GROUP)
    assert w_in.shape[2] == 4 * d and w_pool.shape[1] == len(POOL_WINDOWS)

    x2 = x.reshape(t, d)
    for i in range(depth):
        u, q, k, v, gates = _proj(x2, w_in[i].astype(BF16), tm=512)
        pool_out = _pool(u, w_pool[i].astype(BF16), pool_scale[i][None, :], seq)
        attn_out = _attention(q, k, v, seq, tq=128, tk=128)

        w_r = jnp.concatenate(
            [w_rg[i], w_re[i].transpose(1, 0, 2).reshape(d, N_EXPERTS)], axis=1)
        w_r = jnp.pad(w_r, ((0, 0), (0, LANES - w_r.shape[1])))
        b_r = jnp.pad(jnp.concatenate([b_rg[i], b_re[i].reshape(-1)]),
                      (0, LANES - N_GROUPS - N_EXPERTS))[None, :]
        wr_hi = w_r.astype(BF16)
        wr_lo = (w_r - wr_hi.astype(F32)).astype(BF16)

        x1, x1b, comb = _merge(
            x2, pool_out, attn_out, gates, w_pu[i].astype(BF16), w_au[i].astype(BF16),
            w_o[i].astype(BF16), ln1_g[i][None, :], ln1_b[i][None, :], wr_hi, wr_lo, b_r,
            alpha, tm=512)

        de = w_eg.shape[-1]
        moe = _moe(x1b, comb, w_eg[i].reshape(N_EXPERTS, d, de).astype(BF16),
                   w_eu[i].reshape(N_EXPERTS, d, de).astype(BF16),
                   w_ed[i].reshape(N_EXPERTS, de, d).astype(BF16), tm=1024)

        x2 = _final(x1, moe, p[i].reshape(t, -1), w_pg[i].astype(BF16), w_pp[i].astype(BF16),
                    ln2_g[i][None, :], ln2_b[i][None, :], alpha, tm=512)
    return x2.reshape(bsz, seq, d)
```

```python
import functools
import math

import jax
import jax.numpy as jnp
from jax import lax
from jax.experimental import pallas as pl
from jax.experimental.pallas import tpu as pltpu

F32 = jnp.float32
BF16 = jnp.bfloat16

LANES = 128
POOL_WINDOWS = (2, 4, 8, 16)
HEAD_DIM = 64
N_GROUPS = 4
EXPERTS_PER_GROUP = 8
N_EXPERTS = N_GROUPS * EXPERTS_PER_GROUP
LN_EPS = 1e-5
ROUTE_LANE0 = N_GROUPS
NEG_BIG = -1e30
ATTN_DEAD_LOG2 = -160.0
VMEM_CAP_BYTES = 56 * 1024 * 1024


def _params(sem, vmem_bytes):
    return pltpu.CompilerParams(
        dimension_semantics=sem, vmem_limit_bytes=min(int(vmem_bytes), VMEM_CAP_BYTES))


def _layer_norm(h, g, b):
    mu = jnp.mean(h, axis=-1, keepdims=True)
    c = h - mu
    var = jnp.mean(c * c, axis=-1, keepdims=True)
    return c * lax.rsqrt(var + LN_EPS) * g + b


def _sigmoid(z):
    return 1.0 / (1.0 + jnp.exp(-z))


def _proj_kernel(x_ref, w_ref, u_ref, q_ref, k_ref, v_ref, g_ref, *, width):
    q_scale = math.log2(math.e) / math.sqrt(HEAD_DIM)
    xb = x_ref[...].astype(BF16)

    def mm(lo):
        return jnp.dot(xb, w_ref[:, lo:lo + width], preferred_element_type=F32)

    u_ref[...] = mm(0)
    q_ref[...] = (mm(width) * q_scale).astype(BF16)
    k_ref[...] = mm(2 * width).astype(BF16)
    v_ref[...] = mm(3 * width).astype(BF16)
    for c in range(g_ref.shape[1] // width):
        g_ref[:, c * width:(c + 1) * width] = _sigmoid(mm((4 + c) * width)).astype(BF16)


def _proj(x2, w_in_b, tm):
    t, d = x2.shape
    n = w_in_b.shape[1]
    width = d // 2
    gate_w = n - 4 * width
    row = lambda i: (i, 0)
    blocks = 2 * (tm * d * 4 + d * n * 2 + tm * width * (4 + 3 * 2) + tm * gate_w * 2)
    temps = tm * d * 2 + 2 * tm * width * 4
    return pl.pallas_call(
        functools.partial(_proj_kernel, width=width),
        grid=(t // tm,),
        in_specs=[pl.BlockSpec((tm, d), row), pl.BlockSpec((d, n), lambda i: (0, 0))],
        out_specs=[pl.BlockSpec((tm, width), row)] * 4 + [pl.BlockSpec((tm, gate_w), row)],
        out_shape=[jax.ShapeDtypeStruct((t, width), F32)]
        + [jax.ShapeDtypeStruct((t, width), BF16)] * 3
        + [jax.ShapeDtypeStruct((t, gate_w), BF16)],
        compiler_params=_params(("parallel",), blocks + temps),
        name="proj",
    )(x2, w_in_b)


def _pool_kernel(u_ref, wp_ref, sc_ref, o_ref):
    seq = u_ref.shape[0]
    gd = wp_ref.shape[1]
    row = lax.broadcasted_iota(jnp.int32, (seq, gd), 0)
    for g, w in enumerate(POOL_WINDOWS):
        cols = slice(g * gd, (g + 1) * gd)
        ug = u_ref[:, cols]
        s = ug
        sh = 1
        while sh < w:
            s = s + jnp.where(row >= sh, pltpu.roll(s, sh, axis=0), 0.0)
            sh *= 2
        cnt = jnp.minimum(row + 1, w).astype(F32)
        pooled = s / cnt - ug
        mixed = jnp.dot(pooled.astype(BF16), wp_ref[g], preferred_element_type=F32)
        o_ref[:, cols] = (mixed * sc_ref[:, cols]).astype(BF16)


def _pool(u, w_pool_b, pool_scale, seq):
    t, width = u.shape
    g, gd, _ = w_pool_b.shape
    blocks = 2 * (seq * width * (4 + 2) + g * gd * gd * 2 + width * 4)
    temps = 6 * seq * gd * 4
    return pl.pallas_call(
        _pool_kernel,
        grid=(t // seq,),
        in_specs=[pl.BlockSpec((seq, width), lambda b: (b, 0)),
                  pl.BlockSpec((g, gd, gd), lambda b: (0, 0, 0)),
                  pl.BlockSpec((1, width), lambda b: (0, 0))],
        out_specs=pl.BlockSpec((seq, width), lambda b: (b, 0)),
        out_shape=jax.ShapeDtypeStruct((t, width), BF16),
        compiler_params=_params(("parallel",), blocks + temps),
        name="pool",
    )(u, w_pool_b, pool_scale)


def _attn_kernel(q_ref, k_ref, v_ref, o_ref, *, tb):
    qi = pl.program_id(1)
    pairs = q_ref.shape[1] // LANES
    first_head = lax.broadcasted_iota(jnp.int32, (tb, LANES), 1) < HEAD_DIM
    r = lax.broadcasted_iota(jnp.int32, (2 * tb, 2 * tb), 0)
    c = lax.broadcasted_iota(jnp.int32, (2 * tb, 2 * tb), 1)
    same_head = (r >= tb) == (c >= tb)
    cum = -jnp.concatenate([(same_head & (r > c)).astype(BF16), same_head.astype(BF16)], axis=1)
    kcol = lax.broadcasted_iota(jnp.int32, (tb, 2 * tb), 1)
    kcol = jnp.where(kcol >= tb, kcol - tb, kcol)
    causal = kcol < lax.broadcasted_iota(jnp.int32, (tb, 2 * tb), 0)

    def stack_heads(blk):
        zero = jnp.zeros_like(blk)
        return jnp.concatenate(
            [jnp.where(first_head, blk, zero), jnp.where(first_head, zero, blk)], axis=0)

    def step(j, rems, accs, diagonal):
        ks = pl.multiple_of(j * tb, tb)
        col_blocks = [slice(p * LANES, (p + 1) * LANES) for p in range(pairs)]
        zs = [lax.dot_general(q_ref[:, cols], stack_heads(k_ref[pl.ds(ks, tb), cols]),
                              (((1,), (1,)), ((), ())), preferred_element_type=F32)
              for cols in col_blocks]
        log_betas, sums = [], []
        for z in zs:
            softplus = jnp.maximum(z, 0.0) + jnp.log2(1.0 + jnp.exp2(-jnp.abs(z)))
            log_betas.append(z - softplus)
            if diagonal:
                softplus = jnp.where(causal, softplus, 0.0)
            sums.append(jnp.dot(softplus.astype(BF16), cum, preferred_element_type=F32))
        new_rems, new_accs = [], []
        for p, cols in enumerate(col_blocks):
            later = sums[p][:, :2 * tb]
            if rems is not None:
                later = later + rems[p]
            a = jnp.exp2(log_betas[p] + later)
            if diagonal:
                a = jnp.where(causal, a, 0.0)
            out = jnp.dot(a.astype(BF16), stack_heads(v_ref[pl.ds(ks, tb), cols]),
                          preferred_element_type=F32)
            total = sums[p][:, 2 * tb:]
            new_accs.append(out if accs is None else accs[p] + out)
            new_rems.append(total if rems is None else rems[p] + total)
        return tuple(new_rems), tuple(new_accs)

    def rem_max(rems):
        return jnp.max(functools.reduce(jnp.maximum, rems))

    rems, accs = step(qi, None, None, diagonal=True)

    def cond(carry):
        j, _, _, worst = carry
        return (j >= 0) & (worst > ATTN_DEAD_LOG2)

    def body(carry):
        j, rems, accs, _ = carry
        rems, accs = step(j, rems, accs, diagonal=False)
        return j - 1, rems, accs, rem_max(rems)

    accs = lax.while_loop(cond, body, (qi - 1, rems, accs, rem_max(rems)))[2]
    for p in range(pairs):
        o_ref[:, p * LANES:(p + 1) * LANES] = accs[p].astype(BF16)


def _attention(q, k, v, seq, tb):
    t, width = q.shape
    blocks = 2 * (2 * tb * width * 2 + 2 * seq * width * 2)
    temps = (width // LANES) * 10 * tb * 2 * tb * 4 + 2 * tb * 4 * tb * 2
    qspec = pl.BlockSpec((tb, width), lambda b, i: (b * (seq // tb) + i, 0))
    kvspec = pl.BlockSpec((seq, width), lambda b, i: (b, 0))
    return pl.pallas_call(
        functools.partial(_attn_kernel, tb=tb),
        grid=(t // seq, seq // tb),
        in_specs=[qspec, kvspec, kvspec],
        out_specs=qspec,
        out_shape=jax.ShapeDtypeStruct((t, width), BF16),
        compiler_params=_params(("parallel", "parallel"), blocks + temps),
        name="attn",
    )(q, k, v)


def _merge_kernel(x_ref, po_ref, at_ref, g_ref, wpu_ref, wau_ref, wo_ref, lg_ref, lb_ref,
                  wrh_ref, wrl_ref, br_ref, x1_ref, x1b_ref, comb_ref, *, alpha):
    d = x_ref.shape[1]
    tm = x_ref.shape[0]
    a = jnp.dot(po_ref[...], wpu_ref[...], preferred_element_type=F32)
    b = jnp.dot(at_ref[...], wau_ref[...], preferred_element_type=F32)
    merged = g_ref[:, :d].astype(F32) * a + g_ref[:, d:].astype(F32) * b
    h = alpha * x_ref[...] + jnp.dot(merged.astype(BF16), wo_ref[...], preferred_element_type=F32)
    x1 = _layer_norm(h, lg_ref[...], lb_ref[...])
    x1_ref[...] = x1
    xh = x1.astype(BF16)
    x1b_ref[...] = xh

    xl = (x1 - xh.astype(F32)).astype(BF16)
    wrh = wrh_ref[...]
    logits = (jnp.dot(xh, wrh, preferred_element_type=F32)
              + jnp.dot(xh, wrl_ref[...], preferred_element_type=F32)
              + jnp.dot(xl, wrh, preferred_element_type=F32)) + br_ref[...]

    lane = lax.broadcasted_iota(jnp.int32, (tm, LANES), 1)

    def first_max(vals):
        m = jnp.max(vals, axis=1, keepdims=True)
        idx = jnp.min(jnp.where(vals == m, lane, LANES), axis=1, keepdims=True)
        return m, idx

    is_group = lane < N_GROUPS
    gm, g_idx = first_max(jnp.where(is_group, logits, NEG_BIG))
    g_prob = 1.0 / jnp.sum(jnp.where(is_group, jnp.exp(logits - gm), 0.0), axis=1, keepdims=True)
    lo = ROUTE_LANE0 + EXPERTS_PER_GROUP * g_idx
    sel = jnp.where((lane >= lo) & (lane < lo + EXPERTS_PER_GROUP), logits, NEG_BIG)
    m1, i1 = first_max(sel)
    m2, i2 = first_max(jnp.where(lane == i1, NEG_BIG, sel))
    e21 = jnp.exp(m2 - m1)
    w1 = g_prob / (1.0 + e21)
    w2 = w1 * e21
    comb_ref[...] = jnp.where(lane == i1, w1, 0.0) + jnp.where(lane == i2, w2, 0.0)


def _merge(x2, pool_out, attn_out, gates, w_pu_b, w_au_b, w_o_b, ln_g, ln_b, wr_hi, wr_lo, b_r,
           alpha, tm):
    t, d = x2.shape
    width = pool_out.shape[1]
    row = lambda i: (i, 0)
    fixed = lambda i: (0, 0)
    blocks = 2 * (tm * d * 4 + 2 * tm * width * 2 + tm * 2 * d * 2 + 2 * width * d * 2 + d * d * 2
                  + 2 * d * 4 + 2 * d * LANES * 2 + LANES * 4
                  + tm * d * 4 + tm * d * 2 + tm * LANES * 4)
    temps = 5 * tm * d * 4
    return pl.pallas_call(
        functools.partial(_merge_kernel, alpha=alpha),
        grid=(t // tm,),
        in_specs=[pl.BlockSpec((tm, d), row), pl.BlockSpec((tm, width), row),
                  pl.BlockSpec((tm, width), row), pl.BlockSpec((tm, 2 * d), row),
                  pl.BlockSpec((width, d), fixed), pl.BlockSpec((width, d), fixed),
                  pl.BlockSpec((d, d), fixed), pl.BlockSpec((1, d), fixed), pl.BlockSpec((1, d), fixed),
                  pl.BlockSpec((d, LANES), fixed), pl.BlockSpec((d, LANES), fixed),
                  pl.BlockSpec((1, LANES), fixed)],
        out_specs=[pl.BlockSpec((tm, d), row), pl.BlockSpec((tm, d), row),
                   pl.BlockSpec((tm, LANES), row)],
        out_shape=[jax.ShapeDtypeStruct((t, d), F32), jax.ShapeDtypeStruct((t, d), BF16),
                   jax.ShapeDtypeStruct((t, LANES), F32)],
        compiler_params=_params(("parallel",), blocks + temps),
        name="merge",
    )(x2, pool_out, attn_out, gates, w_pu_b, w_au_b, w_o_b, ln_g, ln_b, wr_hi, wr_lo, b_r)


def _moe_kernel(xb_ref, comb_ref, wg_ref, wu_ref, wd_ref, o_ref, acc_ref):
    e = pl.program_id(1)

    @pl.when(e == 0)
    def _():
        acc_ref[...] = jnp.zeros_like(acc_ref)

    x = xb_ref[...]
    hg = jnp.dot(x, wg_ref[0], preferred_element_type=F32)
    hu = jnp.dot(x, wu_ref[0], preferred_element_type=F32)
    h = hg * _sigmoid(hg) * hu
    y = jnp.dot(h.astype(BF16), wd_ref[0], preferred_element_type=F32)
    lane = lax.broadcasted_iota(jnp.int32, comb_ref.shape, 1)
    c = jnp.sum(jnp.where(lane == e + ROUTE_LANE0, comb_ref[...], 0.0), axis=1, keepdims=True)
    acc_ref[...] += y * c

    @pl.when(e == pl.num_programs(1) - 1)
    def _():
        o_ref[...] = acc_ref[...]


def _moe(x1b, comb, w_eg_b, w_eu_b, w_ed_b, tm):
    t, d = x1b.shape
    ne, _, de = w_eg_b.shape
    row = lambda i, e: (i, 0)
    wsel = lambda i, e: (e, 0, 0)
    blocks = 2 * (tm * d * 2 + tm * LANES * 4 + 3 * d * de * 2 + tm * d * 4) + tm * d * 4
    temps = 3 * tm * de * 4 + tm * d * 4
    return pl.pallas_call(
        _moe_kernel,
        grid=(t // tm, ne),
        in_specs=[pl.BlockSpec((tm, d), row), pl.BlockSpec((tm, LANES), row),
                  pl.BlockSpec((1, d, de), wsel), pl.BlockSpec((1, d, de), wsel),
                  pl.BlockSpec((1, de, d), wsel)],
        out_specs=pl.BlockSpec((tm, d), row),
        out_shape=jax.ShapeDtypeStruct((t, d), F32),
        scratch_shapes=[pltpu.VMEM((tm, d), F32)],
        compiler_params=_params(("parallel", "arbitrary"), blocks + temps),
        name="moe",
    )(x1b, comb, w_eg_b, w_eu_b, w_ed_b)


def _final_kernel(x1_ref, moe_ref, p_ref, wpg_ref, wpp_ref, lg_ref, lb_ref, o_ref, *, alpha):
    x1 = x1_ref[...]
    gate = _sigmoid(jnp.dot(x1.astype(BF16), wpg_ref[...], preferred_element_type=F32))
    emb = jnp.dot(p_ref[...].astype(BF16), wpp_ref[...], preferred_element_type=F32)
    h = alpha * x1 + moe_ref[...] + gate * emb
    o_ref[...] = _layer_norm(h, lg_ref[...], lb_ref[...])


def _final(x1, moe, p2, w_pg_b, w_pp_b, ln_g, ln_b, alpha, tm):
    t, d = x1.shape
    pd = p2.shape[1]
    row = lambda i: (i, 0)
    fixed = lambda i: (0, 0)
    blocks = 2 * (3 * tm * d * 4 + tm * pd * 4 + d * d * 2 + pd * d * 2 + 2 * d * 4)
    temps = 4 * tm * d * 4
    return pl.pallas_call(
        functools.partial(_final_kernel, alpha=alpha),
        grid=(t // tm,),
        in_specs=[pl.BlockSpec((tm, d), row), pl.BlockSpec((tm, d), row), pl.BlockSpec((tm, pd), row),
                  pl.BlockSpec((d, d), fixed), pl.BlockSpec((pd, d), fixed),
                  pl.BlockSpec((1, d), fixed), pl.BlockSpec((1, d), fixed)],
        out_specs=pl.BlockSpec((tm, d), row),
        out_shape=jax.ShapeDtypeStruct((t, d), F32),
        compiler_params=_params(("parallel",), blocks + temps),
        name="final",
    )(x1, moe, p2, w_pg_b, w_pp_b, ln_g, ln_b)


def kernel(x, p, w_in, w_pool, pool_scale, w_pu, w_au, w_o, ln1_g, ln1_b, w_rg, b_rg, w_re, b_re,
           w_eg, w_eu, w_ed, w_pg, w_pp, ln2_g, ln2_b):
    bsz, seq, d = x.shape
    depth = w_in.shape[0]
    t = bsz * seq
    alpha = (2.0 * depth) ** 0.25
    assert w_rg.shape[2] == N_GROUPS and w_re.shape[1:] == (N_GROUPS, d, EXPERTS_PER_GROUP)
    assert w_in.shape[2] == 4 * d and w_pool.shape[1] == len(POOL_WINDOWS)

    x2 = x.reshape(t, d)
    for i in range(depth):
        u, q, k, v, gates = _proj(x2, w_in[i].astype(BF16), tm=512)
        pool_out = _pool(u, w_pool[i].astype(BF16), pool_scale[i][None, :], seq)
        attn_out = _attention(q, k, v, seq, tb=128)

        w_r = jnp.concatenate(
            [w_rg[i], w_re[i].transpose(1, 0, 2).reshape(d, N_EXPERTS)], axis=1)
        w_r = jnp.pad(w_r, ((0, 0), (0, LANES - w_r.shape[1])))
        b_r = jnp.pad(jnp.concatenate([b_rg[i], b_re[i].reshape(-1)]),
                      (0, LANES - N_GROUPS - N_EXPERTS))[None, :]
        wr_hi = w_r.astype(BF16)
        wr_lo = (w_r - wr_hi.astype(F32)).astype(BF16)

        x1, x1b, comb = _merge(
            x2, pool_out, attn_out, gates, w_pu[i].astype(BF16), w_au[i].astype(BF16),
            w_o[i].astype(BF16), ln1_g[i][None, :], ln1_b[i][None, :], wr_hi, wr_lo, b_r,
            alpha, tm=512)

        de = w_eg.shape[-1]
        moe = _moe(x1b, comb, w_eg[i].reshape(N_EXPERTS, d, de).astype(BF16),
                   w_eu[i].reshape(N_EXPERTS, d, de).astype(BF16),
                   w_ed[i].reshape(N_EXPERTS, de, d).astype(BF16), tm=1024)

        x2 = _final(x1, moe, p[i].reshape(t, -1), w_pg[i].astype(BF16), w_pp[i].astype(BF16),
                    ln2_g[i][None, :], ln2_b[i][None, :], alpha, tm=512)
    return x2.reshape(bsz, seq, d)
```

```python
import functools
import math

import jax
import jax.numpy as jnp
from jax import lax
from jax.experimental import pallas as pl
from jax.experimental.pallas import tpu as pltpu

F32 = jnp.float32
BF16 = jnp.bfloat16

LANES = 128
POOL_WINDOWS = (2, 4, 8, 16)
HEAD_DIM = 64
N_GROUPS = 4
EXPERTS_PER_GROUP = 8
N_EXPERTS = N_GROUPS * EXPERTS_PER_GROUP
LN_EPS = 1e-5
GROUP_LANE0 = N_EXPERTS
NEG_BIG = -1e30
ROUTE_BLOCK = 512
CHUNK = 16
EXPERT_TILE = 256
SORT_ROWS = 256
LOCAL_ROWS = -(-(2 * ROUTE_BLOCK + N_EXPERTS * (CHUNK - 1)) // SORT_ROWS) * SORT_ROWS
LOCAL_CHUNKS = LOCAL_ROWS // CHUNK
ATTN_DEAD_LOG2 = -160.0
VMEM_CAP_BYTES = 56 * 1024 * 1024


def _params(sem, vmem_bytes):
    return pltpu.CompilerParams(
        dimension_semantics=sem, vmem_limit_bytes=min(int(vmem_bytes), VMEM_CAP_BYTES))


def _layer_norm(h, g, b):
    mu = jnp.mean(h, axis=-1, keepdims=True)
    c = h - mu
    var = jnp.mean(c * c, axis=-1, keepdims=True)
    return c * lax.rsqrt(var + LN_EPS) * g + b


def _sigmoid(z):
    return 1.0 / (1.0 + jnp.exp(-z))


def _proj_kernel(x_ref, w_ref, u_ref, q_ref, k_ref, v_ref, g_ref, *, width):
    q_scale = math.log2(math.e) / math.sqrt(HEAD_DIM)
    xb = x_ref[...].astype(BF16)

    def mm(lo):
        return jnp.dot(xb, w_ref[:, lo:lo + width], preferred_element_type=F32)

    u_ref[...] = mm(0)
    q_ref[...] = (mm(width) * q_scale).astype(BF16)
    k_ref[...] = mm(2 * width).astype(BF16)
    v_ref[...] = mm(3 * width).astype(BF16)
    for c in range(g_ref.shape[1] // width):
        g_ref[:, c * width:(c + 1) * width] = _sigmoid(mm((4 + c) * width)).astype(BF16)


def _proj(x2, w_in_b, tm):
    t, d = x2.shape
    n = w_in_b.shape[1]
    width = d // 2
    gate_w = n - 4 * width
    row = lambda i: (i, 0)
    blocks = 2 * (tm * d * 4 + d * n * 2 + tm * width * (4 + 3 * 2) + tm * gate_w * 2)
    temps = tm * d * 2 + 2 * tm * width * 4
    return pl.pallas_call(
        functools.partial(_proj_kernel, width=width),
        grid=(t // tm,),
        in_specs=[pl.BlockSpec((tm, d), row), pl.BlockSpec((d, n), lambda i: (0, 0))],
        out_specs=[pl.BlockSpec((tm, width), row)] * 4 + [pl.BlockSpec((tm, gate_w), row)],
        out_shape=[jax.ShapeDtypeStruct((t, width), F32)]
        + [jax.ShapeDtypeStruct((t, width), BF16)] * 3
        + [jax.ShapeDtypeStruct((t, gate_w), BF16)],
        compiler_params=_params(("parallel",), blocks + temps),
        name="proj",
    )(x2, w_in_b)


def _pool_kernel(u_ref, wp_ref, sc_ref, o_ref):
    seq = u_ref.shape[0]
    gd = wp_ref.shape[1]
    row = lax.broadcasted_iota(jnp.int32, (seq, gd), 0)
    for g, w in enumerate(POOL_WINDOWS):
        cols = slice(g * gd, (g + 1) * gd)
        ug = u_ref[:, cols]
        s = ug
        sh = 1
        while sh < w:
            s = s + jnp.where(row >= sh, pltpu.roll(s, sh, axis=0), 0.0)
            sh *= 2
        cnt = jnp.minimum(row + 1, w).astype(F32)
        pooled = s / cnt - ug
        mixed = jnp.dot(pooled.astype(BF16), wp_ref[g], preferred_element_type=F32)
        o_ref[:, cols] = (mixed * sc_ref[:, cols]).astype(BF16)


def _pool(u, w_pool_b, pool_scale, seq):
    t, width = u.shape
    g, gd, _ = w_pool_b.shape
    blocks = 2 * (seq * width * (4 + 2) + g * gd * gd * 2 + width * 4)
    temps = 6 * seq * gd * 4
    return pl.pallas_call(
        _pool_kernel,
        grid=(t // seq,),
        in_specs=[pl.BlockSpec((seq, width), lambda b: (b, 0)),
                  pl.BlockSpec((g, gd, gd), lambda b: (0, 0, 0)),
                  pl.BlockSpec((1, width), lambda b: (0, 0))],
        out_specs=pl.BlockSpec((seq, width), lambda b: (b, 0)),
        out_shape=jax.ShapeDtypeStruct((t, width), BF16),
        compiler_params=_params(("parallel",), blocks + temps),
        name="pool",
    )(u, w_pool_b, pool_scale)


def _attn_kernel(q_ref, k_ref, v_ref, o_ref, *, tb):
    qi = pl.program_id(1)
    pairs = q_ref.shape[1] // LANES
    first_head = lax.broadcasted_iota(jnp.int32, (tb, LANES), 1) < HEAD_DIM
    r = lax.broadcasted_iota(jnp.int32, (2 * tb, 2 * tb), 0)
    c = lax.broadcasted_iota(jnp.int32, (2 * tb, 2 * tb), 1)
    same_head = (r >= tb) == (c >= tb)
    cum = -jnp.concatenate([(same_head & (r > c)).astype(BF16), same_head.astype(BF16)], axis=1)
    kcol = lax.broadcasted_iota(jnp.int32, (tb, 2 * tb), 1)
    kcol = jnp.where(kcol >= tb, kcol - tb, kcol)
    causal = kcol < lax.broadcasted_iota(jnp.int32, (tb, 2 * tb), 0)

    def stack_heads(blk):
        zero = jnp.zeros_like(blk)
        return jnp.concatenate(
            [jnp.where(first_head, blk, zero), jnp.where(first_head, zero, blk)], axis=0)

    def step(j, rems, accs, diagonal):
        ks = pl.multiple_of(j * tb, tb)
        col_blocks = [slice(p * LANES, (p + 1) * LANES) for p in range(pairs)]
        zs = [lax.dot_general(q_ref[:, cols], stack_heads(k_ref[pl.ds(ks, tb), cols]),
                              (((1,), (1,)), ((), ())), preferred_element_type=F32)
              for cols in col_blocks]
        log_betas, sums = [], []
        for z in zs:
            softplus = jnp.maximum(z, 0.0) + jnp.log2(1.0 + jnp.exp2(-jnp.abs(z)))
            log_betas.append(z - softplus)
            if diagonal:
                softplus = jnp.where(causal, softplus, 0.0)
            sums.append(jnp.dot(softplus.astype(BF16), cum, preferred_element_type=F32))
        new_rems, new_accs = [], []
        for p, cols in enumerate(col_blocks):
            later = sums[p][:, :2 * tb]
            if rems is not None:
                later = later + rems[p]
            a = jnp.exp2(log_betas[p] + later)
            if diagonal:
                a = jnp.where(causal, a, 0.0)
            out = jnp.dot(a.astype(BF16), stack_heads(v_ref[pl.ds(ks, tb), cols]),
                          preferred_element_type=F32)
            total = sums[p][:, 2 * tb:]
            new_accs.append(out if accs is None else accs[p] + out)
            new_rems.append(total if rems is None else rems[p] + total)
        return tuple(new_rems), tuple(new_accs)

    def rem_max(rems):
        return jnp.max(functools.reduce(jnp.maximum, rems))

    rems, accs = step(qi, None, None, diagonal=True)

    def cond(carry):
        j, _, _, worst = carry
        return (j >= 0) & (worst > ATTN_DEAD_LOG2)

    def body(carry):
        j, rems, accs, _ = carry
        rems, accs = step(j, rems, accs, diagonal=False)
        return j - 1, rems, accs, rem_max(rems)

    accs = lax.while_loop(cond, body, (qi - 1, rems, accs, rem_max(rems)))[2]
    for p in range(pairs):
        o_ref[:, p * LANES:(p + 1) * LANES] = accs[p].astype(BF16)


def _attention(q, k, v, seq, tb):
    t, width = q.shape
    blocks = 2 * (2 * tb * width * 2 + 2 * seq * width * 2)
    temps = (width // LANES) * 10 * tb * 2 * tb * 4 + 2 * tb * 4 * tb * 2
    qspec = pl.BlockSpec((tb, width), lambda b, i: (b * (seq // tb) + i, 0))
    kvspec = pl.BlockSpec((seq, width), lambda b, i: (b, 0))
    return pl.pallas_call(
        functools.partial(_attn_kernel, tb=tb),
        grid=(t // seq, seq // tb),
        in_specs=[qspec, kvspec, kvspec],
        out_specs=qspec,
        out_shape=jax.ShapeDtypeStruct((t, width), BF16),
        compiler_params=_params(("parallel", "parallel"), blocks + temps),
        name="attn",
    )(q, k, v)


def _merge_kernel(x_ref, po_ref, at_ref, g_ref, wpu_ref, wau_ref, wo_ref, lg_ref, lb_ref,
                  wrh_ref, wrl_ref, br_ref, x1_ref, x1b_ref, comb_ref, sel_ref, cnt_ref, *, alpha):
    d = x_ref.shape[1]
    tm = x_ref.shape[0]
    a = jnp.dot(po_ref[...], wpu_ref[...], preferred_element_type=F32)
    b = jnp.dot(at_ref[...], wau_ref[...], preferred_element_type=F32)
    merged = g_ref[:, :d].astype(F32) * a + g_ref[:, d:].astype(F32) * b
    h = alpha * x_ref[...] + jnp.dot(merged.astype(BF16), wo_ref[...], preferred_element_type=F32)
    x1 = _layer_norm(h, lg_ref[...], lb_ref[...])
    x1_ref[...] = x1
    xh = x1.astype(BF16)
    x1b_ref[...] = xh

    xl = (x1 - xh.astype(F32)).astype(BF16)
    wrh = wrh_ref[...]
    logits = (jnp.dot(xh, wrh, preferred_element_type=F32)
              + jnp.dot(xh, wrl_ref[...], preferred_element_type=F32)
              + jnp.dot(xl, wrh, preferred_element_type=F32)) + br_ref[...]

    lane = lax.broadcasted_iota(jnp.int32, (tm, LANES), 1)

    def first_max(vals):
        m = jnp.max(vals, axis=1, keepdims=True)
        idx = jnp.min(jnp.where(vals == m, lane, LANES), axis=1, keepdims=True)
        return m, idx

    is_group = (lane >= GROUP_LANE0) & (lane < GROUP_LANE0 + N_GROUPS)
    gm, g_lane = first_max(jnp.where(is_group, logits, NEG_BIG))
    g_prob = 1.0 / jnp.sum(jnp.where(is_group, jnp.exp(logits - gm), 0.0), axis=1, keepdims=True)
    lo = EXPERTS_PER_GROUP * (g_lane - GROUP_LANE0)
    in_group = jnp.where((lane >= lo) & (lane < lo + EXPERTS_PER_GROUP), logits, NEG_BIG)
    m1, i1 = first_max(in_group)
    m2, i2 = first_max(jnp.where(lane == i1, NEG_BIG, in_group))
    e21 = jnp.exp(m2 - m1)
    w1 = g_prob / (1.0 + e21)
    w2 = w1 * e21
    comb_ref[...] = jnp.where(lane == i1, w1, 0.0) + jnp.where(lane == i2, w2, 0.0)
    sel = ((lane == i1) | (lane == i2)).astype(F32)
    sel_ref[...] = sel.astype(BF16)
    cnt_ref[0] = jnp.sum(sel, axis=0, keepdims=True)


def _merge(x2, pool_out, attn_out, gates, w_pu_b, w_au_b, w_o_b, ln_g, ln_b, wr_hi, wr_lo, b_r,
           alpha, tm):
    t, d = x2.shape
    width = pool_out.shape[1]
    row = lambda i: (i, 0)
    fixed = lambda i: (0, 0)
    blocks = 2 * (tm * d * 4 + 2 * tm * width * 2 + tm * 2 * d * 2 + 2 * width * d * 2 + d * d * 2
                  + 2 * d * 4 + 2 * d * LANES * 2 + LANES * 4
                  + tm * d * 4 + tm * d * 2 + tm * LANES * (4 + 2) + LANES * 4)
    temps = 5 * tm * d * 4
    return pl.pallas_call(
        functools.partial(_merge_kernel, alpha=alpha),
        grid=(t // tm,),
        in_specs=[pl.BlockSpec((tm, d), row), pl.BlockSpec((tm, width), row),
                  pl.BlockSpec((tm, width), row), pl.BlockSpec((tm, 2 * d), row),
                  pl.BlockSpec((width, d), fixed), pl.BlockSpec((width, d), fixed),
                  pl.BlockSpec((d, d), fixed), pl.BlockSpec((1, d), fixed), pl.BlockSpec((1, d), fixed),
                  pl.BlockSpec((d, LANES), fixed), pl.BlockSpec((d, LANES), fixed),
                  pl.BlockSpec((1, LANES), fixed)],
        out_specs=[pl.BlockSpec((tm, d), row), pl.BlockSpec((tm, d), row),
                   pl.BlockSpec((tm, LANES), row), pl.BlockSpec((tm, LANES), row),
                   pl.BlockSpec((1, 1, LANES), lambda i: (i, 0, 0))],
        out_shape=[jax.ShapeDtypeStruct((t, d), F32), jax.ShapeDtypeStruct((t, d), BF16),
                   jax.ShapeDtypeStruct((t, LANES), F32), jax.ShapeDtypeStruct((t, LANES), BF16),
                   jax.ShapeDtypeStruct((t // tm, 1, LANES), F32)],
        compiler_params=_params(("parallel",), blocks + temps),
        name="merge",
    )(x2, pool_out, attn_out, gates, w_pu_b, w_au_b, w_o_b, ln_g, ln_b, wr_hi, wr_lo, b_r)


def _routing_plan(counts, n_tiles):
    cnt = counts[:, 0, :N_EXPERTS].astype(jnp.int32)
    nch = (cnt + (CHUNK - 1)) // CHUNK
    local_end = jnp.cumsum(nch, axis=1)
    local_start = local_end - nch
    block_chunks = local_end[:, -1]
    before_block = jnp.cumsum(nch, axis=0) - nch
    expert_chunks = jnp.sum(nch, axis=0)
    chunks_per_tile = EXPERT_TILE // CHUNK
    expert_tiles = (expert_chunks + (chunks_per_tile - 1)) // chunks_per_tile
    tiles_end = jnp.cumsum(expert_tiles)
    region_start = (tiles_end - expert_tiles) * chunks_per_tile
    segment_dst = region_start[None, :] + before_block
    c = jnp.arange(LOCAL_CHUNKS, dtype=jnp.int32)
    owner = jnp.minimum(jnp.sum(c[None, :, None] >= local_end[:, None, :], axis=2), N_EXPERTS - 1)
    chunk_dst = (jnp.take_along_axis(segment_dst, owner, axis=1) + c[None, :]
                 - jnp.take_along_axis(local_start, owner, axis=1))
    chunk_dst = jnp.where(c[None, :] < block_chunks[:, None], chunk_dst, 0)
    tiles_used = tiles_end[-1]
    tile = jnp.minimum(jnp.arange(n_tiles, dtype=jnp.int32), tiles_used - 1)
    tile_expert = jnp.minimum(jnp.sum(tile[:, None] >= tiles_end[None, :], axis=1), N_EXPERTS - 1)
    i32 = lambda a: a.astype(jnp.int32)
    return dict(chunk_dst=i32(chunk_dst), block_chunks=i32(block_chunks), tile=i32(tile),
                tile_expert=i32(tile_expert), tiles_used=i32(tiles_used[None]),
                pad_start=i32(region_start + expert_chunks),
                pad_count=i32(expert_tiles * chunks_per_tile - expert_chunks))


def _local_positions(sel):
    tb = sel.shape[0]
    earlier = (lax.broadcasted_iota(jnp.int32, (tb, tb), 1)
               < lax.broadcasted_iota(jnp.int32, (tb, tb), 0)).astype(BF16)
    rank = jnp.dot(earlier, sel, preferred_element_type=F32)
    cnt = jnp.sum(sel.astype(F32), axis=0, keepdims=True)
    nch = jnp.floor((cnt + (CHUNK - 1)) * (1.0 / CHUNK))
    lower = (lax.broadcasted_iota(jnp.int32, (LANES, LANES), 0)
             < lax.broadcasted_iota(jnp.int32, (LANES, LANES), 1)).astype(BF16)
    start = CHUNK * jnp.dot(jnp.broadcast_to(nch, (8, LANES)).astype(BF16), lower,
                            preferred_element_type=F32)[0:1]
    pos = rank + start
    chosen = sel > 0
    pos_lo = jnp.min(jnp.where(chosen, pos, float(LOCAL_ROWS)), axis=1, keepdims=True)
    pos_hi = jnp.max(jnp.where(chosen, pos, -1.0), axis=1, keepdims=True)
    return pos, pos_lo, pos_hi


def _for_each(n, fn):
    lax.fori_loop(0, n, lambda c, carry: (fn(c), carry)[1], 0)


def _scatter_kernel(dst_ref, nchunk_ref, pad_start_ref, pad_count_ref, used_ref, x_ref, sel_ref, g_ref,
                    loc_ref, zero_ref, sems):
    b = pl.program_id(0)
    last = pl.num_programs(0) - 1
    slot = b % 2
    tb = x_ref.shape[0]

    def chunk_copy(blk, c, slot):
        src = pl.multiple_of(c * CHUNK, CHUNK)
        dst = pl.multiple_of(dst_ref[blk, c] * CHUNK, CHUNK)
        return pltpu.make_async_copy(loc_ref.at[slot, pl.ds(src, CHUNK)],
                                     g_ref.at[pl.ds(dst, CHUNK)], sems.at[slot])

    def pad_copy(e, k):
        dst = pl.multiple_of((pad_start_ref[e] + k) * CHUNK, CHUNK)
        return pltpu.make_async_copy(zero_ref.at[pl.ds(0, CHUNK)], g_ref.at[pl.ds(dst, CHUNK)],
                                     sems.at[2])

    def unused_tile_copy(i):
        dst = pl.multiple_of(i * EXPERT_TILE, EXPERT_TILE)
        return pltpu.make_async_copy(zero_ref, g_ref.at[pl.ds(dst, EXPERT_TILE)], sems.at[2])

    @pl.when(b >= 2)
    def _():
        _for_each(nchunk_ref[b - 2], lambda c: chunk_copy(b - 2, c, slot).wait())

    _, pos_lo, pos_hi = _local_positions(sel_ref[...])
    lo = pos_lo.astype(jnp.int32)
    hi = pos_hi.astype(jnp.int32)
    x = x_ref[...]

    def sort_rows(m, carry):
        r0 = pl.multiple_of(m * SORT_ROWS, SORT_ROWS)
        r = r0 + lax.broadcasted_iota(jnp.int32, (tb, SORT_ROWS), 1)
        perm = ((r == lo) | (r == hi)).astype(BF16)
        rows = lax.dot_general(perm, x, (((0,), (0,)), ((), ())), preferred_element_type=F32)
        loc_ref[slot, pl.ds(r0, SORT_ROWS), :] = rows.astype(BF16)
        return carry

    lax.fori_loop(0, (nchunk_ref[b] * CHUNK + (SORT_ROWS - 1)) // SORT_ROWS, sort_rows, 0)
    _for_each(nchunk_ref[b], lambda c: chunk_copy(b, c, slot).start())

    @pl.when(b == last)
    def _():
        zero_ref[...] = jnp.zeros_like(zero_ref)
        n_unused = g_ref.shape[0] // EXPERT_TILE - used_ref[0]
        _for_each(N_EXPERTS, lambda e: _for_each(pad_count_ref[e], lambda k: pad_copy(e, k).start()))
        _for_each(n_unused, lambda i: unused_tile_copy(used_ref[0] + i).start())
        _for_each(N_EXPERTS, lambda e: _for_each(pad_count_ref[e], lambda k: pad_copy(e, k).wait()))
        _for_each(n_unused, lambda i: unused_tile_copy(used_ref[0] + i).wait())

        @pl.when(b >= 1)
        def _():
            _for_each(nchunk_ref[b - 1], lambda c: chunk_copy(b - 1, c, 1 - slot).wait())

        _for_each(nchunk_ref[b], lambda c: chunk_copy(b, c, slot).wait())


def _scatter(x1b, sel, plan, n_tiles):
    t, d = x1b.shape
    tb = ROUTE_BLOCK
    row = lambda b, *_: (b, 0)
    blocks = 2 * (tb * d * 2 + tb * LANES * 2) + 2 * LOCAL_ROWS * d * 2 + EXPERT_TILE * d * 2
    temps = tb * tb * 2 + 6 * tb * LANES * 4 + tb * SORT_ROWS * 6 + SORT_ROWS * d * 6
    return pl.pallas_call(
        _scatter_kernel,
        grid_spec=pltpu.PrefetchScalarGridSpec(
            num_scalar_prefetch=5,
            grid=(t // tb,),
            in_specs=[pl.BlockSpec((tb, d), row), pl.BlockSpec((tb, LANES), row)],
            out_specs=pl.BlockSpec(memory_space=pl.ANY),
            scratch_shapes=[pltpu.VMEM((2, LOCAL_ROWS, d), BF16),
                            pltpu.VMEM((EXPERT_TILE, d), BF16), pltpu.SemaphoreType.DMA((3,))]),
        out_shape=jax.ShapeDtypeStruct((n_tiles * EXPERT_TILE, d), BF16),
        compiler_params=_params(("arbitrary",), blocks + temps),
        name="scatter",
    )(plan["chunk_dst"], plan["block_chunks"], plan["pad_start"], plan["pad_count"],
      plan["tiles_used"], x1b, sel)


def _experts_kernel(tile_ref, expert_ref, used_ref, g_ref, wg_ref, wu_ref, wd_ref, y_ref,
                    wgb_ref, wub_ref, wdb_ref):
    i = pl.program_id(0)
    expert = expert_ref[i]

    @pl.when((i == 0) | (expert != expert_ref[jnp.maximum(i - 1, 0)]))
    def _():
        wgb_ref[...] = wg_ref[0].astype(BF16)
        wub_ref[...] = wu_ref[0].astype(BF16)
        wdb_ref[...] = wd_ref[0].astype(BF16)

    @pl.when(i < used_ref[0])
    def _():
        x = g_ref[...]
        hg = jnp.dot(x, wgb_ref[...], preferred_element_type=F32)
        hu = jnp.dot(x, wub_ref[...], preferred_element_type=F32)
        h = hg * _sigmoid(hg) * hu
        y_ref[...] = jnp.dot(h.astype(BF16), wdb_ref[...], preferred_element_type=F32).astype(BF16)

    @pl.when(i >= used_ref[0])
    def _():
        y_ref[...] = jnp.zeros_like(y_ref)


def _experts(sorted_x, plan, w_eg, w_eu, w_ed):
    rows, d = sorted_x.shape
    _, _, de = w_eg.shape
    tm = EXPERT_TILE
    by_tile = lambda i, tile, expert, used: (tile[i], 0)
    by_expert = lambda i, tile, expert, used: (expert[i], 0, 0)
    blocks = 2 * (2 * tm * d * 2 + 3 * d * de * 4) + 3 * d * de * 2
    temps = 3 * tm * de * 4 + tm * d * 4 + d * de * 4
    return pl.pallas_call(
        _experts_kernel,
        grid_spec=pltpu.PrefetchScalarGridSpec(
            num_scalar_prefetch=3,
            grid=(rows // tm,),
            in_specs=[pl.BlockSpec((tm, d), by_tile), pl.BlockSpec((1, d, de), by_expert),
                      pl.BlockSpec((1, d, de), by_expert), pl.BlockSpec((1, de, d), by_expert)],
            out_specs=pl.BlockSpec((tm, d), lambda i, *_: (i, 0)),
            scratch_shapes=[pltpu.VMEM((d, de), BF16), pltpu.VMEM((d, de), BF16),
                            pltpu.VMEM((de, d), BF16)]),
        out_shape=jax.ShapeDtypeStruct((rows, d), BF16),
        compiler_params=_params(("arbitrary",), blocks + temps),
        name="experts",
    )(plan["tile"], plan["tile_expert"], plan["tiles_used"], sorted_x, w_eg, w_eu, w_ed)


def _combine_kernel(dst_ref, nchunk_ref, x1_ref, comb_ref, sel_ref, p_ref, wpg_ref, wpp_ref,
                    lg_ref, lb_ref, y_ref, o_ref, loc_ref, sems, *, alpha):
    b = pl.program_id(0)
    nb = pl.num_programs(0)
    slot = b % 2
    tb, d = x1_ref.shape

    def chunk_copy(blk, c, slot):
        src = pl.multiple_of(dst_ref[blk, c] * CHUNK, CHUNK)
        dst = pl.multiple_of(c * CHUNK, CHUNK)
        return pltpu.make_async_copy(y_ref.at[pl.ds(src, CHUNK)],
                                     loc_ref.at[slot, pl.ds(dst, CHUNK)], sems.at[slot])

    @pl.when(b == 0)
    def _():
        loc_ref[...] = jnp.zeros_like(loc_ref)
        _for_each(nchunk_ref[0], lambda c: chunk_copy(0, c, 0).start())

    @pl.when(b + 1 < nb)
    def _():
        _for_each(nchunk_ref[b + 1], lambda c: chunk_copy(b + 1, c, 1 - slot).start())

    _for_each(nchunk_ref[b], lambda c: chunk_copy(b, c, slot).wait())

    sel = sel_ref[...]
    pos, pos_lo, pos_hi = _local_positions(sel)
    chosen = sel > 0
    comb = comb_ref[...]
    w_lo = jnp.sum(jnp.where(chosen & (pos == pos_lo), comb, 0.0), axis=1, keepdims=True)
    w_hi = jnp.sum(jnp.where(chosen & (pos == pos_hi), comb, 0.0), axis=1, keepdims=True)
    lo = pos_lo.astype(jnp.int32)
    hi = pos_hi.astype(jnp.int32)

    def unsort_rows(m, acc):
        r0 = pl.multiple_of(m * SORT_ROWS, SORT_ROWS)
        r = r0 + lax.broadcasted_iota(jnp.int32, (tb, SORT_ROWS), 1)
        weights = (jnp.where(r == lo, w_lo, 0.0) + jnp.where(r == hi, w_hi, 0.0)).astype(BF16)
        return acc + jnp.dot(weights, loc_ref[slot, pl.ds(r0, SORT_ROWS), :],
                             preferred_element_type=F32)

    moe = lax.fori_loop(0, (nchunk_ref[b] * CHUNK + (SORT_ROWS - 1)) // SORT_ROWS, unsort_rows,
                        jnp.zeros((tb, d), F32))

    x1 = x1_ref[...]
    gate = _sigmoid(jnp.dot(x1.astype(BF16), wpg_ref[...], preferred_element_type=F32))
    emb = jnp.dot(p_ref[...].astype(BF16), wpp_ref[...], preferred_element_type=F32)
    h = alpha * x1 + moe + gate * emb
    o_ref[...] = _layer_norm(h, lg_ref[...], lb_ref[...])


def _combine(y, plan, x1, comb, sel, p2, w_pg_b, w_pp_b, ln_g, ln_b, alpha):
    t, d = x1.shape
    pd = p2.shape[1]
    tb = ROUTE_BLOCK
    row = lambda b, *_: (b, 0)
    fixed = lambda b, *_: (0, 0)
    blocks = (2 * (2 * tb * d * 4 + tb * LANES * (4 + 2) + tb * pd * 4 + d * d * 2 + pd * d * 2
                   + 2 * d * 4) + 2 * LOCAL_ROWS * d * 2)
    temps = tb * tb * 2 + 8 * tb * LANES * 4 + tb * SORT_ROWS * 10 + 5 * tb * d * 4
    return pl.pallas_call(
        functools.partial(_combine_kernel, alpha=alpha),
        grid_spec=pltpu.PrefetchScalarGridSpec(
            num_scalar_prefetch=2,
            grid=(t // tb,),
            in_specs=[pl.BlockSpec((tb, d), row), pl.BlockSpec((tb, LANES), row),
                      pl.BlockSpec((tb, LANES), row), pl.BlockSpec((tb, pd), row),
                      pl.BlockSpec((d, d), fixed), pl.BlockSpec((pd, d), fixed),
                      pl.BlockSpec((1, d), fixed), pl.BlockSpec((1, d), fixed),
                      pl.BlockSpec(memory_space=pl.ANY)],
            out_specs=pl.BlockSpec((tb, d), row),
            scratch_shapes=[pltpu.VMEM((2, LOCAL_ROWS, d), BF16), pltpu.SemaphoreType.DMA((2,))]),
        out_shape=jax.ShapeDtypeStruct((t, d), F32),
        compiler_params=_params(("arbitrary",), blocks + temps),
        name="combine",
    )(plan["chunk_dst"], plan["block_chunks"], x1, comb, sel, p2, w_pg_b, w_pp_b, ln_g, ln_b, y)


def kernel(x, p, w_in, w_pool, pool_scale, w_pu, w_au, w_o, ln1_g, ln1_b, w_rg, b_rg, w_re, b_re,
           w_eg, w_eu, w_ed, w_pg, w_pp, ln2_g, ln2_b):
    bsz, seq, d = x.shape
    depth = w_in.shape[0]
    t = bsz * seq
    de = w_eg.shape[-1]
    alpha = (2.0 * depth) ** 0.25
    assert w_rg.shape[2] == N_GROUPS and w_re.shape[1:] == (N_GROUPS, d, EXPERTS_PER_GROUP)
    assert w_in.shape[2] == 4 * d and w_pool.shape[1] == len(POOL_WINDOWS)
    assert t % ROUTE_BLOCK == 0 and N_EXPERTS + N_GROUPS <= LANES
    n_blocks = t // ROUTE_BLOCK
    n_tiles = -(-(2 * t + n_blocks * N_EXPERTS * (CHUNK - 1) + N_EXPERTS * (EXPERT_TILE - CHUNK))
                // EXPERT_TILE)

    x2 = x.reshape(t, d)
    for i in range(depth):
        u, q, k, v, gates = _proj(x2, w_in[i].astype(BF16), tm=512)
        pool_out = _pool(u, w_pool[i].astype(BF16), pool_scale[i][None, :], seq)
        attn_out = _attention(q, k, v, seq, tb=128)

        w_r = jnp.concatenate(
            [w_re[i].transpose(1, 0, 2).reshape(d, N_EXPERTS), w_rg[i]], axis=1)
        w_r = jnp.pad(w_r, ((0, 0), (0, LANES - w_r.shape[1])))
        b_r = jnp.pad(jnp.concatenate([b_re[i].reshape(-1), b_rg[i]]),
                      (0, LANES - N_GROUPS - N_EXPERTS))[None, :]
        wr_hi = w_r.astype(BF16)
        wr_lo = (w_r - wr_hi.astype(F32)).astype(BF16)

        x1, x1b, comb, sel, counts = _merge(
            x2, pool_out, attn_out, gates, w_pu[i].astype(BF16), w_au[i].astype(BF16),
            w_o[i].astype(BF16), ln1_g[i][None, :], ln1_b[i][None, :], wr_hi, wr_lo, b_r,
            alpha, tm=ROUTE_BLOCK)

        plan = _routing_plan(counts, n_tiles)
        sorted_x = _scatter(x1b, sel, plan, n_tiles)
        y = _experts(sorted_x, plan, w_eg[i].reshape(N_EXPERTS, d, de),
                     w_eu[i].reshape(N_EXPERTS, d, de), w_ed[i].reshape(N_EXPERTS, de, d))
        x2 = _combine(y, plan, x1, comb, sel, p[i].reshape(t, -1), w_pg[i].astype(BF16),
                      w_pp[i].astype(BF16), ln2_g[i][None, :], ln2_b[i][None, :], alpha)
    return x2.reshape(bsz, seq, d)
```

```python
import functools
import math

import jax
import jax.numpy as jnp
from jax import lax
from jax.experimental import pallas as pl
from jax.experimental.pallas import tpu as pltpu

F32 = jnp.float32
BF16 = jnp.bfloat16

LANES = 128
POOL_WINDOWS = (2, 4, 8, 16)
HEAD_DIM = 64
N_GROUPS = 4
EXPERTS_PER_GROUP = 8
N_EXPERTS = N_GROUPS * EXPERTS_PER_GROUP
LN_EPS = 1e-5
GROUP_LANE0 = N_EXPERTS
NEG_BIG = -1e30
ROUTE_BLOCK = 512
CHUNK = 16
EXPERT_TILE = 256
SORT_ROWS = 256
ROW_SUBS = 2
LOCAL_ROWS = -(-(2 * ROUTE_BLOCK + N_EXPERTS * (CHUNK - 1)) // SORT_ROWS) * SORT_ROWS
LOCAL_CHUNKS = LOCAL_ROWS // CHUNK
ATTN_DEAD_LOG2 = -160.0
VMEM_CAP_BYTES = 56 * 1024 * 1024


def _params(sem, vmem_bytes):
    return pltpu.CompilerParams(
        dimension_semantics=sem, vmem_limit_bytes=min(int(vmem_bytes), VMEM_CAP_BYTES))


def _layer_norm(h, g, b):
    mu = jnp.mean(h, axis=-1, keepdims=True)
    c = h - mu
    var = jnp.mean(c * c, axis=-1, keepdims=True)
    return c * lax.rsqrt(var + LN_EPS) * g + b


def _sigmoid(z):
    return 1.0 / (1.0 + jnp.exp(-z))


def _row_subs(rows):
    return [slice(k * (rows // ROW_SUBS), (k + 1) * (rows // ROW_SUBS)) for k in range(ROW_SUBS)]


def _skewed(subs, first, second):
    out, pending = [], None
    for r in subs:
        mid = first(r)
        if pending is not None:
            out.append(second(*pending))
        pending = (r, mid)
    out.append(second(*pending))
    return out


def _resident(shape):
    return pl.BlockSpec(shape, lambda *_: (0,) * len(shape), pipeline_mode=pl.Buffered(1))


def _cast_once(w_ref, wb_ref):
    @pl.when(pl.program_id(0) == 0)
    def _():
        wb_ref[...] = w_ref[...].astype(BF16)


def _proj_kernel(x_ref, w_ref, u_ref, q_ref, k_ref, v_ref, g_ref, wb_ref, *, width):
    q_scale = math.log2(math.e) / math.sqrt(HEAD_DIM)
    _cast_once(w_ref, wb_ref)
    xb = x_ref[...].astype(BF16)

    def mm(lo):
        return jnp.dot(xb, wb_ref[:, lo:lo + width], preferred_element_type=F32)

    u_ref[...] = mm(0)
    q_ref[...] = (mm(width) * q_scale).astype(BF16)
    k_ref[...] = mm(2 * width).astype(BF16)
    v_ref[...] = mm(3 * width).astype(BF16)
    for c in range(g_ref.shape[1] // width):
        g_ref[:, c * width:(c + 1) * width] = _sigmoid(mm((4 + c) * width)).astype(BF16)


def _proj(x2, w_in, tm):
    t, d = x2.shape
    n = w_in.shape[1]
    width = d // 2
    gate_w = n - 4 * width
    row = lambda i: (i, 0)
    blocks = (2 * (tm * d * 4 + tm * width * (4 + 3 * 2) + tm * gate_w * 2)
              + d * n * (4 + 2))
    temps = tm * d * 2 + 2 * tm * width * 4
    return pl.pallas_call(
        functools.partial(_proj_kernel, width=width),
        grid=(t // tm,),
        in_specs=[pl.BlockSpec((tm, d), row), _resident((d, n))],
        out_specs=[pl.BlockSpec((tm, width), row)] * 4 + [pl.BlockSpec((tm, gate_w), row)],
        out_shape=[jax.ShapeDtypeStruct((t, width), F32)]
        + [jax.ShapeDtypeStruct((t, width), BF16)] * 3
        + [jax.ShapeDtypeStruct((t, gate_w), BF16)],
        scratch_shapes=[pltpu.VMEM((d, n), BF16)],
        compiler_params=_params(("arbitrary",), blocks + temps),
        name="proj",
    )(x2, w_in)


def _pool_kernel(u_ref, wp_ref, sc_ref, o_ref):
    seq = u_ref.shape[0]
    gd = wp_ref.shape[1]
    row = lax.broadcasted_iota(jnp.int32, (seq, gd), 0)
    for g, w in enumerate(POOL_WINDOWS):
        cols = slice(g * gd, (g + 1) * gd)
        ug = u_ref[:, cols]
        s = ug
        sh = 1
        while sh < w:
            s = s + jnp.where(row >= sh, pltpu.roll(s, sh, axis=0), 0.0)
            sh *= 2
        cnt = jnp.minimum(row + 1, w).astype(F32)
        pooled = s / cnt - ug
        mixed = jnp.dot(pooled.astype(BF16), wp_ref[g].astype(BF16), preferred_element_type=F32)
        o_ref[:, cols] = (mixed * sc_ref[:, cols]).astype(BF16)


def _pool(u, w_pool, pool_scale, seq):
    t, width = u.shape
    g, gd, _ = w_pool.shape
    blocks = 2 * (seq * width * (4 + 2) + g * gd * gd * 4 + width * 4)
    temps = 6 * seq * gd * 4
    return pl.pallas_call(
        _pool_kernel,
        grid=(t // seq,),
        in_specs=[pl.BlockSpec((seq, width), lambda b: (b, 0)),
                  pl.BlockSpec((g, gd, gd), lambda b: (0, 0, 0)),
                  pl.BlockSpec((1, width), lambda b: (0, 0))],
        out_specs=pl.BlockSpec((seq, width), lambda b: (b, 0)),
        out_shape=jax.ShapeDtypeStruct((t, width), BF16),
        compiler_params=_params(("parallel",), blocks + temps),
        name="pool",
    )(u, w_pool, pool_scale)


def _attn_kernel(q_ref, k_ref, v_ref, o_ref, *, tb):
    qi = pl.program_id(1)
    pairs = q_ref.shape[1] // LANES
    first_head = lax.broadcasted_iota(jnp.int32, (tb, LANES), 1) < HEAD_DIM
    r = lax.broadcasted_iota(jnp.int32, (2 * tb, 2 * tb), 0)
    c = lax.broadcasted_iota(jnp.int32, (2 * tb, 2 * tb), 1)
    same_head = (r >= tb) == (c >= tb)
    cum = -jnp.concatenate([(same_head & (r > c)).astype(BF16), same_head.astype(BF16)], axis=1)
    kcol = lax.broadcasted_iota(jnp.int32, (tb, 2 * tb), 1)
    kcol = jnp.where(kcol >= tb, kcol - tb, kcol)
    causal = kcol < lax.broadcasted_iota(jnp.int32, (tb, 2 * tb), 0)

    def stack_heads(blk):
        zero = jnp.zeros_like(blk)
        return jnp.concatenate(
            [jnp.where(first_head, blk, zero), jnp.where(first_head, zero, blk)], axis=0)

    def step(j, rems, accs, diagonal):
        ks = pl.multiple_of(j * tb, tb)
        col_blocks = [slice(p * LANES, (p + 1) * LANES) for p in range(pairs)]
        zs = [lax.dot_general(q_ref[:, cols], stack_heads(k_ref[pl.ds(ks, tb), cols]),
                              (((1,), (1,)), ((), ())), preferred_element_type=F32)
              for cols in col_blocks]
        log_betas, sums = [], []
        for z in zs:
            softplus = jnp.maximum(z, 0.0) + jnp.log2(1.0 + jnp.exp2(-jnp.abs(z)))
            log_betas.append(z - softplus)
            if diagonal:
                softplus = jnp.where(causal, softplus, 0.0)
            sums.append(jnp.dot(softplus.astype(BF16), cum, preferred_element_type=F32))
        new_rems, new_accs = [], []
        for p, cols in enumerate(col_blocks):
            later = sums[p][:, :2 * tb]
            if rems is not None:
                later = later + rems[p]
            a = jnp.exp2(log_betas[p] + later)
            if diagonal:
                a = jnp.where(causal, a, 0.0)
            out = jnp.dot(a.astype(BF16), stack_heads(v_ref[pl.ds(ks, tb), cols]),
                          preferred_element_type=F32)
            total = sums[p][:, 2 * tb:]
            new_accs.append(out if accs is None else accs[p] + out)
            new_rems.append(total if rems is None else rems[p] + total)
        return tuple(new_rems), tuple(new_accs)

    def rem_max(rems):
        return jnp.max(functools.reduce(jnp.maximum, rems))

    rems, accs = step(qi, None, None, diagonal=True)

    def cond(carry):
        j, _, _, worst = carry
        return (j >= 0) & (worst > ATTN_DEAD_LOG2)

    def body(carry):
        j, rems, accs, _ = carry
        rems, accs = step(j, rems, accs, diagonal=False)
        return j - 1, rems, accs, rem_max(rems)

    accs = lax.while_loop(cond, body, (qi - 1, rems, accs, rem_max(rems)))[2]
    for p in range(pairs):
        o_ref[:, p * LANES:(p + 1) * LANES] = accs[p].astype(BF16)


def _attention(q, k, v, seq, tb):
    t, width = q.shape
    blocks = 2 * (2 * tb * width * 2 + 2 * seq * width * 2)
    temps = (width // LANES) * 10 * tb * 2 * tb * 4 + 2 * tb * 4 * tb * 2
    qspec = pl.BlockSpec((tb, width), lambda b, i: (b * (seq // tb) + i, 0))
    kvspec = pl.BlockSpec((seq, width), lambda b, i: (b, 0))
    return pl.pallas_call(
        functools.partial(_attn_kernel, tb=tb),
        grid=(t // seq, seq // tb),
        in_specs=[qspec, kvspec, kvspec],
        out_specs=qspec,
        out_shape=jax.ShapeDtypeStruct((t, width), BF16),
        compiler_params=_params(("parallel", "parallel"), blocks + temps),
        name="attn",
    )(q, k, v)


def _route(logits):
    lane = lax.broadcasted_iota(jnp.int32, logits.shape, 1)

    def first_max(vals):
        m = jnp.max(vals, axis=1, keepdims=True)
        idx = jnp.min(jnp.where(vals == m, lane, LANES), axis=1, keepdims=True)
        return m, idx

    is_group = (lane >= GROUP_LANE0) & (lane < GROUP_LANE0 + N_GROUPS)
    gm, g_lane = first_max(jnp.where(is_group, logits, NEG_BIG))
    g_prob = 1.0 / jnp.sum(jnp.where(is_group, jnp.exp(logits - gm), 0.0), axis=1, keepdims=True)
    lo = EXPERTS_PER_GROUP * (g_lane - GROUP_LANE0)
    in_group = jnp.where((lane >= lo) & (lane < lo + EXPERTS_PER_GROUP), logits, NEG_BIG)
    m1, i1 = first_max(in_group)
    m2, i2 = first_max(jnp.where(lane == i1, NEG_BIG, in_group))
    e21 = jnp.exp(m2 - m1)
    w1 = g_prob / (1.0 + e21)
    w2 = w1 * e21
    comb = jnp.where(lane == i1, w1, 0.0) + jnp.where(lane == i2, w2, 0.0)
    sel = ((lane == i1) | (lane == i2)).astype(F32)
    return comb, sel


def _merge_kernel(x_ref, po_ref, at_ref, g_ref, wpu_ref, wau_ref, wo_ref, lg_ref, lb_ref,
                  wr_ref, br_ref, x1_ref, x1b_ref, comb_ref, sel_ref, cnt_ref,
                  wpub_ref, waub_ref, wob_ref, wrb_ref, *, alpha):
    tm, d = x_ref.shape
    _cast_once(wpu_ref, wpub_ref)
    _cast_once(wau_ref, waub_ref)
    _cast_once(wo_ref, wob_ref)

    @pl.when(pl.program_id(0) == 0)
    def _():
        w_r = wr_ref[...]
        hi = w_r.astype(BF16)
        wrb_ref[:, :LANES] = hi
        wrb_ref[:, LANES:] = (w_r - hi.astype(F32)).astype(BF16)

    def mix(r):
        a = jnp.dot(po_ref[r, :], wpub_ref[...], preferred_element_type=F32)
        b = jnp.dot(at_ref[r, :], waub_ref[...], preferred_element_type=F32)
        merged = (g_ref[r, :d].astype(F32) * a + g_ref[r, d:].astype(F32) * b).astype(BF16)
        return alpha * x_ref[r, :] + jnp.dot(merged, wob_ref[...], preferred_element_type=F32)

    def norm_and_route(r, h):
        x1 = _layer_norm(h, lg_ref[...], lb_ref[...])
        xh = x1.astype(BF16)
        x1_ref[r, :] = x1
        x1b_ref[r, :] = xh
        xl = (x1 - xh.astype(F32)).astype(BF16)
        by_hi = jnp.dot(xh, wrb_ref[...], preferred_element_type=F32)
        by_lo = jnp.dot(xl, wrb_ref[:, :LANES], preferred_element_type=F32)
        comb, sel = _route(by_hi[:, :LANES] + by_hi[:, LANES:] + by_lo + br_ref[...])
        comb_ref[r, :] = comb
        sel_ref[r, :] = sel.astype(BF16)
        return jnp.sum(sel, axis=0, keepdims=True)

    cnt_ref[0] = sum(_skewed(_row_subs(tm), mix, norm_and_route))


def _merge(x2, pool_out, attn_out, gates, w_pu, w_au, w_o, ln_g, ln_b, w_r, b_r, alpha, tm):
    t, d = x2.shape
    width = pool_out.shape[1]
    row = lambda i: (i, 0)
    blocks = (2 * (tm * d * 4 + 2 * tm * width * 2 + tm * 2 * d * 2
                   + tm * d * 4 + tm * d * 2 + tm * LANES * (4 + 2) + LANES * 4)
              + (2 * width * d + d * d + d * LANES) * (4 + 2) + 2 * d * 4 + LANES * 4)
    temps = 5 * tm * d * 4
    return pl.pallas_call(
        functools.partial(_merge_kernel, alpha=alpha),
        grid=(t // tm,),
        in_specs=[pl.BlockSpec((tm, d), row), pl.BlockSpec((tm, width), row),
                  pl.BlockSpec((tm, width), row), pl.BlockSpec((tm, 2 * d), row),
                  _resident((width, d)), _resident((width, d)), _resident((d, d)),
                  _resident((1, d)), _resident((1, d)), _resident((d, LANES)),
                  _resident((1, LANES))],
        out_specs=[pl.BlockSpec((tm, d), row), pl.BlockSpec((tm, d), row),
                   pl.BlockSpec((tm, LANES), row), pl.BlockSpec((tm, LANES), row),
                   pl.BlockSpec((1, 1, LANES), lambda i: (i, 0, 0))],
        out_shape=[jax.ShapeDtypeStruct((t, d), F32), jax.ShapeDtypeStruct((t, d), BF16),
                   jax.ShapeDtypeStruct((t, LANES), F32), jax.ShapeDtypeStruct((t, LANES), BF16),
                   jax.ShapeDtypeStruct((t // tm, 1, LANES), F32)],
        scratch_shapes=[pltpu.VMEM((width, d), BF16), pltpu.VMEM((width, d), BF16),
                        pltpu.VMEM((d, d), BF16), pltpu.VMEM((d, 2 * LANES), BF16)],
        compiler_params=_params(("arbitrary",), blocks + temps),
        name="merge",
    )(x2, pool_out, attn_out, gates, w_pu, w_au, w_o, ln_g, ln_b, w_r, b_r)


def _routing_plan(counts, n_tiles):
    cnt = counts[:, 0, :N_EXPERTS].astype(jnp.int32)
    nch = (cnt + (CHUNK - 1)) // CHUNK
    local_end = jnp.cumsum(nch, axis=1)
    local_start = local_end - nch
    block_chunks = local_end[:, -1]
    before_block = jnp.cumsum(nch, axis=0) - nch
    expert_chunks = jnp.sum(nch, axis=0)
    chunks_per_tile = EXPERT_TILE // CHUNK
    expert_tiles = (expert_chunks + (chunks_per_tile - 1)) // chunks_per_tile
    tiles_end = jnp.cumsum(expert_tiles)
    region_start = (tiles_end - expert_tiles) * chunks_per_tile
    segment_dst = region_start[None, :] + before_block
    c = jnp.arange(LOCAL_CHUNKS, dtype=jnp.int32)
    owned = (c[None, :, None] >= local_start[:, None, :]) & (c[None, :, None] < local_end[:, None, :])
    chunk_dst = c[None, :] + jnp.sum(
        jnp.where(owned, (segment_dst - local_start)[:, None, :], 0), axis=2)
    tiles_used = tiles_end[-1]
    tile = jnp.minimum(jnp.arange(n_tiles, dtype=jnp.int32), tiles_used - 1)
    tile_expert = jnp.minimum(jnp.sum(tile[:, None] >= tiles_end[None, :], axis=1), N_EXPERTS - 1)
    i32 = lambda a: a.astype(jnp.int32)
    return dict(chunk_dst=i32(chunk_dst), block_chunks=i32(block_chunks), tile=i32(tile),
                tile_expert=i32(tile_expert), tiles_used=i32(tiles_used[None]),
                pad_start=i32(region_start + expert_chunks),
                pad_count=i32(expert_tiles * chunks_per_tile - expert_chunks))


def _local_positions(sel):
    tb = sel.shape[0]
    earlier = (lax.broadcasted_iota(jnp.int32, (tb, tb), 1)
               < lax.broadcasted_iota(jnp.int32, (tb, tb), 0)).astype(BF16)
    rank = jnp.dot(earlier, sel, preferred_element_type=F32)
    cnt = jnp.sum(sel.astype(F32), axis=0, keepdims=True)
    nch = jnp.floor((cnt + (CHUNK - 1)) * (1.0 / CHUNK))
    lower = (lax.broadcasted_iota(jnp.int32, (LANES, LANES), 0)
             < lax.broadcasted_iota(jnp.int32, (LANES, LANES), 1)).astype(BF16)
    start = CHUNK * jnp.dot(jnp.broadcast_to(nch, (8, LANES)).astype(BF16), lower,
                            preferred_element_type=F32)[0:1]
    pos = rank + start
    chosen = sel > 0
    pos_lo = jnp.min(jnp.where(chosen, pos, float(LOCAL_ROWS)), axis=1, keepdims=True)
    pos_hi = jnp.max(jnp.where(chosen, pos, -1.0), axis=1, keepdims=True)
    return pos, pos_lo, pos_hi


def _for_each(n, fn):
    lax.fori_loop(0, n, lambda c, carry: (fn(c), carry)[1], 0)


def _scatter_kernel(dst_ref, nchunk_ref, pad_start_ref, pad_count_ref, used_ref, x_ref, sel_ref, g_ref,
                    loc_ref, zero_ref, sems):
    b = pl.program_id(0)
    last = pl.num_programs(0) - 1
    slot = b % 2
    tb = x_ref.shape[0]

    def chunk_copy(blk, c, slot):
        src = pl.multiple_of(c * CHUNK, CHUNK)
        dst = pl.multiple_of(dst_ref[blk, c] * CHUNK, CHUNK)
        return pltpu.make_async_copy(loc_ref.at[slot, pl.ds(src, CHUNK)],
                                     g_ref.at[pl.ds(dst, CHUNK)], sems.at[slot])

    def pad_copy(e, k):
        dst = pl.multiple_of((pad_start_ref[e] + k) * CHUNK, CHUNK)
        return pltpu.make_async_copy(zero_ref.at[pl.ds(0, CHUNK)], g_ref.at[pl.ds(dst, CHUNK)],
                                     sems.at[2])

    def unused_tile_copy(i):
        dst = pl.multiple_of(i * EXPERT_TILE, EXPERT_TILE)
        return pltpu.make_async_copy(zero_ref, g_ref.at[pl.ds(dst, EXPERT_TILE)], sems.at[2])

    @pl.when(b >= 2)
    def _():
        _for_each(nchunk_ref[b - 2], lambda c: chunk_copy(b - 2, c, slot).wait())

    _, pos_lo, pos_hi = _local_positions(sel_ref[...])
    lo = pos_lo.astype(jnp.int32)
    hi = pos_hi.astype(jnp.int32)
    x = x_ref[...]

    def sort_rows(m, carry):
        r0 = pl.multiple_of(m * SORT_ROWS, SORT_ROWS)
        r = r0 + lax.broadcasted_iota(jnp.int32, (tb, SORT_ROWS), 1)
        perm = ((r == lo) | (r == hi)).astype(BF16)
        rows = lax.dot_general(perm, x, (((0,), (0,)), ((), ())), preferred_element_type=F32)
        loc_ref[slot, pl.ds(r0, SORT_ROWS), :] = rows.astype(BF16)
        return carry

    lax.fori_loop(0, (nchunk_ref[b] * CHUNK + (SORT_ROWS - 1)) // SORT_ROWS, sort_rows, 0)
    _for_each(nchunk_ref[b], lambda c: chunk_copy(b, c, slot).start())

    @pl.when(b == last)
    def _():
        zero_ref[...] = jnp.zeros_like(zero_ref)
        n_unused = g_ref.shape[0] // EXPERT_TILE - used_ref[0]
        _for_each(N_EXPERTS, lambda e: _for_each(pad_count_ref[e], lambda k: pad_copy(e, k).start()))
        _for_each(n_unused, lambda i: unused_tile_copy(used_ref[0] + i).start())
        _for_each(N_EXPERTS, lambda e: _for_each(pad_count_ref[e], lambda k: pad_copy(e, k).wait()))
        _for_each(n_unused, lambda i: unused_tile_copy(used_ref[0] + i).wait())

        @pl.when(b >= 1)
        def _():
            _for_each(nchunk_ref[b - 1], lambda c: chunk_copy(b - 1, c, 1 - slot).wait())

        _for_each(nchunk_ref[b], lambda c: chunk_copy(b, c, slot).wait())


def _scatter(x1b, sel, plan, n_tiles):
    t, d = x1b.shape
    tb = ROUTE_BLOCK
    row = lambda b, *_: (b, 0)
    blocks = 2 * (tb * d * 2 + tb * LANES * 2) + 2 * LOCAL_ROWS * d * 2 + EXPERT_TILE * d * 2
    temps = tb * tb * 2 + 6 * tb * LANES * 4 + tb * SORT_ROWS * 6 + SORT_ROWS * d * 6
    return pl.pallas_call(
        _scatter_kernel,
        grid_spec=pltpu.PrefetchScalarGridSpec(
            num_scalar_prefetch=5,
            grid=(t // tb,),
            in_specs=[pl.BlockSpec((tb, d), row), pl.BlockSpec((tb, LANES), row)],
            out_specs=pl.BlockSpec(memory_space=pl.ANY),
            scratch_shapes=[pltpu.VMEM((2, LOCAL_ROWS, d), BF16),
                            pltpu.VMEM((EXPERT_TILE, d), BF16), pltpu.SemaphoreType.DMA((3,))]),
        out_shape=jax.ShapeDtypeStruct((n_tiles * EXPERT_TILE, d), BF16),
        compiler_params=_params(("arbitrary",), blocks + temps),
        name="scatter",
    )(plan["chunk_dst"], plan["block_chunks"], plan["pad_start"], plan["pad_count"],
      plan["tiles_used"], x1b, sel)


def _experts_kernel(tile_ref, expert_ref, used_ref, g_ref, wg_ref, wu_ref, wd_ref, y_ref,
                    wgb_ref, wub_ref, wdb_ref):
    i = pl.program_id(0)
    expert = expert_ref[i]

    @pl.when((i == 0) | (expert != expert_ref[jnp.maximum(i - 1, 0)]))
    def _():
        wgb_ref[...] = wg_ref[0].astype(BF16)
        wub_ref[...] = wu_ref[0].astype(BF16)
        wdb_ref[...] = wd_ref[0].astype(BF16)

    @pl.when(i < used_ref[0])
    def _():
        x = g_ref[...]
        hg = jnp.dot(x, wgb_ref[...], preferred_element_type=F32)
        hu = jnp.dot(x, wub_ref[...], preferred_element_type=F32)
        h = hg * _sigmoid(hg) * hu
        y_ref[...] = jnp.dot(h.astype(BF16), wdb_ref[...], preferred_element_type=F32).astype(BF16)

    @pl.when(i >= used_ref[0])
    def _():
        y_ref[...] = jnp.zeros_like(y_ref)


def _experts(sorted_x, plan, w_eg, w_eu, w_ed):
    rows, d = sorted_x.shape
    _, _, de = w_eg.shape
    tm = EXPERT_TILE
    by_tile = lambda i, tile, expert, used: (tile[i], 0)
    by_expert = lambda i, tile, expert, used: (expert[i], 0, 0)
    blocks = 2 * (2 * tm * d * 2 + 3 * d * de * 4) + 3 * d * de * 2
    temps = 3 * tm * de * 4 + tm * d * 4 + d * de * 4
    return pl.pallas_call(
        _experts_kernel,
        grid_spec=pltpu.PrefetchScalarGridSpec(
            num_scalar_prefetch=3,
            grid=(rows // tm,),
            in_specs=[pl.BlockSpec((tm, d), by_tile), pl.BlockSpec((1, d, de), by_expert),
                      pl.BlockSpec((1, d, de), by_expert), pl.BlockSpec((1, de, d), by_expert)],
            out_specs=pl.BlockSpec((tm, d), lambda i, *_: (i, 0)),
            scratch_shapes=[pltpu.VMEM((d, de), BF16), pltpu.VMEM((d, de), BF16),
                            pltpu.VMEM((de, d), BF16)]),
        out_shape=jax.ShapeDtypeStruct((rows, d), BF16),
        compiler_params=_params(("arbitrary",), blocks + temps),
        name="experts",
    )(plan["tile"], plan["tile_expert"], plan["tiles_used"], sorted_x, w_eg, w_eu, w_ed)


def _combine_kernel(dst_ref, nchunk_ref, x1_ref, comb_ref, sel_ref, p_ref, wpg_ref, wpp_ref,
                    lg_ref, lb_ref, y_ref, o_ref, loc_ref, wpgb_ref, wppb_ref, sems, *, alpha):
    b = pl.program_id(0)
    nb = pl.num_programs(0)
    slot = b % 2
    tb, d = x1_ref.shape
    _cast_once(wpg_ref, wpgb_ref)
    _cast_once(wpp_ref, wppb_ref)

    def chunk_copy(blk, c, slot):
        src = pl.multiple_of(dst_ref[blk, c] * CHUNK, CHUNK)
        dst = pl.multiple_of(c * CHUNK, CHUNK)
        return pltpu.make_async_copy(y_ref.at[pl.ds(src, CHUNK)],
                                     loc_ref.at[slot, pl.ds(dst, CHUNK)], sems.at[slot])

    @pl.when(b == 0)
    def _():
        loc_ref[...] = jnp.zeros_like(loc_ref)
        _for_each(nchunk_ref[0], lambda c: chunk_copy(0, c, 0).start())

    @pl.when(b + 1 < nb)
    def _():
        _for_each(nchunk_ref[b + 1], lambda c: chunk_copy(b + 1, c, 1 - slot).start())

    _for_each(nchunk_ref[b], lambda c: chunk_copy(b, c, slot).wait())

    sel = sel_ref[...]
    pos, pos_lo, pos_hi = _local_positions(sel)
    chosen = sel > 0
    comb = comb_ref[...]
    w_lo = jnp.sum(jnp.where(chosen & (pos == pos_lo), comb, 0.0), axis=1, keepdims=True)
    w_hi = jnp.sum(jnp.where(chosen & (pos == pos_hi), comb, 0.0), axis=1, keepdims=True)
    lo = pos_lo.astype(jnp.int32)
    hi = pos_hi.astype(jnp.int32)

    sorted_row = lax.broadcasted_iota(jnp.int32, (tb // ROW_SUBS, LOCAL_ROWS), 1)

    def branches(r):
        weights = (jnp.where(sorted_row == lo[r], w_lo[r], 0.0)
                   + jnp.where(sorted_row == hi[r], w_hi[r], 0.0)).astype(BF16)
        moe = jnp.dot(weights, loc_ref[slot], preferred_element_type=F32)
        gate = jnp.dot(x1_ref[r, :].astype(BF16), wpgb_ref[...], preferred_element_type=F32)
        emb = jnp.dot(p_ref[r, :].astype(BF16), wppb_ref[...], preferred_element_type=F32)
        return moe, gate, emb

    def finish(r, parts):
        moe, gate, emb = parts
        h = alpha * x1_ref[r, :] + moe + _sigmoid(gate) * emb
        o_ref[r, :] = _layer_norm(h, lg_ref[...], lb_ref[...])

    _skewed(_row_subs(tb), branches, finish)


def _combine(y, plan, x1, comb, sel, p2, w_pg, w_pp, ln_g, ln_b, alpha):
    t, d = x1.shape
    pd = p2.shape[1]
    tb = ROUTE_BLOCK
    row = lambda b, *_: (b, 0)
    blocks = (2 * (2 * tb * d * 4 + tb * LANES * (4 + 2) + tb * pd * 4)
              + (d * d + pd * d) * (4 + 2) + 2 * d * 4 + 2 * LOCAL_ROWS * d * 2)
    temps = tb * tb * 2 + 8 * tb * LANES * 4 + tb * LOCAL_ROWS * 10 + 5 * tb * d * 4
    return pl.pallas_call(
        functools.partial(_combine_kernel, alpha=alpha),
        grid_spec=pltpu.PrefetchScalarGridSpec(
            num_scalar_prefetch=2,
            grid=(t // tb,),
            in_specs=[pl.BlockSpec((tb, d), row), pl.BlockSpec((tb, LANES), row),
                      pl.BlockSpec((tb, LANES), row), pl.BlockSpec((tb, pd), row),
                      _resident((d, d)), _resident((pd, d)), _resident((1, d)), _resident((1, d)),
                      pl.BlockSpec(memory_space=pl.ANY)],
            out_specs=pl.BlockSpec((tb, d), row),
            scratch_shapes=[pltpu.VMEM((2, LOCAL_ROWS, d), BF16), pltpu.VMEM((d, d), BF16),
                            pltpu.VMEM((pd, d), BF16), pltpu.SemaphoreType.DMA((2,))]),
        out_shape=jax.ShapeDtypeStruct((t, d), F32),
        compiler_params=_params(("arbitrary",), blocks + temps),
        name="combine",
    )(plan["chunk_dst"], plan["block_chunks"], x1, comb, sel, p2, w_pg, w_pp, ln_g, ln_b, y)


def kernel(x, p, w_in, w_pool, pool_scale, w_pu, w_au, w_o, ln1_g, ln1_b, w_rg, b_rg, w_re, b_re,
           w_eg, w_eu, w_ed, w_pg, w_pp, ln2_g, ln2_b):
    bsz, seq, d = x.shape
    depth = w_in.shape[0]
    t = bsz * seq
    de = w_eg.shape[-1]
    alpha = (2.0 * depth) ** 0.25
    assert w_rg.shape[2] == N_GROUPS and w_re.shape[1:] == (N_GROUPS, d, EXPERTS_PER_GROUP)
    assert w_in.shape[2] == 4 * d and w_pool.shape[1] == len(POOL_WINDOWS)
    assert t % ROUTE_BLOCK == 0 and N_EXPERTS + N_GROUPS <= LANES
    n_blocks = t // ROUTE_BLOCK
    n_tiles = -(-(2 * t + n_blocks * N_EXPERTS * (CHUNK - 1) + N_EXPERTS * (EXPERT_TILE - CHUNK))
                // EXPERT_TILE)

    x2 = x.reshape(t, d)
    for i in range(depth):
        u, q, k, v, gates = _proj(x2, w_in[i], tm=512)
        pool_out = _pool(u, w_pool[i], pool_scale[i][None, :], seq)
        attn_out = _attention(q, k, v, seq, tb=128)

        w_r = jnp.concatenate(
            [w_re[i].transpose(1, 0, 2).reshape(d, N_EXPERTS), w_rg[i]], axis=1)
        w_r = jnp.pad(w_r, ((0, 0), (0, LANES - w_r.shape[1])))
        b_r = jnp.pad(jnp.concatenate([b_re[i].reshape(-1), b_rg[i]]),
                      (0, LANES - N_GROUPS - N_EXPERTS))[None, :]

        x1, x1b, comb, sel, counts = _merge(
            x2, pool_out, attn_out, gates, w_pu[i], w_au[i], w_o[i], ln1_g[i][None, :],
            ln1_b[i][None, :], w_r, b_r, alpha, tm=ROUTE_BLOCK)

        plan = _routing_plan(counts, n_tiles)
        sorted_x = _scatter(x1b, sel, plan, n_tiles)
        y = _experts(sorted_x, plan, w_eg[i].reshape(N_EXPERTS, d, de),
                     w_eu[i].reshape(N_EXPERTS, d, de), w_ed[i].reshape(N_EXPERTS, de, d))
        x2 = _combine(y, plan, x1, comb, sel, p[i].reshape(t, -1), w_pg[i], w_pp[i],
                      ln2_g[i][None, :], ln2_b[i][None, :], alpha)
    return x2.reshape(bsz, seq, d)
```

```python
import functools
import math

import jax
import jax.numpy as jnp
from jax import lax
from jax.experimental import pallas as pl
from jax.experimental.pallas import tpu as pltpu

F32 = jnp.float32
BF16 = jnp.bfloat16

LANES = 128
POOL_WINDOWS = (2, 4, 8, 16)
HEAD_DIM = 64
N_GROUPS = 4
EXPERTS_PER_GROUP = 8
N_EXPERTS = N_GROUPS * EXPERTS_PER_GROUP
LN_EPS = 1e-5
GROUP_LANE0 = N_EXPERTS
NEG_BIG = -1e30
ROUTE_BLOCK = 512
CHUNK = 16
EXPERT_TILE = 256
SORT_ROWS = 256
ROW_SUBS = 2
LOCAL_ROWS = -(-(2 * ROUTE_BLOCK + N_EXPERTS * (CHUNK - 1)) // SORT_ROWS) * SORT_ROWS
LOCAL_CHUNKS = LOCAL_ROWS // CHUNK
ATTN_DEAD_LOG2 = -160.0
VMEM_CAP_BYTES = 56 * 1024 * 1024


def _params(sem, vmem_bytes):
    return pltpu.CompilerParams(
        dimension_semantics=sem, vmem_limit_bytes=min(int(vmem_bytes), VMEM_CAP_BYTES))


def _layer_norm(h, g, b):
    mu = jnp.mean(h, axis=-1, keepdims=True)
    c = h - mu
    var = jnp.mean(c * c, axis=-1, keepdims=True)
    return c * lax.rsqrt(var + LN_EPS) * g + b


def _sigmoid(z):
    return 1.0 / (1.0 + jnp.exp(-z))


def _row_subs(rows):
    return [slice(k * (rows // ROW_SUBS), (k + 1) * (rows // ROW_SUBS)) for k in range(ROW_SUBS)]


def _skewed(subs, first, second):
    out, pending = [], None
    for r in subs:
        mid = first(r)
        if pending is not None:
            out.append(second(*pending))
        pending = (r, mid)
    out.append(second(*pending))
    return out


def _in_hbm(*arrays):
    if not all(isinstance(a, jax.core.Tracer) for a in arrays):
        return list(arrays)
    return [pltpu.with_memory_space_constraint(a, pltpu.HBM) for a in arrays]


def _resident(shape):
    return pl.BlockSpec(shape, lambda *_: (0,) * len(shape), pipeline_mode=pl.Buffered(1))


def _cast_once(w_ref, wb_ref):
    @pl.when(pl.program_id(0) == 0)
    def _():
        wb_ref[...] = w_ref[...].astype(BF16)


def _proj_kernel(x_ref, w_ref, u_ref, q_ref, k_ref, v_ref, g_ref, wb_ref, *, width):
    q_scale = math.log2(math.e) / math.sqrt(HEAD_DIM)
    _cast_once(w_ref, wb_ref)
    xb = x_ref[...].astype(BF16)

    def mm(lo):
        return jnp.dot(xb, wb_ref[:, lo:lo + width], preferred_element_type=F32)

    u_ref[...] = mm(0)
    q_ref[...] = (mm(width) * q_scale).astype(BF16)
    k_ref[...] = mm(2 * width).astype(BF16)
    v_ref[...] = mm(3 * width).astype(BF16)
    for c in range(g_ref.shape[1] // width):
        g_ref[:, c * width:(c + 1) * width] = _sigmoid(mm((4 + c) * width)).astype(BF16)


def _proj(x2, w_in, tm):
    t, d = x2.shape
    n = w_in.shape[1]
    width = d // 2
    gate_w = n - 4 * width
    row = lambda i: (i, 0)
    blocks = (2 * (tm * d * 4 + tm * width * (4 + 3 * 2) + tm * gate_w * 2)
              + d * n * (4 + 2))
    temps = tm * d * 2 + 2 * tm * width * 4
    return pl.pallas_call(
        functools.partial(_proj_kernel, width=width),
        grid=(t // tm,),
        in_specs=[pl.BlockSpec((tm, d), row), _resident((d, n))],
        out_specs=[pl.BlockSpec((tm, width), row)] * 4 + [pl.BlockSpec((tm, gate_w), row)],
        out_shape=[jax.ShapeDtypeStruct((t, width), F32)]
        + [jax.ShapeDtypeStruct((t, width), BF16)] * 3
        + [jax.ShapeDtypeStruct((t, gate_w), BF16)],
        scratch_shapes=[pltpu.VMEM((d, n), BF16)],
        compiler_params=_params(("arbitrary",), blocks + temps),
        name="proj",
    )(x2, w_in)


def _pool_kernel(u_ref, wp_ref, sc_ref, o_ref):
    seq = u_ref.shape[0]
    gd = wp_ref.shape[1]
    row = lax.broadcasted_iota(jnp.int32, (seq, gd), 0)
    for g, w in enumerate(POOL_WINDOWS):
        cols = slice(g * gd, (g + 1) * gd)
        ug = u_ref[:, cols]
        s = ug
        sh = 1
        while sh < w:
            s = s + jnp.where(row >= sh, pltpu.roll(s, sh, axis=0), 0.0)
            sh *= 2
        cnt = jnp.minimum(row + 1, w).astype(F32)
        pooled = s / cnt - ug
        mixed = jnp.dot(pooled.astype(BF16), wp_ref[g].astype(BF16), preferred_element_type=F32)
        o_ref[:, cols] = (mixed * sc_ref[:, cols]).astype(BF16)


def _pool(u, w_pool, pool_scale, seq):
    t, width = u.shape
    g, gd, _ = w_pool.shape
    blocks = 2 * (seq * width * (4 + 2) + g * gd * gd * 4 + width * 4)
    temps = 6 * seq * gd * 4
    return pl.pallas_call(
        _pool_kernel,
        grid=(t // seq,),
        in_specs=[pl.BlockSpec((seq, width), lambda b: (b, 0)),
                  pl.BlockSpec((g, gd, gd), lambda b: (0, 0, 0)),
                  pl.BlockSpec((1, width), lambda b: (0, 0))],
        out_specs=pl.BlockSpec((seq, width), lambda b: (b, 0)),
        out_shape=jax.ShapeDtypeStruct((t, width), BF16),
        compiler_params=_params(("parallel",), blocks + temps),
        name="pool",
    )(u, w_pool, pool_scale)


def _attn_kernel(q_ref, k_ref, v_ref, o_ref, *, tb):
    qi = pl.program_id(1)
    pairs = q_ref.shape[1] // LANES
    first_head = lax.broadcasted_iota(jnp.int32, (tb, LANES), 1) < HEAD_DIM
    r = lax.broadcasted_iota(jnp.int32, (2 * tb, 2 * tb), 0)
    c = lax.broadcasted_iota(jnp.int32, (2 * tb, 2 * tb), 1)
    same_head = (r >= tb) == (c >= tb)
    cum = -jnp.concatenate([(same_head & (r > c)).astype(BF16), same_head.astype(BF16)], axis=1)
    kcol = lax.broadcasted_iota(jnp.int32, (tb, 2 * tb), 1)
    kcol = jnp.where(kcol >= tb, kcol - tb, kcol)
    causal = kcol < lax.broadcasted_iota(jnp.int32, (tb, 2 * tb), 0)

    def stack_heads(blk):
        zero = jnp.zeros_like(blk)
        return jnp.concatenate(
            [jnp.where(first_head, blk, zero), jnp.where(first_head, zero, blk)], axis=0)

    def step(j, rems, accs, diagonal):
        ks = pl.multiple_of(j * tb, tb)
        col_blocks = [slice(p * LANES, (p + 1) * LANES) for p in range(pairs)]
        zs = [lax.dot_general(q_ref[:, cols], stack_heads(k_ref[pl.ds(ks, tb), cols]),
                              (((1,), (1,)), ((), ())), preferred_element_type=F32)
              for cols in col_blocks]
        log_betas, sums = [], []
        for z in zs:
            softplus = jnp.maximum(z, 0.0) + jnp.log2(1.0 + jnp.exp2(-jnp.abs(z)))
            log_betas.append(z - softplus)
            if diagonal:
                softplus = jnp.where(causal, softplus, 0.0)
            sums.append(jnp.dot(softplus.astype(BF16), cum, preferred_element_type=F32))
        new_rems, new_accs = [], []
        for p, cols in enumerate(col_blocks):
            later = sums[p][:, :2 * tb]
            if rems is not None:
                later = later + rems[p]
            a = jnp.exp2(log_betas[p] + later)
            if diagonal:
                a = jnp.where(causal, a, 0.0)
            out = jnp.dot(a.astype(BF16), stack_heads(v_ref[pl.ds(ks, tb), cols]),
                          preferred_element_type=F32)
            total = sums[p][:, 2 * tb:]
            new_accs.append(out if accs is None else accs[p] + out)
            new_rems.append(total if rems is None else rems[p] + total)
        return tuple(new_rems), tuple(new_accs)

    def rem_max(rems):
        return jnp.max(functools.reduce(jnp.maximum, rems))

    rems, accs = step(qi, None, None, diagonal=True)

    def cond(carry):
        j, _, _, worst = carry
        return (j >= 0) & (worst > ATTN_DEAD_LOG2)

    def body(carry):
        j, rems, accs, _ = carry
        rems, accs = step(j, rems, accs, diagonal=False)
        return j - 1, rems, accs, rem_max(rems)

    accs = lax.while_loop(cond, body, (qi - 1, rems, accs, rem_max(rems)))[2]
    for p in range(pairs):
        o_ref[:, p * LANES:(p + 1) * LANES] = accs[p].astype(BF16)


def _attention(q, k, v, seq, tb):
    t, width = q.shape
    blocks = 2 * (2 * tb * width * 2 + 2 * seq * width * 2)
    temps = (width // LANES) * 10 * tb * 2 * tb * 4 + 2 * tb * 4 * tb * 2
    qspec = pl.BlockSpec((tb, width), lambda b, i: (b * (seq // tb) + i, 0))
    kvspec = pl.BlockSpec((seq, width), lambda b, i: (b, 0))
    return pl.pallas_call(
        functools.partial(_attn_kernel, tb=tb),
        grid=(t // seq, seq // tb),
        in_specs=[qspec, kvspec, kvspec],
        out_specs=qspec,
        out_shape=jax.ShapeDtypeStruct((t, width), BF16),
        compiler_params=_params(("parallel", "parallel"), blocks + temps),
        name="attn",
    )(q, k, v)


def _route(logits):
    lane = lax.broadcasted_iota(jnp.int32, logits.shape, 1)

    def first_max(vals):
        m = jnp.max(vals, axis=1, keepdims=True)
        idx = jnp.min(jnp.where(vals == m, lane, LANES), axis=1, keepdims=True)
        return m, idx

    is_group = (lane >= GROUP_LANE0) & (lane < GROUP_LANE0 + N_GROUPS)
    gm, g_lane = first_max(jnp.where(is_group, logits, NEG_BIG))
    g_prob = 1.0 / jnp.sum(jnp.where(is_group, jnp.exp(logits - gm), 0.0), axis=1, keepdims=True)
    lo = EXPERTS_PER_GROUP * (g_lane - GROUP_LANE0)
    in_group = jnp.where((lane >= lo) & (lane < lo + EXPERTS_PER_GROUP), logits, NEG_BIG)
    m1, i1 = first_max(in_group)
    m2, i2 = first_max(jnp.where(lane == i1, NEG_BIG, in_group))
    e21 = jnp.exp(m2 - m1)
    w1 = g_prob / (1.0 + e21)
    w2 = w1 * e21
    comb = jnp.where(lane == i1, w1, 0.0) + jnp.where(lane == i2, w2, 0.0)
    sel = ((lane == i1) | (lane == i2)).astype(F32)
    return comb, sel


def _merge_kernel(x_ref, po_ref, at_ref, g_ref, wpu_ref, wau_ref, wo_ref, lg_ref, lb_ref,
                  wr_ref, br_ref, x1_ref, x1b_ref, comb_ref, sel_ref, cnt_ref,
                  wpub_ref, waub_ref, wob_ref, wrb_ref, *, alpha):
    tm, d = x_ref.shape
    _cast_once(wpu_ref, wpub_ref)
    _cast_once(wau_ref, waub_ref)
    _cast_once(wo_ref, wob_ref)

    @pl.when(pl.program_id(0) == 0)
    def _():
        w_r = wr_ref[...]
        hi = w_r.astype(BF16)
        wrb_ref[:, :LANES] = hi
        wrb_ref[:, LANES:] = (w_r - hi.astype(F32)).astype(BF16)

    def mix(r):
        a = jnp.dot(po_ref[r, :], wpub_ref[...], preferred_element_type=F32)
        b = jnp.dot(at_ref[r, :], waub_ref[...], preferred_element_type=F32)
        merged = (g_ref[r, :d].astype(F32) * a + g_ref[r, d:].astype(F32) * b).astype(BF16)
        return alpha * x_ref[r, :] + jnp.dot(merged, wob_ref[...], preferred_element_type=F32)

    def norm_and_route(r, h):
        x1 = _layer_norm(h, lg_ref[...], lb_ref[...])
        xh = x1.astype(BF16)
        x1_ref[r, :] = x1
        x1b_ref[r, :] = xh
        xl = (x1 - xh.astype(F32)).astype(BF16)
        by_hi = jnp.dot(xh, wrb_ref[...], preferred_element_type=F32)
        by_lo = jnp.dot(xl, wrb_ref[:, :LANES], preferred_element_type=F32)
        comb, sel = _route(by_hi[:, :LANES] + by_hi[:, LANES:] + by_lo + br_ref[...])
        comb_ref[r, :] = comb
        sel_ref[r, :] = sel.astype(BF16)
        return jnp.sum(sel, axis=0, keepdims=True)

    cnt_ref[0] = sum(_skewed(_row_subs(tm), mix, norm_and_route))


def _merge(x2, pool_out, attn_out, gates, w_pu, w_au, w_o, ln_g, ln_b, w_r, b_r, alpha, tm):
    t, d = x2.shape
    width = pool_out.shape[1]
    row = lambda i: (i, 0)
    blocks = (2 * (tm * d * 4 + 2 * tm * width * 2 + tm * 2 * d * 2
                   + tm * d * 4 + tm * d * 2 + tm * LANES * (4 + 2) + LANES * 4)
              + (2 * width * d + d * d + d * LANES) * (4 + 2) + 2 * d * 4 + LANES * 4)
    temps = 5 * tm * d * 4
    return pl.pallas_call(
        functools.partial(_merge_kernel, alpha=alpha),
        grid=(t // tm,),
        in_specs=[pl.BlockSpec((tm, d), row), pl.BlockSpec((tm, width), row),
                  pl.BlockSpec((tm, width), row), pl.BlockSpec((tm, 2 * d), row),
                  _resident((width, d)), _resident((width, d)), _resident((d, d)),
                  _resident((1, d)), _resident((1, d)), _resident((d, LANES)),
                  _resident((1, LANES))],
        out_specs=[pl.BlockSpec((tm, d), row), pl.BlockSpec((tm, d), row),
                   pl.BlockSpec((tm, LANES), row), pl.BlockSpec((tm, LANES), row),
                   pl.BlockSpec((1, 1, LANES), lambda i: (i, 0, 0))],
        out_shape=[jax.ShapeDtypeStruct((t, d), F32), jax.ShapeDtypeStruct((t, d), BF16),
                   jax.ShapeDtypeStruct((t, LANES), F32), jax.ShapeDtypeStruct((t, LANES), BF16),
                   jax.ShapeDtypeStruct((t // tm, 1, LANES), F32)],
        scratch_shapes=[pltpu.VMEM((width, d), BF16), pltpu.VMEM((width, d), BF16),
                        pltpu.VMEM((d, d), BF16), pltpu.VMEM((d, 2 * LANES), BF16)],
        compiler_params=_params(("arbitrary",), blocks + temps),
        name="merge",
    )(x2, pool_out, attn_out, gates, w_pu, w_au, w_o, ln_g, ln_b, w_r, b_r)


def _routing_plan(counts):
    cnt = counts[:, 0, :N_EXPERTS].astype(jnp.int32)
    nch = (cnt + (CHUNK - 1)) // CHUNK
    local_end = jnp.cumsum(nch, axis=1)
    local_start = local_end - nch
    block_chunks = local_end[:, -1]
    before_block = jnp.cumsum(nch, axis=0) - nch
    expert_chunks = jnp.sum(nch, axis=0)
    chunks_per_tile = EXPERT_TILE // CHUNK
    expert_tiles = (expert_chunks + (chunks_per_tile - 1)) // chunks_per_tile
    tiles_end = jnp.cumsum(expert_tiles)
    region_start = (tiles_end - expert_tiles) * chunks_per_tile
    segment_dst = region_start[None, :] + before_block
    c = jnp.arange(LOCAL_CHUNKS, dtype=jnp.int32)
    owned = (c[None, :, None] >= local_start[:, None, :]) & (c[None, :, None] < local_end[:, None, :])
    chunk_dst = c[None, :] + jnp.sum(
        jnp.where(owned, (segment_dst - local_start)[:, None, :], 0), axis=2)
    i32 = lambda a: a.astype(jnp.int32)
    return dict(chunk_dst=i32(chunk_dst), block_chunks=i32(block_chunks),
                expert_tiles=i32(expert_tiles), tiles_used=i32(tiles_end[-1:]),
                pad_start=i32(region_start + expert_chunks),
                pad_count=i32(expert_tiles * chunks_per_tile - expert_chunks))


def _local_positions(sel):
    tb = sel.shape[0]
    earlier = (lax.broadcasted_iota(jnp.int32, (tb, tb), 1)
               < lax.broadcasted_iota(jnp.int32, (tb, tb), 0)).astype(BF16)
    rank = jnp.dot(earlier, sel, preferred_element_type=F32)
    cnt = jnp.sum(sel.astype(F32), axis=0, keepdims=True)
    nch = jnp.floor((cnt + (CHUNK - 1)) * (1.0 / CHUNK))
    lower = (lax.broadcasted_iota(jnp.int32, (LANES, LANES), 0)
             < lax.broadcasted_iota(jnp.int32, (LANES, LANES), 1)).astype(BF16)
    start = CHUNK * jnp.dot(jnp.broadcast_to(nch, (8, LANES)).astype(BF16), lower,
                            preferred_element_type=F32)[0:1]
    pos = rank + start
    chosen = sel > 0
    pos_lo = jnp.min(jnp.where(chosen, pos, float(LOCAL_ROWS)), axis=1, keepdims=True)
    pos_hi = jnp.max(jnp.where(chosen, pos, -1.0), axis=1, keepdims=True)
    return pos, pos_lo, pos_hi


def _for_each(n, fn):
    lax.fori_loop(0, n, lambda c, carry: (fn(c), carry)[1], 0)


def _scatter_kernel(dst_ref, nchunk_ref, pad_start_ref, pad_count_ref, used_ref, x_ref, sel_ref, g_ref,
                    loc_ref, zero_ref, sems):
    b = pl.program_id(0)
    last = pl.num_programs(0) - 1
    slot = b % 2
    tb = x_ref.shape[0]

    def chunk_copy(blk, c, slot):
        src = pl.multiple_of(c * CHUNK, CHUNK)
        dst = pl.multiple_of(dst_ref[blk, c] * CHUNK, CHUNK)
        return pltpu.make_async_copy(loc_ref.at[slot, pl.ds(src, CHUNK)],
                                     g_ref.at[pl.ds(dst, CHUNK)], sems.at[slot])

    def pad_copy(e, k):
        dst = pl.multiple_of((pad_start_ref[e] + k) * CHUNK, CHUNK)
        return pltpu.make_async_copy(zero_ref.at[pl.ds(0, CHUNK)], g_ref.at[pl.ds(dst, CHUNK)],
                                     sems.at[2])

    def unused_tile_copy(i):
        dst = pl.multiple_of(i * EXPERT_TILE, EXPERT_TILE)
        return pltpu.make_async_copy(zero_ref, g_ref.at[pl.ds(dst, EXPERT_TILE)], sems.at[2])

    @pl.when(b >= 2)
    def _():
        _for_each(nchunk_ref[b - 2], lambda c: chunk_copy(b - 2, c, slot).wait())

    _, pos_lo, pos_hi = _local_positions(sel_ref[...])
    lo = pos_lo.astype(jnp.int32)
    hi = pos_hi.astype(jnp.int32)
    x = x_ref[...]

    def sort_rows(m, carry):
        r0 = pl.multiple_of(m * SORT_ROWS, SORT_ROWS)
        r = r0 + lax.broadcasted_iota(jnp.int32, (tb, SORT_ROWS), 1)
        perm = ((r == lo) | (r == hi)).astype(BF16)
        rows = lax.dot_general(perm, x, (((0,), (0,)), ((), ())), preferred_element_type=F32)
        loc_ref[slot, pl.ds(r0, SORT_ROWS), :] = rows.astype(BF16)
        return carry

    lax.fori_loop(0, (nchunk_ref[b] * CHUNK + (SORT_ROWS - 1)) // SORT_ROWS, sort_rows, 0)
    _for_each(nchunk_ref[b], lambda c: chunk_copy(b, c, slot).start())

    @pl.when(b == last)
    def _():
        zero_ref[...] = jnp.zeros_like(zero_ref)
        n_unused = g_ref.shape[0] // EXPERT_TILE - used_ref[0]
        _for_each(N_EXPERTS, lambda e: _for_each(pad_count_ref[e], lambda k: pad_copy(e, k).start()))
        _for_each(n_unused, lambda i: unused_tile_copy(used_ref[0] + i).start())
        _for_each(N_EXPERTS, lambda e: _for_each(pad_count_ref[e], lambda k: pad_copy(e, k).wait()))
        _for_each(n_unused, lambda i: unused_tile_copy(used_ref[0] + i).wait())

        @pl.when(b >= 1)
        def _():
            _for_each(nchunk_ref[b - 1], lambda c: chunk_copy(b - 1, c, 1 - slot).wait())

        _for_each(nchunk_ref[b], lambda c: chunk_copy(b, c, slot).wait())


def _scatter(x1b, sel, plan, n_tiles):
    t, d = x1b.shape
    tb = ROUTE_BLOCK
    row = lambda b, *_: (b, 0)
    blocks = 2 * (tb * d * 2 + tb * LANES * 2) + 2 * LOCAL_ROWS * d * 2 + EXPERT_TILE * d * 2
    temps = tb * tb * 2 + 6 * tb * LANES * 4 + tb * SORT_ROWS * 6 + SORT_ROWS * d * 6
    return pl.pallas_call(
        _scatter_kernel,
        grid_spec=pltpu.PrefetchScalarGridSpec(
            num_scalar_prefetch=5,
            grid=(t // tb,),
            in_specs=[pl.BlockSpec((tb, d), row), pl.BlockSpec((tb, LANES), row)],
            out_specs=pl.BlockSpec(memory_space=pl.ANY),
            scratch_shapes=[pltpu.VMEM((2, LOCAL_ROWS, d), BF16),
                            pltpu.VMEM((EXPERT_TILE, d), BF16), pltpu.SemaphoreType.DMA((3,))]),
        out_shape=jax.ShapeDtypeStruct((n_tiles * EXPERT_TILE, d), BF16),
        compiler_params=_params(("arbitrary",), blocks + temps),
        name="scatter",
    )(plan["chunk_dst"], plan["block_chunks"], plan["pad_start"], plan["pad_count"],
      plan["tiles_used"], x1b, sel)


def _experts_kernel(ntile_ref, used_ref, g_ref, wg_ref, wu_ref, wd_ref, y_ref,
                    x_buf, y_buf, wg_buf, wu_buf, wd_buf, wgb_ref, wub_ref, wdb_ref,
                    x_sems, y_sems, w_sems):
    n_experts = wg_ref.shape[0]
    tm = x_buf.shape[1]
    used = used_ref[0]

    def weight_copies(e, slot):
        return [pltpu.make_async_copy(src.at[e], dst.at[slot], w_sems.at[slot])
                for src, dst in ((wg_ref, wg_buf), (wu_ref, wu_buf), (wd_ref, wd_buf))]

    def x_copy(t, slot):
        rows = pl.ds(pl.multiple_of(t * tm, tm), tm)
        return pltpu.make_async_copy(g_ref.at[rows], x_buf.at[slot], x_sems.at[slot])

    def y_copy(t, slot):
        rows = pl.ds(pl.multiple_of(t * tm, tm), tm)
        return pltpu.make_async_copy(y_buf.at[slot], y_ref.at[rows], y_sems.at[slot])

    for c in weight_copies(0, 0):
        c.start()
    x_copy(0, 0).start()

    def run_expert(e, t0):
        wslot = e % 2
        for c in weight_copies(e, wslot):
            c.wait()

        @pl.when(e + 1 < n_experts)
        def _():
            for c in weight_copies(e + 1, 1 - wslot):
                c.start()

        @pl.when(ntile_ref[e] > 0)
        def _():
            wgb_ref[...] = wg_buf[wslot].astype(BF16)
            wub_ref[...] = wu_buf[wslot].astype(BF16)
            wdb_ref[...] = wd_buf[wslot].astype(BF16)

        def run_tile(_, t):
            slot = t % 2
            x_copy(t, slot).wait()

            @pl.when(t + 1 < used)
            def _():
                x_copy(t + 1, 1 - slot).start()

            @pl.when(t >= 2)
            def _():
                y_copy(t - 2, slot).wait()

            x = x_buf[slot]
            hg = jnp.dot(x, wgb_ref[...], preferred_element_type=F32)
            hu = jnp.dot(x, wub_ref[...], preferred_element_type=F32)
            h = hg * _sigmoid(hg) * hu
            y_buf[slot] = jnp.dot(h.astype(BF16), wdb_ref[...],
                                  preferred_element_type=F32).astype(BF16)
            y_copy(t, slot).start()
            return t + 1

        return lax.fori_loop(0, ntile_ref[e], run_tile, t0)

    lax.fori_loop(0, n_experts, run_expert, 0)

    @pl.when(used >= 2)
    def _():
        y_copy(used - 2, used % 2).wait()

    y_copy(used - 1, (used - 1) % 2).wait()
    y_buf[0] = jnp.zeros((tm, y_buf.shape[2]), BF16)
    n_unused = y_ref.shape[0] // tm - used
    _for_each(n_unused, lambda i: y_copy(used + i, 0).start())
    _for_each(n_unused, lambda i: y_copy(used + i, 0).wait())


def _experts(sorted_x, plan, w_eg, w_eu, w_ed):
    rows, d = sorted_x.shape
    _, _, de = w_eg.shape
    tm = EXPERT_TILE
    any_space = pl.BlockSpec(memory_space=pl.ANY)
    scratch = 2 * 2 * tm * d * 2 + 2 * 3 * d * de * 4 + 3 * d * de * 2
    temps = 3 * tm * de * 4 + tm * d * 4 + d * de * 4
    return pl.pallas_call(
        _experts_kernel,
        grid_spec=pltpu.PrefetchScalarGridSpec(
            num_scalar_prefetch=2,
            grid=(1,),
            in_specs=[any_space] * 4,
            out_specs=any_space,
            scratch_shapes=[pltpu.VMEM((2, tm, d), BF16), pltpu.VMEM((2, tm, d), BF16),
                            pltpu.VMEM((2, d, de), F32), pltpu.VMEM((2, d, de), F32),
                            pltpu.VMEM((2, de, d), F32),
                            pltpu.VMEM((d, de), BF16), pltpu.VMEM((d, de), BF16),
                            pltpu.VMEM((de, d), BF16),
                            pltpu.SemaphoreType.DMA((2,)), pltpu.SemaphoreType.DMA((2,)),
                            pltpu.SemaphoreType.DMA((2,))]),
        out_shape=jax.ShapeDtypeStruct((rows, d), BF16),
        compiler_params=_params(("arbitrary",), scratch + temps),
        name="experts",
    )(plan["expert_tiles"], plan["tiles_used"], sorted_x, w_eg, w_eu, w_ed)


def _combine_kernel(dst_ref, nchunk_ref, x1_ref, comb_ref, sel_ref, p_ref, wpg_ref, wpp_ref,
                    lg_ref, lb_ref, y_ref, o_ref, loc_ref, wpgb_ref, wppb_ref, sems, *, alpha):
    b = pl.program_id(0)
    nb = pl.num_programs(0)
    slot = b % 2
    tb, d = x1_ref.shape
    _cast_once(wpg_ref, wpgb_ref)
    _cast_once(wpp_ref, wppb_ref)

    def chunk_copy(blk, c, slot):
        src = pl.multiple_of(dst_ref[blk, c] * CHUNK, CHUNK)
        dst = pl.multiple_of(c * CHUNK, CHUNK)
        return pltpu.make_async_copy(y_ref.at[pl.ds(src, CHUNK)],
                                     loc_ref.at[slot, pl.ds(dst, CHUNK)], sems.at[slot])

    @pl.when(b == 0)
    def _():
        loc_ref[...] = jnp.zeros_like(loc_ref)
        _for_each(nchunk_ref[0], lambda c: chunk_copy(0, c, 0).start())

    @pl.when(b + 1 < nb)
    def _():
        _for_each(nchunk_ref[b + 1], lambda c: chunk_copy(b + 1, c, 1 - slot).start())

    _for_each(nchunk_ref[b], lambda c: chunk_copy(b, c, slot).wait())

    sel = sel_ref[...]
    pos, pos_lo, pos_hi = _local_positions(sel)
    chosen = sel > 0
    comb = comb_ref[...]
    w_lo = jnp.sum(jnp.where(chosen & (pos == pos_lo), comb, 0.0), axis=1, keepdims=True)
    w_hi = jnp.sum(jnp.where(chosen & (pos == pos_hi), comb, 0.0), axis=1, keepdims=True)
    lo = pos_lo.astype(jnp.int32)
    hi = pos_hi.astype(jnp.int32)

    sorted_row = lax.broadcasted_iota(jnp.int32, (tb // ROW_SUBS, LOCAL_ROWS), 1)

    def branches(r):
        weights = (jnp.where(sorted_row == lo[r], w_lo[r], 0.0)
                   + jnp.where(sorted_row == hi[r], w_hi[r], 0.0)).astype(BF16)
        moe = jnp.dot(weights, loc_ref[slot], preferred_element_type=F32)
        gate = jnp.dot(x1_ref[r, :].astype(BF16), wpgb_ref[...], preferred_element_type=F32)
        emb = jnp.dot(p_ref[r, :].astype(BF16), wppb_ref[...], preferred_element_type=F32)
        return moe, gate, emb

    def finish(r, parts):
        moe, gate, emb = parts
        h = alpha * x1_ref[r, :] + moe + _sigmoid(gate) * emb
        o_ref[r, :] = _layer_norm(h, lg_ref[...], lb_ref[...])

    _skewed(_row_subs(tb), branches, finish)


def _combine(y, plan, x1, comb, sel, p2, w_pg, w_pp, ln_g, ln_b, alpha):
    t, d = x1.shape
    pd = p2.shape[1]
    tb = ROUTE_BLOCK
    row = lambda b, *_: (b, 0)
    blocks = (2 * (2 * tb * d * 4 + tb * LANES * (4 + 2) + tb * pd * 4)
              + (d * d + pd * d) * (4 + 2) + 2 * d * 4 + 2 * LOCAL_ROWS * d * 2)
    temps = tb * tb * 2 + 8 * tb * LANES * 4 + tb * LOCAL_ROWS * 10 + 5 * tb * d * 4
    return pl.pallas_call(
        functools.partial(_combine_kernel, alpha=alpha),
        grid_spec=pltpu.PrefetchScalarGridSpec(
            num_scalar_prefetch=2,
            grid=(t // tb,),
            in_specs=[pl.BlockSpec((tb, d), row), pl.BlockSpec((tb, LANES), row),
                      pl.BlockSpec((tb, LANES), row), pl.BlockSpec((tb, pd), row),
                      _resident((d, d)), _resident((pd, d)), _resident((1, d)), _resident((1, d)),
                      pl.BlockSpec(memory_space=pl.ANY)],
            out_specs=pl.BlockSpec((tb, d), row),
            scratch_shapes=[pltpu.VMEM((2, LOCAL_ROWS, d), BF16), pltpu.VMEM((d, d), BF16),
                            pltpu.VMEM((pd, d), BF16), pltpu.SemaphoreType.DMA((2,))]),
        out_shape=jax.ShapeDtypeStruct((t, d), F32),
        compiler_params=_params(("arbitrary",), blocks + temps),
        name="combine",
    )(plan["chunk_dst"], plan["block_chunks"], x1, comb, sel, p2, w_pg, w_pp, ln_g, ln_b, y)


def kernel(x, p, w_in, w_pool, pool_scale, w_pu, w_au, w_o, ln1_g, ln1_b, w_rg, b_rg, w_re, b_re,
           w_eg, w_eu, w_ed, w_pg, w_pp, ln2_g, ln2_b):
    bsz, seq, d = x.shape
    depth = w_in.shape[0]
    t = bsz * seq
    de = w_eg.shape[-1]
    alpha = (2.0 * depth) ** 0.25
    assert w_rg.shape[2] == N_GROUPS and w_re.shape[1:] == (N_GROUPS, d, EXPERTS_PER_GROUP)
    assert w_in.shape[2] == 4 * d and w_pool.shape[1] == len(POOL_WINDOWS)
    assert t % ROUTE_BLOCK == 0 and N_EXPERTS + N_GROUPS <= LANES
    n_blocks = t // ROUTE_BLOCK
    n_tiles = -(-(2 * t + n_blocks * N_EXPERTS * (CHUNK - 1) + N_EXPERTS * (EXPERT_TILE - CHUNK))
                // EXPERT_TILE)

    x2 = x.reshape(t, d)
    for i in range(depth):
        u, q, k, v, gates = _in_hbm(*_proj(x2, w_in[i], tm=512))
        pool_out, = _in_hbm(_pool(u, w_pool[i], pool_scale[i][None, :], seq))
        attn_out, = _in_hbm(_attention(q, k, v, seq, tb=128))

        w_r = jnp.concatenate(
            [w_re[i].transpose(1, 0, 2).reshape(d, N_EXPERTS), w_rg[i]], axis=1)
        w_r = jnp.pad(w_r, ((0, 0), (0, LANES - w_r.shape[1])))
        b_r = jnp.pad(jnp.concatenate([b_re[i].reshape(-1), b_rg[i]]),
                      (0, LANES - N_GROUPS - N_EXPERTS))[None, :]

        x1, x1b, comb, sel, counts = _merge(
            x2, pool_out, attn_out, gates, *_in_hbm(w_pu[i], w_au[i], w_o[i]), ln1_g[i][None, :],
            ln1_b[i][None, :], w_r, b_r, alpha, tm=ROUTE_BLOCK)
        x1, x1b, comb, sel = _in_hbm(x1, x1b, comb, sel)

        plan = _routing_plan(counts)
        sorted_x, = _in_hbm(_scatter(x1b, sel, plan, n_tiles))
        y, = _in_hbm(_experts(sorted_x, plan, w_eg[i].reshape(N_EXPERTS, d, de),
                              w_eu[i].reshape(N_EXPERTS, d, de), w_ed[i].reshape(N_EXPERTS, de, d)))
        x2 = _combine(y, plan, x1, comb, sel, p[i].reshape(t, -1), *_in_hbm(w_pg[i], w_pp[i]),
                      ln2_g[i][None, :], ln2_b[i][None, :], alpha)
    return x2.reshape(bsz, seq, d)
```

```python
import functools
import math

import jax
import jax.numpy as jnp
from jax import lax
from jax.experimental import pallas as pl
from jax.experimental.pallas import tpu as pltpu

F32 = jnp.float32
BF16 = jnp.bfloat16

LANES = 128
POOL_WINDOWS = (2, 4, 8, 16)
HEAD_DIM = 64
N_GROUPS = 4
EXPERTS_PER_GROUP = 8
N_EXPERTS = N_GROUPS * EXPERTS_PER_GROUP
LN_EPS = 1e-5
GROUP_LANE0 = N_EXPERTS
NEG_BIG = -1e30
ROUTE_BLOCK = 512
CHUNK = 16
EXPERT_TILE = 256
MACRO_TILES = 4
SORT_ROWS = 256
ROW_SUBS = 2
LOCAL_ROWS = -(-(2 * ROUTE_BLOCK + N_EXPERTS * (CHUNK - 1)) // SORT_ROWS) * SORT_ROWS
LOCAL_CHUNKS = LOCAL_ROWS // CHUNK
ATTN_DEAD_LOG2 = -160.0
VMEM_CAP_BYTES = 56 * 1024 * 1024


def _params(sem, vmem_bytes):
    return pltpu.CompilerParams(
        dimension_semantics=sem, vmem_limit_bytes=min(int(vmem_bytes), VMEM_CAP_BYTES))


def _layer_norm(h, g, b):
    mu = jnp.mean(h, axis=-1, keepdims=True)
    c = h - mu
    var = jnp.mean(c * c, axis=-1, keepdims=True)
    return c * lax.rsqrt(var + LN_EPS) * g + b


def _sigmoid(z):
    return 1.0 / (1.0 + jnp.exp(-z))


def _row_subs(rows):
    return [slice(k * (rows // ROW_SUBS), (k + 1) * (rows // ROW_SUBS)) for k in range(ROW_SUBS)]


def _skewed(subs, first, second):
    out, pending = [], None
    for r in subs:
        mid = first(r)
        if pending is not None:
            out.append(second(*pending))
        pending = (r, mid)
    out.append(second(*pending))
    return out


def _in_hbm(*arrays):
    if not all(isinstance(a, jax.core.Tracer) for a in arrays):
        return list(arrays)
    return [pltpu.with_memory_space_constraint(a, pltpu.HBM) for a in arrays]


def _resident(shape):
    return pl.BlockSpec(shape, lambda *_: (0,) * len(shape), pipeline_mode=pl.Buffered(1))


def _cast_once(w_ref, wb_ref):
    @pl.when(pl.program_id(0) == 0)
    def _():
        wb_ref[...] = w_ref[...].astype(BF16)


def _proj_kernel(x_ref, w_ref, u_ref, q_ref, k_ref, v_ref, g_ref, wb_ref, *, width):
    q_scale = math.log2(math.e) / math.sqrt(HEAD_DIM)
    _cast_once(w_ref, wb_ref)
    xb = x_ref[...].astype(BF16)

    def mm(lo):
        return jnp.dot(xb, wb_ref[:, lo:lo + width], preferred_element_type=F32)

    u_ref[...] = mm(0)
    q_ref[...] = (mm(width) * q_scale).astype(BF16)
    k_ref[...] = mm(2 * width).astype(BF16)
    v_ref[...] = mm(3 * width).astype(BF16)
    for c in range(g_ref.shape[1] // width):
        g_ref[:, c * width:(c + 1) * width] = _sigmoid(mm((4 + c) * width)).astype(BF16)


def _proj(x2, w_in, tm):
    t, d = x2.shape
    n = w_in.shape[1]
    width = d // 2
    gate_w = n - 4 * width
    row = lambda i: (i, 0)
    blocks = (2 * (tm * d * 4 + tm * width * (4 + 3 * 2) + tm * gate_w * 2)
              + d * n * (4 + 2))
    temps = tm * d * 2 + 2 * tm * width * 4
    return pl.pallas_call(
        functools.partial(_proj_kernel, width=width),
        grid=(t // tm,),
        in_specs=[pl.BlockSpec((tm, d), row), _resident((d, n))],
        out_specs=[pl.BlockSpec((tm, width), row)] * 4 + [pl.BlockSpec((tm, gate_w), row)],
        out_shape=[jax.ShapeDtypeStruct((t, width), F32)]
        + [jax.ShapeDtypeStruct((t, width), BF16)] * 3
        + [jax.ShapeDtypeStruct((t, gate_w), BF16)],
        scratch_shapes=[pltpu.VMEM((d, n), BF16)],
        compiler_params=_params(("arbitrary",), blocks + temps),
        name="proj",
    )(x2, w_in)


def _pool_kernel(u_ref, wp_ref, sc_ref, o_ref):
    seq = u_ref.shape[0]
    gd = wp_ref.shape[1]
    row = lax.broadcasted_iota(jnp.int32, (seq, gd), 0)
    for g, w in enumerate(POOL_WINDOWS):
        cols = slice(g * gd, (g + 1) * gd)
        ug = u_ref[:, cols]
        s = ug
        sh = 1
        while sh < w:
            s = s + jnp.where(row >= sh, pltpu.roll(s, sh, axis=0), 0.0)
            sh *= 2
        cnt = jnp.minimum(row + 1, w).astype(F32)
        pooled = s / cnt - ug
        mixed = jnp.dot(pooled.astype(BF16), wp_ref[g].astype(BF16), preferred_element_type=F32)
        o_ref[:, cols] = (mixed * sc_ref[:, cols]).astype(BF16)


def _pool(u, w_pool, pool_scale, seq):
    t, width = u.shape
    g, gd, _ = w_pool.shape
    blocks = 2 * (seq * width * (4 + 2) + g * gd * gd * 4 + width * 4)
    temps = 6 * seq * gd * 4
    return pl.pallas_call(
        _pool_kernel,
        grid=(t // seq,),
        in_specs=[pl.BlockSpec((seq, width), lambda b: (b, 0)),
                  pl.BlockSpec((g, gd, gd), lambda b: (0, 0, 0)),
                  pl.BlockSpec((1, width), lambda b: (0, 0))],
        out_specs=pl.BlockSpec((seq, width), lambda b: (b, 0)),
        out_shape=jax.ShapeDtypeStruct((t, width), BF16),
        compiler_params=_params(("parallel",), blocks + temps),
        name="pool",
    )(u, w_pool, pool_scale)


def _attn_kernel(q_ref, k_ref, v_ref, o_ref, *, tb):
    qi = pl.program_id(1)
    pairs = q_ref.shape[1] // LANES
    first_head = lax.broadcasted_iota(jnp.int32, (tb, LANES), 1) < HEAD_DIM
    r = lax.broadcasted_iota(jnp.int32, (2 * tb, 2 * tb), 0)
    c = lax.broadcasted_iota(jnp.int32, (2 * tb, 2 * tb), 1)
    same_head = (r >= tb) == (c >= tb)
    cum = -jnp.concatenate([(same_head & (r > c)).astype(BF16), same_head.astype(BF16)], axis=1)
    kcol = lax.broadcasted_iota(jnp.int32, (tb, 2 * tb), 1)
    kcol = jnp.where(kcol >= tb, kcol - tb, kcol)
    causal = kcol < lax.broadcasted_iota(jnp.int32, (tb, 2 * tb), 0)

    def stack_heads(blk):
        zero = jnp.zeros_like(blk)
        return jnp.concatenate(
            [jnp.where(first_head, blk, zero), jnp.where(first_head, zero, blk)], axis=0)

    def step(j, rems, accs, diagonal):
        ks = pl.multiple_of(j * tb, tb)
        col_blocks = [slice(p * LANES, (p + 1) * LANES) for p in range(pairs)]
        zs = [lax.dot_general(q_ref[:, cols], stack_heads(k_ref[pl.ds(ks, tb), cols]),
                              (((1,), (1,)), ((), ())), preferred_element_type=F32)
              for cols in col_blocks]
        log_betas, sums = [], []
        for z in zs:
            softplus = jnp.maximum(z, 0.0) + jnp.log2(1.0 + jnp.exp2(-jnp.abs(z)))
            log_betas.append(z - softplus)
            if diagonal:
                softplus = jnp.where(causal, softplus, 0.0)
            sums.append(jnp.dot(softplus.astype(BF16), cum, preferred_element_type=F32))
        new_rems, new_accs = [], []
        for p, cols in enumerate(col_blocks):
            later = sums[p][:, :2 * tb]
            if rems is not None:
                later = later + rems[p]
            a = jnp.exp2(log_betas[p] + later)
            if diagonal:
                a = jnp.where(causal, a, 0.0)
            out = jnp.dot(a.astype(BF16), stack_heads(v_ref[pl.ds(ks, tb), cols]),
                          preferred_element_type=F32)
            total = sums[p][:, 2 * tb:]
            new_accs.append(out if accs is None else accs[p] + out)
            new_rems.append(total if rems is None else rems[p] + total)
        return tuple(new_rems), tuple(new_accs)

    def rem_max(rems):
        return jnp.max(functools.reduce(jnp.maximum, rems))

    rems, accs = step(qi, None, None, diagonal=True)

    def cond(carry):
        j, _, _, worst = carry
        return (j >= 0) & (worst > ATTN_DEAD_LOG2)

    def body(carry):
        j, rems, accs, _ = carry
        rems, accs = step(j, rems, accs, diagonal=False)
        return j - 1, rems, accs, rem_max(rems)

    accs = lax.while_loop(cond, body, (qi - 1, rems, accs, rem_max(rems)))[2]
    for p in range(pairs):
        o_ref[:, p * LANES:(p + 1) * LANES] = accs[p].astype(BF16)


def _attention(q, k, v, seq, tb):
    t, width = q.shape
    blocks = 2 * (2 * tb * width * 2 + 2 * seq * width * 2)
    temps = (width // LANES) * 10 * tb * 2 * tb * 4 + 2 * tb * 4 * tb * 2
    qspec = pl.BlockSpec((tb, width), lambda b, i: (b * (seq // tb) + i, 0))
    kvspec = pl.BlockSpec((seq, width), lambda b, i: (b, 0))
    return pl.pallas_call(
        functools.partial(_attn_kernel, tb=tb),
        grid=(t // seq, seq // tb),
        in_specs=[qspec, kvspec, kvspec],
        out_specs=qspec,
        out_shape=jax.ShapeDtypeStruct((t, width), BF16),
        compiler_params=_params(("parallel", "parallel"), blocks + temps),
        name="attn",
    )(q, k, v)


def _route(logits):
    lane = lax.broadcasted_iota(jnp.int32, logits.shape, 1)

    def first_max(vals):
        m = jnp.max(vals, axis=1, keepdims=True)
        idx = jnp.min(jnp.where(vals == m, lane, LANES), axis=1, keepdims=True)
        return m, idx

    is_group = (lane >= GROUP_LANE0) & (lane < GROUP_LANE0 + N_GROUPS)
    gm, g_lane = first_max(jnp.where(is_group, logits, NEG_BIG))
    g_prob = 1.0 / jnp.sum(jnp.where(is_group, jnp.exp(logits - gm), 0.0), axis=1, keepdims=True)
    lo = EXPERTS_PER_GROUP * (g_lane - GROUP_LANE0)
    in_group = jnp.where((lane >= lo) & (lane < lo + EXPERTS_PER_GROUP), logits, NEG_BIG)
    m1, i1 = first_max(in_group)
    m2, i2 = first_max(jnp.where(lane == i1, NEG_BIG, in_group))
    e21 = jnp.exp(m2 - m1)
    w1 = g_prob / (1.0 + e21)
    w2 = w1 * e21
    comb = jnp.where(lane == i1, w1, 0.0) + jnp.where(lane == i2, w2, 0.0)
    sel = ((lane == i1) | (lane == i2)).astype(F32)
    return comb, sel


def _merge_kernel(x_ref, po_ref, at_ref, g_ref, wpu_ref, wau_ref, wo_ref, lg_ref, lb_ref,
                  wr_ref, br_ref, x1_ref, x1b_ref, comb_ref, sel_ref, cnt_ref,
                  wpub_ref, waub_ref, wob_ref, wrb_ref, *, alpha):
    tm, d = x_ref.shape
    _cast_once(wpu_ref, wpub_ref)
    _cast_once(wau_ref, waub_ref)
    _cast_once(wo_ref, wob_ref)

    @pl.when(pl.program_id(0) == 0)
    def _():
        w_r = wr_ref[...]
        hi = w_r.astype(BF16)
        wrb_ref[:, :LANES] = hi
        wrb_ref[:, LANES:] = (w_r - hi.astype(F32)).astype(BF16)

    def mix(r):
        a = jnp.dot(po_ref[r, :], wpub_ref[...], preferred_element_type=F32)
        b = jnp.dot(at_ref[r, :], waub_ref[...], preferred_element_type=F32)
        merged = (g_ref[r, :d].astype(F32) * a + g_ref[r, d:].astype(F32) * b).astype(BF16)
        return alpha * x_ref[r, :] + jnp.dot(merged, wob_ref[...], preferred_element_type=F32)

    def norm_and_route(r, h):
        x1 = _layer_norm(h, lg_ref[...], lb_ref[...])
        xh = x1.astype(BF16)
        x1_ref[r, :] = x1
        x1b_ref[r, :] = xh
        xl = (x1 - xh.astype(F32)).astype(BF16)
        by_hi = jnp.dot(xh, wrb_ref[...], preferred_element_type=F32)
        by_lo = jnp.dot(xl, wrb_ref[:, :LANES], preferred_element_type=F32)
        comb, sel = _route(by_hi[:, :LANES] + by_hi[:, LANES:] + by_lo + br_ref[...])
        comb_ref[r, :] = comb
        sel_ref[r, :] = sel.astype(BF16)
        return jnp.sum(sel, axis=0, keepdims=True)

    cnt_ref[0] = sum(_skewed(_row_subs(tm), mix, norm_and_route))


def _merge(x2, pool_out, attn_out, gates, w_pu, w_au, w_o, ln_g, ln_b, w_r, b_r, alpha, tm):
    t, d = x2.shape
    width = pool_out.shape[1]
    row = lambda i: (i, 0)
    blocks = (2 * (tm * d * 4 + 2 * tm * width * 2 + tm * 2 * d * 2
                   + tm * d * 4 + tm * d * 2 + tm * LANES * (4 + 2) + LANES * 4)
              + (2 * width * d + d * d + d * LANES) * (4 + 2) + 2 * d * 4 + LANES * 4)
    temps = 5 * tm * d * 4
    return pl.pallas_call(
        functools.partial(_merge_kernel, alpha=alpha),
        grid=(t // tm,),
        in_specs=[pl.BlockSpec((tm, d), row), pl.BlockSpec((tm, width), row),
                  pl.BlockSpec((tm, width), row), pl.BlockSpec((tm, 2 * d), row),
                  _resident((width, d)), _resident((width, d)), _resident((d, d)),
                  _resident((1, d)), _resident((1, d)), _resident((d, LANES)),
                  _resident((1, LANES))],
        out_specs=[pl.BlockSpec((tm, d), row), pl.BlockSpec((tm, d), row),
                   pl.BlockSpec((tm, LANES), row), pl.BlockSpec((tm, LANES), row),
                   pl.BlockSpec((1, 1, LANES), lambda i: (i, 0, 0))],
        out_shape=[jax.ShapeDtypeStruct((t, d), F32), jax.ShapeDtypeStruct((t, d), BF16),
                   jax.ShapeDtypeStruct((t, LANES), F32), jax.ShapeDtypeStruct((t, LANES), BF16),
                   jax.ShapeDtypeStruct((t // tm, 1, LANES), F32)],
        scratch_shapes=[pltpu.VMEM((width, d), BF16), pltpu.VMEM((width, d), BF16),
                        pltpu.VMEM((d, d), BF16), pltpu.VMEM((d, 2 * LANES), BF16)],
        compiler_params=_params(("arbitrary",), blocks + temps),
        name="merge",
    )(x2, pool_out, attn_out, gates, w_pu, w_au, w_o, ln_g, ln_b, w_r, b_r)


def _routing_plan(counts):
    cnt = counts[:, 0, :N_EXPERTS].astype(jnp.int32)
    nch = (cnt + (CHUNK - 1)) // CHUNK
    local_end = jnp.cumsum(nch, axis=1)
    local_start = local_end - nch
    block_chunks = local_end[:, -1]
    before_block = jnp.cumsum(nch, axis=0) - nch
    expert_chunks = jnp.sum(nch, axis=0)
    chunks_per_tile = EXPERT_TILE // CHUNK
    expert_tiles = (expert_chunks + (chunks_per_tile - 1)) // chunks_per_tile
    tiles_end = jnp.cumsum(expert_tiles)
    region_start = (tiles_end - expert_tiles) * chunks_per_tile
    segment_dst = region_start[None, :] + before_block
    c = jnp.arange(LOCAL_CHUNKS, dtype=jnp.int32)
    owned = (c[None, :, None] >= local_start[:, None, :]) & (c[None, :, None] < local_end[:, None, :])
    chunk_dst = c[None, :] + jnp.sum(
        jnp.where(owned, (segment_dst - local_start)[:, None, :], 0), axis=2)
    i32 = lambda a: a.astype(jnp.int32)
    return dict(chunk_dst=i32(chunk_dst), block_chunks=i32(block_chunks),
                expert_tiles=i32(expert_tiles), tiles_used=i32(tiles_end[-1:]),
                pad_start=i32(region_start + expert_chunks),
                pad_count=i32(expert_tiles * chunks_per_tile - expert_chunks))


def _local_positions(sel):
    tb = sel.shape[0]
    earlier = (lax.broadcasted_iota(jnp.int32, (tb, tb), 1)
               < lax.broadcasted_iota(jnp.int32, (tb, tb), 0)).astype(BF16)
    rank = jnp.dot(earlier, sel, preferred_element_type=F32)
    cnt = jnp.sum(sel.astype(F32), axis=0, keepdims=True)
    nch = jnp.floor((cnt + (CHUNK - 1)) * (1.0 / CHUNK))
    lower = (lax.broadcasted_iota(jnp.int32, (LANES, LANES), 0)
             < lax.broadcasted_iota(jnp.int32, (LANES, LANES), 1)).astype(BF16)
    start = CHUNK * jnp.dot(jnp.broadcast_to(nch, (8, LANES)).astype(BF16), lower,
                            preferred_element_type=F32)[0:1]
    pos = rank + start
    chosen = sel > 0
    pos_lo = jnp.min(jnp.where(chosen, pos, float(LOCAL_ROWS)), axis=1, keepdims=True)
    pos_hi = jnp.max(jnp.where(chosen, pos, -1.0), axis=1, keepdims=True)
    return pos, pos_lo, pos_hi


def _for_each(n, fn):
    lax.fori_loop(0, n, lambda c, carry: (fn(c), carry)[1], 0)


def _scatter_kernel(dst_ref, nchunk_ref, pad_start_ref, pad_count_ref, used_ref, x_ref, sel_ref, g_ref,
                    loc_ref, zero_ref, sems):
    b = pl.program_id(0)
    last = pl.num_programs(0) - 1
    slot = b % 2
    tb = x_ref.shape[0]

    def chunk_copy(blk, c, slot):
        src = pl.multiple_of(c * CHUNK, CHUNK)
        dst = pl.multiple_of(dst_ref[blk, c] * CHUNK, CHUNK)
        return pltpu.make_async_copy(loc_ref.at[slot, pl.ds(src, CHUNK)],
                                     g_ref.at[pl.ds(dst, CHUNK)], sems.at[slot])

    def pad_copy(e, k):
        dst = pl.multiple_of((pad_start_ref[e] + k) * CHUNK, CHUNK)
        return pltpu.make_async_copy(zero_ref.at[pl.ds(0, CHUNK)], g_ref.at[pl.ds(dst, CHUNK)],
                                     sems.at[2])

    def unused_tile_copy(i):
        dst = pl.multiple_of(i * EXPERT_TILE, EXPERT_TILE)
        return pltpu.make_async_copy(zero_ref, g_ref.at[pl.ds(dst, EXPERT_TILE)], sems.at[2])

    @pl.when(b >= 2)
    def _():
        _for_each(nchunk_ref[b - 2], lambda c: chunk_copy(b - 2, c, slot).wait())

    _, pos_lo, pos_hi = _local_positions(sel_ref[...])
    lo = pos_lo.astype(jnp.int32)
    hi = pos_hi.astype(jnp.int32)
    x = x_ref[...]

    def sort_rows(m, carry):
        r0 = pl.multiple_of(m * SORT_ROWS, SORT_ROWS)
        r = r0 + lax.broadcasted_iota(jnp.int32, (tb, SORT_ROWS), 1)
        perm = ((r == lo) | (r == hi)).astype(BF16)
        rows = lax.dot_general(perm, x, (((0,), (0,)), ((), ())), preferred_element_type=F32)
        loc_ref[slot, pl.ds(r0, SORT_ROWS), :] = rows.astype(BF16)
        return carry

    lax.fori_loop(0, (nchunk_ref[b] * CHUNK + (SORT_ROWS - 1)) // SORT_ROWS, sort_rows, 0)
    _for_each(nchunk_ref[b], lambda c: chunk_copy(b, c, slot).start())

    @pl.when(b == last)
    def _():
        zero_ref[...] = jnp.zeros_like(zero_ref)
        n_unused = g_ref.shape[0] // EXPERT_TILE - used_ref[0]
        _for_each(N_EXPERTS, lambda e: _for_each(pad_count_ref[e], lambda k: pad_copy(e, k).start()))
        _for_each(n_unused, lambda i: unused_tile_copy(used_ref[0] + i).start())
        _for_each(N_EXPERTS, lambda e: _for_each(pad_count_ref[e], lambda k: pad_copy(e, k).wait()))
        _for_each(n_unused, lambda i: unused_tile_copy(used_ref[0] + i).wait())

        @pl.when(b >= 1)
        def _():
            _for_each(nchunk_ref[b - 1], lambda c: chunk_copy(b - 1, c, 1 - slot).wait())

        _for_each(nchunk_ref[b], lambda c: chunk_copy(b, c, slot).wait())


def _scatter(x1b, sel, plan, n_tiles):
    t, d = x1b.shape
    tb = ROUTE_BLOCK
    row = lambda b, *_: (b, 0)
    blocks = 2 * (tb * d * 2 + tb * LANES * 2) + 2 * LOCAL_ROWS * d * 2 + EXPERT_TILE * d * 2
    temps = tb * tb * 2 + 6 * tb * LANES * 4 + tb * SORT_ROWS * 6 + SORT_ROWS * d * 6
    return pl.pallas_call(
        _scatter_kernel,
        grid_spec=pltpu.PrefetchScalarGridSpec(
            num_scalar_prefetch=5,
            grid=(t // tb,),
            in_specs=[pl.BlockSpec((tb, d), row), pl.BlockSpec((tb, LANES), row)],
            out_specs=pl.BlockSpec(memory_space=pl.ANY),
            scratch_shapes=[pltpu.VMEM((2, LOCAL_ROWS, d), BF16),
                            pltpu.VMEM((EXPERT_TILE, d), BF16), pltpu.SemaphoreType.DMA((3,))]),
        out_shape=jax.ShapeDtypeStruct((n_tiles * EXPERT_TILE, d), BF16),
        compiler_params=_params(("arbitrary",), blocks + temps),
        name="scatter",
    )(plan["chunk_dst"], plan["block_chunks"], plan["pad_start"], plan["pad_count"],
      plan["tiles_used"], x1b, sel)


def _experts_kernel(ntile_ref, used_ref, g_ref, wg_ref, wu_ref, wd_ref, y_ref,
                    x_buf, y_buf, wg_buf, wu_buf, wd_buf, wgb_ref, wub_ref, wdb_ref,
                    x_sems, y_sems, w_sems):
    n_experts = wg_ref.shape[0]
    tm = EXPERT_TILE
    used = used_ref[0]

    def weight_copies(e, slot):
        return [pltpu.make_async_copy(src.at[e], dst.at[slot], w_sems.at[slot])
                for src, dst in ((wg_ref, wg_buf), (wu_ref, wu_buf), (wd_ref, wd_buf))]

    def x_copy(t, slot):
        rows = pl.ds(pl.multiple_of(t * tm, tm), MACRO_TILES * tm)
        return pltpu.make_async_copy(g_ref.at[rows], x_buf.at[slot], x_sems.at[slot])

    def y_copy(t, j, slot):
        src = pl.ds(pl.multiple_of(j * tm, tm), tm)
        dst = pl.ds(pl.multiple_of((t + j) * tm, tm), tm)
        return pltpu.make_async_copy(y_buf.at[slot, src], y_ref.at[dst], y_sems.at[slot])

    def mlp(slot, rows):
        x = x_buf[slot, :rows, :]
        hg = jnp.dot(x, wgb_ref[...], preferred_element_type=F32)
        hu = jnp.dot(x, wub_ref[...], preferred_element_type=F32)
        h = hg * _sigmoid(hg) * hu
        y_buf[slot, :rows, :] = jnp.dot(h.astype(BF16), wdb_ref[...],
                                        preferred_element_type=F32).astype(BF16)

    for c in weight_copies(0, 0):
        c.start()
    x_copy(0, 0).start()

    def run_expert(e, carry):
        wslot = e % 2
        for c in weight_copies(e, wslot):
            c.wait()

        @pl.when(e + 1 < n_experts)
        def _():
            for c in weight_copies(e + 1, 1 - wslot):
                c.start()

        @pl.when(ntile_ref[e] > 0)
        def _():
            wgb_ref[...] = wg_buf[wslot].astype(BF16)
            wub_ref[...] = wu_buf[wslot].astype(BF16)
            wdb_ref[...] = wd_buf[wslot].astype(BF16)

        def run_macro(m, carry):
            t, step, k1, t1, k2, t2 = carry
            k = jnp.minimum(MACRO_TILES, ntile_ref[e] - m * MACRO_TILES)
            slot = step % 2
            x_copy(t, slot).wait()

            @pl.when(t + k < used)
            def _():
                x_copy(t + k, 1 - slot).start()

            _for_each(k2, lambda j: y_copy(t2, j, slot).wait())
            for tiles in range(1, MACRO_TILES + 1):
                @pl.when(k == tiles)
                def _():
                    mlp(slot, tiles * tm)

            _for_each(k, lambda j: y_copy(t, j, slot).start())
            return t + k, step + 1, k, t, k1, t1

        n_macro = (ntile_ref[e] + (MACRO_TILES - 1)) // MACRO_TILES
        return lax.fori_loop(0, n_macro, run_macro, carry)

    zero = jnp.int32(0)
    _, step, k1, t1, k2, t2 = lax.fori_loop(0, n_experts, run_expert, (zero,) * 6)
    _for_each(k2, lambda j: y_copy(t2, j, step % 2).wait())
    _for_each(k1, lambda j: y_copy(t1, j, (step + 1) % 2).wait())
    y_buf[0, :tm, :] = jnp.zeros((tm, y_buf.shape[2]), BF16)
    n_unused = y_ref.shape[0] // tm - used
    _for_each(n_unused, lambda i: y_copy(used + i, 0, 0).start())
    _for_each(n_unused, lambda i: y_copy(used + i, 0, 0).wait())


def _experts(sorted_x, plan, w_eg, w_eu, w_ed):
    rows, d = sorted_x.shape
    _, _, de = w_eg.shape
    tm = EXPERT_TILE
    any_space = pl.BlockSpec(memory_space=pl.ANY)
    big = MACRO_TILES * tm
    scratch = 2 * 2 * big * d * 2 + 2 * 3 * d * de * 4 + 3 * d * de * 2
    temps = 3 * big * de * 4 + big * d * 4 + d * de * 4
    return pl.pallas_call(
        _experts_kernel,
        grid_spec=pltpu.PrefetchScalarGridSpec(
            num_scalar_prefetch=2,
            grid=(1,),
            in_specs=[any_space] * 4,
            out_specs=any_space,
            scratch_shapes=[pltpu.VMEM((2, big, d), BF16), pltpu.VMEM((2, big, d), BF16),
                            pltpu.VMEM((2, d, de), F32), pltpu.VMEM((2, d, de), F32),
                            pltpu.VMEM((2, de, d), F32),
                            pltpu.VMEM((d, de), BF16), pltpu.VMEM((d, de), BF16),
                            pltpu.VMEM((de, d), BF16),
                            pltpu.SemaphoreType.DMA((2,)), pltpu.SemaphoreType.DMA((2,)),
                            pltpu.SemaphoreType.DMA((2,))]),
        out_shape=jax.ShapeDtypeStruct((rows, d), BF16),
        compiler_params=_params(("arbitrary",), scratch + temps),
        name="experts",
    )(plan["expert_tiles"], plan["tiles_used"], sorted_x, w_eg, w_eu, w_ed)


def _combine_kernel(dst_ref, nchunk_ref, x1_ref, comb_ref, sel_ref, p_ref, wpg_ref, wpp_ref,
                    lg_ref, lb_ref, y_ref, o_ref, loc_ref, wpgb_ref, wppb_ref, sems, *, alpha):
    b = pl.program_id(0)
    nb = pl.num_programs(0)
    slot = b % 2
    tb, d = x1_ref.shape
    _cast_once(wpg_ref, wpgb_ref)
    _cast_once(wpp_ref, wppb_ref)

    def chunk_copy(blk, c, slot):
        src = pl.multiple_of(dst_ref[blk, c] * CHUNK, CHUNK)
        dst = pl.multiple_of(c * CHUNK, CHUNK)
        return pltpu.make_async_copy(y_ref.at[pl.ds(src, CHUNK)],
                                     loc_ref.at[slot, pl.ds(dst, CHUNK)], sems.at[slot])

    @pl.when(b == 0)
    def _():
        loc_ref[...] = jnp.zeros_like(loc_ref)
        _for_each(nchunk_ref[0], lambda c: chunk_copy(0, c, 0).start())

    @pl.when(b + 1 < nb)
    def _():
        _for_each(nchunk_ref[b + 1], lambda c: chunk_copy(b + 1, c, 1 - slot).start())

    _for_each(nchunk_ref[b], lambda c: chunk_copy(b, c, slot).wait())

    sel = sel_ref[...]
    pos, pos_lo, pos_hi = _local_positions(sel)
    chosen = sel > 0
    comb = comb_ref[...]
    w_lo = jnp.sum(jnp.where(chosen & (pos == pos_lo), comb, 0.0), axis=1, keepdims=True)
    w_hi = jnp.sum(jnp.where(chosen & (pos == pos_hi), comb, 0.0), axis=1, keepdims=True)
    lo = pos_lo.astype(jnp.int32)
    hi = pos_hi.astype(jnp.int32)

    sorted_row = lax.broadcasted_iota(jnp.int32, (tb // ROW_SUBS, LOCAL_ROWS), 1)

    def branches(r):
        weights = (jnp.where(sorted_row == lo[r], w_lo[r], 0.0)
                   + jnp.where(sorted_row == hi[r], w_hi[r], 0.0)).astype(BF16)
        moe = jnp.dot(weights, loc_ref[slot], preferred_element_type=F32)
        gate = jnp.dot(x1_ref[r, :].astype(BF16), wpgb_ref[...], preferred_element_type=F32)
        emb = jnp.dot(p_ref[r, :].astype(BF16), wppb_ref[...], preferred_element_type=F32)
        return moe, gate, emb

    def finish(r, parts):
        moe, gate, emb = parts
        h = alpha * x1_ref[r, :] + moe + _sigmoid(gate) * emb
        o_ref[r, :] = _layer_norm(h, lg_ref[...], lb_ref[...])

    _skewed(_row_subs(tb), branches, finish)


def _combine(y, plan, x1, comb, sel, p2, w_pg, w_pp, ln_g, ln_b, alpha):
    t, d = x1.shape
    pd = p2.shape[1]
    tb = ROUTE_BLOCK
    row = lambda b, *_: (b, 0)
    blocks = (2 * (2 * tb * d * 4 + tb * LANES * (4 + 2) + tb * pd * 4)
              + (d * d + pd * d) * (4 + 2) + 2 * d * 4 + 2 * LOCAL_ROWS * d * 2)
    temps = tb * tb * 2 + 8 * tb * LANES * 4 + tb * LOCAL_ROWS * 10 + 5 * tb * d * 4
    return pl.pallas_call(
        functools.partial(_combine_kernel, alpha=alpha),
        grid_spec=pltpu.PrefetchScalarGridSpec(
            num_scalar_prefetch=2,
            grid=(t // tb,),
            in_specs=[pl.BlockSpec((tb, d), row), pl.BlockSpec((tb, LANES), row),
                      pl.BlockSpec((tb, LANES), row), pl.BlockSpec((tb, pd), row),
                      _resident((d, d)), _resident((pd, d)), _resident((1, d)), _resident((1, d)),
                      pl.BlockSpec(memory_space=pl.ANY)],
            out_specs=pl.BlockSpec((tb, d), row),
            scratch_shapes=[pltpu.VMEM((2, LOCAL_ROWS, d), BF16), pltpu.VMEM((d, d), BF16),
                            pltpu.VMEM((pd, d), BF16), pltpu.SemaphoreType.DMA((2,))]),
        out_shape=jax.ShapeDtypeStruct((t, d), F32),
        compiler_params=_params(("arbitrary",), blocks + temps),
        name="combine",
    )(plan["chunk_dst"], plan["block_chunks"], x1, comb, sel, p2, w_pg, w_pp, ln_g, ln_b, y)


def kernel(x, p, w_in, w_pool, pool_scale, w_pu, w_au, w_o, ln1_g, ln1_b, w_rg, b_rg, w_re, b_re,
           w_eg, w_eu, w_ed, w_pg, w_pp, ln2_g, ln2_b):
    bsz, seq, d = x.shape
    depth = w_in.shape[0]
    t = bsz * seq
    de = w_eg.shape[-1]
    alpha = (2.0 * depth) ** 0.25
    assert w_rg.shape[2] == N_GROUPS and w_re.shape[1:] == (N_GROUPS, d, EXPERTS_PER_GROUP)
    assert w_in.shape[2] == 4 * d and w_pool.shape[1] == len(POOL_WINDOWS)
    assert t % ROUTE_BLOCK == 0 and N_EXPERTS + N_GROUPS <= LANES
    n_blocks = t // ROUTE_BLOCK
    n_tiles = -(-(2 * t + n_blocks * N_EXPERTS * (CHUNK - 1) + N_EXPERTS * (EXPERT_TILE - CHUNK))
                // EXPERT_TILE) + MACRO_TILES - 1

    x2 = x.reshape(t, d)
    for i in range(depth):
        u, q, k, v, gates = _in_hbm(*_proj(x2, w_in[i], tm=512))
        pool_out, = _in_hbm(_pool(u, w_pool[i], pool_scale[i][None, :], seq))
        attn_out, = _in_hbm(_attention(q, k, v, seq, tb=128))

        w_r = jnp.concatenate(
            [w_re[i].transpose(1, 0, 2).reshape(d, N_EXPERTS), w_rg[i]], axis=1)
        w_r = jnp.pad(w_r, ((0, 0), (0, LANES - w_r.shape[1])))
        b_r = jnp.pad(jnp.concatenate([b_re[i].reshape(-1), b_rg[i]]),
                      (0, LANES - N_GROUPS - N_EXPERTS))[None, :]

        x1, x1b, comb, sel, counts = _merge(
            x2, pool_out, attn_out, gates, *_in_hbm(w_pu[i], w_au[i], w_o[i]), ln1_g[i][None, :],
            ln1_b[i][None, :], w_r, b_r, alpha, tm=ROUTE_BLOCK)
        x1, x1b, comb, sel = _in_hbm(x1, x1b, comb, sel)

        plan = _routing_plan(counts)
        sorted_x, = _in_hbm(_scatter(x1b, sel, plan, n_tiles))
        y, = _in_hbm(_experts(sorted_x, plan, w_eg[i].reshape(N_EXPERTS, d, de),
                              w_eu[i].reshape(N_EXPERTS, d, de), w_ed[i].reshape(N_EXPERTS, de, d)))
        x2 = _combine(y, plan, x1, comb, sel, p[i].reshape(t, -1), *_in_hbm(w_pg[i], w_pp[i]),
                      ln2_g[i][None, :], ln2_b[i][None, :], alpha)
    return x2.reshape(bsz, seq, d)
```

```python
import functools
import math

import jax
import jax.numpy as jnp
from jax import lax
from jax.experimental import pallas as pl
from jax.experimental.pallas import tpu as pltpu

F32 = jnp.float32
BF16 = jnp.bfloat16

LANES = 128
POOL_WINDOWS = (2, 4, 8, 16)
HEAD_DIM = 64
N_GROUPS = 4
EXPERTS_PER_GROUP = 8
N_EXPERTS = N_GROUPS * EXPERTS_PER_GROUP
LN_EPS = 1e-5
GROUP_LANE0 = N_EXPERTS
NEG_BIG = -1e30
ROUTE_BLOCK = 512
CHUNK = 16
EXPERT_TILE = 256
MACRO_TILES = 4
SORT_ROWS = 256
ROW_SUBS = 2
LOCAL_ROWS = -(-(2 * ROUTE_BLOCK + N_EXPERTS * (CHUNK - 1)) // SORT_ROWS) * SORT_ROWS
ATTN_DEAD_LOG2 = -160.0
VMEM_CAP_BYTES = 56 * 1024 * 1024


def _params(sem, vmem_bytes):
    return pltpu.CompilerParams(
        dimension_semantics=sem, vmem_limit_bytes=min(int(vmem_bytes), VMEM_CAP_BYTES))


def _layer_norm(h, g, b):
    mu = jnp.mean(h, axis=-1, keepdims=True)
    c = h - mu
    var = jnp.mean(c * c, axis=-1, keepdims=True)
    return c * lax.rsqrt(var + LN_EPS) * g + b


def _sigmoid(z):
    return 1.0 / (1.0 + jnp.exp(-z))


def _row_subs(rows):
    return [slice(k * (rows // ROW_SUBS), (k + 1) * (rows // ROW_SUBS)) for k in range(ROW_SUBS)]


def _skewed(subs, first, second):
    out, pending = [], None
    for r in subs:
        mid = first(r)
        if pending is not None:
            out.append(second(*pending))
        pending = (r, mid)
    out.append(second(*pending))
    return out


def _in_hbm(*arrays):
    if not all(isinstance(a, jax.core.Tracer) for a in arrays):
        return list(arrays)
    return [pltpu.with_memory_space_constraint(a, pltpu.HBM) for a in arrays]


def _resident(shape):
    return pl.BlockSpec(shape, lambda *_: (0,) * len(shape), pipeline_mode=pl.Buffered(1))


def _cast_once(w_ref, wb_ref):
    @pl.when(pl.program_id(0) == 0)
    def _():
        wb_ref[...] = w_ref[...].astype(BF16)


def _proj_kernel(x_ref, w_ref, u_ref, q_ref, k_ref, v_ref, g_ref, wb_ref, *, width):
    q_scale = math.log2(math.e) / math.sqrt(HEAD_DIM)
    _cast_once(w_ref, wb_ref)
    xb = x_ref[...].astype(BF16)

    def mm(lo):
        return jnp.dot(xb, wb_ref[:, lo:lo + width], preferred_element_type=F32)

    u_ref[...] = mm(0)
    q_ref[...] = (mm(width) * q_scale).astype(BF16)
    k_ref[...] = mm(2 * width).astype(BF16)
    v_ref[...] = mm(3 * width).astype(BF16)
    for c in range(g_ref.shape[1] // width):
        g_ref[:, c * width:(c + 1) * width] = _sigmoid(mm((4 + c) * width)).astype(BF16)


def _proj(x2, w_in, tm):
    t, d = x2.shape
    n = w_in.shape[1]
    width = d // 2
    gate_w = n - 4 * width
    row = lambda i: (i, 0)
    blocks = (2 * (tm * d * 4 + tm * width * (4 + 3 * 2) + tm * gate_w * 2)
              + d * n * (4 + 2))
    temps = tm * d * 2 + 2 * tm * width * 4
    return pl.pallas_call(
        functools.partial(_proj_kernel, width=width),
        grid=(t // tm,),
        in_specs=[pl.BlockSpec((tm, d), row), _resident((d, n))],
        out_specs=[pl.BlockSpec((tm, width), row)] * 4 + [pl.BlockSpec((tm, gate_w), row)],
        out_shape=[jax.ShapeDtypeStruct((t, width), F32)]
        + [jax.ShapeDtypeStruct((t, width), BF16)] * 3
        + [jax.ShapeDtypeStruct((t, gate_w), BF16)],
        scratch_shapes=[pltpu.VMEM((d, n), BF16)],
        compiler_params=_params(("arbitrary",), blocks + temps),
        name="proj",
    )(x2, w_in)


def _pool_kernel(u_ref, wp_ref, sc_ref, o_ref):
    seq = u_ref.shape[0]
    gd = wp_ref.shape[1]
    row = lax.broadcasted_iota(jnp.int32, (seq, gd), 0)
    for g, w in enumerate(POOL_WINDOWS):
        cols = slice(g * gd, (g + 1) * gd)
        ug = u_ref[:, cols]
        s = ug
        sh = 1
        while sh < w:
            s = s + jnp.where(row >= sh, pltpu.roll(s, sh, axis=0), 0.0)
            sh *= 2
        cnt = jnp.minimum(row + 1, w).astype(F32)
        pooled = s / cnt - ug
        mixed = jnp.dot(pooled.astype(BF16), wp_ref[g].astype(BF16), preferred_element_type=F32)
        o_ref[:, cols] = (mixed * sc_ref[:, cols]).astype(BF16)


def _pool(u, w_pool, pool_scale, seq):
    t, width = u.shape
    g, gd, _ = w_pool.shape
    blocks = 2 * (seq * width * (4 + 2) + g * gd * gd * 4 + width * 4)
    temps = 6 * seq * gd * 4
    return pl.pallas_call(
        _pool_kernel,
        grid=(t // seq,),
        in_specs=[pl.BlockSpec((seq, width), lambda b: (b, 0)),
                  pl.BlockSpec((g, gd, gd), lambda b: (0, 0, 0)),
                  pl.BlockSpec((1, width), lambda b: (0, 0))],
        out_specs=pl.BlockSpec((seq, width), lambda b: (b, 0)),
        out_shape=jax.ShapeDtypeStruct((t, width), BF16),
        compiler_params=_params(("parallel",), blocks + temps),
        name="pool",
    )(u, w_pool, pool_scale)


def _attn_kernel(q_ref, k_ref, v_ref, o_ref, *, tb):
    qi = pl.program_id(1)
    pairs = q_ref.shape[1] // LANES
    first_head = lax.broadcasted_iota(jnp.int32, (tb, LANES), 1) < HEAD_DIM
    r = lax.broadcasted_iota(jnp.int32, (2 * tb, 2 * tb), 0)
    c = lax.broadcasted_iota(jnp.int32, (2 * tb, 2 * tb), 1)
    same_head = (r >= tb) == (c >= tb)
    cum = -jnp.concatenate([(same_head & (r > c)).astype(BF16), same_head.astype(BF16)], axis=1)
    kcol = lax.broadcasted_iota(jnp.int32, (tb, 2 * tb), 1)
    kcol = jnp.where(kcol >= tb, kcol - tb, kcol)
    causal = kcol < lax.broadcasted_iota(jnp.int32, (tb, 2 * tb), 0)

    def stack_heads(blk):
        zero = jnp.zeros_like(blk)
        return jnp.concatenate(
            [jnp.where(first_head, blk, zero), jnp.where(first_head, zero, blk)], axis=0)

    def step(j, rems, accs, diagonal):
        ks = pl.multiple_of(j * tb, tb)
        col_blocks = [slice(p * LANES, (p + 1) * LANES) for p in range(pairs)]
        zs = [lax.dot_general(q_ref[:, cols], stack_heads(k_ref[pl.ds(ks, tb), cols]),
                              (((1,), (1,)), ((), ())), preferred_element_type=F32)
              for cols in col_blocks]
        log_betas, sums = [], []
        for z in zs:
            softplus = jnp.maximum(z, 0.0) + jnp.log2(1.0 + jnp.exp2(-jnp.abs(z)))
            log_betas.append(z - softplus)
            if diagonal:
                softplus = jnp.where(causal, softplus, 0.0)
            sums.append(jnp.dot(softplus.astype(BF16), cum, preferred_element_type=F32))
        new_rems, new_accs = [], []
        for p, cols in enumerate(col_blocks):
            later = sums[p][:, :2 * tb]
            if rems is not None:
                later = later + rems[p]
            a = jnp.exp2(log_betas[p] + later)
            if diagonal:
                a = jnp.where(causal, a, 0.0)
            out = jnp.dot(a.astype(BF16), stack_heads(v_ref[pl.ds(ks, tb), cols]),
                          preferred_element_type=F32)
            total = sums[p][:, 2 * tb:]
            new_accs.append(out if accs is None else accs[p] + out)
            new_rems.append(total if rems is None else rems[p] + total)
        return tuple(new_rems), tuple(new_accs)

    def rem_max(rems):
        return jnp.max(functools.reduce(jnp.maximum, rems))

    rems, accs = step(qi, None, None, diagonal=True)

    def cond(carry):
        j, _, _, worst = carry
        return (j >= 0) & (worst > ATTN_DEAD_LOG2)

    def body(carry):
        j, rems, accs, _ = carry
        rems, accs = step(j, rems, accs, diagonal=False)
        return j - 1, rems, accs, rem_max(rems)

    accs = lax.while_loop(cond, body, (qi - 1, rems, accs, rem_max(rems)))[2]
    for p in range(pairs):
        o_ref[:, p * LANES:(p + 1) * LANES] = accs[p].astype(BF16)


def _attention(q, k, v, seq, tb):
    t, width = q.shape
    blocks = 2 * (2 * tb * width * 2 + 2 * seq * width * 2)
    temps = (width // LANES) * 10 * tb * 2 * tb * 4 + 2 * tb * 4 * tb * 2
    qspec = pl.BlockSpec((tb, width), lambda b, i: (b * (seq // tb) + i, 0))
    kvspec = pl.BlockSpec((seq, width), lambda b, i: (b, 0))
    return pl.pallas_call(
        functools.partial(_attn_kernel, tb=tb),
        grid=(t // seq, seq // tb),
        in_specs=[qspec, kvspec, kvspec],
        out_specs=qspec,
        out_shape=jax.ShapeDtypeStruct((t, width), BF16),
        compiler_params=_params(("parallel", "parallel"), blocks + temps),
        name="attn",
    )(q, k, v)


def _route(logits):
    lane = lax.broadcasted_iota(jnp.int32, logits.shape, 1)

    def first_max(vals):
        m = jnp.max(vals, axis=1, keepdims=True)
        idx = jnp.min(jnp.where(vals == m, lane, LANES), axis=1, keepdims=True)
        return m, idx

    is_group = (lane >= GROUP_LANE0) & (lane < GROUP_LANE0 + N_GROUPS)
    gm, g_lane = first_max(jnp.where(is_group, logits, NEG_BIG))
    g_prob = 1.0 / jnp.sum(jnp.where(is_group, jnp.exp(logits - gm), 0.0), axis=1, keepdims=True)
    lo = EXPERTS_PER_GROUP * (g_lane - GROUP_LANE0)
    in_group = jnp.where((lane >= lo) & (lane < lo + EXPERTS_PER_GROUP), logits, NEG_BIG)
    m1, i1 = first_max(in_group)
    m2, i2 = first_max(jnp.where(lane == i1, NEG_BIG, in_group))
    e21 = jnp.exp(m2 - m1)
    w1 = g_prob / (1.0 + e21)
    w2 = w1 * e21
    comb = jnp.where(lane == i1, w1, 0.0) + jnp.where(lane == i2, w2, 0.0)
    sel = ((lane == i1) | (lane == i2)).astype(F32)
    return comb, sel


def _merge_kernel(x_ref, po_ref, at_ref, g_ref, wpu_ref, wau_ref, wo_ref, lg_ref, lb_ref,
                  wr_ref, br_ref, x1_ref, x1b_ref, comb_ref, sel_ref, cnt_ref,
                  wpub_ref, waub_ref, wob_ref, wrb_ref, *, alpha):
    tm, d = x_ref.shape
    _cast_once(wpu_ref, wpub_ref)
    _cast_once(wau_ref, waub_ref)
    _cast_once(wo_ref, wob_ref)

    @pl.when(pl.program_id(0) == 0)
    def _():
        w_r = wr_ref[...]
        hi = w_r.astype(BF16)
        wrb_ref[:, :LANES] = hi
        wrb_ref[:, LANES:] = (w_r - hi.astype(F32)).astype(BF16)

    def mix(r):
        a = jnp.dot(po_ref[r, :], wpub_ref[...], preferred_element_type=F32)
        b = jnp.dot(at_ref[r, :], waub_ref[...], preferred_element_type=F32)
        merged = (g_ref[r, :d].astype(F32) * a + g_ref[r, d:].astype(F32) * b).astype(BF16)
        return alpha * x_ref[r, :] + jnp.dot(merged, wob_ref[...], preferred_element_type=F32)

    def norm_and_route(r, h):
        x1 = _layer_norm(h, lg_ref[...], lb_ref[...])
        xh = x1.astype(BF16)
        x1_ref[r, :] = x1
        x1b_ref[r, :] = xh
        xl = (x1 - xh.astype(F32)).astype(BF16)
        by_hi = jnp.dot(xh, wrb_ref[...], preferred_element_type=F32)
        by_lo = jnp.dot(xl, wrb_ref[:, :LANES], preferred_element_type=F32)
        comb, sel = _route(by_hi[:, :LANES] + by_hi[:, LANES:] + by_lo + br_ref[...])
        comb_ref[r, :] = comb
        sel_ref[r, :] = sel.astype(BF16)
        return jnp.sum(sel, axis=0, keepdims=True)

    cnt_ref[0] = sum(_skewed(_row_subs(tm), mix, norm_and_route))


def _merge(x2, pool_out, attn_out, gates, w_pu, w_au, w_o, ln_g, ln_b, w_r, b_r, alpha, tm):
    t, d = x2.shape
    width = pool_out.shape[1]
    row = lambda i: (i, 0)
    blocks = (2 * (tm * d * 4 + 2 * tm * width * 2 + tm * 2 * d * 2
                   + tm * d * 4 + tm * d * 2 + tm * LANES * (4 + 2) + LANES * 4)
              + (2 * width * d + d * d + d * LANES) * (4 + 2) + 2 * d * 4 + LANES * 4)
    temps = 5 * tm * d * 4
    return pl.pallas_call(
        functools.partial(_merge_kernel, alpha=alpha),
        grid=(t // tm,),
        in_specs=[pl.BlockSpec((tm, d), row), pl.BlockSpec((tm, width), row),
                  pl.BlockSpec((tm, width), row), pl.BlockSpec((tm, 2 * d), row),
                  _resident((width, d)), _resident((width, d)), _resident((d, d)),
                  _resident((1, d)), _resident((1, d)), _resident((d, LANES)),
                  _resident((1, LANES))],
        out_specs=[pl.BlockSpec((tm, d), row), pl.BlockSpec((tm, d), row),
                   pl.BlockSpec((tm, LANES), row), pl.BlockSpec((tm, LANES), row),
                   pl.BlockSpec((1, 1, LANES), lambda i: (i, 0, 0))],
        out_shape=[jax.ShapeDtypeStruct((t, d), F32), jax.ShapeDtypeStruct((t, d), BF16),
                   jax.ShapeDtypeStruct((t, LANES), F32), jax.ShapeDtypeStruct((t, LANES), BF16),
                   jax.ShapeDtypeStruct((t // tm, 1, LANES), F32)],
        scratch_shapes=[pltpu.VMEM((width, d), BF16), pltpu.VMEM((width, d), BF16),
                        pltpu.VMEM((d, d), BF16), pltpu.VMEM((d, 2 * LANES), BF16)],
        compiler_params=_params(("arbitrary",), blocks + temps),
        name="merge",
    )(x2, pool_out, attn_out, gates, w_pu, w_au, w_o, ln_g, ln_b, w_r, b_r)


def _routing_plan(counts):
    cnt = counts[:, 0, :N_EXPERTS].astype(jnp.int32)
    nch = (cnt + (CHUNK - 1)) // CHUNK
    local_end = jnp.cumsum(nch, axis=1)
    local_start = local_end - nch
    block_chunks = local_end[:, -1]
    before_block = jnp.cumsum(nch, axis=0) - nch
    expert_chunks = jnp.sum(nch, axis=0)
    chunks_per_tile = EXPERT_TILE // CHUNK
    expert_tiles = (expert_chunks + (chunks_per_tile - 1)) // chunks_per_tile
    tiles_end = jnp.cumsum(expert_tiles)
    region_start = (tiles_end - expert_tiles) * chunks_per_tile
    segment_dst = region_start[None, :] + before_block
    i32 = lambda a: a.astype(jnp.int32)
    return dict(seg_len=i32(nch), seg_src=i32(local_start), seg_dst=i32(segment_dst),
                block_chunks=i32(block_chunks),
                expert_tiles=i32(expert_tiles), tiles_used=i32(tiles_end[-1:]),
                pad_start=i32(region_start + expert_chunks),
                pad_count=i32(expert_tiles * chunks_per_tile - expert_chunks))


def _local_positions(sel):
    tb = sel.shape[0]
    earlier = (lax.broadcasted_iota(jnp.int32, (tb, tb), 1)
               < lax.broadcasted_iota(jnp.int32, (tb, tb), 0)).astype(BF16)
    rank = jnp.dot(earlier, sel, preferred_element_type=F32)
    cnt = jnp.sum(sel.astype(F32), axis=0, keepdims=True)
    nch = jnp.floor((cnt + (CHUNK - 1)) * (1.0 / CHUNK))
    lower = (lax.broadcasted_iota(jnp.int32, (LANES, LANES), 0)
             < lax.broadcasted_iota(jnp.int32, (LANES, LANES), 1)).astype(BF16)
    start = CHUNK * jnp.dot(jnp.broadcast_to(nch, (8, LANES)).astype(BF16), lower,
                            preferred_element_type=F32)[0:1]
    pos = rank + start
    chosen = sel > 0
    pos_lo = jnp.min(jnp.where(chosen, pos, float(LOCAL_ROWS)), axis=1, keepdims=True)
    pos_hi = jnp.max(jnp.where(chosen, pos, -1.0), axis=1, keepdims=True)
    return pos, pos_lo, pos_hi


def _for_each(n, fn):
    lax.fori_loop(0, n, lambda c, carry: (fn(c), carry)[1], 0)


def _segment_rows(len_ref, src_ref, dst_ref, blk, e):
    n = len_ref[blk, e] * CHUNK
    src = pl.multiple_of(src_ref[blk, e] * CHUNK, CHUNK)
    dst = pl.multiple_of(dst_ref[blk, e] * CHUNK, CHUNK)
    return n, src, dst


def _scatter_kernel(len_ref, src_ref, dst_ref, nchunk_ref, pad_start_ref, pad_count_ref, used_ref,
                    x_ref, sel_ref, g_ref, loc_ref, zero_ref, sems):
    b = pl.program_id(0)
    last = pl.num_programs(0) - 1
    slot = b % 2
    tb = x_ref.shape[0]

    def start_block(blk, slot):
        def start_segment(e):
            n, src, dst = _segment_rows(len_ref, src_ref, dst_ref, blk, e)

            @pl.when(n > 0)
            def _():
                pltpu.make_async_copy(loc_ref.at[slot, pl.ds(src, n)], g_ref.at[pl.ds(dst, n)],
                                      sems.at[slot]).start()

        _for_each(N_EXPERTS, start_segment)

    def wait_block(blk, slot):
        n = nchunk_ref[blk] * CHUNK
        pltpu.make_async_copy(loc_ref.at[slot, pl.ds(0, n)], g_ref.at[pl.ds(0, n)],
                              sems.at[slot]).wait()

    def pad_copy(e):
        n = pad_count_ref[e] * CHUNK
        dst = pl.multiple_of(pad_start_ref[e] * CHUNK, CHUNK)
        return n, pltpu.make_async_copy(zero_ref.at[pl.ds(0, n)], g_ref.at[pl.ds(dst, n)],
                                        sems.at[2])

    def unused_tile_copy(i):
        dst = pl.multiple_of(i * EXPERT_TILE, EXPERT_TILE)
        return pltpu.make_async_copy(zero_ref, g_ref.at[pl.ds(dst, EXPERT_TILE)], sems.at[2])

    @pl.when(b >= 2)
    def _():
        wait_block(b - 2, slot)

    _, pos_lo, pos_hi = _local_positions(sel_ref[...])
    lo = pos_lo.astype(jnp.int32)
    hi = pos_hi.astype(jnp.int32)
    x = x_ref[...]

    r = lax.broadcasted_iota(jnp.int32, (tb, LOCAL_ROWS), 1)
    perm = ((r == lo) | (r == hi)).astype(BF16)
    rows = lax.dot_general(perm, x, (((0,), (0,)), ((), ())), preferred_element_type=F32)
    loc_ref[slot] = rows.astype(BF16)
    start_block(b, slot)

    @pl.when(b == last)
    def _():
        zero_ref[...] = jnp.zeros_like(zero_ref)
        n_unused = g_ref.shape[0] // EXPERT_TILE - used_ref[0]

        def each_pad(act):
            def one(e):
                n, copy = pad_copy(e)

                @pl.when(n > 0)
                def _():
                    act(copy)

            _for_each(N_EXPERTS, one)

        each_pad(lambda copy: copy.start())
        _for_each(n_unused, lambda i: unused_tile_copy(used_ref[0] + i).start())
        each_pad(lambda copy: copy.wait())
        _for_each(n_unused, lambda i: unused_tile_copy(used_ref[0] + i).wait())

        @pl.when(b >= 1)
        def _():
            wait_block(b - 1, 1 - slot)

        wait_block(b, slot)


def _scatter(x1b, sel, plan, n_tiles):
    t, d = x1b.shape
    tb = ROUTE_BLOCK
    row = lambda b, *_: (b, 0)
    blocks = 2 * (tb * d * 2 + tb * LANES * 2) + 2 * LOCAL_ROWS * d * 2 + EXPERT_TILE * d * 2
    temps = tb * tb * 2 + 6 * tb * LANES * 4 + tb * LOCAL_ROWS * 6 + LOCAL_ROWS * d * 6
    return pl.pallas_call(
        _scatter_kernel,
        grid_spec=pltpu.PrefetchScalarGridSpec(
            num_scalar_prefetch=7,
            grid=(t // tb,),
            in_specs=[pl.BlockSpec((tb, d), row), pl.BlockSpec((tb, LANES), row)],
            out_specs=pl.BlockSpec(memory_space=pl.ANY),
            scratch_shapes=[pltpu.VMEM((2, LOCAL_ROWS, d), BF16),
                            pltpu.VMEM((EXPERT_TILE, d), BF16), pltpu.SemaphoreType.DMA((3,))]),
        out_shape=jax.ShapeDtypeStruct((n_tiles * EXPERT_TILE, d), BF16),
        compiler_params=_params(("arbitrary",), blocks + temps),
        name="scatter",
    )(plan["seg_len"], plan["seg_src"], plan["seg_dst"], plan["block_chunks"], plan["pad_start"],
      plan["pad_count"], plan["tiles_used"], x1b, sel)


def _experts_kernel(ntile_ref, used_ref, g_ref, wg_ref, wu_ref, wd_ref, y_ref,
                    x_buf, y_buf, wg_buf, wu_buf, wd_buf, wgb_ref, wub_ref, wdb_ref,
                    x_sems, y_sems, w_sems):
    n_experts = wg_ref.shape[0]
    tm = EXPERT_TILE
    used = used_ref[0]

    def weight_copies(e, slot):
        return [pltpu.make_async_copy(src.at[e], dst.at[slot], w_sems.at[slot])
                for src, dst in ((wg_ref, wg_buf), (wu_ref, wu_buf), (wd_ref, wd_buf))]

    def x_copy(t, slot):
        rows = pl.ds(pl.multiple_of(t * tm, tm), MACRO_TILES * tm)
        return pltpu.make_async_copy(g_ref.at[rows], x_buf.at[slot], x_sems.at[slot])

    def y_copy(t, j, slot):
        src = pl.ds(pl.multiple_of(j * tm, tm), tm)
        dst = pl.ds(pl.multiple_of((t + j) * tm, tm), tm)
        return pltpu.make_async_copy(y_buf.at[slot, src], y_ref.at[dst], y_sems.at[slot])

    def mlp(slot, rows):
        x = x_buf[slot, :rows, :]
        hg = jnp.dot(x, wgb_ref[...], preferred_element_type=F32)
        hu = jnp.dot(x, wub_ref[...], preferred_element_type=F32)
        h = hg * _sigmoid(hg) * hu
        y_buf[slot, :rows, :] = jnp.dot(h.astype(BF16), wdb_ref[...],
                                        preferred_element_type=F32).astype(BF16)

    for c in weight_copies(0, 0):
        c.start()
    x_copy(0, 0).start()

    def run_expert(e, carry):
        wslot = e % 2
        for c in weight_copies(e, wslot):
            c.wait()

        @pl.when(e + 1 < n_experts)
        def _():
            for c in weight_copies(e + 1, 1 - wslot):
                c.start()

        @pl.when(ntile_ref[e] > 0)
        def _():
            wgb_ref[...] = wg_buf[wslot].astype(BF16)
            wub_ref[...] = wu_buf[wslot].astype(BF16)
            wdb_ref[...] = wd_buf[wslot].astype(BF16)

        def run_macro(m, carry):
            t, step, k1, t1, k2, t2 = carry
            k = jnp.minimum(MACRO_TILES, ntile_ref[e] - m * MACRO_TILES)
            slot = step % 2
            x_copy(t, slot).wait()

            @pl.when(t + k < used)
            def _():
                x_copy(t + k, 1 - slot).start()

            _for_each(k2, lambda j: y_copy(t2, j, slot).wait())
            for tiles in range(1, MACRO_TILES + 1):
                @pl.when(k == tiles)
                def _():
                    mlp(slot, tiles * tm)

            _for_each(k, lambda j: y_copy(t, j, slot).start())
            return t + k, step + 1, k, t, k1, t1

        n_macro = (ntile_ref[e] + (MACRO_TILES - 1)) // MACRO_TILES
        return lax.fori_loop(0, n_macro, run_macro, carry)

    zero = jnp.int32(0)
    _, step, k1, t1, k2, t2 = lax.fori_loop(0, n_experts, run_expert, (zero,) * 6)
    _for_each(k2, lambda j: y_copy(t2, j, step % 2).wait())
    _for_each(k1, lambda j: y_copy(t1, j, (step + 1) % 2).wait())
    y_buf[0, :tm, :] = jnp.zeros((tm, y_buf.shape[2]), BF16)
    n_unused = y_ref.shape[0] // tm - used
    _for_each(n_unused, lambda i: y_copy(used + i, 0, 0).start())
    _for_each(n_unused, lambda i: y_copy(used + i, 0, 0).wait())


def _experts(sorted_x, plan, w_eg, w_eu, w_ed):
    rows, d = sorted_x.shape
    _, _, de = w_eg.shape
    tm = EXPERT_TILE
    any_space = pl.BlockSpec(memory_space=pl.ANY)
    big = MACRO_TILES * tm
    scratch = 2 * 2 * big * d * 2 + 2 * 3 * d * de * 4 + 3 * d * de * 2
    temps = 3 * big * de * 4 + big * d * 4 + d * de * 4
    return pl.pallas_call(
        _experts_kernel,
        grid_spec=pltpu.PrefetchScalarGridSpec(
            num_scalar_prefetch=2,
            grid=(1,),
            in_specs=[any_space] * 4,
            out_specs=any_space,
            scratch_shapes=[pltpu.VMEM((2, big, d), BF16), pltpu.VMEM((2, big, d), BF16),
                            pltpu.VMEM((2, d, de), F32), pltpu.VMEM((2, d, de), F32),
                            pltpu.VMEM((2, de, d), F32),
                            pltpu.VMEM((d, de), BF16), pltpu.VMEM((d, de), BF16),
                            pltpu.VMEM((de, d), BF16),
                            pltpu.SemaphoreType.DMA((2,)), pltpu.SemaphoreType.DMA((2,)),
                            pltpu.SemaphoreType.DMA((2,))]),
        out_shape=jax.ShapeDtypeStruct((rows, d), BF16),
        compiler_params=_params(("arbitrary",), scratch + temps),
        name="experts",
    )(plan["expert_tiles"], plan["tiles_used"], sorted_x, w_eg, w_eu, w_ed)


def _combine_kernel(len_ref, src_ref, dst_ref, nchunk_ref, x1_ref, comb_ref, sel_ref, p_ref,
                    wpg_ref, wpp_ref,
                    lg_ref, lb_ref, y_ref, o_ref, loc_ref, wpgb_ref, wppb_ref, sems, *, alpha):
    b = pl.program_id(0)
    nb = pl.num_programs(0)
    slot = b % 2
    tb, d = x1_ref.shape
    _cast_once(wpg_ref, wpgb_ref)
    _cast_once(wpp_ref, wppb_ref)

    def start_block(blk, slot):
        def start_segment(e):
            n, local, sorted_at = _segment_rows(len_ref, src_ref, dst_ref, blk, e)

            @pl.when(n > 0)
            def _():
                pltpu.make_async_copy(y_ref.at[pl.ds(sorted_at, n)],
                                      loc_ref.at[slot, pl.ds(local, n)], sems.at[slot]).start()

        _for_each(N_EXPERTS, start_segment)

    @pl.when(b == 0)
    def _():
        loc_ref[...] = jnp.zeros_like(loc_ref)
        start_block(0, 0)

    @pl.when(b + 1 < nb)
    def _():
        start_block(b + 1, 1 - slot)

    n_rows = nchunk_ref[b] * CHUNK
    pltpu.make_async_copy(y_ref.at[pl.ds(0, n_rows)], loc_ref.at[slot, pl.ds(0, n_rows)],
                          sems.at[slot]).wait()

    sel = sel_ref[...]
    pos, pos_lo, pos_hi = _local_positions(sel)
    chosen = sel > 0
    comb = comb_ref[...]
    w_lo = jnp.sum(jnp.where(chosen & (pos == pos_lo), comb, 0.0), axis=1, keepdims=True)
    w_hi = jnp.sum(jnp.where(chosen & (pos == pos_hi), comb, 0.0), axis=1, keepdims=True)
    lo = pos_lo.astype(jnp.int32)
    hi = pos_hi.astype(jnp.int32)

    sorted_row = lax.broadcasted_iota(jnp.int32, (tb // ROW_SUBS, LOCAL_ROWS), 1)

    def branches(r):
        weights = (jnp.where(sorted_row == lo[r], w_lo[r], 0.0)
                   + jnp.where(sorted_row == hi[r], w_hi[r], 0.0)).astype(BF16)
        moe = jnp.dot(weights, loc_ref[slot], preferred_element_type=F32)
        gate = jnp.dot(x1_ref[r, :].astype(BF16), wpgb_ref[...], preferred_element_type=F32)
        emb = jnp.dot(p_ref[r, :].astype(BF16), wppb_ref[...], preferred_element_type=F32)
        return moe, gate, emb

    def finish(r, parts):
        moe, gate, emb = parts
        h = alpha * x1_ref[r, :] + moe + _sigmoid(gate) * emb
        o_ref[r, :] = _layer_norm(h, lg_ref[...], lb_ref[...])

    _skewed(_row_subs(tb), branches, finish)


def _combine(y, plan, x1, comb, sel, p2, w_pg, w_pp, ln_g, ln_b, alpha):
    t, d = x1.shape
    pd = p2.shape[1]
    tb = ROUTE_BLOCK
    row = lambda b, *_: (b, 0)
    blocks = (2 * (2 * tb * d * 4 + tb * LANES * (4 + 2) + tb * pd * 4)
              + (d * d + pd * d) * (4 + 2) + 2 * d * 4 + 2 * LOCAL_ROWS * d * 2)
    temps = tb * tb * 2 + 8 * tb * LANES * 4 + tb * LOCAL_ROWS * 10 + 5 * tb * d * 4
    return pl.pallas_call(
        functools.partial(_combine_kernel, alpha=alpha),
        grid_spec=pltpu.PrefetchScalarGridSpec(
            num_scalar_prefetch=4,
            grid=(t // tb,),
            in_specs=[pl.BlockSpec((tb, d), row), pl.BlockSpec((tb, LANES), row),
                      pl.BlockSpec((tb, LANES), row), pl.BlockSpec((tb, pd), row),
                      _resident((d, d)), _resident((pd, d)), _resident((1, d)), _resident((1, d)),
                      pl.BlockSpec(memory_space=pl.ANY)],
            out_specs=pl.BlockSpec((tb, d), row),
            scratch_shapes=[pltpu.VMEM((2, LOCAL_ROWS, d), BF16), pltpu.VMEM((d, d), BF16),
                            pltpu.VMEM((pd, d), BF16), pltpu.SemaphoreType.DMA((2,))]),
        out_shape=jax.ShapeDtypeStruct((t, d), F32),
        compiler_params=_params(("arbitrary",), blocks + temps),
        name="combine",
    )(plan["seg_len"], plan["seg_src"], plan["seg_dst"], plan["block_chunks"], x1, comb, sel, p2,
      w_pg, w_pp, ln_g, ln_b, y)


def kernel(x, p, w_in, w_pool, pool_scale, w_pu, w_au, w_o, ln1_g, ln1_b, w_rg, b_rg, w_re, b_re,
           w_eg, w_eu, w_ed, w_pg, w_pp, ln2_g, ln2_b):
    bsz, seq, d = x.shape
    depth = w_in.shape[0]
    t = bsz * seq
    de = w_eg.shape[-1]
    alpha = (2.0 * depth) ** 0.25
    assert w_rg.shape[2] == N_GROUPS and w_re.shape[1:] == (N_GROUPS, d, EXPERTS_PER_GROUP)
    assert w_in.shape[2] == 4 * d and w_pool.shape[1] == len(POOL_WINDOWS)
    assert t % ROUTE_BLOCK == 0 and N_EXPERTS + N_GROUPS <= LANES
    n_blocks = t // ROUTE_BLOCK
    n_tiles = -(-(2 * t + n_blocks * N_EXPERTS * (CHUNK - 1) + N_EXPERTS * (EXPERT_TILE - CHUNK))
                // EXPERT_TILE) + MACRO_TILES - 1

    x2 = x.reshape(t, d)
    for i in range(depth):
        u, q, k, v, gates = _in_hbm(*_proj(x2, w_in[i], tm=512))
        pool_out, = _in_hbm(_pool(u, w_pool[i], pool_scale[i][None, :], seq))
        attn_out, = _in_hbm(_attention(q, k, v, seq, tb=128))

        w_r = jnp.concatenate(
            [w_re[i].transpose(1, 0, 2).reshape(d, N_EXPERTS), w_rg[i]], axis=1)
        w_r = jnp.pad(w_r, ((0, 0), (0, LANES - w_r.shape[1])))
        b_r = jnp.pad(jnp.concatenate([b_re[i].reshape(-1), b_rg[i]]),
                      (0, LANES - N_GROUPS - N_EXPERTS))[None, :]

        x1, x1b, comb, sel, counts = _merge(
            x2, pool_out, attn_out, gates, *_in_hbm(w_pu[i], w_au[i], w_o[i]), ln1_g[i][None, :],
            ln1_b[i][None, :], w_r, b_r, alpha, tm=ROUTE_BLOCK)
        x1, x1b, comb, sel = _in_hbm(x1, x1b, comb, sel)

        plan = _routing_plan(counts)
        sorted_x, = _in_hbm(_scatter(x1b, sel, plan, n_tiles))
        y, = _in_hbm(_experts(sorted_x, plan, w_eg[i].reshape(N_EXPERTS, d, de),
                              w_eu[i].reshape(N_EXPERTS, d, de), w_ed[i].reshape(N_EXPERTS, de, d)))
        x2 = _combine(y, plan, x1, comb, sel, p[i].reshape(t, -1), *_in_hbm(w_pg[i], w_pp[i]),
                      ln2_g[i][None, :], ln2_b[i][None, :], alpha)
    return x2.reshape(bsz, seq, d)
```

```python
import functools
import math

import jax
import jax.numpy as jnp
from jax import lax
from jax.experimental import pallas as pl
from jax.experimental.pallas import tpu as pltpu

F32 = jnp.float32
BF16 = jnp.bfloat16

LANES = 128
POOL_WINDOWS = (2, 4, 8, 16)
HEAD_DIM = 64
N_GROUPS = 4
EXPERTS_PER_GROUP = 8
N_EXPERTS = N_GROUPS * EXPERTS_PER_GROUP
LN_EPS = 1e-5
GROUP_LANE0 = N_EXPERTS
NEG_BIG = -1e30
ROUTE_BLOCK = 512
CHUNK = 16
EXPERT_TILE = 256
MACRO_TILES = 4
SORT_ROWS = 256
ROW_SUBS = 2
LOCAL_ROWS = -(-(2 * ROUTE_BLOCK + N_EXPERTS * (CHUNK - 1)) // SORT_ROWS) * SORT_ROWS
ATTN_DEAD_LOG2 = -160.0
VMEM_CAP_BYTES = 56 * 1024 * 1024


def _params(sem, vmem_bytes):
    return pltpu.CompilerParams(
        dimension_semantics=sem, vmem_limit_bytes=min(int(vmem_bytes), VMEM_CAP_BYTES))


def _layer_norm(h, g, b):
    mu = jnp.mean(h, axis=-1, keepdims=True)
    c = h - mu
    var = jnp.mean(c * c, axis=-1, keepdims=True)
    return c * lax.rsqrt(var + LN_EPS) * g + b


def _sigmoid(z):
    return 1.0 / (1.0 + jnp.exp(-z))


def _row_subs(rows):
    return [slice(k * (rows // ROW_SUBS), (k + 1) * (rows // ROW_SUBS)) for k in range(ROW_SUBS)]


def _skewed(subs, first, second):
    out, pending = [], None
    for r in subs:
        mid = first(r)
        if pending is not None:
            out.append(second(*pending))
        pending = (r, mid)
    out.append(second(*pending))
    return out


def _in_hbm(*arrays):
    if not all(isinstance(a, jax.core.Tracer) for a in arrays):
        return list(arrays)
    return [pltpu.with_memory_space_constraint(a, pltpu.HBM) for a in arrays]


def _resident(shape):
    return pl.BlockSpec(shape, lambda *_: (0,) * len(shape), pipeline_mode=pl.Buffered(1))


def _cast_once(w_ref, wb_ref):
    @pl.when(pl.program_id(0) == 0)
    def _():
        wb_ref[...] = w_ref[...].astype(BF16)


def _proj_kernel(x_ref, w_ref, u_ref, q_ref, k_ref, v_ref, g_ref, wb_ref, *, width):
    q_scale = math.log2(math.e) / math.sqrt(HEAD_DIM)
    _cast_once(w_ref, wb_ref)
    xb = x_ref[...].astype(BF16)

    def mm(lo):
        return jnp.dot(xb, wb_ref[:, lo:lo + width], preferred_element_type=F32)

    u_ref[...] = mm(0)
    q_ref[...] = (mm(width) * q_scale).astype(BF16)
    k_ref[...] = mm(2 * width).astype(BF16)
    v_ref[...] = mm(3 * width).astype(BF16)
    for c in range(g_ref.shape[1] // width):
        g_ref[:, c * width:(c + 1) * width] = _sigmoid(mm((4 + c) * width)).astype(BF16)


def _proj(x2, w_in, tm):
    t, d = x2.shape
    n = w_in.shape[1]
    width = d // 2
    gate_w = n - 4 * width
    row = lambda i: (i, 0)
    blocks = (2 * (tm * d * 4 + tm * width * (4 + 3 * 2) + tm * gate_w * 2)
              + d * n * (4 + 2))
    temps = tm * d * 2 + 2 * tm * width * 4
    return pl.pallas_call(
        functools.partial(_proj_kernel, width=width),
        grid=(t // tm,),
        in_specs=[pl.BlockSpec((tm, d), row), _resident((d, n))],
        out_specs=[pl.BlockSpec((tm, width), row)] * 4 + [pl.BlockSpec((tm, gate_w), row)],
        out_shape=[jax.ShapeDtypeStruct((t, width), F32)]
        + [jax.ShapeDtypeStruct((t, width), BF16)] * 3
        + [jax.ShapeDtypeStruct((t, gate_w), BF16)],
        scratch_shapes=[pltpu.VMEM((d, n), BF16)],
        compiler_params=_params(("arbitrary",), blocks + temps),
        name="proj",
    )(x2, w_in)


def _pool_kernel(u_ref, wp_ref, sc_ref, o_ref):
    seq = u_ref.shape[0]
    gd = wp_ref.shape[1]
    row = lax.broadcasted_iota(jnp.int32, (seq, gd), 0)
    for g, w in enumerate(POOL_WINDOWS):
        cols = slice(g * gd, (g + 1) * gd)
        ug = u_ref[:, cols]
        s = ug
        sh = 1
        while sh < w:
            s = s + jnp.where(row >= sh, pltpu.roll(s, sh, axis=0), 0.0)
            sh *= 2
        cnt = jnp.minimum(row + 1, w).astype(F32)
        pooled = s / cnt - ug
        mixed = jnp.dot(pooled.astype(BF16), wp_ref[g].astype(BF16), preferred_element_type=F32)
        o_ref[:, cols] = (mixed * sc_ref[:, cols]).astype(BF16)


def _pool(u, w_pool, pool_scale, seq):
    t, width = u.shape
    g, gd, _ = w_pool.shape
    blocks = 2 * (seq * width * (4 + 2) + g * gd * gd * 4 + width * 4)
    temps = 6 * seq * gd * 4
    return pl.pallas_call(
        _pool_kernel,
        grid=(t // seq,),
        in_specs=[pl.BlockSpec((seq, width), lambda b: (b, 0)),
                  pl.BlockSpec((g, gd, gd), lambda b: (0, 0, 0)),
                  pl.BlockSpec((1, width), lambda b: (0, 0))],
        out_specs=pl.BlockSpec((seq, width), lambda b: (b, 0)),
        out_shape=jax.ShapeDtypeStruct((t, width), BF16),
        compiler_params=_params(("parallel",), blocks + temps),
        name="pool",
    )(u, w_pool, pool_scale)


def _attn_kernel(q_ref, k_ref, v_ref, o_ref, rem_ref, acc_ref, *, tb):
    qi = pl.program_id(1)
    pairs = q_ref.shape[1] // LANES
    col_blocks = [slice(p * LANES, (p + 1) * LANES) for p in range(pairs)]
    first_head = lax.broadcasted_iota(jnp.int32, (tb, LANES), 1) < HEAD_DIM
    r = lax.broadcasted_iota(jnp.int32, (2 * tb, 2 * tb), 0)
    c = lax.broadcasted_iota(jnp.int32, (2 * tb, 2 * tb), 1)
    same_head = (r >= tb) == (c >= tb)
    cum = -jnp.concatenate([(same_head & (r > c)).astype(BF16), same_head.astype(BF16)], axis=1)
    kcol = lax.broadcasted_iota(jnp.int32, (tb, 2 * tb), 1)
    kcol = jnp.where(kcol >= tb, kcol - tb, kcol)
    causal = kcol < lax.broadcasted_iota(jnp.int32, (tb, 2 * tb), 0)

    def stack_heads(blk):
        zero = jnp.zeros_like(blk)
        return jnp.concatenate(
            [jnp.where(first_head, blk, zero), jnp.where(first_head, zero, blk)], axis=0)

    def scores(j, rows, diagonal):
        ks = pl.multiple_of(j * tb, tb)
        zs = [lax.dot_general(q_ref[:rows, cols], stack_heads(k_ref[pl.ds(ks, tb), cols]),
                              (((1,), (1,)), ((), ())), preferred_element_type=F32)
              for cols in col_blocks]
        log_betas, sums = [], []
        for z in zs:
            softplus = jnp.maximum(z, 0.0) + jnp.log2(1.0 + jnp.exp2(-jnp.abs(z)))
            log_betas.append(z - softplus)
            if diagonal:
                softplus = jnp.where(causal, softplus, 0.0)
            sums.append(jnp.dot(softplus.astype(BF16), cum, preferred_element_type=F32))
        return log_betas, sums

    def weighted_values(j, cols, log_beta, later):
        ks = pl.multiple_of(j * tb, tb)
        a = jnp.exp2(log_beta + later)
        return jnp.dot(a.astype(BF16), stack_heads(v_ref[pl.ds(ks, tb), cols]),
                       preferred_element_type=F32)

    def first_step(with_previous):
        lb_d, sums_d = scores(qi, tb, diagonal=True)
        if with_previous:
            lb_p, sums_p = scores(qi - 1, tb, diagonal=False)
        for p, cols in enumerate(col_blocks):
            ks = pl.multiple_of(qi * tb, tb)
            a = jnp.where(causal, jnp.exp2(lb_d[p] + sums_d[p][:, :2 * tb]), 0.0)
            out = jnp.dot(a.astype(BF16), stack_heads(v_ref[pl.ds(ks, tb), cols]),
                          preferred_element_type=F32)
            total = sums_d[p][:, 2 * tb:]
            if with_previous:
                out = out + weighted_values(qi - 1, cols, lb_p[p], sums_p[p][:, :2 * tb] + total)
                total = total + sums_p[p][:, 2 * tb:]
            acc_ref[p] = out
            rem_ref[p] = total

    def step(j, rows):
        log_betas, sums = scores(j, rows, diagonal=False)
        for p, cols in enumerate(col_blocks):
            later = sums[p][:, :2 * tb] + rem_ref[p, :rows, :]
            acc_ref[p, :rows, :] += weighted_values(j, cols, log_betas[p], later)
            rem_ref[p, :rows, :] += sums[p][:, 2 * tb:]

    def live_rows():
        worst = functools.reduce(jnp.maximum, [rem_ref[p] for p in range(pairs)])
        live = jnp.max(worst, axis=1, keepdims=True) > ATTN_DEAD_LOG2
        row = lax.broadcasted_iota(jnp.int32, (tb, 1), 0)
        return jnp.max(jnp.where(live, row + 1, 0))

    @pl.when(qi == 0)
    def _():
        first_step(with_previous=False)

    @pl.when(qi > 0)
    def _():
        first_step(with_previous=True)

    row_counts = (tb, tb // 2, tb // 4)

    def cond(carry):
        j, n_live = carry
        return (j >= 0) & (n_live > 0)

    def body(carry):
        j, n_live = carry
        for rows, fewer in zip(row_counts, row_counts[1:] + (0,)):
            @pl.when((n_live > fewer) & (n_live <= rows))
            def _():
                step(j, rows)

        return j - 1, live_rows()

    lax.while_loop(cond, body, (qi - 2, live_rows()))
    for p, cols in enumerate(col_blocks):
        o_ref[:, cols] = acc_ref[p].astype(BF16)


def _attention(q, k, v, seq, tb):
    t, width = q.shape
    pairs = width // LANES
    blocks = 2 * (2 * tb * width * 2 + 2 * seq * width * 2) + pairs * tb * (2 * tb + LANES) * 4
    temps = pairs * 10 * tb * 2 * tb * 4 + 2 * tb * 4 * tb * 2
    qspec = pl.BlockSpec((tb, width), lambda b, i: (b * (seq // tb) + i, 0))
    kvspec = pl.BlockSpec((seq, width), lambda b, i: (b, 0))
    return pl.pallas_call(
        functools.partial(_attn_kernel, tb=tb),
        grid=(t // seq, seq // tb),
        in_specs=[qspec, kvspec, kvspec],
        out_specs=qspec,
        out_shape=jax.ShapeDtypeStruct((t, width), BF16),
        scratch_shapes=[pltpu.VMEM((pairs, tb, 2 * tb), F32), pltpu.VMEM((pairs, tb, LANES), F32)],
        compiler_params=_params(("parallel", "parallel"), blocks + temps),
        name="attn",
    )(q, k, v)


def _route(logits):
    lane = lax.broadcasted_iota(jnp.int32, logits.shape, 1)

    def first_max(vals):
        m = jnp.max(vals, axis=1, keepdims=True)
        idx = jnp.min(jnp.where(vals == m, lane, LANES), axis=1, keepdims=True)
        return m, idx

    is_group = (lane >= GROUP_LANE0) & (lane < GROUP_LANE0 + N_GROUPS)
    gm, g_lane = first_max(jnp.where(is_group, logits, NEG_BIG))
    g_prob = 1.0 / jnp.sum(jnp.where(is_group, jnp.exp(logits - gm), 0.0), axis=1, keepdims=True)
    lo = EXPERTS_PER_GROUP * (g_lane - GROUP_LANE0)
    in_group = jnp.where((lane >= lo) & (lane < lo + EXPERTS_PER_GROUP), logits, NEG_BIG)
    m1, i1 = first_max(in_group)
    m2, i2 = first_max(jnp.where(lane == i1, NEG_BIG, in_group))
    e21 = jnp.exp(m2 - m1)
    w1 = g_prob / (1.0 + e21)
    w2 = w1 * e21
    comb = jnp.where(lane == i1, w1, 0.0) + jnp.where(lane == i2, w2, 0.0)
    sel = ((lane == i1) | (lane == i2)).astype(F32)
    return comb, sel


def _merge_kernel(x_ref, po_ref, at_ref, g_ref, wpu_ref, wau_ref, wo_ref, lg_ref, lb_ref,
                  wr_ref, br_ref, x1_ref, x1b_ref, comb_ref, sel_ref, cnt_ref,
                  wpub_ref, waub_ref, wob_ref, wrb_ref, *, alpha):
    tm, d = x_ref.shape
    _cast_once(wpu_ref, wpub_ref)
    _cast_once(wau_ref, waub_ref)
    _cast_once(wo_ref, wob_ref)

    @pl.when(pl.program_id(0) == 0)
    def _():
        w_r = wr_ref[...]
        hi = w_r.astype(BF16)
        wrb_ref[:, :LANES] = hi
        wrb_ref[:, LANES:] = (w_r - hi.astype(F32)).astype(BF16)

    def mix(r):
        a = jnp.dot(po_ref[r, :], wpub_ref[...], preferred_element_type=F32)
        b = jnp.dot(at_ref[r, :], waub_ref[...], preferred_element_type=F32)
        merged = (g_ref[r, :d].astype(F32) * a + g_ref[r, d:].astype(F32) * b).astype(BF16)
        return alpha * x_ref[r, :] + jnp.dot(merged, wob_ref[...], preferred_element_type=F32)

    def norm_and_route(r, h):
        x1 = _layer_norm(h, lg_ref[...], lb_ref[...])
        xh = x1.astype(BF16)
        x1_ref[r, :] = x1
        x1b_ref[r, :] = xh
        xl = (x1 - xh.astype(F32)).astype(BF16)
        by_hi = jnp.dot(xh, wrb_ref[...], preferred_element_type=F32)
        by_lo = jnp.dot(xl, wrb_ref[:, :LANES], preferred_element_type=F32)
        comb, sel = _route(by_hi[:, :LANES] + by_hi[:, LANES:] + by_lo + br_ref[...])
        comb_ref[r, :] = comb
        sel_ref[r, :] = sel.astype(BF16)
        return jnp.sum(sel, axis=0, keepdims=True)

    cnt_ref[0] = sum(_skewed(_row_subs(tm), mix, norm_and_route))


def _merge(x2, pool_out, attn_out, gates, w_pu, w_au, w_o, ln_g, ln_b, w_r, b_r, alpha, tm):
    t, d = x2.shape
    width = pool_out.shape[1]
    row = lambda i: (i, 0)
    blocks = (2 * (tm * d * 4 + 2 * tm * width * 2 + tm * 2 * d * 2
                   + tm * d * 4 + tm * d * 2 + tm * LANES * (4 + 2) + LANES * 4)
              + (2 * width * d + d * d + d * LANES) * (4 + 2) + 2 * d * 4 + LANES * 4)
    temps = 5 * tm * d * 4
    return pl.pallas_call(
        functools.partial(_merge_kernel, alpha=alpha),
        grid=(t // tm,),
        in_specs=[pl.BlockSpec((tm, d), row), pl.BlockSpec((tm, width), row),
                  pl.BlockSpec((tm, width), row), pl.BlockSpec((tm, 2 * d), row),
                  _resident((width, d)), _resident((width, d)), _resident((d, d)),
                  _resident((1, d)), _resident((1, d)), _resident((d, LANES)),
                  _resident((1, LANES))],
        out_specs=[pl.BlockSpec((tm, d), row), pl.BlockSpec((tm, d), row),
                   pl.BlockSpec((tm, LANES), row), pl.BlockSpec((tm, LANES), row),
                   pl.BlockSpec((1, 1, LANES), lambda i: (i, 0, 0))],
        out_shape=[jax.ShapeDtypeStruct((t, d), F32), jax.ShapeDtypeStruct((t, d), BF16),
                   jax.ShapeDtypeStruct((t, LANES), F32), jax.ShapeDtypeStruct((t, LANES), BF16),
                   jax.ShapeDtypeStruct((t // tm, 1, LANES), F32)],
        scratch_shapes=[pltpu.VMEM((width, d), BF16), pltpu.VMEM((width, d), BF16),
                        pltpu.VMEM((d, d), BF16), pltpu.VMEM((d, 2 * LANES), BF16)],
        compiler_params=_params(("arbitrary",), blocks + temps),
        name="merge",
    )(x2, pool_out, attn_out, gates, w_pu, w_au, w_o, ln_g, ln_b, w_r, b_r)


def _routing_plan(counts):
    cnt = counts[:, 0, :N_EXPERTS].astype(jnp.int32)
    nch = (cnt + (CHUNK - 1)) // CHUNK
    local_end = jnp.cumsum(nch, axis=1)
    local_start = local_end - nch
    block_chunks = local_end[:, -1]
    before_block = jnp.cumsum(nch, axis=0) - nch
    expert_chunks = jnp.sum(nch, axis=0)
    chunks_per_tile = EXPERT_TILE // CHUNK
    expert_tiles = (expert_chunks + (chunks_per_tile - 1)) // chunks_per_tile
    tiles_end = jnp.cumsum(expert_tiles)
    region_start = (tiles_end - expert_tiles) * chunks_per_tile
    segment_dst = region_start[None, :] + before_block
    i32 = lambda a: a.astype(jnp.int32)
    return dict(seg_len=i32(nch), seg_src=i32(local_start), seg_dst=i32(segment_dst),
                block_chunks=i32(block_chunks),
                expert_tiles=i32(expert_tiles), tiles_used=i32(tiles_end[-1:]),
                pad_start=i32(region_start + expert_chunks),
                pad_count=i32(expert_tiles * chunks_per_tile - expert_chunks))


def _local_positions(sel):
    tb = sel.shape[0]
    earlier = (lax.broadcasted_iota(jnp.int32, (tb, tb), 1)
               < lax.broadcasted_iota(jnp.int32, (tb, tb), 0)).astype(BF16)
    rank = jnp.dot(earlier, sel, preferred_element_type=F32)
    cnt = jnp.sum(sel.astype(F32), axis=0, keepdims=True)
    nch = jnp.floor((cnt + (CHUNK - 1)) * (1.0 / CHUNK))
    lower = (lax.broadcasted_iota(jnp.int32, (LANES, LANES), 0)
             < lax.broadcasted_iota(jnp.int32, (LANES, LANES), 1)).astype(BF16)
    start = CHUNK * jnp.dot(jnp.broadcast_to(nch, (8, LANES)).astype(BF16), lower,
                            preferred_element_type=F32)[0:1]
    pos = rank + start
    chosen = sel > 0
    pos_lo = jnp.min(jnp.where(chosen, pos, float(LOCAL_ROWS)), axis=1, keepdims=True)
    pos_hi = jnp.max(jnp.where(chosen, pos, -1.0), axis=1, keepdims=True)
    return pos, pos_lo, pos_hi


def _for_each(n, fn):
    lax.fori_loop(0, n, lambda c, carry: (fn(c), carry)[1], 0)


def _segment_rows(len_ref, src_ref, dst_ref, blk, e):
    n = len_ref[blk, e] * CHUNK
    src = pl.multiple_of(src_ref[blk, e] * CHUNK, CHUNK)
    dst = pl.multiple_of(dst_ref[blk, e] * CHUNK, CHUNK)
    return n, src, dst


def _scatter_kernel(len_ref, src_ref, dst_ref, nchunk_ref, pad_start_ref, pad_count_ref, used_ref,
                    x_ref, sel_ref, g_ref, loc_ref, zero_ref, sems):
    b = pl.program_id(0)
    last = pl.num_programs(0) - 1
    slot = b % 2
    tb = x_ref.shape[0]

    def start_block(blk, slot):
        def start_segment(e):
            n, src, dst = _segment_rows(len_ref, src_ref, dst_ref, blk, e)

            @pl.when(n > 0)
            def _():
                pltpu.make_async_copy(loc_ref.at[slot, pl.ds(src, n)], g_ref.at[pl.ds(dst, n)],
                                      sems.at[slot]).start()

        _for_each(N_EXPERTS, start_segment)

    def wait_block(blk, slot):
        n = nchunk_ref[blk] * CHUNK
        pltpu.make_async_copy(loc_ref.at[slot, pl.ds(0, n)], g_ref.at[pl.ds(0, n)],
                              sems.at[slot]).wait()

    def pad_copy(e):
        n = pad_count_ref[e] * CHUNK
        dst = pl.multiple_of(pad_start_ref[e] * CHUNK, CHUNK)
        return n, pltpu.make_async_copy(zero_ref.at[pl.ds(0, n)], g_ref.at[pl.ds(dst, n)],
                                        sems.at[2])

    def unused_tile_copy(i):
        dst = pl.multiple_of(i * EXPERT_TILE, EXPERT_TILE)
        return pltpu.make_async_copy(zero_ref, g_ref.at[pl.ds(dst, EXPERT_TILE)], sems.at[2])

    @pl.when(b >= 2)
    def _():
        wait_block(b - 2, slot)

    _, pos_lo, pos_hi = _local_positions(sel_ref[...])
    lo = pos_lo.astype(jnp.int32)
    hi = pos_hi.astype(jnp.int32)
    x = x_ref[...]

    r = lax.broadcasted_iota(jnp.int32, (tb, LOCAL_ROWS), 1)
    perm = ((r == lo) | (r == hi)).astype(BF16)
    rows = lax.dot_general(perm, x, (((0,), (0,)), ((), ())), preferred_element_type=F32)
    loc_ref[slot] = rows.astype(BF16)
    start_block(b, slot)

    @pl.when(b == last)
    def _():
        zero_ref[...] = jnp.zeros_like(zero_ref)
        n_unused = g_ref.shape[0] // EXPERT_TILE - used_ref[0]

        def each_pad(act):
            def one(e):
                n, copy = pad_copy(e)

                @pl.when(n > 0)
                def _():
                    act(copy)

            _for_each(N_EXPERTS, one)

        each_pad(lambda copy: copy.start())
        _for_each(n_unused, lambda i: unused_tile_copy(used_ref[0] + i).start())
        each_pad(lambda copy: copy.wait())
        _for_each(n_unused, lambda i: unused_tile_copy(used_ref[0] + i).wait())

        @pl.when(b >= 1)
        def _():
            wait_block(b - 1, 1 - slot)

        wait_block(b, slot)


def _scatter(x1b, sel, plan, n_tiles):
    t, d = x1b.shape
    tb = ROUTE_BLOCK
    row = lambda b, *_: (b, 0)
    blocks = 2 * (tb * d * 2 + tb * LANES * 2) + 2 * LOCAL_ROWS * d * 2 + EXPERT_TILE * d * 2
    temps = tb * tb * 2 + 6 * tb * LANES * 4 + tb * LOCAL_ROWS * 6 + LOCAL_ROWS * d * 6
    return pl.pallas_call(
        _scatter_kernel,
        grid_spec=pltpu.PrefetchScalarGridSpec(
            num_scalar_prefetch=7,
            grid=(t // tb,),
            in_specs=[pl.BlockSpec((tb, d), row), pl.BlockSpec((tb, LANES), row)],
            out_specs=pl.BlockSpec(memory_space=pl.ANY),
            scratch_shapes=[pltpu.VMEM((2, LOCAL_ROWS, d), BF16),
                            pltpu.VMEM((EXPERT_TILE, d), BF16), pltpu.SemaphoreType.DMA((3,))]),
        out_shape=jax.ShapeDtypeStruct((n_tiles * EXPERT_TILE, d), BF16),
        compiler_params=_params(("arbitrary",), blocks + temps),
        name="scatter",
    )(plan["seg_len"], plan["seg_src"], plan["seg_dst"], plan["block_chunks"], plan["pad_start"],
      plan["pad_count"], plan["tiles_used"], x1b, sel)


def _experts_kernel(ntile_ref, used_ref, g_ref, wg_ref, wu_ref, wd_ref, y_ref,
                    x_buf, y_buf, wg_buf, wu_buf, wd_buf, wgb_ref, wub_ref, wdb_ref,
                    x_sems, y_sems, w_sems):
    n_experts = wg_ref.shape[0]
    tm = EXPERT_TILE
    used = used_ref[0]

    def weight_copies(e, slot):
        return [pltpu.make_async_copy(src.at[e], dst.at[slot], w_sems.at[slot])
                for src, dst in ((wg_ref, wg_buf), (wu_ref, wu_buf), (wd_ref, wd_buf))]

    def x_copy(t, slot):
        rows = pl.ds(pl.multiple_of(t * tm, tm), MACRO_TILES * tm)
        return pltpu.make_async_copy(g_ref.at[rows], x_buf.at[slot], x_sems.at[slot])

    def y_copy(t, j, slot):
        src = pl.ds(pl.multiple_of(j * tm, tm), tm)
        dst = pl.ds(pl.multiple_of((t + j) * tm, tm), tm)
        return pltpu.make_async_copy(y_buf.at[slot, src], y_ref.at[dst], y_sems.at[slot])

    def mlp(slot, rows):
        x = x_buf[slot, :rows, :]
        hg = jnp.dot(x, wgb_ref[...], preferred_element_type=F32)
        hu = jnp.dot(x, wub_ref[...], preferred_element_type=F32)
        h = hg * _sigmoid(hg) * hu
        y_buf[slot, :rows, :] = jnp.dot(h.astype(BF16), wdb_ref[...],
                                        preferred_element_type=F32).astype(BF16)

    for c in weight_copies(0, 0):
        c.start()
    x_copy(0, 0).start()

    def run_expert(e, carry):
        wslot = e % 2
        for c in weight_copies(e, wslot):
            c.wait()

        @pl.when(e + 1 < n_experts)
        def _():
            for c in weight_copies(e + 1, 1 - wslot):
                c.start()

        @pl.when(ntile_ref[e] > 0)
        def _():
            wgb_ref[...] = wg_buf[wslot].astype(BF16)
            wub_ref[...] = wu_buf[wslot].astype(BF16)
            wdb_ref[...] = wd_buf[wslot].astype(BF16)

        def run_macro(m, carry):
            t, step, k1, t1, k2, t2 = carry
            k = jnp.minimum(MACRO_TILES, ntile_ref[e] - m * MACRO_TILES)
            slot = step % 2
            x_copy(t, slot).wait()

            @pl.when(t + k < used)
            def _():
                x_copy(t + k, 1 - slot).start()

            _for_each(k2, lambda j: y_copy(t2, j, slot).wait())
            for tiles in range(1, MACRO_TILES + 1):
                @pl.when(k == tiles)
                def _():
                    mlp(slot, tiles * tm)

            _for_each(k, lambda j: y_copy(t, j, slot).start())
            return t + k, step + 1, k, t, k1, t1

        n_macro = (ntile_ref[e] + (MACRO_TILES - 1)) // MACRO_TILES
        return lax.fori_loop(0, n_macro, run_macro, carry)

    zero = jnp.int32(0)
    _, step, k1, t1, k2, t2 = lax.fori_loop(0, n_experts, run_expert, (zero,) * 6)
    _for_each(k2, lambda j: y_copy(t2, j, step % 2).wait())
    _for_each(k1, lambda j: y_copy(t1, j, (step + 1) % 2).wait())
    y_buf[0, :tm, :] = jnp.zeros((tm, y_buf.shape[2]), BF16)
    n_unused = y_ref.shape[0] // tm - used
    _for_each(n_unused, lambda i: y_copy(used + i, 0, 0).start())
    _for_each(n_unused, lambda i: y_copy(used + i, 0, 0).wait())


def _experts(sorted_x, plan, w_eg, w_eu, w_ed):
    rows, d = sorted_x.shape
    _, _, de = w_eg.shape
    tm = EXPERT_TILE
    any_space = pl.BlockSpec(memory_space=pl.ANY)
    big = MACRO_TILES * tm
    scratch = 2 * 2 * big * d * 2 + 2 * 3 * d * de * 4 + 3 * d * de * 2
    temps = 3 * big * de * 4 + big * d * 4 + d * de * 4
    return pl.pallas_call(
        _experts_kernel,
        grid_spec=pltpu.PrefetchScalarGridSpec(
            num_scalar_prefetch=2,
            grid=(1,),
            in_specs=[any_space] * 4,
            out_specs=any_space,
            scratch_shapes=[pltpu.VMEM((2, big, d), BF16), pltpu.VMEM((2, big, d), BF16),
                            pltpu.VMEM((2, d, de), F32), pltpu.VMEM((2, d, de), F32),
                            pltpu.VMEM((2, de, d), F32),
                            pltpu.VMEM((d, de), BF16), pltpu.VMEM((d, de), BF16),
                            pltpu.VMEM((de, d), BF16),
                            pltpu.SemaphoreType.DMA((2,)), pltpu.SemaphoreType.DMA((2,)),
                            pltpu.SemaphoreType.DMA((2,))]),
        out_shape=jax.ShapeDtypeStruct((rows, d), BF16),
        compiler_params=_params(("arbitrary",), scratch + temps),
        name="experts",
    )(plan["expert_tiles"], plan["tiles_used"], sorted_x, w_eg, w_eu, w_ed)


def _combine_kernel(len_ref, src_ref, dst_ref, nchunk_ref, x1_ref, comb_ref, sel_ref, p_ref,
                    wpg_ref, wpp_ref,
                    lg_ref, lb_ref, y_ref, o_ref, loc_ref, wpgb_ref, wppb_ref, sems, *, alpha):
    b = pl.program_id(0)
    nb = pl.num_programs(0)
    slot = b % 2
    tb, d = x1_ref.shape
    _cast_once(wpg_ref, wpgb_ref)
    _cast_once(wpp_ref, wppb_ref)

    def start_block(blk, slot):
        def start_segment(e):
            n, local, sorted_at = _segment_rows(len_ref, src_ref, dst_ref, blk, e)

            @pl.when(n > 0)
            def _():
                pltpu.make_async_copy(y_ref.at[pl.ds(sorted_at, n)],
                                      loc_ref.at[slot, pl.ds(local, n)], sems.at[slot]).start()

        _for_each(N_EXPERTS, start_segment)

    @pl.when(b == 0)
    def _():
        loc_ref[...] = jnp.zeros_like(loc_ref)
        start_block(0, 0)

    @pl.when(b + 1 < nb)
    def _():
        start_block(b + 1, 1 - slot)

    n_rows = nchunk_ref[b] * CHUNK
    pltpu.make_async_copy(y_ref.at[pl.ds(0, n_rows)], loc_ref.at[slot, pl.ds(0, n_rows)],
                          sems.at[slot]).wait()

    sel = sel_ref[...]
    pos, pos_lo, pos_hi = _local_positions(sel)
    chosen = sel > 0
    comb = comb_ref[...]
    w_lo = jnp.sum(jnp.where(chosen & (pos == pos_lo), comb, 0.0), axis=1, keepdims=True)
    w_hi = jnp.sum(jnp.where(chosen & (pos == pos_hi), comb, 0.0), axis=1, keepdims=True)
    lo = pos_lo.astype(jnp.int32)
    hi = pos_hi.astype(jnp.int32)

    sorted_row = lax.broadcasted_iota(jnp.int32, (tb // ROW_SUBS, LOCAL_ROWS), 1)

    def branches(r):
        weights = (jnp.where(sorted_row == lo[r], w_lo[r], 0.0)
                   + jnp.where(sorted_row == hi[r], w_hi[r], 0.0)).astype(BF16)
        moe = jnp.dot(weights, loc_ref[slot], preferred_element_type=F32)
        gate = jnp.dot(x1_ref[r, :].astype(BF16), wpgb_ref[...], preferred_element_type=F32)
        emb = jnp.dot(p_ref[r, :].astype(BF16), wppb_ref[...], preferred_element_type=F32)
        return moe, gate, emb

    def finish(r, parts):
        moe, gate, emb = parts
        h = alpha * x1_ref[r, :] + moe + _sigmoid(gate) * emb
        o_ref[r, :] = _layer_norm(h, lg_ref[...], lb_ref[...])

    _skewed(_row_subs(tb), branches, finish)


def _combine(y, plan, x1, comb, sel, p2, w_pg, w_pp, ln_g, ln_b, alpha):
    t, d = x1.shape
    pd = p2.shape[1]
    tb = ROUTE_BLOCK
    row = lambda b, *_: (b, 0)
    blocks = (2 * (2 * tb * d * 4 + tb * LANES * (4 + 2) + tb * pd * 4)
              + (d * d + pd * d) * (4 + 2) + 2 * d * 4 + 2 * LOCAL_ROWS * d * 2)
    temps = tb * tb * 2 + 8 * tb * LANES * 4 + tb * LOCAL_ROWS * 10 + 5 * tb * d * 4
    return pl.pallas_call(
        functools.partial(_combine_kernel, alpha=alpha),
        grid_spec=pltpu.PrefetchScalarGridSpec(
            num_scalar_prefetch=4,
            grid=(t // tb,),
            in_specs=[pl.BlockSpec((tb, d), row), pl.BlockSpec((tb, LANES), row),
                      pl.BlockSpec((tb, LANES), row), pl.BlockSpec((tb, pd), row),
                      _resident((d, d)), _resident((pd, d)), _resident((1, d)), _resident((1, d)),
                      pl.BlockSpec(memory_space=pl.ANY)],
            out_specs=pl.BlockSpec((tb, d), row),
            scratch_shapes=[pltpu.VMEM((2, LOCAL_ROWS, d), BF16), pltpu.VMEM((d, d), BF16),
                            pltpu.VMEM((pd, d), BF16), pltpu.SemaphoreType.DMA((2,))]),
        out_shape=jax.ShapeDtypeStruct((t, d), F32),
        compiler_params=_params(("arbitrary",), blocks + temps),
        name="combine",
    )(plan["seg_len"], plan["seg_src"], plan["seg_dst"], plan["block_chunks"], x1, comb, sel, p2,
      w_pg, w_pp, ln_g, ln_b, y)


def kernel(x, p, w_in, w_pool, pool_scale, w_pu, w_au, w_o, ln1_g, ln1_b, w_rg, b_rg, w_re, b_re,
           w_eg, w_eu, w_ed, w_pg, w_pp, ln2_g, ln2_b):
    bsz, seq, d = x.shape
    depth = w_in.shape[0]
    t = bsz * seq
    de = w_eg.shape[-1]
    alpha = (2.0 * depth) ** 0.25
    assert w_rg.shape[2] == N_GROUPS and w_re.shape[1:] == (N_GROUPS, d, EXPERTS_PER_GROUP)
    assert w_in.shape[2] == 4 * d and w_pool.shape[1] == len(POOL_WINDOWS)
    assert t % ROUTE_BLOCK == 0 and N_EXPERTS + N_GROUPS <= LANES
    n_blocks = t // ROUTE_BLOCK
    n_tiles = -(-(2 * t + n_blocks * N_EXPERTS * (CHUNK - 1) + N_EXPERTS * (EXPERT_TILE - CHUNK))
                // EXPERT_TILE) + MACRO_TILES - 1

    x2 = x.reshape(t, d)
    for i in range(depth):
        u, q, k, v, gates = _in_hbm(*_proj(x2, w_in[i], tm=512))
        pool_out, = _in_hbm(_pool(u, w_pool[i], pool_scale[i][None, :], seq))
        attn_out, = _in_hbm(_attention(q, k, v, seq, tb=128))

        w_r = jnp.concatenate(
            [w_re[i].transpose(1, 0, 2).reshape(d, N_EXPERTS), w_rg[i]], axis=1)
        w_r = jnp.pad(w_r, ((0, 0), (0, LANES - w_r.shape[1])))
        b_r = jnp.pad(jnp.concatenate([b_re[i].reshape(-1), b_rg[i]]),
                      (0, LANES - N_GROUPS - N_EXPERTS))[None, :]

        x1, x1b, comb, sel, counts = _merge(
            x2, pool_out, attn_out, gates, *_in_hbm(w_pu[i], w_au[i], w_o[i]), ln1_g[i][None, :],
            ln1_b[i][None, :], w_r, b_r, alpha, tm=ROUTE_BLOCK)
        x1, x1b, comb, sel = _in_hbm(x1, x1b, comb, sel)

        plan = _routing_plan(counts)
        sorted_x, = _in_hbm(_scatter(x1b, sel, plan, n_tiles))
        y, = _in_hbm(_experts(sorted_x, plan, w_eg[i].reshape(N_EXPERTS, d, de),
                              w_eu[i].reshape(N_EXPERTS, d, de), w_ed[i].reshape(N_EXPERTS, de, d)))
        x2 = _combine(y, plan, x1, comb, sel, p[i].reshape(t, -1), *_in_hbm(w_pg[i], w_pp[i]),
                      ln2_g[i][None, :], ln2_b[i][None, :], alpha)
    return x2.reshape(bsz, seq, d)
```

```python
import functools
import math

import jax
import jax.numpy as jnp
from jax import lax
from jax.experimental import pallas as pl
from jax.experimental.pallas import tpu as pltpu

F32 = jnp.float32
BF16 = jnp.bfloat16

LANES = 128
POOL_WINDOWS = (2, 4, 8, 16)
HEAD_DIM = 64
N_GROUPS = 4
EXPERTS_PER_GROUP = 8
N_EXPERTS = N_GROUPS * EXPERTS_PER_GROUP
LN_EPS = 1e-5
GROUP_LANE0 = N_EXPERTS
NEG_BIG = -1e30
ROUTE_BLOCK = 512
CHUNK = 16
EXPERT_TILE = 256
MACRO_TILES = 4
SORT_ROWS = 256
ROW_SUBS = 2
LOCAL_ROWS = -(-(2 * ROUTE_BLOCK + N_EXPERTS * (CHUNK - 1)) // SORT_ROWS) * SORT_ROWS
ATTN_DEAD_LOG2 = -160.0
VMEM_CAP_BYTES = 56 * 1024 * 1024


def _params(sem, vmem_bytes):
    return pltpu.CompilerParams(
        dimension_semantics=sem, vmem_limit_bytes=min(int(vmem_bytes), VMEM_CAP_BYTES))


def _layer_norm(h, g, b):
    mu = jnp.mean(h, axis=-1, keepdims=True)
    c = h - mu
    var = jnp.mean(c * c, axis=-1, keepdims=True)
    return c * lax.rsqrt(var + LN_EPS) * g + b


def _sigmoid(z):
    return 1.0 / (1.0 + jnp.exp(-z))


def _row_subs(rows):
    return [slice(k * (rows // ROW_SUBS), (k + 1) * (rows // ROW_SUBS)) for k in range(ROW_SUBS)]


def _skewed(subs, first, second):
    out, pending = [], None
    for r in subs:
        mid = first(r)
        if pending is not None:
            out.append(second(*pending))
        pending = (r, mid)
    out.append(second(*pending))
    return out


def _in_hbm(*arrays):
    if not all(isinstance(a, jax.core.Tracer) for a in arrays):
        return list(arrays)
    return [pltpu.with_memory_space_constraint(a, pltpu.HBM) for a in arrays]


def _resident(shape):
    return pl.BlockSpec(shape, lambda *_: (0,) * len(shape), pipeline_mode=pl.Buffered(1))


def _cast_once(w_ref, wb_ref):
    @pl.when(pl.program_id(0) == 0)
    def _():
        wb_ref[...] = w_ref[...].astype(BF16)


def _proj_kernel(x_ref, w_ref, u_ref, q_ref, k_ref, v_ref, g_ref, wb_ref, *, width):
    q_scale = math.log2(math.e) / math.sqrt(HEAD_DIM)
    _cast_once(w_ref, wb_ref)
    xb = x_ref[...].astype(BF16)

    def mm(lo):
        return jnp.dot(xb, wb_ref[:, lo:lo + width], preferred_element_type=F32)

    u_ref[...] = mm(0)
    q_ref[...] = (mm(width) * q_scale).astype(BF16)
    k_ref[...] = mm(2 * width).astype(BF16)
    v_ref[...] = mm(3 * width).astype(BF16)
    for c in range(g_ref.shape[1] // width):
        g_ref[:, c * width:(c + 1) * width] = _sigmoid(mm((4 + c) * width)).astype(BF16)


def _proj(x2, w_in, tm):
    t, d = x2.shape
    n = w_in.shape[1]
    width = d // 2
    gate_w = n - 4 * width
    row = lambda i: (i, 0)
    blocks = (2 * (tm * d * 4 + tm * width * (4 + 3 * 2) + tm * gate_w * 2)
              + d * n * (4 + 2))
    temps = tm * d * 2 + 2 * tm * width * 4
    return pl.pallas_call(
        functools.partial(_proj_kernel, width=width),
        grid=(t // tm,),
        in_specs=[pl.BlockSpec((tm, d), row), _resident((d, n))],
        out_specs=[pl.BlockSpec((tm, width), row)] * 4 + [pl.BlockSpec((tm, gate_w), row)],
        out_shape=[jax.ShapeDtypeStruct((t, width), F32)]
        + [jax.ShapeDtypeStruct((t, width), BF16)] * 3
        + [jax.ShapeDtypeStruct((t, gate_w), BF16)],
        scratch_shapes=[pltpu.VMEM((d, n), BF16)],
        compiler_params=_params(("arbitrary",), blocks + temps),
        name="proj",
    )(x2, w_in)


def _pool_kernel(u_ref, wp_ref, sc_ref, o_ref):
    seq = u_ref.shape[0]
    gd = wp_ref.shape[1]
    row = lax.broadcasted_iota(jnp.int32, (seq, gd), 0)
    for g, w in enumerate(POOL_WINDOWS):
        cols = slice(g * gd, (g + 1) * gd)
        ug = u_ref[:, cols]
        s = ug
        sh = 1
        while sh < w:
            s = s + jnp.where(row >= sh, pltpu.roll(s, sh, axis=0), 0.0)
            sh *= 2
        cnt = jnp.minimum(row + 1, w).astype(F32)
        pooled = s / cnt - ug
        mixed = jnp.dot(pooled.astype(BF16), wp_ref[g].astype(BF16), preferred_element_type=F32)
        o_ref[:, cols] = (mixed * sc_ref[:, cols]).astype(BF16)


def _pool(u, w_pool, pool_scale, seq):
    t, width = u.shape
    g, gd, _ = w_pool.shape
    blocks = 2 * (seq * width * (4 + 2) + g * gd * gd * 4 + width * 4)
    temps = 6 * seq * gd * 4
    return pl.pallas_call(
        _pool_kernel,
        grid=(t // seq,),
        in_specs=[pl.BlockSpec((seq, width), lambda b: (b, 0)),
                  pl.BlockSpec((g, gd, gd), lambda b: (0, 0, 0)),
                  pl.BlockSpec((1, width), lambda b: (0, 0))],
        out_specs=pl.BlockSpec((seq, width), lambda b: (b, 0)),
        out_shape=jax.ShapeDtypeStruct((t, width), BF16),
        compiler_params=_params(("parallel",), blocks + temps),
        name="pool",
    )(u, w_pool, pool_scale)


def _attn_kernel(q_ref, k_ref, v_ref, o_ref, rem_ref, acc_ref, *, tb):
    qi = pl.program_id(1)
    pairs = q_ref.shape[1] // LANES
    col_blocks = [slice(p * LANES, (p + 1) * LANES) for p in range(pairs)]
    first_head = lax.broadcasted_iota(jnp.int32, (tb, LANES), 1) < HEAD_DIM
    r = lax.broadcasted_iota(jnp.int32, (2 * tb, 2 * tb), 0)
    c = lax.broadcasted_iota(jnp.int32, (2 * tb, 2 * tb), 1)
    same_head = (r >= tb) == (c >= tb)
    cum = -jnp.concatenate([(same_head & (r > c)).astype(BF16), same_head.astype(BF16)], axis=1)
    kcol = lax.broadcasted_iota(jnp.int32, (tb, 2 * tb), 1)
    kcol = jnp.where(kcol >= tb, kcol - tb, kcol)
    causal = kcol < lax.broadcasted_iota(jnp.int32, (tb, 2 * tb), 0)

    def stack_heads(blk):
        zero = jnp.zeros_like(blk)
        return jnp.concatenate(
            [jnp.where(first_head, blk, zero), jnp.where(first_head, zero, blk)], axis=0)

    def scores(j, rows, diagonal):
        ks = pl.multiple_of(j * tb, tb)
        zs = [lax.dot_general(q_ref[rows, cols], stack_heads(k_ref[pl.ds(ks, tb), cols]),
                              (((1,), (1,)), ((), ())), preferred_element_type=F32)
              for cols in col_blocks]
        log_betas, sums = [], []
        for z in zs:
            softplus = jnp.maximum(z, 0.0) + jnp.log2(1.0 + jnp.exp2(-jnp.abs(z)))
            log_betas.append(z - softplus)
            if diagonal:
                softplus = jnp.where(causal, softplus, 0.0)
            sums.append(jnp.dot(softplus.astype(BF16), cum, preferred_element_type=F32))
        return log_betas, sums

    def weighted_values(j, cols, log_beta, later):
        ks = pl.multiple_of(j * tb, tb)
        a = jnp.exp2(log_beta + later)
        return jnp.dot(a.astype(BF16), stack_heads(v_ref[pl.ds(ks, tb), cols]),
                       preferred_element_type=F32)

    def first_step(n_before):
        full, half = slice(0, tb), slice(0, tb // 2)
        lb_d, sums_d = scores(qi, full, diagonal=True)
        if n_before >= 1:
            lb_1, sums_1 = scores(qi - 1, full, diagonal=False)
        if n_before >= 2:
            lb_2, sums_2 = scores(qi - 2, half, diagonal=False)
        owed = []
        for p, cols in enumerate(col_blocks):
            ks = pl.multiple_of(qi * tb, tb)
            a = jnp.where(causal, jnp.exp2(lb_d[p] + sums_d[p][:, :2 * tb]), 0.0)
            out = jnp.dot(a.astype(BF16), stack_heads(v_ref[pl.ds(ks, tb), cols]),
                          preferred_element_type=F32)
            total = sums_d[p][:, 2 * tb:]
            if n_before >= 1:
                out = out + weighted_values(qi - 1, cols, lb_1[p], sums_1[p][:, :2 * tb] + total)
                total = total + sums_1[p][:, 2 * tb:]
            acc_ref[p] = out
            rem_ref[p] = total
            if n_before >= 2:
                owed.append(total[tb // 2:, :])
                later = sums_2[p][:, :2 * tb] + total[half, :]
                acc_ref[p, half, :] += weighted_values(qi - 2, cols, lb_2[p], later)
                rem_ref[p, half, :] += sums_2[p][:, 2 * tb:]
        if n_before < 2:
            return None
        return jnp.max(functools.reduce(jnp.maximum, owed)) > ATTN_DEAD_LOG2

    def step(j, rows):
        log_betas, sums = scores(j, rows, diagonal=False)
        for p, cols in enumerate(col_blocks):
            later = sums[p][:, :2 * tb] + rem_ref[p, rows, :]
            acc_ref[p, rows, :] += weighted_values(j, cols, log_betas[p], later)
            rem_ref[p, rows, :] += sums[p][:, 2 * tb:]

    def live_rows():
        worst = functools.reduce(jnp.maximum, [rem_ref[p] for p in range(pairs)])
        live = jnp.max(worst, axis=1, keepdims=True) > ATTN_DEAD_LOG2
        row = lax.broadcasted_iota(jnp.int32, (tb, 1), 0)
        return jnp.max(jnp.where(live, row + 1, 0))

    for n_before in (0, 1):
        @pl.when(qi == n_before)
        def _():
            first_step(n_before)

    @pl.when(qi >= 2)
    def _():
        second_half_owed = first_step(2)

        @pl.when(second_half_owed)
        def _():
            step(qi - 2, slice(tb // 2, tb))

    row_counts = (tb, tb // 2, tb // 4)

    def cond(carry):
        j, n_live = carry
        return (j >= 0) & (n_live > 0)

    def body(carry):
        j, n_live = carry
        for rows, fewer in zip(row_counts, row_counts[1:] + (0,)):
            @pl.when((n_live > fewer) & (n_live <= rows))
            def _():
                step(j, slice(0, rows))

        return j - 1, live_rows()

    lax.while_loop(cond, body, (qi - 3, live_rows()))
    for p, cols in enumerate(col_blocks):
        o_ref[:, cols] = acc_ref[p].astype(BF16)


def _attention(q, k, v, seq, tb):
    t, width = q.shape
    pairs = width // LANES
    blocks = 2 * (2 * tb * width * 2 + 2 * seq * width * 2) + pairs * tb * (2 * tb + LANES) * 4
    temps = pairs * 10 * tb * 2 * tb * 4 + 2 * tb * 4 * tb * 2
    qspec = pl.BlockSpec((tb, width), lambda b, i: (b * (seq // tb) + i, 0))
    kvspec = pl.BlockSpec((seq, width), lambda b, i: (b, 0))
    return pl.pallas_call(
        functools.partial(_attn_kernel, tb=tb),
        grid=(t // seq, seq // tb),
        in_specs=[qspec, kvspec, kvspec],
        out_specs=qspec,
        out_shape=jax.ShapeDtypeStruct((t, width), BF16),
        scratch_shapes=[pltpu.VMEM((pairs, tb, 2 * tb), F32), pltpu.VMEM((pairs, tb, LANES), F32)],
        compiler_params=_params(("parallel", "parallel"), blocks + temps),
        name="attn",
    )(q, k, v)


def _route(logits):
    lane = lax.broadcasted_iota(jnp.int32, logits.shape, 1)

    def first_max(vals):
        m = jnp.max(vals, axis=1, keepdims=True)
        idx = jnp.min(jnp.where(vals == m, lane, LANES), axis=1, keepdims=True)
        return m, idx

    is_group = (lane >= GROUP_LANE0) & (lane < GROUP_LANE0 + N_GROUPS)
    gm, g_lane = first_max(jnp.where(is_group, logits, NEG_BIG))
    g_prob = 1.0 / jnp.sum(jnp.where(is_group, jnp.exp(logits - gm), 0.0), axis=1, keepdims=True)
    lo = EXPERTS_PER_GROUP * (g_lane - GROUP_LANE0)
    in_group = jnp.where((lane >= lo) & (lane < lo + EXPERTS_PER_GROUP), logits, NEG_BIG)
    m1, i1 = first_max(in_group)
    m2, i2 = first_max(jnp.where(lane == i1, NEG_BIG, in_group))
    e21 = jnp.exp(m2 - m1)
    w1 = g_prob / (1.0 + e21)
    w2 = w1 * e21
    comb = jnp.where(lane == i1, w1, 0.0) + jnp.where(lane == i2, w2, 0.0)
    sel = ((lane == i1) | (lane == i2)).astype(F32)
    return comb, sel


def _merge_kernel(x_ref, po_ref, at_ref, g_ref, wpu_ref, wau_ref, wo_ref, lg_ref, lb_ref,
                  wr_ref, br_ref, x1_ref, x1b_ref, comb_ref, sel_ref, cnt_ref,
                  wpub_ref, waub_ref, wob_ref, wrb_ref, *, alpha):
    tm, d = x_ref.shape
    _cast_once(wpu_ref, wpub_ref)
    _cast_once(wau_ref, waub_ref)
    _cast_once(wo_ref, wob_ref)

    @pl.when(pl.program_id(0) == 0)
    def _():
        w_r = wr_ref[...]
        hi = w_r.astype(BF16)
        wrb_ref[:, :LANES] = hi
        wrb_ref[:, LANES:] = (w_r - hi.astype(F32)).astype(BF16)

    def mix(r):
        a = jnp.dot(po_ref[r, :], wpub_ref[...], preferred_element_type=F32)
        b = jnp.dot(at_ref[r, :], waub_ref[...], preferred_element_type=F32)
        merged = (g_ref[r, :d].astype(F32) * a + g_ref[r, d:].astype(F32) * b).astype(BF16)
        return alpha * x_ref[r, :] + jnp.dot(merged, wob_ref[...], preferred_element_type=F32)

    def norm_and_route(r, h):
        x1 = _layer_norm(h, lg_ref[...], lb_ref[...])
        xh = x1.astype(BF16)
        x1_ref[r, :] = x1
        x1b_ref[r, :] = xh
        xl = (x1 - xh.astype(F32)).astype(BF16)
        by_hi = jnp.dot(xh, wrb_ref[...], preferred_element_type=F32)
        by_lo = jnp.dot(xl, wrb_ref[:, :LANES], preferred_element_type=F32)
        comb, sel = _route(by_hi[:, :LANES] + by_hi[:, LANES:] + by_lo + br_ref[...])
        comb_ref[r, :] = comb
        sel_ref[r, :] = sel.astype(BF16)
        return jnp.sum(sel, axis=0, keepdims=True)

    cnt_ref[0] = sum(_skewed(_row_subs(tm), mix, norm_and_route))


def _merge(x2, pool_out, attn_out, gates, w_pu, w_au, w_o, ln_g, ln_b, w_r, b_r, alpha, tm):
    t, d = x2.shape
    width = pool_out.shape[1]
    row = lambda i: (i, 0)
    blocks = (2 * (tm * d * 4 + 2 * tm * width * 2 + tm * 2 * d * 2
                   + tm * d * 4 + tm * d * 2 + tm * LANES * (4 + 2) + LANES * 4)
              + (2 * width * d + d * d + d * LANES) * (4 + 2) + 2 * d * 4 + LANES * 4)
    temps = 5 * tm * d * 4
    return pl.pallas_call(
        functools.partial(_merge_kernel, alpha=alpha),
        grid=(t // tm,),
        in_specs=[pl.BlockSpec((tm, d), row), pl.BlockSpec((tm, width), row),
                  pl.BlockSpec((tm, width), row), pl.BlockSpec((tm, 2 * d), row),
                  _resident((width, d)), _resident((width, d)), _resident((d, d)),
                  _resident((1, d)), _resident((1, d)), _resident((d, LANES)),
                  _resident((1, LANES))],
        out_specs=[pl.BlockSpec((tm, d), row), pl.BlockSpec((tm, d), row),
                   pl.BlockSpec((tm, LANES), row), pl.BlockSpec((tm, LANES), row),
                   pl.BlockSpec((1, 1, LANES), lambda i: (i, 0, 0))],
        out_shape=[jax.ShapeDtypeStruct((t, d), F32), jax.ShapeDtypeStruct((t, d), BF16),
                   jax.ShapeDtypeStruct((t, LANES), F32), jax.ShapeDtypeStruct((t, LANES), BF16),
                   jax.ShapeDtypeStruct((t // tm, 1, LANES), F32)],
        scratch_shapes=[pltpu.VMEM((width, d), BF16), pltpu.VMEM((width, d), BF16),
                        pltpu.VMEM((d, d), BF16), pltpu.VMEM((d, 2 * LANES), BF16)],
        compiler_params=_params(("arbitrary",), blocks + temps),
        name="merge",
    )(x2, pool_out, attn_out, gates, w_pu, w_au, w_o, ln_g, ln_b, w_r, b_r)


def _routing_plan(counts):
    cnt = counts[:, 0, :N_EXPERTS].astype(jnp.int32)
    nch = (cnt + (CHUNK - 1)) // CHUNK
    local_end = jnp.cumsum(nch, axis=1)
    local_start = local_end - nch
    block_chunks = local_end[:, -1]
    before_block = jnp.cumsum(nch, axis=0) - nch
    expert_chunks = jnp.sum(nch, axis=0)
    chunks_per_tile = EXPERT_TILE // CHUNK
    expert_tiles = (expert_chunks + (chunks_per_tile - 1)) // chunks_per_tile
    tiles_end = jnp.cumsum(expert_tiles)
    region_start = (tiles_end - expert_tiles) * chunks_per_tile
    segment_dst = region_start[None, :] + before_block
    i32 = lambda a: a.astype(jnp.int32)
    return dict(seg_len=i32(nch), seg_src=i32(local_start), seg_dst=i32(segment_dst),
                block_chunks=i32(block_chunks),
                expert_tiles=i32(expert_tiles), tiles_used=i32(tiles_end[-1:]),
                pad_start=i32(region_start + expert_chunks),
                pad_count=i32(expert_tiles * chunks_per_tile - expert_chunks))


def _local_positions(sel):
    tb = sel.shape[0]
    earlier = (lax.broadcasted_iota(jnp.int32, (tb, tb), 1)
               < lax.broadcasted_iota(jnp.int32, (tb, tb), 0)).astype(BF16)
    rank = jnp.dot(earlier, sel, preferred_element_type=F32)
    cnt = jnp.sum(sel.astype(F32), axis=0, keepdims=True)
    nch = jnp.floor((cnt + (CHUNK - 1)) * (1.0 / CHUNK))
    lower = (lax.broadcasted_iota(jnp.int32, (LANES, LANES), 0)
             < lax.broadcasted_iota(jnp.int32, (LANES, LANES), 1)).astype(BF16)
    start = CHUNK * jnp.dot(jnp.broadcast_to(nch, (8, LANES)).astype(BF16), lower,
                            preferred_element_type=F32)[0:1]
    pos = rank + start
    chosen = sel > 0
    pos_lo = jnp.min(jnp.where(chosen, pos, float(LOCAL_ROWS)), axis=1, keepdims=True)
    pos_hi = jnp.max(jnp.where(chosen, pos, -1.0), axis=1, keepdims=True)
    return pos, pos_lo, pos_hi


def _for_each(n, fn):
    lax.fori_loop(0, n, lambda c, carry: (fn(c), carry)[1], 0)


def _segment_rows(len_ref, src_ref, dst_ref, blk, e):
    n = len_ref[blk, e] * CHUNK
    src = pl.multiple_of(src_ref[blk, e] * CHUNK, CHUNK)
    dst = pl.multiple_of(dst_ref[blk, e] * CHUNK, CHUNK)
    return n, src, dst


def _scatter_kernel(len_ref, src_ref, dst_ref, nchunk_ref, pad_start_ref, pad_count_ref, used_ref,
                    x_ref, sel_ref, g_ref, loc_ref, zero_ref, sems):
    b = pl.program_id(0)
    last = pl.num_programs(0) - 1
    slot = b % 2
    tb = x_ref.shape[0]

    def start_block(blk, slot):
        def start_segment(e):
            n, src, dst = _segment_rows(len_ref, src_ref, dst_ref, blk, e)

            @pl.when(n > 0)
            def _():
                pltpu.make_async_copy(loc_ref.at[slot, pl.ds(src, n)], g_ref.at[pl.ds(dst, n)],
                                      sems.at[slot]).start()

        _for_each(N_EXPERTS, start_segment)

    def wait_block(blk, slot):
        n = nchunk_ref[blk] * CHUNK
        pltpu.make_async_copy(loc_ref.at[slot, pl.ds(0, n)], g_ref.at[pl.ds(0, n)],
                              sems.at[slot]).wait()

    def pad_copy(e):
        n = pad_count_ref[e] * CHUNK
        dst = pl.multiple_of(pad_start_ref[e] * CHUNK, CHUNK)
        return n, pltpu.make_async_copy(zero_ref.at[pl.ds(0, n)], g_ref.at[pl.ds(dst, n)],
                                        sems.at[2])

    def unused_tile_copy(i):
        dst = pl.multiple_of(i * EXPERT_TILE, EXPERT_TILE)
        return pltpu.make_async_copy(zero_ref, g_ref.at[pl.ds(dst, EXPERT_TILE)], sems.at[2])

    @pl.when(b >= 2)
    def _():
        wait_block(b - 2, slot)

    _, pos_lo, pos_hi = _local_positions(sel_ref[...])
    lo = pos_lo.astype(jnp.int32)
    hi = pos_hi.astype(jnp.int32)
    x = x_ref[...]

    r = lax.broadcasted_iota(jnp.int32, (tb, LOCAL_ROWS), 1)
    perm = ((r == lo) | (r == hi)).astype(BF16)
    rows = lax.dot_general(perm, x, (((0,), (0,)), ((), ())), preferred_element_type=F32)
    loc_ref[slot] = rows.astype(BF16)
    start_block(b, slot)

    @pl.when(b == last)
    def _():
        zero_ref[...] = jnp.zeros_like(zero_ref)
        n_unused = g_ref.shape[0] // EXPERT_TILE - used_ref[0]

        def each_pad(act):
            def one(e):
                n, copy = pad_copy(e)

                @pl.when(n > 0)
                def _():
                    act(copy)

            _for_each(N_EXPERTS, one)

        each_pad(lambda copy: copy.start())
        _for_each(n_unused, lambda i: unused_tile_copy(used_ref[0] + i).start())
        each_pad(lambda copy: copy.wait())
        _for_each(n_unused, lambda i: unused_tile_copy(used_ref[0] + i).wait())

        @pl.when(b >= 1)
        def _():
            wait_block(b - 1, 1 - slot)

        wait_block(b, slot)


def _scatter(x1b, sel, plan, n_tiles):
    t, d = x1b.shape
    tb = ROUTE_BLOCK
    row = lambda b, *_: (b, 0)
    blocks = 2 * (tb * d * 2 + tb * LANES * 2) + 2 * LOCAL_ROWS * d * 2 + EXPERT_TILE * d * 2
    temps = tb * tb * 2 + 6 * tb * LANES * 4 + tb * LOCAL_ROWS * 6 + LOCAL_ROWS * d * 6
    return pl.pallas_call(
        _scatter_kernel,
        grid_spec=pltpu.PrefetchScalarGridSpec(
            num_scalar_prefetch=7,
            grid=(t // tb,),
            in_specs=[pl.BlockSpec((tb, d), row), pl.BlockSpec((tb, LANES), row)],
            out_specs=pl.BlockSpec(memory_space=pl.ANY),
            scratch_shapes=[pltpu.VMEM((2, LOCAL_ROWS, d), BF16),
                            pltpu.VMEM((EXPERT_TILE, d), BF16), pltpu.SemaphoreType.DMA((3,))]),
        out_shape=jax.ShapeDtypeStruct((n_tiles * EXPERT_TILE, d), BF16),
        compiler_params=_params(("arbitrary",), blocks + temps),
        name="scatter",
    )(plan["seg_len"], plan["seg_src"], plan["seg_dst"], plan["block_chunks"], plan["pad_start"],
      plan["pad_count"], plan["tiles_used"], x1b, sel)


def _experts_kernel(ntile_ref, used_ref, g_ref, wg_ref, wu_ref, wd_ref, y_ref,
                    x_buf, y_buf, wg_buf, wu_buf, wd_buf, wgb_ref, wub_ref, wdb_ref,
                    x_sems, y_sems, w_sems):
    n_experts = wg_ref.shape[0]
    tm = EXPERT_TILE
    used = used_ref[0]

    def weight_copies(e, slot):
        return [pltpu.make_async_copy(src.at[e], dst.at[slot], w_sems.at[slot])
                for src, dst in ((wg_ref, wg_buf), (wu_ref, wu_buf), (wd_ref, wd_buf))]

    def x_copy(t, slot):
        rows = pl.ds(pl.multiple_of(t * tm, tm), MACRO_TILES * tm)
        return pltpu.make_async_copy(g_ref.at[rows], x_buf.at[slot], x_sems.at[slot])

    def y_copy(t, j, slot):
        src = pl.ds(pl.multiple_of(j * tm, tm), tm)
        dst = pl.ds(pl.multiple_of((t + j) * tm, tm), tm)
        return pltpu.make_async_copy(y_buf.at[slot, src], y_ref.at[dst], y_sems.at[slot])

    def mlp(slot, rows):
        x = x_buf[slot, :rows, :]
        hg = jnp.dot(x, wgb_ref[...], preferred_element_type=F32)
        hu = jnp.dot(x, wub_ref[...], preferred_element_type=F32)
        h = hg * _sigmoid(hg) * hu
        y_buf[slot, :rows, :] = jnp.dot(h.astype(BF16), wdb_ref[...],
                                        preferred_element_type=F32).astype(BF16)

    for c in weight_copies(0, 0):
        c.start()
    x_copy(0, 0).start()

    def run_expert(e, carry):
        wslot = e % 2
        for c in weight_copies(e, wslot):
            c.wait()

        @pl.when(e + 1 < n_experts)
        def _():
            for c in weight_copies(e + 1, 1 - wslot):
                c.start()

        @pl.when(ntile_ref[e] > 0)
        def _():
            wgb_ref[...] = wg_buf[wslot].astype(BF16)
            wub_ref[...] = wu_buf[wslot].astype(BF16)
            wdb_ref[...] = wd_buf[wslot].astype(BF16)

        def run_macro(m, carry):
            t, step, k1, t1, k2, t2 = carry
            k = jnp.minimum(MACRO_TILES, ntile_ref[e] - m * MACRO_TILES)
            slot = step % 2
            x_copy(t, slot).wait()

            @pl.when(t + k < used)
            def _():
                x_copy(t + k, 1 - slot).start()

            _for_each(k2, lambda j: y_copy(t2, j, slot).wait())
            for tiles in range(1, MACRO_TILES + 1):
                @pl.when(k == tiles)
                def _():
                    mlp(slot, tiles * tm)

            _for_each(k, lambda j: y_copy(t, j, slot).start())
            return t + k, step + 1, k, t, k1, t1

        n_macro = (ntile_ref[e] + (MACRO_TILES - 1)) // MACRO_TILES
        return lax.fori_loop(0, n_macro, run_macro, carry)

    zero = jnp.int32(0)
    _, step, k1, t1, k2, t2 = lax.fori_loop(0, n_experts, run_expert, (zero,) * 6)
    _for_each(k2, lambda j: y_copy(t2, j, step % 2).wait())
    _for_each(k1, lambda j: y_copy(t1, j, (step + 1) % 2).wait())
    y_buf[0, :tm, :] = jnp.zeros((tm, y_buf.shape[2]), BF16)
    n_unused = y_ref.shape[0] // tm - used
    _for_each(n_unused, lambda i: y_copy(used + i, 0, 0).start())
    _for_each(n_unused, lambda i: y_copy(used + i, 0, 0).wait())


def _experts(sorted_x, plan, w_eg, w_eu, w_ed):
    rows, d = sorted_x.shape
    _, _, de = w_eg.shape
    tm = EXPERT_TILE
    any_space = pl.BlockSpec(memory_space=pl.ANY)
    big = MACRO_TILES * tm
    scratch = 2 * 2 * big * d * 2 + 2 * 3 * d * de * 4 + 3 * d * de * 2
    temps = 3 * big * de * 4 + big * d * 4 + d * de * 4
    return pl.pallas_call(
        _experts_kernel,
        grid_spec=pltpu.PrefetchScalarGridSpec(
            num_scalar_prefetch=2,
            grid=(1,),
            in_specs=[any_space] * 4,
            out_specs=any_space,
            scratch_shapes=[pltpu.VMEM((2, big, d), BF16), pltpu.VMEM((2, big, d), BF16),
                            pltpu.VMEM((2, d, de), F32), pltpu.VMEM((2, d, de), F32),
                            pltpu.VMEM((2, de, d), F32),
                            pltpu.VMEM((d, de), BF16), pltpu.VMEM((d, de), BF16),
                            pltpu.VMEM((de, d), BF16),
                            pltpu.SemaphoreType.DMA((2,)), pltpu.SemaphoreType.DMA((2,)),
                            pltpu.SemaphoreType.DMA((2,))]),
        out_shape=jax.ShapeDtypeStruct((rows, d), BF16),
        compiler_params=_params(("arbitrary",), scratch + temps),
        name="experts",
    )(plan["expert_tiles"], plan["tiles_used"], sorted_x, w_eg, w_eu, w_ed)


def _combine_kernel(len_ref, src_ref, dst_ref, nchunk_ref, x1_ref, comb_ref, sel_ref, p_ref,
                    wpg_ref, wpp_ref,
                    lg_ref, lb_ref, y_ref, o_ref, loc_ref, wpgb_ref, wppb_ref, sems, *, alpha):
    b = pl.program_id(0)
    nb = pl.num_programs(0)
    slot = b % 2
    tb, d = x1_ref.shape
    _cast_once(wpg_ref, wpgb_ref)
    _cast_once(wpp_ref, wppb_ref)

    def start_block(blk, slot):
        def start_segment(e):
            n, local, sorted_at = _segment_rows(len_ref, src_ref, dst_ref, blk, e)

            @pl.when(n > 0)
            def _():
                pltpu.make_async_copy(y_ref.at[pl.ds(sorted_at, n)],
                                      loc_ref.at[slot, pl.ds(local, n)], sems.at[slot]).start()

        _for_each(N_EXPERTS, start_segment)

    @pl.when(b == 0)
    def _():
        loc_ref[...] = jnp.zeros_like(loc_ref)
        start_block(0, 0)

    @pl.when(b + 1 < nb)
    def _():
        start_block(b + 1, 1 - slot)

    n_rows = nchunk_ref[b] * CHUNK
    pltpu.make_async_copy(y_ref.at[pl.ds(0, n_rows)], loc_ref.at[slot, pl.ds(0, n_rows)],
                          sems.at[slot]).wait()

    sel = sel_ref[...]
    pos, pos_lo, pos_hi = _local_positions(sel)
    chosen = sel > 0
    comb = comb_ref[...]
    w_lo = jnp.sum(jnp.where(chosen & (pos == pos_lo), comb, 0.0), axis=1, keepdims=True)
    w_hi = jnp.sum(jnp.where(chosen & (pos == pos_hi), comb, 0.0), axis=1, keepdims=True)
    lo = pos_lo.astype(jnp.int32)
    hi = pos_hi.astype(jnp.int32)

    sorted_row = lax.broadcasted_iota(jnp.int32, (tb // ROW_SUBS, LOCAL_ROWS), 1)

    def branches(r):
        weights = (jnp.where(sorted_row == lo[r], w_lo[r], 0.0)
                   + jnp.where(sorted_row == hi[r], w_hi[r], 0.0)).astype(BF16)
        moe = jnp.dot(weights, loc_ref[slot], preferred_element_type=F32)
        gate = jnp.dot(x1_ref[r, :].astype(BF16), wpgb_ref[...], preferred_element_type=F32)
        emb = jnp.dot(p_ref[r, :].astype(BF16), wppb_ref[...], preferred_element_type=F32)
        return moe, gate, emb

    def finish(r, parts):
        moe, gate, emb = parts
        h = alpha * x1_ref[r, :] + moe + _sigmoid(gate) * emb
        o_ref[r, :] = _layer_norm(h, lg_ref[...], lb_ref[...])

    _skewed(_row_subs(tb), branches, finish)


def _combine(y, plan, x1, comb, sel, p2, w_pg, w_pp, ln_g, ln_b, alpha):
    t, d = x1.shape
    pd = p2.shape[1]
    tb = ROUTE_BLOCK
    row = lambda b, *_: (b, 0)
    blocks = (2 * (2 * tb * d * 4 + tb * LANES * (4 + 2) + tb * pd * 4)
              + (d * d + pd * d) * (4 + 2) + 2 * d * 4 + 2 * LOCAL_ROWS * d * 2)
    temps = tb * tb * 2 + 8 * tb * LANES * 4 + tb * LOCAL_ROWS * 10 + 5 * tb * d * 4
    return pl.pallas_call(
        functools.partial(_combine_kernel, alpha=alpha),
        grid_spec=pltpu.PrefetchScalarGridSpec(
            num_scalar_prefetch=4,
            grid=(t // tb,),
            in_specs=[pl.BlockSpec((tb, d), row), pl.BlockSpec((tb, LANES), row),
                      pl.BlockSpec((tb, LANES), row), pl.BlockSpec((tb, pd), row),
                      _resident((d, d)), _resident((pd, d)), _resident((1, d)), _resident((1, d)),
                      pl.BlockSpec(memory_space=pl.ANY)],
            out_specs=pl.BlockSpec((tb, d), row),
            scratch_shapes=[pltpu.VMEM((2, LOCAL_ROWS, d), BF16), pltpu.VMEM((d, d), BF16),
                            pltpu.VMEM((pd, d), BF16), pltpu.SemaphoreType.DMA((2,))]),
        out_shape=jax.ShapeDtypeStruct((t, d), F32),
        compiler_params=_params(("arbitrary",), blocks + temps),
        name="combine",
    )(plan["seg_len"], plan["seg_src"], plan["seg_dst"], plan["block_chunks"], x1, comb, sel, p2,
      w_pg, w_pp, ln_g, ln_b, y)


def kernel(x, p, w_in, w_pool, pool_scale, w_pu, w_au, w_o, ln1_g, ln1_b, w_rg, b_rg, w_re, b_re,
           w_eg, w_eu, w_ed, w_pg, w_pp, ln2_g, ln2_b):
    bsz, seq, d = x.shape
    depth = w_in.shape[0]
    t = bsz * seq
    de = w_eg.shape[-1]
    alpha = (2.0 * depth) ** 0.25
    assert w_rg.shape[2] == N_GROUPS and w_re.shape[1:] == (N_GROUPS, d, EXPERTS_PER_GROUP)
    assert w_in.shape[2] == 4 * d and w_pool.shape[1] == len(POOL_WINDOWS)
    assert t % ROUTE_BLOCK == 0 and N_EXPERTS + N_GROUPS <= LANES
    n_blocks = t // ROUTE_BLOCK
    n_tiles = -(-(2 * t + n_blocks * N_EXPERTS * (CHUNK - 1) + N_EXPERTS * (EXPERT_TILE - CHUNK))
                // EXPERT_TILE) + MACRO_TILES - 1

    x2 = x.reshape(t, d)
    for i in range(depth):
        u, q, k, v, gates = _in_hbm(*_proj(x2, w_in[i], tm=512))
        pool_out, = _in_hbm(_pool(u, w_pool[i], pool_scale[i][None, :], seq))
        attn_out, = _in_hbm(_attention(q, k, v, seq, tb=128))

        w_r = jnp.concatenate(
            [w_re[i].transpose(1, 0, 2).reshape(d, N_EXPERTS), w_rg[i]], axis=1)
        w_r = jnp.pad(w_r, ((0, 0), (0, LANES - w_r.shape[1])))
        b_r = jnp.pad(jnp.concatenate([b_re[i].reshape(-1), b_rg[i]]),
                      (0, LANES - N_GROUPS - N_EXPERTS))[None, :]

        x1, x1b, comb, sel, counts = _merge(
            x2, pool_out, attn_out, gates, *_in_hbm(w_pu[i], w_au[i], w_o[i]), ln1_g[i][None, :],
            ln1_b[i][None, :], w_r, b_r, alpha, tm=ROUTE_BLOCK)
        x1, x1b, comb, sel = _in_hbm(x1, x1b, comb, sel)

        plan = _routing_plan(counts)
        sorted_x, = _in_hbm(_scatter(x1b, sel, plan, n_tiles))
        y, = _in_hbm(_experts(sorted_x, plan, w_eg[i].reshape(N_EXPERTS, d, de),
                              w_eu[i].reshape(N_EXPERTS, d, de), w_ed[i].reshape(N_EXPERTS, de, d)))
        x2 = _combine(y, plan, x1, comb, sel, p[i].reshape(t, -1), *_in_hbm(w_pg[i], w_pp[i]),
                      ln2_g[i][None, :], ln2_b[i][None, :], alpha)
    return x2.reshape(bsz, seq, d)
```

```python
import functools
import math

import jax
import jax.numpy as jnp
from jax import lax
from jax.experimental import pallas as pl
from jax.experimental.pallas import tpu as pltpu

F32 = jnp.float32
BF16 = jnp.bfloat16

LANES = 128
POOL_WINDOWS = (2, 4, 8, 16)
POOL_HALO = 16
HEAD_DIM = 64
N_GROUPS = 4
EXPERTS_PER_GROUP = 8
N_EXPERTS = N_GROUPS * EXPERTS_PER_GROUP
LN_EPS = 1e-5
GROUP_LANE0 = N_EXPERTS
NEG_BIG = -1e30
ROUTE_BLOCK = 512
CHUNK = 16
EXPERT_TILE = 256
MACRO_TILES = 4
SORT_ROWS = 256
ROW_SUBS = 2
LOCAL_ROWS = -(-(2 * ROUTE_BLOCK + N_EXPERTS * (CHUNK - 1)) // SORT_ROWS) * SORT_ROWS
ATTN_DEAD_LOG2 = -160.0
VMEM_CAP_BYTES = 56 * 1024 * 1024


def _params(sem, vmem_bytes):
    return pltpu.CompilerParams(
        dimension_semantics=sem, vmem_limit_bytes=min(int(vmem_bytes), VMEM_CAP_BYTES))


def _layer_norm(h, g, b):
    mu = jnp.mean(h, axis=-1, keepdims=True)
    c = h - mu
    var = jnp.mean(c * c, axis=-1, keepdims=True)
    return c * lax.rsqrt(var + LN_EPS) * g + b


def _sigmoid(z):
    return 1.0 / (1.0 + jnp.exp(-z))


def _row_subs(rows):
    return [slice(k * (rows // ROW_SUBS), (k + 1) * (rows // ROW_SUBS)) for k in range(ROW_SUBS)]


def _skewed(subs, first, second):
    out, pending = [], None
    for r in subs:
        mid = first(r)
        if pending is not None:
            out.append(second(*pending))
        pending = (r, mid)
    out.append(second(*pending))
    return out


def _in_hbm(*arrays):
    if not all(isinstance(a, jax.core.Tracer) for a in arrays):
        return list(arrays)
    return [pltpu.with_memory_space_constraint(a, pltpu.HBM) for a in arrays]


def _resident(shape):
    return pl.BlockSpec(shape, lambda *_: (0,) * len(shape), pipeline_mode=pl.Buffered(1))


def _cast_once(w_ref, wb_ref):
    @pl.when(pl.program_id(0) == 0)
    def _():
        wb_ref[...] = w_ref[...].astype(BF16)


def _proj_kernel(x_ref, w_ref, wp_ref, sc_ref, po_ref, q_ref, k_ref, v_ref, g_ref, wb_ref, halo_ref,
                 *, width, tiles_per_seq):
    q_scale = math.log2(math.e) / math.sqrt(HEAD_DIM)
    tile_in_seq = pl.program_id(0) % tiles_per_seq
    _cast_once(w_ref, wb_ref)
    xb = x_ref[...].astype(BF16)

    def mm(lo):
        return jnp.dot(xb, wb_ref[:, lo:lo + width], preferred_element_type=F32)

    @pl.when(tile_in_seq == 0)
    def _():
        halo_ref[...] = jnp.zeros_like(halo_ref)

    u = mm(0)
    tm = u.shape[0]
    gd = wp_ref.shape[1]
    pos = tile_in_seq * tm + lax.broadcasted_iota(jnp.int32, (tm, gd), 0)

    def pool_group(g, w):
        cols = slice(g * gd, (g + 1) * gd)
        ug = u[:, cols]
        s = jnp.concatenate([halo_ref[:, cols], ug], axis=0)
        sh = 1
        while sh < w:
            s = s + pltpu.roll(s, sh, axis=0)
            sh *= 2
        cnt = jnp.minimum(pos + 1, w).astype(F32)
        pooled = s[POOL_HALO:, :] / cnt - ug
        mixed = jnp.dot(pooled.astype(BF16), wp_ref[g].astype(BF16), preferred_element_type=F32)
        po_ref[:, cols] = (mixed * sc_ref[:, cols]).astype(BF16)

    def write_q():
        q_ref[...] = (mm(width) * q_scale).astype(BF16)

    def write_k():
        k_ref[...] = mm(2 * width).astype(BF16)

    def write_v():
        v_ref[...] = mm(3 * width).astype(BF16)

    def gate_chunk(c):
        g_ref[:, c * width:(c + 1) * width] = _sigmoid(mm((4 + c) * width)).astype(BF16)

    matmuls = [write_q, write_k, write_v] + [
        functools.partial(gate_chunk, c) for c in range(g_ref.shape[1] // width)]
    for n, matmul in enumerate(matmuls):
        matmul()
        if n < len(POOL_WINDOWS):
            pool_group(n, POOL_WINDOWS[n])
    halo_ref[...] = u[tm - POOL_HALO:, :]


def _proj(x2, w_in, w_pool, pool_scale, seq, tm):
    t, d = x2.shape
    n = w_in.shape[1]
    width = d // 2
    gate_w = n - 4 * width
    groups, gd, _ = w_pool.shape
    assert seq % tm == 0 and groups * gd == width and max(POOL_WINDOWS) <= POOL_HALO + 1
    row = lambda i: (i, 0)
    blocks = (2 * (tm * d * 4 + tm * width * 4 * 2 + tm * gate_w * 2)
              + d * n * (4 + 2) + groups * gd * gd * 4 + width * 4 + POOL_HALO * width * 4)
    temps = tm * d * 2 + 3 * tm * width * 4 + 6 * (tm + POOL_HALO) * gd * 4
    return pl.pallas_call(
        functools.partial(_proj_kernel, width=width, tiles_per_seq=seq // tm),
        grid=(t // tm,),
        in_specs=[pl.BlockSpec((tm, d), row), _resident((d, n)), _resident((groups, gd, gd)),
                  _resident((1, width))],
        out_specs=[pl.BlockSpec((tm, width), row)] * 4 + [pl.BlockSpec((tm, gate_w), row)],
        out_shape=[jax.ShapeDtypeStruct((t, width), BF16)] * 4
        + [jax.ShapeDtypeStruct((t, gate_w), BF16)],
        scratch_shapes=[pltpu.VMEM((d, n), BF16), pltpu.VMEM((POOL_HALO, width), F32)],
        compiler_params=_params(("arbitrary",), blocks + temps),
        name="proj",
    )(x2, w_in, w_pool, pool_scale)


def _attn_kernel(q_ref, k_ref, v_ref, o_ref, rem_ref, acc_ref, *, tb):
    qi = pl.program_id(1)
    pairs = q_ref.shape[1] // LANES
    col_blocks = [slice(p * LANES, (p + 1) * LANES) for p in range(pairs)]
    first_head = lax.broadcasted_iota(jnp.int32, (tb, LANES), 1) < HEAD_DIM
    r = lax.broadcasted_iota(jnp.int32, (2 * tb, 2 * tb), 0)
    c = lax.broadcasted_iota(jnp.int32, (2 * tb, 2 * tb), 1)
    same_head = (r >= tb) == (c >= tb)
    cum = -jnp.concatenate([(same_head & (r >= c)).astype(BF16), same_head.astype(BF16)], axis=1)
    kcol = lax.broadcasted_iota(jnp.int32, (tb, 2 * tb), 1)
    kcol = jnp.where(kcol >= tb, kcol - tb, kcol)
    causal = kcol < lax.broadcasted_iota(jnp.int32, (tb, 2 * tb), 0)

    def stack_heads(blk):
        zero = jnp.zeros_like(blk)
        return jnp.concatenate(
            [jnp.where(first_head, blk, zero), jnp.where(first_head, zero, blk)], axis=0)

    def scores(j, rows, diagonal):
        ks = pl.multiple_of(j * tb, tb)
        zs = [lax.dot_general(q_ref[rows, cols], stack_heads(k_ref[pl.ds(ks, tb), cols]),
                              (((1,), (1,)), ((), ())), preferred_element_type=F32)
              for cols in col_blocks]
        sums = []
        for z in zs:
            softplus = jnp.maximum(z, 0.0) + jnp.log2(1.0 + jnp.exp2(-jnp.abs(z)))
            if diagonal:
                softplus = jnp.where(causal, softplus, 0.0)
            sums.append(jnp.dot(softplus.astype(BF16), cum, preferred_element_type=F32))
        return zs, sums

    def weighted_values(j, cols, z, later):
        ks = pl.multiple_of(j * tb, tb)
        a = jnp.exp2(z + later)
        return jnp.dot(a.astype(BF16), stack_heads(v_ref[pl.ds(ks, tb), cols]),
                       preferred_element_type=F32)

    def first_step(n_before):
        full, half = slice(0, tb), slice(0, tb // 2)
        z_d, sums_d = scores(qi, full, diagonal=True)
        if n_before >= 1:
            z_1, sums_1 = scores(qi - 1, full, diagonal=False)
        if n_before >= 2:
            z_2, sums_2 = scores(qi - 2, half, diagonal=False)
        owed = []
        for p, cols in enumerate(col_blocks):
            ks = pl.multiple_of(qi * tb, tb)
            a = jnp.where(causal, jnp.exp2(z_d[p] + sums_d[p][:, :2 * tb]), 0.0)
            out = jnp.dot(a.astype(BF16), stack_heads(v_ref[pl.ds(ks, tb), cols]),
                          preferred_element_type=F32)
            total = sums_d[p][:, 2 * tb:]
            if n_before >= 1:
                out = out + weighted_values(qi - 1, cols, z_1[p], sums_1[p][:, :2 * tb] + total)
                total = total + sums_1[p][:, 2 * tb:]
            acc_ref[p] = out
            rem_ref[p] = total
            if n_before >= 2:
                owed.append(total[tb // 2:, :])
                later = sums_2[p][:, :2 * tb] + total[half, :]
                acc_ref[p, half, :] += weighted_values(qi - 2, cols, z_2[p], later)
                rem_ref[p, half, :] += sums_2[p][:, 2 * tb:]
        if n_before < 2:
            return None
        return jnp.max(functools.reduce(jnp.maximum, owed)) > ATTN_DEAD_LOG2

    def step(j, rows):
        zs, sums = scores(j, rows, diagonal=False)
        for p, cols in enumerate(col_blocks):
            later = sums[p][:, :2 * tb] + rem_ref[p, rows, :]
            acc_ref[p, rows, :] += weighted_values(j, cols, zs[p], later)
            rem_ref[p, rows, :] += sums[p][:, 2 * tb:]

    def live_rows():
        worst = functools.reduce(jnp.maximum, [rem_ref[p] for p in range(pairs)])
        live = jnp.max(worst, axis=1, keepdims=True) > ATTN_DEAD_LOG2
        row = lax.broadcasted_iota(jnp.int32, (tb, 1), 0)
        return jnp.max(jnp.where(live, row + 1, 0))

    for n_before in (0, 1):
        @pl.when(qi == n_before)
        def _():
            first_step(n_before)

    @pl.when(qi >= 2)
    def _():
        second_half_owed = first_step(2)

        @pl.when(second_half_owed)
        def _():
            step(qi - 2, slice(tb // 2, tb))

    row_counts = (tb, tb // 2, tb // 4)

    def cond(carry):
        j, n_live = carry
        return (j >= 0) & (n_live > 0)

    def body(carry):
        j, n_live = carry
        for rows, fewer in zip(row_counts, row_counts[1:] + (0,)):
            @pl.when((n_live > fewer) & (n_live <= rows))
            def _():
                step(j, slice(0, rows))

        return j - 1, live_rows()

    lax.while_loop(cond, body, (qi - 3, live_rows()))
    for p, cols in enumerate(col_blocks):
        o_ref[:, cols] = acc_ref[p].astype(BF16)


def _attention(q, k, v, seq, tb):
    t, width = q.shape
    pairs = width // LANES
    blocks = 2 * (2 * tb * width * 2 + 2 * seq * width * 2) + pairs * tb * (2 * tb + LANES) * 4
    temps = pairs * 10 * tb * 2 * tb * 4 + 2 * tb * 4 * tb * 2
    qspec = pl.BlockSpec((tb, width), lambda b, i: (b * (seq // tb) + i, 0))
    kvspec = pl.BlockSpec((seq, width), lambda b, i: (b, 0))
    return pl.pallas_call(
        functools.partial(_attn_kernel, tb=tb),
        grid=(t // seq, seq // tb),
        in_specs=[qspec, kvspec, kvspec],
        out_specs=qspec,
        out_shape=jax.ShapeDtypeStruct((t, width), BF16),
        scratch_shapes=[pltpu.VMEM((pairs, tb, 2 * tb), F32), pltpu.VMEM((pairs, tb, LANES), F32)],
        compiler_params=_params(("parallel", "parallel"), blocks + temps),
        name="attn",
    )(q, k, v)


def _route(logits):
    lane = lax.broadcasted_iota(jnp.int32, logits.shape, 1)

    def first_max(vals):
        m = jnp.max(vals, axis=1, keepdims=True)
        idx = jnp.min(jnp.where(vals == m, lane, LANES), axis=1, keepdims=True)
        return m, idx

    is_group = (lane >= GROUP_LANE0) & (lane < GROUP_LANE0 + N_GROUPS)
    gm, g_lane = first_max(jnp.where(is_group, logits, NEG_BIG))
    g_prob = 1.0 / jnp.sum(jnp.where(is_group, jnp.exp(logits - gm), 0.0), axis=1, keepdims=True)
    lo = EXPERTS_PER_GROUP * (g_lane - GROUP_LANE0)
    in_group = jnp.where((lane >= lo) & (lane < lo + EXPERTS_PER_GROUP), logits, NEG_BIG)
    m1, i1 = first_max(in_group)
    m2, i2 = first_max(jnp.where(lane == i1, NEG_BIG, in_group))
    e21 = jnp.exp(m2 - m1)
    w1 = g_prob / (1.0 + e21)
    w2 = w1 * e21
    comb = jnp.where(lane == i1, w1, 0.0) + jnp.where(lane == i2, w2, 0.0)
    sel = ((lane == i1) | (lane == i2)).astype(F32)
    return comb, sel


def _merge_kernel(x_ref, po_ref, at_ref, g_ref, wpu_ref, wau_ref, wo_ref, lg_ref, lb_ref,
                  wr_ref, br_ref, x1_ref, x1b_ref, comb_ref, sel_ref, cnt_ref,
                  wpub_ref, waub_ref, wob_ref, wrb_ref, *, alpha):
    tm, d = x_ref.shape
    _cast_once(wpu_ref, wpub_ref)
    _cast_once(wau_ref, waub_ref)
    _cast_once(wo_ref, wob_ref)

    @pl.when(pl.program_id(0) == 0)
    def _():
        w_r = wr_ref[...]
        hi = w_r.astype(BF16)
        wrb_ref[:, :LANES] = hi
        wrb_ref[:, LANES:] = (w_r - hi.astype(F32)).astype(BF16)

    def mix(r):
        a = jnp.dot(po_ref[r, :], wpub_ref[...], preferred_element_type=F32)
        b = jnp.dot(at_ref[r, :], waub_ref[...], preferred_element_type=F32)
        merged = (g_ref[r, :d].astype(F32) * a + g_ref[r, d:].astype(F32) * b).astype(BF16)
        return alpha * x_ref[r, :] + jnp.dot(merged, wob_ref[...], preferred_element_type=F32)

    def norm_and_route(r, h):
        x1 = _layer_norm(h, lg_ref[...], lb_ref[...])
        xh = x1.astype(BF16)
        x1_ref[r, :] = x1
        x1b_ref[r, :] = xh
        xl = (x1 - xh.astype(F32)).astype(BF16)
        by_hi = jnp.dot(xh, wrb_ref[...], preferred_element_type=F32)
        by_lo = jnp.dot(xl, wrb_ref[:, :LANES], preferred_element_type=F32)
        comb, sel = _route(by_hi[:, :LANES] + by_hi[:, LANES:] + by_lo + br_ref[...])
        comb_ref[r, :] = comb
        sel_ref[r, :] = sel.astype(BF16)
        return jnp.sum(sel, axis=0, keepdims=True)

    cnt_ref[0] = sum(_skewed(_row_subs(tm), mix, norm_and_route))


def _merge(x2, pool_out, attn_out, gates, w_pu, w_au, w_o, ln_g, ln_b, w_r, b_r, alpha, tm):
    t, d = x2.shape
    width = pool_out.shape[1]
    row = lambda i: (i, 0)
    blocks = (2 * (tm * d * 4 + 2 * tm * width * 2 + tm * 2 * d * 2
                   + tm * d * 4 + tm * d * 2 + tm * LANES * (4 + 2) + LANES * 4)
              + (2 * width * d + d * d + d * LANES) * (4 + 2) + 2 * d * 4 + LANES * 4)
    temps = 5 * tm * d * 4
    return pl.pallas_call(
        functools.partial(_merge_kernel, alpha=alpha),
        grid=(t // tm,),
        in_specs=[pl.BlockSpec((tm, d), row), pl.BlockSpec((tm, width), row),
                  pl.BlockSpec((tm, width), row), pl.BlockSpec((tm, 2 * d), row),
                  _resident((width, d)), _resident((width, d)), _resident((d, d)),
                  _resident((1, d)), _resident((1, d)), _resident((d, LANES)),
                  _resident((1, LANES))],
        out_specs=[pl.BlockSpec((tm, d), row), pl.BlockSpec((tm, d), row),
                   pl.BlockSpec((tm, LANES), row), pl.BlockSpec((tm, LANES), row),
                   pl.BlockSpec((1, 1, LANES), lambda i: (i, 0, 0))],
        out_shape=[jax.ShapeDtypeStruct((t, d), F32), jax.ShapeDtypeStruct((t, d), BF16),
                   jax.ShapeDtypeStruct((t, LANES), F32), jax.ShapeDtypeStruct((t, LANES), BF16),
                   jax.ShapeDtypeStruct((t // tm, 1, LANES), F32)],
        scratch_shapes=[pltpu.VMEM((width, d), BF16), pltpu.VMEM((width, d), BF16),
                        pltpu.VMEM((d, d), BF16), pltpu.VMEM((d, 2 * LANES), BF16)],
        compiler_params=_params(("arbitrary",), blocks + temps),
        name="merge",
    )(x2, pool_out, attn_out, gates, w_pu, w_au, w_o, ln_g, ln_b, w_r, b_r)


def _routing_plan(counts):
    cnt = counts[:, 0, :N_EXPERTS].astype(jnp.int32)
    nch = (cnt + (CHUNK - 1)) // CHUNK
    local_end = jnp.cumsum(nch, axis=1)
    local_start = local_end - nch
    block_chunks = local_end[:, -1]
    before_block = jnp.cumsum(nch, axis=0) - nch
    expert_chunks = jnp.sum(nch, axis=0)
    chunks_per_tile = EXPERT_TILE // CHUNK
    expert_tiles = (expert_chunks + (chunks_per_tile - 1)) // chunks_per_tile
    tiles_end = jnp.cumsum(expert_tiles)
    region_start = (tiles_end - expert_tiles) * chunks_per_tile
    segment_dst = region_start[None, :] + before_block
    i32 = lambda a: a.astype(jnp.int32)
    return dict(seg_len=i32(nch), seg_src=i32(local_start), seg_dst=i32(segment_dst),
                block_chunks=i32(block_chunks),
                expert_tiles=i32(expert_tiles), tiles_used=i32(tiles_end[-1:]),
                pad_start=i32(region_start + expert_chunks),
                pad_count=i32(expert_tiles * chunks_per_tile - expert_chunks))


def _local_positions(sel):
    tb = sel.shape[0]
    earlier = (lax.broadcasted_iota(jnp.int32, (tb, tb), 1)
               < lax.broadcasted_iota(jnp.int32, (tb, tb), 0)).astype(BF16)
    rank = jnp.dot(earlier, sel, preferred_element_type=F32)
    cnt = jnp.sum(sel.astype(F32), axis=0, keepdims=True)
    nch = jnp.floor((cnt + (CHUNK - 1)) * (1.0 / CHUNK))
    lower = (lax.broadcasted_iota(jnp.int32, (LANES, LANES), 0)
             < lax.broadcasted_iota(jnp.int32, (LANES, LANES), 1)).astype(BF16)
    start = CHUNK * jnp.dot(jnp.broadcast_to(nch, (8, LANES)).astype(BF16), lower,
                            preferred_element_type=F32)[0:1]
    pos = rank + start
    chosen = sel > 0
    pos_lo = jnp.min(jnp.where(chosen, pos, float(LOCAL_ROWS)), axis=1, keepdims=True)
    pos_hi = jnp.max(jnp.where(chosen, pos, -1.0), axis=1, keepdims=True)
    return pos, pos_lo, pos_hi


def _for_each(n, fn):
    lax.fori_loop(0, n, lambda c, carry: (fn(c), carry)[1], 0)


def _segment_rows(len_ref, src_ref, dst_ref, blk, e):
    n = len_ref[blk, e] * CHUNK
    src = pl.multiple_of(src_ref[blk, e] * CHUNK, CHUNK)
    dst = pl.multiple_of(dst_ref[blk, e] * CHUNK, CHUNK)
    return n, src, dst


def _scatter_kernel(len_ref, src_ref, dst_ref, nchunk_ref, pad_start_ref, pad_count_ref, used_ref,
                    x_ref, sel_ref, g_ref, loc_ref, zero_ref, sems):
    b = pl.program_id(0)
    last = pl.num_programs(0) - 1
    slot = b % 2
    tb = x_ref.shape[0]

    def start_block(blk, slot):
        def start_segment(e):
            n, src, dst = _segment_rows(len_ref, src_ref, dst_ref, blk, e)

            @pl.when(n > 0)
            def _():
                pltpu.make_async_copy(loc_ref.at[slot, pl.ds(src, n)], g_ref.at[pl.ds(dst, n)],
                                      sems.at[slot]).start()

        _for_each(N_EXPERTS, start_segment)

    def wait_block(blk, slot):
        n = nchunk_ref[blk] * CHUNK
        pltpu.make_async_copy(loc_ref.at[slot, pl.ds(0, n)], g_ref.at[pl.ds(0, n)],
                              sems.at[slot]).wait()

    def pad_copy(e):
        n = pad_count_ref[e] * CHUNK
        dst = pl.multiple_of(pad_start_ref[e] * CHUNK, CHUNK)
        return n, pltpu.make_async_copy(zero_ref.at[pl.ds(0, n)], g_ref.at[pl.ds(dst, n)],
                                        sems.at[2])

    def unused_tile_copy(i):
        dst = pl.multiple_of(i * EXPERT_TILE, EXPERT_TILE)
        return pltpu.make_async_copy(zero_ref, g_ref.at[pl.ds(dst, EXPERT_TILE)], sems.at[2])

    @pl.when(b >= 2)
    def _():
        wait_block(b - 2, slot)

    _, pos_lo, pos_hi = _local_positions(sel_ref[...])
    lo = pos_lo.astype(jnp.int32)
    hi = pos_hi.astype(jnp.int32)
    x = x_ref[...]

    r = lax.broadcasted_iota(jnp.int32, (tb, LOCAL_ROWS), 1)
    perm = ((r == lo) | (r == hi)).astype(BF16)
    rows = lax.dot_general(perm, x, (((0,), (0,)), ((), ())), preferred_element_type=F32)
    loc_ref[slot] = rows.astype(BF16)
    start_block(b, slot)

    @pl.when(b == last)
    def _():
        zero_ref[...] = jnp.zeros_like(zero_ref)
        n_unused = g_ref.shape[0] // EXPERT_TILE - used_ref[0]

        def each_pad(act):
            def one(e):
                n, copy = pad_copy(e)

                @pl.when(n > 0)
                def _():
                    act(copy)

            _for_each(N_EXPERTS, one)

        each_pad(lambda copy: copy.start())
        _for_each(n_unused, lambda i: unused_tile_copy(used_ref[0] + i).start())
        each_pad(lambda copy: copy.wait())
        _for_each(n_unused, lambda i: unused_tile_copy(used_ref[0] + i).wait())

        @pl.when(b >= 1)
        def _():
            wait_block(b - 1, 1 - slot)

        wait_block(b, slot)


def _scatter(x1b, sel, plan, n_tiles):
    t, d = x1b.shape
    tb = ROUTE_BLOCK
    row = lambda b, *_: (b, 0)
    blocks = 2 * (tb * d * 2 + tb * LANES * 2) + 2 * LOCAL_ROWS * d * 2 + EXPERT_TILE * d * 2
    temps = tb * tb * 2 + 6 * tb * LANES * 4 + tb * LOCAL_ROWS * 6 + LOCAL_ROWS * d * 6
    return pl.pallas_call(
        _scatter_kernel,
        grid_spec=pltpu.PrefetchScalarGridSpec(
            num_scalar_prefetch=7,
            grid=(t // tb,),
            in_specs=[pl.BlockSpec((tb, d), row), pl.BlockSpec((tb, LANES), row)],
            out_specs=pl.BlockSpec(memory_space=pl.ANY),
            scratch_shapes=[pltpu.VMEM((2, LOCAL_ROWS, d), BF16),
                            pltpu.VMEM((EXPERT_TILE, d), BF16), pltpu.SemaphoreType.DMA((3,))]),
        out_shape=jax.ShapeDtypeStruct((n_tiles * EXPERT_TILE, d), BF16),
        compiler_params=_params(("arbitrary",), blocks + temps),
        name="scatter",
    )(plan["seg_len"], plan["seg_src"], plan["seg_dst"], plan["block_chunks"], plan["pad_start"],
      plan["pad_count"], plan["tiles_used"], x1b, sel)


def _experts_kernel(ntile_ref, used_ref, g_ref, wg_ref, wu_ref, wd_ref, y_ref,
                    x_buf, y_buf, wg_buf, wu_buf, wd_buf, wgb_ref, wub_ref, wdb_ref,
                    x_sems, y_sems, w_sems):
    n_experts = wg_ref.shape[0]
    tm = EXPERT_TILE
    used = used_ref[0]

    def weight_copies(e, slot):
        return [pltpu.make_async_copy(src.at[e], dst.at[slot], w_sems.at[slot])
                for src, dst in ((wg_ref, wg_buf), (wu_ref, wu_buf), (wd_ref, wd_buf))]

    def x_copy(t, slot):
        rows = pl.ds(pl.multiple_of(t * tm, tm), MACRO_TILES * tm)
        return pltpu.make_async_copy(g_ref.at[rows], x_buf.at[slot], x_sems.at[slot])

    def y_copy(t, j, slot):
        src = pl.ds(pl.multiple_of(j * tm, tm), tm)
        dst = pl.ds(pl.multiple_of((t + j) * tm, tm), tm)
        return pltpu.make_async_copy(y_buf.at[slot, src], y_ref.at[dst], y_sems.at[slot])

    def mlp(slot, rows):
        x = x_buf[slot, :rows, :]
        hg = jnp.dot(x, wgb_ref[...], preferred_element_type=F32)
        hu = jnp.dot(x, wub_ref[...], preferred_element_type=F32)
        h = hg * _sigmoid(hg) * hu
        y_buf[slot, :rows, :] = jnp.dot(h.astype(BF16), wdb_ref[...],
                                        preferred_element_type=F32).astype(BF16)

    for c in weight_copies(0, 0):
        c.start()
    x_copy(0, 0).start()

    def run_expert(e, carry):
        wslot = e % 2
        for c in weight_copies(e, wslot):
            c.wait()

        @pl.when(e + 1 < n_experts)
        def _():
            for c in weight_copies(e + 1, 1 - wslot):
                c.start()

        @pl.when(ntile_ref[e] > 0)
        def _():
            wgb_ref[...] = wg_buf[wslot].astype(BF16)
            wub_ref[...] = wu_buf[wslot].astype(BF16)
            wdb_ref[...] = wd_buf[wslot].astype(BF16)

        def run_macro(m, carry):
            t, step, k1, t1, k2, t2 = carry
            k = jnp.minimum(MACRO_TILES, ntile_ref[e] - m * MACRO_TILES)
            slot = step % 2
            x_copy(t, slot).wait()

            @pl.when(t + k < used)
            def _():
                x_copy(t + k, 1 - slot).start()

            _for_each(k2, lambda j: y_copy(t2, j, slot).wait())
            for tiles in range(1, MACRO_TILES + 1):
                @pl.when(k == tiles)
                def _():
                    mlp(slot, tiles * tm)

            _for_each(k, lambda j: y_copy(t, j, slot).start())
            return t + k, step + 1, k, t, k1, t1

        n_macro = (ntile_ref[e] + (MACRO_TILES - 1)) // MACRO_TILES
        return lax.fori_loop(0, n_macro, run_macro, carry)

    zero = jnp.int32(0)
    _, step, k1, t1, k2, t2 = lax.fori_loop(0, n_experts, run_expert, (zero,) * 6)
    _for_each(k2, lambda j: y_copy(t2, j, step % 2).wait())
    _for_each(k1, lambda j: y_copy(t1, j, (step + 1) % 2).wait())
    y_buf[0, :tm, :] = jnp.zeros((tm, y_buf.shape[2]), BF16)
    n_unused = y_ref.shape[0] // tm - used
    _for_each(n_unused, lambda i: y_copy(used + i, 0, 0).start())
    _for_each(n_unused, lambda i: y_copy(used + i, 0, 0).wait())


def _experts(sorted_x, plan, w_eg, w_eu, w_ed):
    rows, d = sorted_x.shape
    _, _, de = w_eg.shape
    tm = EXPERT_TILE
    any_space = pl.BlockSpec(memory_space=pl.ANY)
    big = MACRO_TILES * tm
    scratch = 2 * 2 * big * d * 2 + 2 * 3 * d * de * 4 + 3 * d * de * 2
    temps = 3 * big * de * 4 + big * d * 4 + d * de * 4
    return pl.pallas_call(
        _experts_kernel,
        grid_spec=pltpu.PrefetchScalarGridSpec(
            num_scalar_prefetch=2,
            grid=(1,),
            in_specs=[any_space] * 4,
            out_specs=any_space,
            scratch_shapes=[pltpu.VMEM((2, big, d), BF16), pltpu.VMEM((2, big, d), BF16),
                            pltpu.VMEM((2, d, de), F32), pltpu.VMEM((2, d, de), F32),
                            pltpu.VMEM((2, de, d), F32),
                            pltpu.VMEM((d, de), BF16), pltpu.VMEM((d, de), BF16),
                            pltpu.VMEM((de, d), BF16),
                            pltpu.SemaphoreType.DMA((2,)), pltpu.SemaphoreType.DMA((2,)),
                            pltpu.SemaphoreType.DMA((2,))]),
        out_shape=jax.ShapeDtypeStruct((rows, d), BF16),
        compiler_params=_params(("arbitrary",), scratch + temps),
        name="experts",
    )(plan["expert_tiles"], plan["tiles_used"], sorted_x, w_eg, w_eu, w_ed)


def _combine_kernel(len_ref, src_ref, dst_ref, nchunk_ref, x1_ref, comb_ref, sel_ref, p_ref,
                    wpg_ref, wpp_ref,
                    lg_ref, lb_ref, y_ref, o_ref, loc_ref, wpgb_ref, wppb_ref, sems, *, alpha):
    b = pl.program_id(0)
    nb = pl.num_programs(0)
    slot = b % 2
    tb, d = x1_ref.shape
    _cast_once(wpg_ref, wpgb_ref)
    _cast_once(wpp_ref, wppb_ref)

    def start_block(blk, slot):
        def start_segment(e):
            n, local, sorted_at = _segment_rows(len_ref, src_ref, dst_ref, blk, e)

            @pl.when(n > 0)
            def _():
                pltpu.make_async_copy(y_ref.at[pl.ds(sorted_at, n)],
                                      loc_ref.at[slot, pl.ds(local, n)], sems.at[slot]).start()

        _for_each(N_EXPERTS, start_segment)

    @pl.when(b == 0)
    def _():
        loc_ref[...] = jnp.zeros_like(loc_ref)
        start_block(0, 0)

    @pl.when(b + 1 < nb)
    def _():
        start_block(b + 1, 1 - slot)

    n_rows = nchunk_ref[b] * CHUNK
    pltpu.make_async_copy(y_ref.at[pl.ds(0, n_rows)], loc_ref.at[slot, pl.ds(0, n_rows)],
                          sems.at[slot]).wait()

    sel = sel_ref[...]
    pos, pos_lo, pos_hi = _local_positions(sel)
    chosen = sel > 0
    comb = comb_ref[...]
    w_lo = jnp.sum(jnp.where(chosen & (pos == pos_lo), comb, 0.0), axis=1, keepdims=True)
    w_hi = jnp.sum(jnp.where(chosen & (pos == pos_hi), comb, 0.0), axis=1, keepdims=True)
    lo = pos_lo.astype(jnp.int32)
    hi = pos_hi.astype(jnp.int32)

    sorted_row = lax.broadcasted_iota(jnp.int32, (tb // ROW_SUBS, LOCAL_ROWS), 1)

    def branches(r):
        weights = (jnp.where(sorted_row == lo[r], w_lo[r], 0.0)
                   + jnp.where(sorted_row == hi[r], w_hi[r], 0.0)).astype(BF16)
        moe = jnp.dot(weights, loc_ref[slot], preferred_element_type=F32)
        gate = jnp.dot(x1_ref[r, :].astype(BF16), wpgb_ref[...], preferred_element_type=F32)
        emb = jnp.dot(p_ref[r, :].astype(BF16), wppb_ref[...], preferred_element_type=F32)
        return moe, gate, emb

    def finish(r, parts):
        moe, gate, emb = parts
        h = alpha * x1_ref[r, :] + moe + _sigmoid(gate) * emb
        o_ref[r, :] = _layer_norm(h, lg_ref[...], lb_ref[...])

    _skewed(_row_subs(tb), branches, finish)


def _combine(y, plan, x1, comb, sel, p2, w_pg, w_pp, ln_g, ln_b, alpha):
    t, d = x1.shape
    pd = p2.shape[1]
    tb = ROUTE_BLOCK
    row = lambda b, *_: (b, 0)
    blocks = (2 * (2 * tb * d * 4 + tb * LANES * (4 + 2) + tb * pd * 4)
              + (d * d + pd * d) * (4 + 2) + 2 * d * 4 + 2 * LOCAL_ROWS * d * 2)
    temps = tb * tb * 2 + 8 * tb * LANES * 4 + tb * LOCAL_ROWS * 10 + 5 * tb * d * 4
    return pl.pallas_call(
        functools.partial(_combine_kernel, alpha=alpha),
        grid_spec=pltpu.PrefetchScalarGridSpec(
            num_scalar_prefetch=4,
            grid=(t // tb,),
            in_specs=[pl.BlockSpec((tb, d), row), pl.BlockSpec((tb, LANES), row),
                      pl.BlockSpec((tb, LANES), row), pl.BlockSpec((tb, pd), row),
                      _resident((d, d)), _resident((pd, d)), _resident((1, d)), _resident((1, d)),
                      pl.BlockSpec(memory_space=pl.ANY)],
            out_specs=pl.BlockSpec((tb, d), row),
            scratch_shapes=[pltpu.VMEM((2, LOCAL_ROWS, d), BF16), pltpu.VMEM((d, d), BF16),
                            pltpu.VMEM((pd, d), BF16), pltpu.SemaphoreType.DMA((2,))]),
        out_shape=jax.ShapeDtypeStruct((t, d), F32),
        compiler_params=_params(("arbitrary",), blocks + temps),
        name="combine",
    )(plan["seg_len"], plan["seg_src"], plan["seg_dst"], plan["block_chunks"], x1, comb, sel, p2,
      w_pg, w_pp, ln_g, ln_b, y)


def kernel(x, p, w_in, w_pool, pool_scale, w_pu, w_au, w_o, ln1_g, ln1_b, w_rg, b_rg, w_re, b_re,
           w_eg, w_eu, w_ed, w_pg, w_pp, ln2_g, ln2_b):
    bsz, seq, d = x.shape
    depth = w_in.shape[0]
    t = bsz * seq
    de = w_eg.shape[-1]
    alpha = (2.0 * depth) ** 0.25
    assert w_rg.shape[2] == N_GROUPS and w_re.shape[1:] == (N_GROUPS, d, EXPERTS_PER_GROUP)
    assert w_in.shape[2] == 4 * d and w_pool.shape[1] == len(POOL_WINDOWS)
    assert t % ROUTE_BLOCK == 0 and N_EXPERTS + N_GROUPS <= LANES
    n_blocks = t // ROUTE_BLOCK
    n_tiles = -(-(2 * t + n_blocks * N_EXPERTS * (CHUNK - 1) + N_EXPERTS * (EXPERT_TILE - CHUNK))
                // EXPERT_TILE) + MACRO_TILES - 1

    x2 = x.reshape(t, d)
    for i in range(depth):
        pool_out, q, k, v, gates = _in_hbm(
            *_proj(x2, w_in[i], w_pool[i], pool_scale[i][None, :], seq, tm=512))
        attn_out, = _in_hbm(_attention(q, k, v, seq, tb=128))

        w_r = jnp.concatenate(
            [w_re[i].transpose(1, 0, 2).reshape(d, N_EXPERTS), w_rg[i]], axis=1)
        w_r = jnp.pad(w_r, ((0, 0), (0, LANES - w_r.shape[1])))
        b_r = jnp.pad(jnp.concatenate([b_re[i].reshape(-1), b_rg[i]]),
                      (0, LANES - N_GROUPS - N_EXPERTS))[None, :]

        x1, x1b, comb, sel, counts = _merge(
            x2, pool_out, attn_out, gates, *_in_hbm(w_pu[i], w_au[i], w_o[i]), ln1_g[i][None, :],
            ln1_b[i][None, :], w_r, b_r, alpha, tm=ROUTE_BLOCK)
        x1, x1b, comb, sel = _in_hbm(x1, x1b, comb, sel)

        plan = _routing_plan(counts)
        sorted_x, = _in_hbm(_scatter(x1b, sel, plan, n_tiles))
        y, = _in_hbm(_experts(sorted_x, plan, w_eg[i].reshape(N_EXPERTS, d, de),
                              w_eu[i].reshape(N_EXPERTS, d, de), w_ed[i].reshape(N_EXPERTS, de, d)))
        x2 = _combine(y, plan, x1, comb, sel, p[i].reshape(t, -1), *_in_hbm(w_pg[i], w_pp[i]),
                      ln2_g[i][None, :], ln2_b[i][None, :], alpha)
    return x2.reshape(bsz, seq, d)
```

```python
import functools
import math

import jax
import jax.numpy as jnp
from jax import lax
from jax.experimental import pallas as pl
from jax.experimental.pallas import tpu as pltpu

F32 = jnp.float32
BF16 = jnp.bfloat16

LANES = 128
POOL_WINDOWS = (2, 4, 8, 16)
POOL_HALO = 16
HEAD_DIM = 64
N_GROUPS = 4
EXPERTS_PER_GROUP = 8
N_EXPERTS = N_GROUPS * EXPERTS_PER_GROUP
LN_EPS = 1e-5
GROUP_LANE0 = N_EXPERTS
NEG_BIG = -1e30
ROUTE_BLOCK = 512
CHUNK = 16
EXPERT_TILE = 256
MACRO_TILES = 4
SORT_ROWS = 256
ROW_SUBS = 2
LOCAL_ROWS = -(-(2 * ROUTE_BLOCK + N_EXPERTS * (CHUNK - 1)) // SORT_ROWS) * SORT_ROWS
ATTN_DEAD_LOG2 = -160.0
VMEM_CAP_BYTES = 56 * 1024 * 1024


def _params(sem, vmem_bytes):
    return pltpu.CompilerParams(
        dimension_semantics=sem, vmem_limit_bytes=min(int(vmem_bytes), VMEM_CAP_BYTES))


def _layer_norm(h, g, b):
    mu = jnp.mean(h, axis=-1, keepdims=True)
    c = h - mu
    var = jnp.mean(c * c, axis=-1, keepdims=True)
    return c * lax.rsqrt(var + LN_EPS) * g + b


def _sigmoid(z):
    return 1.0 / (1.0 + jnp.exp(-z))


def _row_subs(rows):
    return [slice(k * (rows // ROW_SUBS), (k + 1) * (rows // ROW_SUBS)) for k in range(ROW_SUBS)]


def _skewed(subs, first, second):
    out, pending = [], None
    for r in subs:
        mid = first(r)
        if pending is not None:
            out.append(second(*pending))
        pending = (r, mid)
    out.append(second(*pending))
    return out


def _in_hbm(*arrays):
    if not all(isinstance(a, jax.core.Tracer) for a in arrays):
        return list(arrays)
    return [pltpu.with_memory_space_constraint(a, pltpu.HBM) for a in arrays]


def _resident(shape):
    return pl.BlockSpec(shape, lambda *_: (0,) * len(shape), pipeline_mode=pl.Buffered(1))


def _cast_once(w_ref, wb_ref):
    @pl.when(pl.program_id(0) == 0)
    def _():
        wb_ref[...] = w_ref[...].astype(BF16)


def _proj_kernel(x_ref, w_ref, wp_ref, sc_ref, po_ref, q_ref, k_ref, v_ref, g_ref, wb_ref, halo_ref,
                 *, width, tiles_per_seq):
    q_scale = math.log2(math.e) / math.sqrt(HEAD_DIM)
    tile_in_seq = pl.program_id(0) % tiles_per_seq
    _cast_once(w_ref, wb_ref)
    xb = x_ref[...].astype(BF16)

    def mm(lo):
        return jnp.dot(xb, wb_ref[:, lo:lo + width], preferred_element_type=F32)

    @pl.when(tile_in_seq == 0)
    def _():
        halo_ref[...] = jnp.zeros_like(halo_ref)

    u = mm(0)
    tm = u.shape[0]
    gd = wp_ref.shape[1]
    pos = tile_in_seq * tm + lax.broadcasted_iota(jnp.int32, (tm, gd), 0)

    def pool_group(g, w):
        cols = slice(g * gd, (g + 1) * gd)
        ug = u[:, cols]
        s = jnp.concatenate([halo_ref[:, cols], ug], axis=0)
        sh = 1
        while sh < w:
            s = s + pltpu.roll(s, sh, axis=0)
            sh *= 2
        cnt = jnp.minimum(pos + 1, w).astype(F32)
        pooled = s[POOL_HALO:, :] / cnt - ug
        mixed = jnp.dot(pooled.astype(BF16), wp_ref[g].astype(BF16), preferred_element_type=F32)
        po_ref[:, cols] = (mixed * sc_ref[:, cols]).astype(BF16)

    def write_q():
        q_ref[...] = (mm(width) * q_scale).astype(BF16)

    def write_k():
        k_ref[...] = mm(2 * width).astype(BF16)

    def write_v():
        v_ref[...] = mm(3 * width).astype(BF16)

    def gate_chunk(c):
        g_ref[:, c * width:(c + 1) * width] = _sigmoid(mm((4 + c) * width)).astype(BF16)

    matmuls = [write_q, write_k, write_v] + [
        functools.partial(gate_chunk, c) for c in range(g_ref.shape[1] // width)]
    for n, matmul in enumerate(matmuls):
        matmul()
        if n < len(POOL_WINDOWS):
            pool_group(n, POOL_WINDOWS[n])
    halo_ref[...] = u[tm - POOL_HALO:, :]


def _proj(x2, w_in, w_pool, pool_scale, seq, tm):
    t, d = x2.shape
    n = w_in.shape[1]
    width = d // 2
    gate_w = n - 4 * width
    groups, gd, _ = w_pool.shape
    assert seq % tm == 0 and groups * gd == width and max(POOL_WINDOWS) <= POOL_HALO + 1
    row = lambda i: (i, 0)
    blocks = (2 * (tm * d * 4 + tm * width * 4 * 2 + tm * gate_w * 2)
              + d * n * (4 + 2) + groups * gd * gd * 4 + width * 4 + POOL_HALO * width * 4)
    temps = tm * d * 2 + 3 * tm * width * 4 + 6 * (tm + POOL_HALO) * gd * 4
    return pl.pallas_call(
        functools.partial(_proj_kernel, width=width, tiles_per_seq=seq // tm),
        grid=(t // tm,),
        in_specs=[pl.BlockSpec((tm, d), row), _resident((d, n)), _resident((groups, gd, gd)),
                  _resident((1, width))],
        out_specs=[pl.BlockSpec((tm, width), row)] * 4 + [pl.BlockSpec((tm, gate_w), row)],
        out_shape=[jax.ShapeDtypeStruct((t, width), BF16)] * 4
        + [jax.ShapeDtypeStruct((t, gate_w), BF16)],
        scratch_shapes=[pltpu.VMEM((d, n), BF16), pltpu.VMEM((POOL_HALO, width), F32)],
        compiler_params=_params(("arbitrary",), blocks + temps),
        name="proj",
    )(x2, w_in, w_pool, pool_scale)


def _attn_kernel(q_ref, k_ref, v_ref, o_ref, rem_ref, acc_ref, *, tb):
    qi = pl.program_id(1)
    pairs = q_ref.shape[1] // LANES
    col_blocks = [slice(p * LANES, (p + 1) * LANES) for p in range(pairs)]
    first_head = lax.broadcasted_iota(jnp.int32, (tb, LANES), 1) < HEAD_DIM
    r = lax.broadcasted_iota(jnp.int32, (2 * tb, 2 * tb), 0)
    c = lax.broadcasted_iota(jnp.int32, (2 * tb, 2 * tb), 1)
    same_head = (r >= tb) == (c >= tb)
    cum = -jnp.concatenate([(same_head & (r >= c)).astype(BF16), same_head.astype(BF16)], axis=1)
    kcol = lax.broadcasted_iota(jnp.int32, (tb, 2 * tb), 1)
    kcol = jnp.where(kcol >= tb, kcol - tb, kcol)
    causal = kcol < lax.broadcasted_iota(jnp.int32, (tb, 2 * tb), 0)

    def stack_heads(blk):
        zero = jnp.zeros_like(blk)
        return jnp.concatenate(
            [jnp.where(first_head, blk, zero), jnp.where(first_head, zero, blk)], axis=0)

    def scores(j, rows, diagonal):
        ks = pl.multiple_of(j * tb, tb)
        zs = [lax.dot_general(q_ref[rows, cols], stack_heads(k_ref[pl.ds(ks, tb), cols]),
                              (((1,), (1,)), ((), ())), preferred_element_type=F32)
              for cols in col_blocks]
        sums = []
        for z in zs:
            softplus = jnp.maximum(z, 0.0) + jnp.log2(1.0 + jnp.exp2(-jnp.abs(z)))
            if diagonal:
                softplus = jnp.where(causal, softplus, 0.0)
            sums.append(jnp.dot(softplus.astype(BF16), cum, preferred_element_type=F32))
        return zs, sums

    def weighted_values(j, cols, z, later):
        ks = pl.multiple_of(j * tb, tb)
        a = jnp.exp2(z + later)
        return jnp.dot(a.astype(BF16), stack_heads(v_ref[pl.ds(ks, tb), cols]),
                       preferred_element_type=F32)

    def first_step(n_before):
        full, half = slice(0, tb), slice(0, tb // 2)
        z_d, sums_d = scores(qi, full, diagonal=True)
        if n_before >= 1:
            z_1, sums_1 = scores(qi - 1, full, diagonal=False)
        if n_before >= 2:
            z_2, sums_2 = scores(qi - 2, half, diagonal=False)
        owed = []
        for p, cols in enumerate(col_blocks):
            ks = pl.multiple_of(qi * tb, tb)
            a = jnp.where(causal, jnp.exp2(z_d[p] + sums_d[p][:, :2 * tb]), 0.0)
            out = jnp.dot(a.astype(BF16), stack_heads(v_ref[pl.ds(ks, tb), cols]),
                          preferred_element_type=F32)
            total = sums_d[p][:, 2 * tb:]
            if n_before >= 1:
                out = out + weighted_values(qi - 1, cols, z_1[p], sums_1[p][:, :2 * tb] + total)
                total = total + sums_1[p][:, 2 * tb:]
            acc_ref[p] = out
            rem_ref[p] = total
            if n_before >= 2:
                owed.append(total[tb // 2:, :])
                later = sums_2[p][:, :2 * tb] + total[half, :]
                acc_ref[p, half, :] += weighted_values(qi - 2, cols, z_2[p], later)
                rem_ref[p, half, :] += sums_2[p][:, 2 * tb:]
        if n_before < 2:
            return None
        return jnp.max(functools.reduce(jnp.maximum, owed)) > ATTN_DEAD_LOG2

    def step(j, rows):
        zs, sums = scores(j, rows, diagonal=False)
        for p, cols in enumerate(col_blocks):
            later = sums[p][:, :2 * tb] + rem_ref[p, rows, :]
            acc_ref[p, rows, :] += weighted_values(j, cols, zs[p], later)
            rem_ref[p, rows, :] += sums[p][:, 2 * tb:]

    def live_rows():
        worst = functools.reduce(jnp.maximum, [rem_ref[p] for p in range(pairs)])
        live = jnp.max(worst, axis=1, keepdims=True) > ATTN_DEAD_LOG2
        row = lax.broadcasted_iota(jnp.int32, (tb, 1), 0)
        return jnp.max(jnp.where(live, row + 1, 0))

    for n_before in (0, 1):
        @pl.when(qi == n_before)
        def _():
            first_step(n_before)

    @pl.when(qi >= 2)
    def _():
        second_half_owed = first_step(2)

        @pl.when(second_half_owed)
        def _():
            step(qi - 2, slice(tb // 2, tb))

    row_counts = (tb, tb // 2, tb // 4)

    def cond(carry):
        j, n_live = carry
        return (j >= 0) & (n_live > 0)

    def body(carry):
        j, n_live = carry
        for rows, fewer in zip(row_counts, row_counts[1:] + (0,)):
            @pl.when((n_live > fewer) & (n_live <= rows))
            def _():
                step(j, slice(0, rows))

        return j - 1, live_rows()

    lax.while_loop(cond, body, (qi - 3, live_rows()))
    for p, cols in enumerate(col_blocks):
        o_ref[:, cols] = acc_ref[p].astype(BF16)


def _attention(q, k, v, seq, tb):
    t, width = q.shape
    pairs = width // LANES
    blocks = 2 * (2 * tb * width * 2 + 2 * seq * width * 2) + pairs * tb * (2 * tb + LANES) * 4
    temps = pairs * 10 * tb * 2 * tb * 4 + 2 * tb * 4 * tb * 2
    qspec = pl.BlockSpec((tb, width), lambda b, i: (b * (seq // tb) + i, 0))
    kvspec = pl.BlockSpec((seq, width), lambda b, i: (b, 0))
    return pl.pallas_call(
        functools.partial(_attn_kernel, tb=tb),
        grid=(t // seq, seq // tb),
        in_specs=[qspec, kvspec, kvspec],
        out_specs=qspec,
        out_shape=jax.ShapeDtypeStruct((t, width), BF16),
        scratch_shapes=[pltpu.VMEM((pairs, tb, 2 * tb), F32), pltpu.VMEM((pairs, tb, LANES), F32)],
        compiler_params=_params(("parallel", "parallel"), blocks + temps),
        name="attn",
    )(q, k, v)


def _route(logits):
    lane = lax.broadcasted_iota(jnp.int32, logits.shape, 1)

    def first_max(vals):
        m = jnp.max(vals, axis=1, keepdims=True)
        idx = jnp.min(jnp.where(vals == m, lane, LANES), axis=1, keepdims=True)
        return m, idx

    is_group = (lane >= GROUP_LANE0) & (lane < GROUP_LANE0 + N_GROUPS)
    gm, g_lane = first_max(jnp.where(is_group, logits, NEG_BIG))
    g_prob = 1.0 / jnp.sum(jnp.where(is_group, jnp.exp(logits - gm), 0.0), axis=1, keepdims=True)
    lo = EXPERTS_PER_GROUP * (g_lane - GROUP_LANE0)
    in_group = jnp.where((lane >= lo) & (lane < lo + EXPERTS_PER_GROUP), logits, NEG_BIG)
    m1, i1 = first_max(in_group)
    m2, i2 = first_max(jnp.where(lane == i1, NEG_BIG, in_group))
    e21 = jnp.exp(m2 - m1)
    w1 = g_prob / (1.0 + e21)
    w2 = w1 * e21
    comb = jnp.where(lane == i1, w1, 0.0) + jnp.where(lane == i2, w2, 0.0)
    sel = ((lane == i1) | (lane == i2)).astype(F32)
    return comb, sel


def _merge_kernel(x_ref, po_ref, at_ref, g_ref, wpu_ref, wau_ref, wo_ref, lg_ref, lb_ref,
                  wr_ref, br_ref, x1_ref, x1b_ref, comb_ref, sel_ref, cnt_ref,
                  wpub_ref, waub_ref, wob_ref, wrb_ref, *, alpha):
    tm, d = x_ref.shape
    _cast_once(wpu_ref, wpub_ref)
    _cast_once(wau_ref, waub_ref)
    _cast_once(wo_ref, wob_ref)

    @pl.when(pl.program_id(0) == 0)
    def _():
        w_r = wr_ref[...]
        hi = w_r.astype(BF16)
        wrb_ref[:, :LANES] = hi
        wrb_ref[:, LANES:] = (w_r - hi.astype(F32)).astype(BF16)

    def mix(r):
        a = jnp.dot(po_ref[r, :], wpub_ref[...], preferred_element_type=F32)
        b = jnp.dot(at_ref[r, :], waub_ref[...], preferred_element_type=F32)
        merged = (g_ref[r, :d].astype(F32) * a + g_ref[r, d:].astype(F32) * b).astype(BF16)
        return alpha * x_ref[r, :] + jnp.dot(merged, wob_ref[...], preferred_element_type=F32)

    def norm_and_route(r, h):
        x1 = _layer_norm(h, lg_ref[...], lb_ref[...])
        xh = x1.astype(BF16)
        x1_ref[r, :] = x1
        x1b_ref[r, :] = xh
        xl = (x1 - xh.astype(F32)).astype(BF16)
        by_hi = jnp.dot(xh, wrb_ref[...], preferred_element_type=F32)
        by_lo = jnp.dot(xl, wrb_ref[:, :LANES], preferred_element_type=F32)
        comb, sel = _route(by_hi[:, :LANES] + by_hi[:, LANES:] + by_lo + br_ref[...])
        comb_ref[r, :] = comb
        sel_ref[r, :] = sel.astype(BF16)
        return jnp.sum(sel, axis=0, keepdims=True)

    cnt_ref[0] = sum(_skewed(_row_subs(tm), mix, norm_and_route))


def _merge(x2, pool_out, attn_out, gates, w_pu, w_au, w_o, ln_g, ln_b, w_r, b_r, alpha, tm):
    t, d = x2.shape
    width = pool_out.shape[1]
    row = lambda i: (i, 0)
    blocks = (2 * (tm * d * 4 + 2 * tm * width * 2 + tm * 2 * d * 2
                   + tm * d * 4 + tm * d * 2 + tm * LANES * (4 + 2) + LANES * 4)
              + (2 * width * d + d * d + d * LANES) * (4 + 2) + 2 * d * 4 + LANES * 4)
    temps = 5 * tm * d * 4
    return pl.pallas_call(
        functools.partial(_merge_kernel, alpha=alpha),
        grid=(t // tm,),
        in_specs=[pl.BlockSpec((tm, d), row), pl.BlockSpec((tm, width), row),
                  pl.BlockSpec((tm, width), row), pl.BlockSpec((tm, 2 * d), row),
                  _resident((width, d)), _resident((width, d)), _resident((d, d)),
                  _resident((1, d)), _resident((1, d)), _resident((d, LANES)),
                  _resident((1, LANES))],
        out_specs=[pl.BlockSpec((tm, d), row), pl.BlockSpec((tm, d), row),
                   pl.BlockSpec((tm, LANES), row), pl.BlockSpec((tm, LANES), row),
                   pl.BlockSpec((1, 1, LANES), lambda i: (i, 0, 0))],
        out_shape=[jax.ShapeDtypeStruct((t, d), F32), jax.ShapeDtypeStruct((t, d), BF16),
                   jax.ShapeDtypeStruct((t, LANES), F32), jax.ShapeDtypeStruct((t, LANES), BF16),
                   jax.ShapeDtypeStruct((t // tm, 1, LANES), F32)],
        scratch_shapes=[pltpu.VMEM((width, d), BF16), pltpu.VMEM((width, d), BF16),
                        pltpu.VMEM((d, d), BF16), pltpu.VMEM((d, 2 * LANES), BF16)],
        compiler_params=_params(("arbitrary",), blocks + temps),
        name="merge",
    )(x2, pool_out, attn_out, gates, w_pu, w_au, w_o, ln_g, ln_b, w_r, b_r)


def _routing_plan(counts):
    cnt = counts[:, 0, :N_EXPERTS].astype(jnp.int32)
    nch = (cnt + (CHUNK - 1)) // CHUNK
    local_end = jnp.cumsum(nch, axis=1)
    local_start = local_end - nch
    block_chunks = local_end[:, -1]
    before_block = jnp.cumsum(nch, axis=0) - nch
    expert_chunks = jnp.sum(nch, axis=0)
    chunks_per_tile = EXPERT_TILE // CHUNK
    expert_tiles = (expert_chunks + (chunks_per_tile - 1)) // chunks_per_tile
    tiles_end = jnp.cumsum(expert_tiles)
    region_start = (tiles_end - expert_tiles) * chunks_per_tile
    segment_dst = region_start[None, :] + before_block
    i32 = lambda a: a.astype(jnp.int32)
    return dict(seg_len=i32(nch), seg_src=i32(local_start), seg_dst=i32(segment_dst),
                block_chunks=i32(block_chunks),
                expert_tiles=i32(expert_tiles), tiles_used=i32(tiles_end[-1:]),
                pad_start=i32(region_start + expert_chunks),
                pad_count=i32(expert_tiles * chunks_per_tile - expert_chunks))


def _local_positions(sel):
    tb = sel.shape[0]
    earlier = (lax.broadcasted_iota(jnp.int32, (tb, tb), 1)
               < lax.broadcasted_iota(jnp.int32, (tb, tb), 0)).astype(BF16)
    rank = jnp.dot(earlier, sel, preferred_element_type=F32)
    cnt = jnp.sum(sel.astype(F32), axis=0, keepdims=True)
    nch = jnp.floor((cnt + (CHUNK - 1)) * (1.0 / CHUNK))
    lower = (lax.broadcasted_iota(jnp.int32, (LANES, LANES), 0)
             < lax.broadcasted_iota(jnp.int32, (LANES, LANES), 1)).astype(BF16)
    start = CHUNK * jnp.dot(jnp.broadcast_to(nch, (8, LANES)).astype(BF16), lower,
                            preferred_element_type=F32)[0:1]
    pos = rank + start
    chosen = sel > 0
    pos_lo = jnp.min(jnp.where(chosen, pos, float(LOCAL_ROWS)), axis=1, keepdims=True)
    pos_hi = jnp.max(jnp.where(chosen, pos, -1.0), axis=1, keepdims=True)
    return pos, pos_lo, pos_hi


def _for_each(n, fn):
    lax.fori_loop(0, n, lambda c, carry: (fn(c), carry)[1], 0)


def _segment_rows(len_ref, src_ref, dst_ref, blk, e):
    n = len_ref[blk, e] * CHUNK
    src = pl.multiple_of(src_ref[blk, e] * CHUNK, CHUNK)
    dst = pl.multiple_of(dst_ref[blk, e] * CHUNK, CHUNK)
    return n, src, dst


def _scatter_kernel(len_ref, src_ref, dst_ref, nchunk_ref, pad_start_ref, pad_count_ref, used_ref,
                    x_ref, sel_ref, g_ref, loc_ref, zero_ref, sems):
    b = pl.program_id(0)
    last = pl.num_programs(0) - 1
    slot = b % 2
    tb = x_ref.shape[0]

    def start_block(blk, slot):
        def start_segment(e):
            n, src, dst = _segment_rows(len_ref, src_ref, dst_ref, blk, e)

            @pl.when(n > 0)
            def _():
                pltpu.make_async_copy(loc_ref.at[slot, pl.ds(src, n)], g_ref.at[pl.ds(dst, n)],
                                      sems.at[slot]).start()

        _for_each(N_EXPERTS, start_segment)

    def wait_block(blk, slot):
        n = nchunk_ref[blk] * CHUNK
        pltpu.make_async_copy(loc_ref.at[slot, pl.ds(0, n)], g_ref.at[pl.ds(0, n)],
                              sems.at[slot]).wait()

    def pad_copy(e):
        n = pad_count_ref[e] * CHUNK
        dst = pl.multiple_of(pad_start_ref[e] * CHUNK, CHUNK)
        return n, pltpu.make_async_copy(zero_ref.at[pl.ds(0, n)], g_ref.at[pl.ds(dst, n)],
                                        sems.at[2])

    def unused_tile_copy(i):
        dst = pl.multiple_of(i * EXPERT_TILE, EXPERT_TILE)
        return pltpu.make_async_copy(zero_ref, g_ref.at[pl.ds(dst, EXPERT_TILE)], sems.at[2])

    @pl.when(b >= 2)
    def _():
        wait_block(b - 2, slot)

    _, pos_lo, pos_hi = _local_positions(sel_ref[...])
    lo = pos_lo.astype(jnp.int32)
    hi = pos_hi.astype(jnp.int32)
    x = x_ref[...]

    r = lax.broadcasted_iota(jnp.int32, (tb, LOCAL_ROWS), 1)
    perm = ((r == lo) | (r == hi)).astype(BF16)
    rows = lax.dot_general(perm, x, (((0,), (0,)), ((), ())), preferred_element_type=F32)
    loc_ref[slot] = rows.astype(BF16)
    start_block(b, slot)

    @pl.when(b == last)
    def _():
        zero_ref[...] = jnp.zeros_like(zero_ref)
        n_unused = g_ref.shape[0] // EXPERT_TILE - used_ref[0]

        def each_pad(act):
            def one(e):
                n, copy = pad_copy(e)

                @pl.when(n > 0)
                def _():
                    act(copy)

            _for_each(N_EXPERTS, one)

        each_pad(lambda copy: copy.start())
        _for_each(n_unused, lambda i: unused_tile_copy(used_ref[0] + i).start())
        each_pad(lambda copy: copy.wait())
        _for_each(n_unused, lambda i: unused_tile_copy(used_ref[0] + i).wait())

        @pl.when(b >= 1)
        def _():
            wait_block(b - 1, 1 - slot)

        wait_block(b, slot)


def _scatter(x1b, sel, plan, n_tiles):
    t, d = x1b.shape
    tb = ROUTE_BLOCK
    row = lambda b, *_: (b, 0)
    blocks = 2 * (tb * d * 2 + tb * LANES * 2) + 2 * LOCAL_ROWS * d * 2 + EXPERT_TILE * d * 2
    temps = tb * tb * 2 + 6 * tb * LANES * 4 + tb * LOCAL_ROWS * 6 + LOCAL_ROWS * d * 6
    return pl.pallas_call(
        _scatter_kernel,
        grid_spec=pltpu.PrefetchScalarGridSpec(
            num_scalar_prefetch=7,
            grid=(t // tb,),
            in_specs=[pl.BlockSpec((tb, d), row), pl.BlockSpec((tb, LANES), row)],
            out_specs=pl.BlockSpec(memory_space=pl.ANY),
            scratch_shapes=[pltpu.VMEM((2, LOCAL_ROWS, d), BF16),
                            pltpu.VMEM((EXPERT_TILE, d), BF16), pltpu.SemaphoreType.DMA((3,))]),
        out_shape=jax.ShapeDtypeStruct((n_tiles * EXPERT_TILE, d), BF16),
        compiler_params=_params(("arbitrary",), blocks + temps),
        name="scatter",
    )(plan["seg_len"], plan["seg_src"], plan["seg_dst"], plan["block_chunks"], plan["pad_start"],
      plan["pad_count"], plan["tiles_used"], x1b, sel)


def _experts_kernel(ntile_ref, used_ref, g_ref, wg_ref, wu_ref, wd_ref, y_ref,
                    x_buf, y_buf, wg_buf, wu_buf, wd_buf, wgb_ref, wub_ref, wdb_ref,
                    x_sems, y_sems, w_sems):
    n_experts = wg_ref.shape[0]
    tm = EXPERT_TILE
    used = used_ref[0]

    def weight_copies(e, slot):
        return [pltpu.make_async_copy(src.at[e], dst.at[slot], w_sems.at[slot])
                for src, dst in ((wg_ref, wg_buf), (wu_ref, wu_buf), (wd_ref, wd_buf))]

    def x_copy(t, slot):
        rows = pl.ds(pl.multiple_of(t * tm, tm), MACRO_TILES * tm)
        return pltpu.make_async_copy(g_ref.at[rows], x_buf.at[slot], x_sems.at[slot])

    def y_copy(t, k, slot):
        n = k * tm
        dst = pl.ds(pl.multiple_of(t * tm, tm), n)
        return pltpu.make_async_copy(y_buf.at[slot, pl.ds(0, n)], y_ref.at[dst], y_sems.at[slot])

    def wait_y(t, k, slot):
        @pl.when(k > 0)
        def _():
            y_copy(t, k, slot).wait()

    def mlp(slot, rows):
        x = x_buf[slot, :rows, :]
        hg = jnp.dot(x, wgb_ref[...], preferred_element_type=F32)
        hu = jnp.dot(x, wub_ref[...], preferred_element_type=F32)
        h = hg * _sigmoid(hg) * hu
        y_buf[slot, :rows, :] = jnp.dot(h.astype(BF16), wdb_ref[...],
                                        preferred_element_type=F32).astype(BF16)

    for c in weight_copies(0, 0):
        c.start()
    x_copy(0, 0).start()

    def run_expert(e, carry):
        wslot = e % 2
        for c in weight_copies(e, wslot):
            c.wait()

        @pl.when(e + 1 < n_experts)
        def _():
            for c in weight_copies(e + 1, 1 - wslot):
                c.start()

        @pl.when(ntile_ref[e] > 0)
        def _():
            wgb_ref[...] = wg_buf[wslot].astype(BF16)
            wub_ref[...] = wu_buf[wslot].astype(BF16)
            wdb_ref[...] = wd_buf[wslot].astype(BF16)

        def run_macro(m, carry):
            t, step, k1, t1, k2, t2 = carry
            k = tiles_base + (m < tiles_extra).astype(jnp.int32)
            slot = step % 2
            x_copy(t, slot).wait()

            @pl.when(t + k < used)
            def _():
                x_copy(t + k, 1 - slot).start()

            wait_y(t2, k2, slot)
            for tiles in range(1, MACRO_TILES + 1):
                @pl.when(k == tiles)
                def _():
                    mlp(slot, tiles * tm)

            y_copy(t, k, slot).start()
            return t + k, step + 1, k, t, k1, t1

        n_macro = (ntile_ref[e] + (MACRO_TILES - 1)) // MACRO_TILES
        tiles_base = ntile_ref[e] // jnp.maximum(n_macro, 1)
        tiles_extra = ntile_ref[e] - tiles_base * n_macro
        return lax.fori_loop(0, n_macro, run_macro, carry)

    zero = jnp.int32(0)
    _, step, k1, t1, k2, t2 = lax.fori_loop(0, n_experts, run_expert, (zero,) * 6)
    wait_y(t2, k2, step % 2)
    wait_y(t1, k1, (step + 1) % 2)
    y_buf[0, :tm, :] = jnp.zeros((tm, y_buf.shape[2]), BF16)
    n_unused = y_ref.shape[0] // tm - used
    _for_each(n_unused, lambda i: y_copy(used + i, 1, 0).start())
    _for_each(n_unused, lambda i: y_copy(used + i, 1, 0).wait())


def _experts(sorted_x, plan, w_eg, w_eu, w_ed):
    rows, d = sorted_x.shape
    _, _, de = w_eg.shape
    tm = EXPERT_TILE
    any_space = pl.BlockSpec(memory_space=pl.ANY)
    big = MACRO_TILES * tm
    scratch = 2 * 2 * big * d * 2 + 2 * 3 * d * de * 4 + 3 * d * de * 2
    temps = 3 * big * de * 4 + big * d * 4 + d * de * 4
    return pl.pallas_call(
        _experts_kernel,
        grid_spec=pltpu.PrefetchScalarGridSpec(
            num_scalar_prefetch=2,
            grid=(1,),
            in_specs=[any_space] * 4,
            out_specs=any_space,
            scratch_shapes=[pltpu.VMEM((2, big, d), BF16), pltpu.VMEM((2, big, d), BF16),
                            pltpu.VMEM((2, d, de), F32), pltpu.VMEM((2, d, de), F32),
                            pltpu.VMEM((2, de, d), F32),
                            pltpu.VMEM((d, de), BF16), pltpu.VMEM((d, de), BF16),
                            pltpu.VMEM((de, d), BF16),
                            pltpu.SemaphoreType.DMA((2,)), pltpu.SemaphoreType.DMA((2,)),
                            pltpu.SemaphoreType.DMA((2,))]),
        out_shape=jax.ShapeDtypeStruct((rows, d), BF16),
        compiler_params=_params(("arbitrary",), scratch + temps),
        name="experts",
    )(plan["expert_tiles"], plan["tiles_used"], sorted_x, w_eg, w_eu, w_ed)


def _combine_kernel(len_ref, src_ref, dst_ref, nchunk_ref, x1_ref, comb_ref, sel_ref, p_ref,
                    wpg_ref, wpp_ref,
                    lg_ref, lb_ref, y_ref, o_ref, loc_ref, wpgb_ref, wppb_ref, sems, *, alpha):
    b = pl.program_id(0)
    nb = pl.num_programs(0)
    slot = b % 2
    tb, d = x1_ref.shape
    _cast_once(wpg_ref, wpgb_ref)
    _cast_once(wpp_ref, wppb_ref)

    def start_block(blk, slot):
        def start_segment(e):
            n, local, sorted_at = _segment_rows(len_ref, src_ref, dst_ref, blk, e)

            @pl.when(n > 0)
            def _():
                pltpu.make_async_copy(y_ref.at[pl.ds(sorted_at, n)],
                                      loc_ref.at[slot, pl.ds(local, n)], sems.at[slot]).start()

        _for_each(N_EXPERTS, start_segment)

    @pl.when(b == 0)
    def _():
        loc_ref[...] = jnp.zeros_like(loc_ref)
        start_block(0, 0)

    @pl.when(b + 1 < nb)
    def _():
        start_block(b + 1, 1 - slot)

    n_rows = nchunk_ref[b] * CHUNK
    pltpu.make_async_copy(y_ref.at[pl.ds(0, n_rows)], loc_ref.at[slot, pl.ds(0, n_rows)],
                          sems.at[slot]).wait()

    sel = sel_ref[...]
    pos, pos_lo, pos_hi = _local_positions(sel)
    chosen = sel > 0
    comb = comb_ref[...]
    w_lo = jnp.sum(jnp.where(chosen & (pos == pos_lo), comb, 0.0), axis=1, keepdims=True)
    w_hi = jnp.sum(jnp.where(chosen & (pos == pos_hi), comb, 0.0), axis=1, keepdims=True)
    lo = pos_lo.astype(jnp.int32)
    hi = pos_hi.astype(jnp.int32)

    sorted_row = lax.broadcasted_iota(jnp.int32, (tb // ROW_SUBS, LOCAL_ROWS), 1)

    def branches(r):
        weights = (jnp.where(sorted_row == lo[r], w_lo[r], 0.0)
                   + jnp.where(sorted_row == hi[r], w_hi[r], 0.0)).astype(BF16)
        moe = jnp.dot(weights, loc_ref[slot], preferred_element_type=F32)
        gate = jnp.dot(x1_ref[r, :].astype(BF16), wpgb_ref[...], preferred_element_type=F32)
        emb = jnp.dot(p_ref[r, :].astype(BF16), wppb_ref[...], preferred_element_type=F32)
        return moe, gate, emb

    def finish(r, parts):
        moe, gate, emb = parts
        h = alpha * x1_ref[r, :] + moe + _sigmoid(gate) * emb
        o_ref[r, :] = _layer_norm(h, lg_ref[...], lb_ref[...])

    _skewed(_row_subs(tb), branches, finish)


def _combine(y, plan, x1, comb, sel, p2, w_pg, w_pp, ln_g, ln_b, alpha):
    t, d = x1.shape
    pd = p2.shape[1]
    tb = ROUTE_BLOCK
    row = lambda b, *_: (b, 0)
    blocks = (2 * (2 * tb * d * 4 + tb * LANES * (4 + 2) + tb * pd * 4)
              + (d * d + pd * d) * (4 + 2) + 2 * d * 4 + 2 * LOCAL_ROWS * d * 2)
    temps = tb * tb * 2 + 8 * tb * LANES * 4 + tb * LOCAL_ROWS * 10 + 5 * tb * d * 4
    return pl.pallas_call(
        functools.partial(_combine_kernel, alpha=alpha),
        grid_spec=pltpu.PrefetchScalarGridSpec(
            num_scalar_prefetch=4,
            grid=(t // tb,),
            in_specs=[pl.BlockSpec((tb, d), row), pl.BlockSpec((tb, LANES), row),
                      pl.BlockSpec((tb, LANES), row), pl.BlockSpec((tb, pd), row),
                      _resident((d, d)), _resident((pd, d)), _resident((1, d)), _resident((1, d)),
                      pl.BlockSpec(memory_space=pl.ANY)],
            out_specs=pl.BlockSpec((tb, d), row),
            scratch_shapes=[pltpu.VMEM((2, LOCAL_ROWS, d), BF16), pltpu.VMEM((d, d), BF16),
                            pltpu.VMEM((pd, d), BF16), pltpu.SemaphoreType.DMA((2,))]),
        out_shape=jax.ShapeDtypeStruct((t, d), F32),
        compiler_params=_params(("arbitrary",), blocks + temps),
        name="combine",
    )(plan["seg_len"], plan["seg_src"], plan["seg_dst"], plan["block_chunks"], x1, comb, sel, p2,
      w_pg, w_pp, ln_g, ln_b, y)


def kernel(x, p, w_in, w_pool, pool_scale, w_pu, w_au, w_o, ln1_g, ln1_b, w_rg, b_rg, w_re, b_re,
           w_eg, w_eu, w_ed, w_pg, w_pp, ln2_g, ln2_b):
    bsz, seq, d = x.shape
    depth = w_in.shape[0]
    t = bsz * seq
    de = w_eg.shape[-1]
    alpha = (2.0 * depth) ** 0.25
    assert w_rg.shape[2] == N_GROUPS and w_re.shape[1:] == (N_GROUPS, d, EXPERTS_PER_GROUP)
    assert w_in.shape[2] == 4 * d and w_pool.shape[1] == len(POOL_WINDOWS)
    assert t % ROUTE_BLOCK == 0 and N_EXPERTS + N_GROUPS <= LANES
    n_blocks = t // ROUTE_BLOCK
    n_tiles = -(-(2 * t + n_blocks * N_EXPERTS * (CHUNK - 1) + N_EXPERTS * (EXPERT_TILE - CHUNK))
                // EXPERT_TILE) + MACRO_TILES - 1

    x2 = x.reshape(t, d)
    for i in range(depth):
        pool_out, q, k, v, gates = _in_hbm(
            *_proj(x2, w_in[i], w_pool[i], pool_scale[i][None, :], seq, tm=512))
        attn_out, = _in_hbm(_attention(q, k, v, seq, tb=128))

        w_r = jnp.concatenate(
            [w_re[i].transpose(1, 0, 2).reshape(d, N_EXPERTS), w_rg[i]], axis=1)
        w_r = jnp.pad(w_r, ((0, 0), (0, LANES - w_r.shape[1])))
        b_r = jnp.pad(jnp.concatenate([b_re[i].reshape(-1), b_rg[i]]),
                      (0, LANES - N_GROUPS - N_EXPERTS))[None, :]

        x1, x1b, comb, sel, counts = _merge(
            x2, pool_out, attn_out, gates, *_in_hbm(w_pu[i], w_au[i], w_o[i]), ln1_g[i][None, :],
            ln1_b[i][None, :], w_r, b_r, alpha, tm=ROUTE_BLOCK)
        x1, x1b, comb, sel = _in_hbm(x1, x1b, comb, sel)

        plan = _routing_plan(counts)
        sorted_x, = _in_hbm(_scatter(x1b, sel, plan, n_tiles))
        y, = _in_hbm(_experts(sorted_x, plan, w_eg[i].reshape(N_EXPERTS, d, de),
                              w_eu[i].reshape(N_EXPERTS, d, de), w_ed[i].reshape(N_EXPERTS, de, d)))
        x2 = _combine(y, plan, x1, comb, sel, p[i].reshape(t, -1), *_in_hbm(w_pg[i], w_pp[i]),
                      ln2_g[i][None, :], ln2_b[i][None, :], alpha)
    return x2.reshape(bsz, seq, d)
```

```python
import functools
import math

import jax
import jax.numpy as jnp
from jax import lax
from jax.experimental import pallas as pl
from jax.experimental.pallas import tpu as pltpu

F32 = jnp.float32
BF16 = jnp.bfloat16

LANES = 128
POOL_WINDOWS = (2, 4, 8, 16)
POOL_HALO = 16
HEAD_DIM = 64
N_GROUPS = 4
EXPERTS_PER_GROUP = 8
N_EXPERTS = N_GROUPS * EXPERTS_PER_GROUP
LN_EPS = 1e-5
GROUP_LANE0 = N_EXPERTS
NEG_BIG = -1e30
ROUTE_BLOCK = 512
CHUNK = 16
EXPERT_TILE = 256
MACRO_TILES = 6
SORT_ROWS = 256
ROW_SUBS = 2
LOCAL_ROWS = -(-(2 * ROUTE_BLOCK + N_EXPERTS * (CHUNK - 1)) // SORT_ROWS) * SORT_ROWS
ATTN_DEAD_LOG2 = -160.0
VMEM_CAP_BYTES = 56 * 1024 * 1024


def _params(sem, vmem_bytes):
    return pltpu.CompilerParams(
        dimension_semantics=sem, vmem_limit_bytes=min(int(vmem_bytes), VMEM_CAP_BYTES))


def _layer_norm(h, g, b):
    mu = jnp.mean(h, axis=-1, keepdims=True)
    c = h - mu
    var = jnp.mean(c * c, axis=-1, keepdims=True)
    return c * lax.rsqrt(var + LN_EPS) * g + b


def _sigmoid(z):
    return 1.0 / (1.0 + jnp.exp(-z))


def _row_subs(rows):
    return [slice(k * (rows // ROW_SUBS), (k + 1) * (rows // ROW_SUBS)) for k in range(ROW_SUBS)]


def _skewed(subs, first, second):
    out, pending = [], None
    for r in subs:
        mid = first(r)
        if pending is not None:
            out.append(second(*pending))
        pending = (r, mid)
    out.append(second(*pending))
    return out


def _in_hbm(*arrays):
    if not all(isinstance(a, jax.core.Tracer) for a in arrays):
        return list(arrays)
    return [pltpu.with_memory_space_constraint(a, pltpu.HBM) for a in arrays]


def _resident(shape):
    return pl.BlockSpec(shape, lambda *_: (0,) * len(shape), pipeline_mode=pl.Buffered(1))


def _cast_once(w_ref, wb_ref):
    @pl.when(pl.program_id(0) == 0)
    def _():
        wb_ref[...] = w_ref[...].astype(BF16)


def _proj_kernel(x_ref, w_ref, wp_ref, sc_ref, po_ref, q_ref, k_ref, v_ref, g_ref, wb_ref, halo_ref,
                 *, width, tiles_per_seq):
    q_scale = math.log2(math.e) / math.sqrt(HEAD_DIM)
    tile_in_seq = pl.program_id(0) % tiles_per_seq
    _cast_once(w_ref, wb_ref)
    xb = x_ref[...].astype(BF16)

    def mm(lo):
        return jnp.dot(xb, wb_ref[:, lo:lo + width], preferred_element_type=F32)

    @pl.when(tile_in_seq == 0)
    def _():
        halo_ref[...] = jnp.zeros_like(halo_ref)

    u = mm(0)
    tm = u.shape[0]
    gd = wp_ref.shape[1]
    pos = tile_in_seq * tm + lax.broadcasted_iota(jnp.int32, (tm, gd), 0)

    def pool_group(g, w):
        cols = slice(g * gd, (g + 1) * gd)
        ug = u[:, cols]
        s = jnp.concatenate([halo_ref[:, cols], ug], axis=0)
        sh = 1
        while sh < w:
            s = s + pltpu.roll(s, sh, axis=0)
            sh *= 2
        cnt = jnp.minimum(pos + 1, w).astype(F32)
        pooled = s[POOL_HALO:, :] / cnt - ug
        mixed = jnp.dot(pooled.astype(BF16), wp_ref[g].astype(BF16), preferred_element_type=F32)
        po_ref[:, cols] = (mixed * sc_ref[:, cols]).astype(BF16)

    def write_q():
        q_ref[...] = (mm(width) * q_scale).astype(BF16)

    def write_k():
        k_ref[...] = mm(2 * width).astype(BF16)

    def write_v():
        v_ref[...] = mm(3 * width).astype(BF16)

    def gate_chunk(c):
        g_ref[:, c * width:(c + 1) * width] = _sigmoid(mm((4 + c) * width)).astype(BF16)

    matmuls = [write_q, write_k, write_v] + [
        functools.partial(gate_chunk, c) for c in range(g_ref.shape[1] // width)]
    for n, matmul in enumerate(matmuls):
        matmul()
        if n < len(POOL_WINDOWS):
            pool_group(n, POOL_WINDOWS[n])
    halo_ref[...] = u[tm - POOL_HALO:, :]


def _proj(x2, w_in, w_pool, pool_scale, seq, tm):
    t, d = x2.shape
    n = w_in.shape[1]
    width = d // 2
    gate_w = n - 4 * width
    groups, gd, _ = w_pool.shape
    assert seq % tm == 0 and groups * gd == width and max(POOL_WINDOWS) <= POOL_HALO + 1
    row = lambda i: (i, 0)
    blocks = (2 * (tm * d * 4 + tm * width * 4 * 2 + tm * gate_w * 2)
              + d * n * (4 + 2) + groups * gd * gd * 4 + width * 4 + POOL_HALO * width * 4)
    temps = tm * d * 2 + 3 * tm * width * 4 + 6 * (tm + POOL_HALO) * gd * 4
    return pl.pallas_call(
        functools.partial(_proj_kernel, width=width, tiles_per_seq=seq // tm),
        grid=(t // tm,),
        in_specs=[pl.BlockSpec((tm, d), row), _resident((d, n)), _resident((groups, gd, gd)),
                  _resident((1, width))],
        out_specs=[pl.BlockSpec((tm, width), row)] * 4 + [pl.BlockSpec((tm, gate_w), row)],
        out_shape=[jax.ShapeDtypeStruct((t, width), BF16)] * 4
        + [jax.ShapeDtypeStruct((t, gate_w), BF16)],
        scratch_shapes=[pltpu.VMEM((d, n), BF16), pltpu.VMEM((POOL_HALO, width), F32)],
        compiler_params=_params(("arbitrary",), blocks + temps),
        name="proj",
    )(x2, w_in, w_pool, pool_scale)


def _attn_kernel(q_ref, k_ref, v_ref, o_ref, rem_ref, acc_ref, *, tb):
    qi = pl.program_id(1)
    pairs = q_ref.shape[1] // LANES
    col_blocks = [slice(p * LANES, (p + 1) * LANES) for p in range(pairs)]
    first_head = lax.broadcasted_iota(jnp.int32, (tb, LANES), 1) < HEAD_DIM
    r = lax.broadcasted_iota(jnp.int32, (2 * tb, 2 * tb), 0)
    c = lax.broadcasted_iota(jnp.int32, (2 * tb, 2 * tb), 1)
    same_head = (r >= tb) == (c >= tb)
    cum = -jnp.concatenate([(same_head & (r >= c)).astype(BF16), same_head.astype(BF16)], axis=1)
    kcol = lax.broadcasted_iota(jnp.int32, (tb, 2 * tb), 1)
    kcol = jnp.where(kcol >= tb, kcol - tb, kcol)
    causal = kcol < lax.broadcasted_iota(jnp.int32, (tb, 2 * tb), 0)

    def stack_heads(blk):
        zero = jnp.zeros_like(blk)
        return jnp.concatenate(
            [jnp.where(first_head, blk, zero), jnp.where(first_head, zero, blk)], axis=0)

    def scores(j, rows, diagonal):
        ks = pl.multiple_of(j * tb, tb)
        zs = [lax.dot_general(q_ref[rows, cols], stack_heads(k_ref[pl.ds(ks, tb), cols]),
                              (((1,), (1,)), ((), ())), preferred_element_type=F32)
              for cols in col_blocks]
        sums = []
        for z in zs:
            softplus = jnp.maximum(z, 0.0) + jnp.log2(1.0 + jnp.exp2(-jnp.abs(z)))
            if diagonal:
                softplus = jnp.where(causal, softplus, 0.0)
            sums.append(jnp.dot(softplus.astype(BF16), cum, preferred_element_type=F32))
        return zs, sums

    def weighted_values(j, cols, z, later):
        ks = pl.multiple_of(j * tb, tb)
        a = jnp.exp2(z + later)
        return jnp.dot(a.astype(BF16), stack_heads(v_ref[pl.ds(ks, tb), cols]),
                       preferred_element_type=F32)

    def first_step(n_before):
        full, half = slice(0, tb), slice(0, tb // 2)
        z_d, sums_d = scores(qi, full, diagonal=True)
        if n_before >= 1:
            z_1, sums_1 = scores(qi - 1, full, diagonal=False)
        if n_before >= 2:
            z_2, sums_2 = scores(qi - 2, half, diagonal=False)
        owed = []
        for p, cols in enumerate(col_blocks):
            ks = pl.multiple_of(qi * tb, tb)
            a = jnp.where(causal, jnp.exp2(z_d[p] + sums_d[p][:, :2 * tb]), 0.0)
            out = jnp.dot(a.astype(BF16), stack_heads(v_ref[pl.ds(ks, tb), cols]),
                          preferred_element_type=F32)
            total = sums_d[p][:, 2 * tb:]
            if n_before >= 1:
                out = out + weighted_values(qi - 1, cols, z_1[p], sums_1[p][:, :2 * tb] + total)
                total = total + sums_1[p][:, 2 * tb:]
            acc_ref[p] = out
            rem_ref[p] = total
            if n_before >= 2:
                owed.append(total[tb // 2:, :])
                later = sums_2[p][:, :2 * tb] + total[half, :]
                acc_ref[p, half, :] += weighted_values(qi - 2, cols, z_2[p], later)
                rem_ref[p, half, :] += sums_2[p][:, 2 * tb:]
        if n_before < 2:
            return None
        return jnp.max(functools.reduce(jnp.maximum, owed)) > ATTN_DEAD_LOG2

    def step(j, rows):
        zs, sums = scores(j, rows, diagonal=False)
        for p, cols in enumerate(col_blocks):
            later = sums[p][:, :2 * tb] + rem_ref[p, rows, :]
            acc_ref[p, rows, :] += weighted_values(j, cols, zs[p], later)
            rem_ref[p, rows, :] += sums[p][:, 2 * tb:]

    def live_rows():
        worst = functools.reduce(jnp.maximum, [rem_ref[p] for p in range(pairs)])
        live = jnp.max(worst, axis=1, keepdims=True) > ATTN_DEAD_LOG2
        row = lax.broadcasted_iota(jnp.int32, (tb, 1), 0)
        return jnp.max(jnp.where(live, row + 1, 0))

    for n_before in (0, 1):
        @pl.when(qi == n_before)
        def _():
            first_step(n_before)

    @pl.when(qi >= 2)
    def _():
        second_half_owed = first_step(2)

        @pl.when(second_half_owed)
        def _():
            step(qi - 2, slice(tb // 2, tb))

    row_counts = (tb, tb // 2, tb // 4)

    def cond(carry):
        j, n_live = carry
        return (j >= 0) & (n_live > 0)

    def body(carry):
        j, n_live = carry
        for rows, fewer in zip(row_counts, row_counts[1:] + (0,)):
            @pl.when((n_live > fewer) & (n_live <= rows))
            def _():
                step(j, slice(0, rows))

        return j - 1, live_rows()

    lax.while_loop(cond, body, (qi - 3, live_rows()))
    for p, cols in enumerate(col_blocks):
        o_ref[:, cols] = acc_ref[p].astype(BF16)


def _attention(q, k, v, seq, tb):
    t, width = q.shape
    pairs = width // LANES
    blocks = 2 * (2 * tb * width * 2 + 2 * seq * width * 2) + pairs * tb * (2 * tb + LANES) * 4
    temps = pairs * 10 * tb * 2 * tb * 4 + 2 * tb * 4 * tb * 2
    qspec = pl.BlockSpec((tb, width), lambda b, i: (b * (seq // tb) + i, 0))
    kvspec = pl.BlockSpec((seq, width), lambda b, i: (b, 0))
    return pl.pallas_call(
        functools.partial(_attn_kernel, tb=tb),
        grid=(t // seq, seq // tb),
        in_specs=[qspec, kvspec, kvspec],
        out_specs=qspec,
        out_shape=jax.ShapeDtypeStruct((t, width), BF16),
        scratch_shapes=[pltpu.VMEM((pairs, tb, 2 * tb), F32), pltpu.VMEM((pairs, tb, LANES), F32)],
        compiler_params=_params(("parallel", "parallel"), blocks + temps),
        name="attn",
    )(q, k, v)


def _route(logits):
    lane = lax.broadcasted_iota(jnp.int32, logits.shape, 1)

    def first_max(vals):
        m = jnp.max(vals, axis=1, keepdims=True)
        idx = jnp.min(jnp.where(vals == m, lane, LANES), axis=1, keepdims=True)
        return m, idx

    is_group = (lane >= GROUP_LANE0) & (lane < GROUP_LANE0 + N_GROUPS)
    gm, g_lane = first_max(jnp.where(is_group, logits, NEG_BIG))
    g_prob = 1.0 / jnp.sum(jnp.where(is_group, jnp.exp(logits - gm), 0.0), axis=1, keepdims=True)
    lo = EXPERTS_PER_GROUP * (g_lane - GROUP_LANE0)
    in_group = jnp.where((lane >= lo) & (lane < lo + EXPERTS_PER_GROUP), logits, NEG_BIG)
    m1, i1 = first_max(in_group)
    m2, i2 = first_max(jnp.where(lane == i1, NEG_BIG, in_group))
    e21 = jnp.exp(m2 - m1)
    w1 = g_prob / (1.0 + e21)
    w2 = w1 * e21
    comb = jnp.where(lane == i1, w1, 0.0) + jnp.where(lane == i2, w2, 0.0)
    sel = ((lane == i1) | (lane == i2)).astype(F32)
    return comb, sel


def _merge_kernel(x_ref, po_ref, at_ref, g_ref, wpu_ref, wau_ref, wo_ref, lg_ref, lb_ref,
                  wr_ref, br_ref, x1_ref, x1b_ref, comb_ref, sel_ref, cnt_ref,
                  wpub_ref, waub_ref, wob_ref, wrb_ref, *, alpha):
    tm, d = x_ref.shape
    _cast_once(wpu_ref, wpub_ref)
    _cast_once(wau_ref, waub_ref)
    _cast_once(wo_ref, wob_ref)

    @pl.when(pl.program_id(0) == 0)
    def _():
        w_r = wr_ref[...]
        hi = w_r.astype(BF16)
        wrb_ref[:, :LANES] = hi
        wrb_ref[:, LANES:] = (w_r - hi.astype(F32)).astype(BF16)

    def mix(r):
        a = jnp.dot(po_ref[r, :], wpub_ref[...], preferred_element_type=F32)
        b = jnp.dot(at_ref[r, :], waub_ref[...], preferred_element_type=F32)
        merged = (g_ref[r, :d].astype(F32) * a + g_ref[r, d:].astype(F32) * b).astype(BF16)
        return alpha * x_ref[r, :] + jnp.dot(merged, wob_ref[...], preferred_element_type=F32)

    def norm_and_route(r, h):
        x1 = _layer_norm(h, lg_ref[...], lb_ref[...])
        xh = x1.astype(BF16)
        x1_ref[r, :] = x1
        x1b_ref[r, :] = xh
        xl = (x1 - xh.astype(F32)).astype(BF16)
        by_hi = jnp.dot(xh, wrb_ref[...], preferred_element_type=F32)
        by_lo = jnp.dot(xl, wrb_ref[:, :LANES], preferred_element_type=F32)
        comb, sel = _route(by_hi[:, :LANES] + by_hi[:, LANES:] + by_lo + br_ref[...])
        comb_ref[r, :] = comb
        sel_ref[r, :] = sel.astype(BF16)
        return jnp.sum(sel, axis=0, keepdims=True)

    cnt_ref[0] = sum(_skewed(_row_subs(tm), mix, norm_and_route))


def _merge(x2, pool_out, attn_out, gates, w_pu, w_au, w_o, ln_g, ln_b, w_r, b_r, alpha, tm):
    t, d = x2.shape
    width = pool_out.shape[1]
    row = lambda i: (i, 0)
    blocks = (2 * (tm * d * 4 + 2 * tm * width * 2 + tm * 2 * d * 2
                   + tm * d * 4 + tm * d * 2 + tm * LANES * (4 + 2) + LANES * 4)
              + (2 * width * d + d * d + d * LANES) * (4 + 2) + 2 * d * 4 + LANES * 4)
    temps = 5 * tm * d * 4
    return pl.pallas_call(
        functools.partial(_merge_kernel, alpha=alpha),
        grid=(t // tm,),
        in_specs=[pl.BlockSpec((tm, d), row), pl.BlockSpec((tm, width), row),
                  pl.BlockSpec((tm, width), row), pl.BlockSpec((tm, 2 * d), row),
                  _resident((width, d)), _resident((width, d)), _resident((d, d)),
                  _resident((1, d)), _resident((1, d)), _resident((d, LANES)),
                  _resident((1, LANES))],
        out_specs=[pl.BlockSpec((tm, d), row), pl.BlockSpec((tm, d), row),
                   pl.BlockSpec((tm, LANES), row), pl.BlockSpec((tm, LANES), row),
                   pl.BlockSpec((1, 1, LANES), lambda i: (i, 0, 0))],
        out_shape=[jax.ShapeDtypeStruct((t, d), F32), jax.ShapeDtypeStruct((t, d), BF16),
                   jax.ShapeDtypeStruct((t, LANES), F32), jax.ShapeDtypeStruct((t, LANES), BF16),
                   jax.ShapeDtypeStruct((t // tm, 1, LANES), F32)],
        scratch_shapes=[pltpu.VMEM((width, d), BF16), pltpu.VMEM((width, d), BF16),
                        pltpu.VMEM((d, d), BF16), pltpu.VMEM((d, 2 * LANES), BF16)],
        compiler_params=_params(("arbitrary",), blocks + temps),
        name="merge",
    )(x2, pool_out, attn_out, gates, w_pu, w_au, w_o, ln_g, ln_b, w_r, b_r)


def _routing_plan(counts):
    cnt = counts[:, 0, :N_EXPERTS].astype(jnp.int32)
    nch = (cnt + (CHUNK - 1)) // CHUNK
    local_end = jnp.cumsum(nch, axis=1)
    local_start = local_end - nch
    block_chunks = local_end[:, -1]
    before_block = jnp.cumsum(nch, axis=0) - nch
    expert_chunks = jnp.sum(nch, axis=0)
    chunks_per_tile = EXPERT_TILE // CHUNK
    expert_tiles = (expert_chunks + (chunks_per_tile - 1)) // chunks_per_tile
    tiles_end = jnp.cumsum(expert_tiles)
    region_start = (tiles_end - expert_tiles) * chunks_per_tile
    segment_dst = region_start[None, :] + before_block
    i32 = lambda a: a.astype(jnp.int32)
    return dict(seg_len=i32(nch), seg_src=i32(local_start), seg_dst=i32(segment_dst),
                block_chunks=i32(block_chunks),
                expert_tiles=i32(expert_tiles), tiles_used=i32(tiles_end[-1:]),
                pad_start=i32(region_start + expert_chunks),
                pad_count=i32(expert_tiles * chunks_per_tile - expert_chunks))


def _local_positions(sel):
    tb = sel.shape[0]
    earlier = (lax.broadcasted_iota(jnp.int32, (tb, tb), 1)
               < lax.broadcasted_iota(jnp.int32, (tb, tb), 0)).astype(BF16)
    rank = jnp.dot(earlier, sel, preferred_element_type=F32)
    cnt = jnp.sum(sel.astype(F32), axis=0, keepdims=True)
    nch = jnp.floor((cnt + (CHUNK - 1)) * (1.0 / CHUNK))
    lower = (lax.broadcasted_iota(jnp.int32, (LANES, LANES), 0)
             < lax.broadcasted_iota(jnp.int32, (LANES, LANES), 1)).astype(BF16)
    start = CHUNK * jnp.dot(jnp.broadcast_to(nch, (8, LANES)).astype(BF16), lower,
                            preferred_element_type=F32)[0:1]
    pos = rank + start
    chosen = sel > 0
    pos_lo = jnp.min(jnp.where(chosen, pos, float(LOCAL_ROWS)), axis=1, keepdims=True)
    pos_hi = jnp.max(jnp.where(chosen, pos, -1.0), axis=1, keepdims=True)
    return pos, pos_lo, pos_hi


def _for_each(n, fn):
    lax.fori_loop(0, n, lambda c, carry: (fn(c), carry)[1], 0)


def _segment_rows(len_ref, src_ref, dst_ref, blk, e):
    n = len_ref[blk, e] * CHUNK
    src = pl.multiple_of(src_ref[blk, e] * CHUNK, CHUNK)
    dst = pl.multiple_of(dst_ref[blk, e] * CHUNK, CHUNK)
    return n, src, dst


def _scatter_kernel(len_ref, src_ref, dst_ref, nchunk_ref, pad_start_ref, pad_count_ref, used_ref,
                    x_ref, sel_ref, g_ref, loc_ref, zero_ref, sems):
    b = pl.program_id(0)
    last = pl.num_programs(0) - 1
    slot = b % 2
    tb = x_ref.shape[0]

    def start_block(blk, slot):
        def start_segment(e):
            n, src, dst = _segment_rows(len_ref, src_ref, dst_ref, blk, e)

            @pl.when(n > 0)
            def _():
                pltpu.make_async_copy(loc_ref.at[slot, pl.ds(src, n)], g_ref.at[pl.ds(dst, n)],
                                      sems.at[slot]).start()

        _for_each(N_EXPERTS, start_segment)

    def wait_block(blk, slot):
        n = nchunk_ref[blk] * CHUNK
        pltpu.make_async_copy(loc_ref.at[slot, pl.ds(0, n)], g_ref.at[pl.ds(0, n)],
                              sems.at[slot]).wait()

    def pad_copy(e):
        n = pad_count_ref[e] * CHUNK
        dst = pl.multiple_of(pad_start_ref[e] * CHUNK, CHUNK)
        return n, pltpu.make_async_copy(zero_ref.at[pl.ds(0, n)], g_ref.at[pl.ds(dst, n)],
                                        sems.at[2])

    def unused_tile_copy(i):
        dst = pl.multiple_of(i * EXPERT_TILE, EXPERT_TILE)
        return pltpu.make_async_copy(zero_ref, g_ref.at[pl.ds(dst, EXPERT_TILE)], sems.at[2])

    @pl.when(b >= 2)
    def _():
        wait_block(b - 2, slot)

    _, pos_lo, pos_hi = _local_positions(sel_ref[...])
    lo = pos_lo.astype(jnp.int32)
    hi = pos_hi.astype(jnp.int32)
    x = x_ref[...]

    r = lax.broadcasted_iota(jnp.int32, (tb, LOCAL_ROWS), 1)
    perm = ((r == lo) | (r == hi)).astype(BF16)
    rows = lax.dot_general(perm, x, (((0,), (0,)), ((), ())), preferred_element_type=F32)
    loc_ref[slot] = rows.astype(BF16)
    start_block(b, slot)

    @pl.when(b == last)
    def _():
        zero_ref[...] = jnp.zeros_like(zero_ref)
        n_unused = g_ref.shape[0] // EXPERT_TILE - used_ref[0]

        def each_pad(act):
            def one(e):
                n, copy = pad_copy(e)

                @pl.when(n > 0)
                def _():
                    act(copy)

            _for_each(N_EXPERTS, one)

        each_pad(lambda copy: copy.start())
        _for_each(n_unused, lambda i: unused_tile_copy(used_ref[0] + i).start())
        each_pad(lambda copy: copy.wait())
        _for_each(n_unused, lambda i: unused_tile_copy(used_ref[0] + i).wait())

        @pl.when(b >= 1)
        def _():
            wait_block(b - 1, 1 - slot)

        wait_block(b, slot)


def _scatter(x1b, sel, plan, n_tiles):
    t, d = x1b.shape
    tb = ROUTE_BLOCK
    row = lambda b, *_: (b, 0)
    blocks = 2 * (tb * d * 2 + tb * LANES * 2) + 2 * LOCAL_ROWS * d * 2 + EXPERT_TILE * d * 2
    temps = tb * tb * 2 + 6 * tb * LANES * 4 + tb * LOCAL_ROWS * 6 + LOCAL_ROWS * d * 6
    return pl.pallas_call(
        _scatter_kernel,
        grid_spec=pltpu.PrefetchScalarGridSpec(
            num_scalar_prefetch=7,
            grid=(t // tb,),
            in_specs=[pl.BlockSpec((tb, d), row), pl.BlockSpec((tb, LANES), row)],
            out_specs=pl.BlockSpec(memory_space=pl.ANY),
            scratch_shapes=[pltpu.VMEM((2, LOCAL_ROWS, d), BF16),
                            pltpu.VMEM((EXPERT_TILE, d), BF16), pltpu.SemaphoreType.DMA((3,))]),
        out_shape=jax.ShapeDtypeStruct((n_tiles * EXPERT_TILE, d), BF16),
        compiler_params=_params(("arbitrary",), blocks + temps),
        name="scatter",
    )(plan["seg_len"], plan["seg_src"], plan["seg_dst"], plan["block_chunks"], plan["pad_start"],
      plan["pad_count"], plan["tiles_used"], x1b, sel)


def _experts_kernel(ntile_ref, used_ref, g_ref, wg_ref, wu_ref, wd_ref, y_ref,
                    x_buf, y_buf, wg_buf, wu_buf, wd_buf, wgb_ref, wub_ref, wdb_ref,
                    x_sems, y_sems, w_sems):
    n_experts = wg_ref.shape[0]
    tm = EXPERT_TILE
    used = used_ref[0]

    def weight_copies(e, slot):
        return [pltpu.make_async_copy(src.at[e], dst.at[slot], w_sems.at[slot])
                for src, dst in ((wg_ref, wg_buf), (wu_ref, wu_buf), (wd_ref, wd_buf))]

    def x_copy(t, slot):
        rows = pl.ds(pl.multiple_of(t * tm, tm), MACRO_TILES * tm)
        return pltpu.make_async_copy(g_ref.at[rows], x_buf.at[slot], x_sems.at[slot])

    def y_copy(t, k, slot):
        n = k * tm
        dst = pl.ds(pl.multiple_of(t * tm, tm), n)
        return pltpu.make_async_copy(y_buf.at[slot, pl.ds(0, n)], y_ref.at[dst], y_sems.at[slot])

    def wait_y(t, k, slot):
        @pl.when(k > 0)
        def _():
            y_copy(t, k, slot).wait()

    def mlp(slot, rows):
        x = x_buf[slot, :rows, :]
        hg = jnp.dot(x, wgb_ref[...], preferred_element_type=F32)
        hu = jnp.dot(x, wub_ref[...], preferred_element_type=F32)
        h = hg * _sigmoid(hg) * hu
        y_buf[slot, :rows, :] = jnp.dot(h.astype(BF16), wdb_ref[...],
                                        preferred_element_type=F32).astype(BF16)

    for c in weight_copies(0, 0):
        c.start()
    x_copy(0, 0).start()

    def run_expert(e, carry):
        wslot = e % 2
        for c in weight_copies(e, wslot):
            c.wait()

        @pl.when(e + 1 < n_experts)
        def _():
            for c in weight_copies(e + 1, 1 - wslot):
                c.start()

        @pl.when(ntile_ref[e] > 0)
        def _():
            wgb_ref[...] = wg_buf[wslot].astype(BF16)
            wub_ref[...] = wu_buf[wslot].astype(BF16)
            wdb_ref[...] = wd_buf[wslot].astype(BF16)

        def run_macro(m, carry):
            t, step, k1, t1, k2, t2 = carry
            k = jnp.minimum(MACRO_TILES, ntile_ref[e] - m * MACRO_TILES)
            slot = step % 2
            x_copy(t, slot).wait()

            @pl.when(t + k < used)
            def _():
                x_copy(t + k, 1 - slot).start()

            wait_y(t2, k2, slot)
            for tiles in range(1, MACRO_TILES + 1):
                @pl.when(k == tiles)
                def _():
                    mlp(slot, tiles * tm)

            y_copy(t, k, slot).start()
            return t + k, step + 1, k, t, k1, t1

        n_macro = (ntile_ref[e] + (MACRO_TILES - 1)) // MACRO_TILES
        return lax.fori_loop(0, n_macro, run_macro, carry)

    zero = jnp.int32(0)
    _, step, k1, t1, k2, t2 = lax.fori_loop(0, n_experts, run_expert, (zero,) * 6)
    wait_y(t2, k2, step % 2)
    wait_y(t1, k1, (step + 1) % 2)
    y_buf[0, :tm, :] = jnp.zeros((tm, y_buf.shape[2]), BF16)
    n_unused = y_ref.shape[0] // tm - used
    _for_each(n_unused, lambda i: y_copy(used + i, 1, 0).start())
    _for_each(n_unused, lambda i: y_copy(used + i, 1, 0).wait())


def _experts(sorted_x, plan, w_eg, w_eu, w_ed):
    rows, d = sorted_x.shape
    _, _, de = w_eg.shape
    tm = EXPERT_TILE
    any_space = pl.BlockSpec(memory_space=pl.ANY)
    big = MACRO_TILES * tm
    scratch = 2 * 2 * big * d * 2 + 2 * 3 * d * de * 4 + 3 * d * de * 2
    temps = 3 * big * de * 4 + big * d * 4 + d * de * 4
    return pl.pallas_call(
        _experts_kernel,
        grid_spec=pltpu.PrefetchScalarGridSpec(
            num_scalar_prefetch=2,
            grid=(1,),
            in_specs=[any_space] * 4,
            out_specs=any_space,
            scratch_shapes=[pltpu.VMEM((2, big, d), BF16), pltpu.VMEM((2, big, d), BF16),
                            pltpu.VMEM((2, d, de), F32), pltpu.VMEM((2, d, de), F32),
                            pltpu.VMEM((2, de, d), F32),
                            pltpu.VMEM((d, de), BF16), pltpu.VMEM((d, de), BF16),
                            pltpu.VMEM((de, d), BF16),
                            pltpu.SemaphoreType.DMA((2,)), pltpu.SemaphoreType.DMA((2,)),
                            pltpu.SemaphoreType.DMA((2,))]),
        out_shape=jax.ShapeDtypeStruct((rows, d), BF16),
        compiler_params=_params(("arbitrary",), scratch + temps),
        name="experts",
    )(plan["expert_tiles"], plan["tiles_used"], sorted_x, w_eg, w_eu, w_ed)


def _combine_kernel(len_ref, src_ref, dst_ref, nchunk_ref, x1_ref, comb_ref, sel_ref, p_ref,
                    wpg_ref, wpp_ref,
                    lg_ref, lb_ref, y_ref, o_ref, loc_ref, wpgb_ref, wppb_ref, sems, *, alpha):
    b = pl.program_id(0)
    nb = pl.num_programs(0)
    slot = b % 2
    tb, d = x1_ref.shape
    _cast_once(wpg_ref, wpgb_ref)
    _cast_once(wpp_ref, wppb_ref)

    def start_block(blk, slot):
        def start_segment(e):
            n, local, sorted_at = _segment_rows(len_ref, src_ref, dst_ref, blk, e)

            @pl.when(n > 0)
            def _():
                pltpu.make_async_copy(y_ref.at[pl.ds(sorted_at, n)],
                                      loc_ref.at[slot, pl.ds(local, n)], sems.at[slot]).start()

        _for_each(N_EXPERTS, start_segment)

    @pl.when(b == 0)
    def _():
        loc_ref[...] = jnp.zeros_like(loc_ref)
        start_block(0, 0)

    @pl.when(b + 1 < nb)
    def _():
        start_block(b + 1, 1 - slot)

    n_rows = nchunk_ref[b] * CHUNK
    pltpu.make_async_copy(y_ref.at[pl.ds(0, n_rows)], loc_ref.at[slot, pl.ds(0, n_rows)],
                          sems.at[slot]).wait()

    sel = sel_ref[...]
    pos, pos_lo, pos_hi = _local_positions(sel)
    chosen = sel > 0
    comb = comb_ref[...]
    w_lo = jnp.sum(jnp.where(chosen & (pos == pos_lo), comb, 0.0), axis=1, keepdims=True)
    w_hi = jnp.sum(jnp.where(chosen & (pos == pos_hi), comb, 0.0), axis=1, keepdims=True)
    lo = pos_lo.astype(jnp.int32)
    hi = pos_hi.astype(jnp.int32)

    sorted_row = lax.broadcasted_iota(jnp.int32, (tb // ROW_SUBS, LOCAL_ROWS), 1)

    def branches(r):
        weights = (jnp.where(sorted_row == lo[r], w_lo[r], 0.0)
                   + jnp.where(sorted_row == hi[r], w_hi[r], 0.0)).astype(BF16)
        moe = jnp.dot(weights, loc_ref[slot], preferred_element_type=F32)
        gate = jnp.dot(x1_ref[r, :].astype(BF16), wpgb_ref[...], preferred_element_type=F32)
        emb = jnp.dot(p_ref[r, :].astype(BF16), wppb_ref[...], preferred_element_type=F32)
        return moe, gate, emb

    def finish(r, parts):
        moe, gate, emb = parts
        h = alpha * x1_ref[r, :] + moe + _sigmoid(gate) * emb
        o_ref[r, :] = _layer_norm(h, lg_ref[...], lb_ref[...])

    _skewed(_row_subs(tb), branches, finish)


def _combine(y, plan, x1, comb, sel, p2, w_pg, w_pp, ln_g, ln_b, alpha):
    t, d = x1.shape
    pd = p2.shape[1]
    tb = ROUTE_BLOCK
    row = lambda b, *_: (b, 0)
    blocks = (2 * (2 * tb * d * 4 + tb * LANES * (4 + 2) + tb * pd * 4)
              + (d * d + pd * d) * (4 + 2) + 2 * d * 4 + 2 * LOCAL_ROWS * d * 2)
    temps = tb * tb * 2 + 8 * tb * LANES * 4 + tb * LOCAL_ROWS * 10 + 5 * tb * d * 4
    return pl.pallas_call(
        functools.partial(_combine_kernel, alpha=alpha),
        grid_spec=pltpu.PrefetchScalarGridSpec(
            num_scalar_prefetch=4,
            grid=(t // tb,),
            in_specs=[pl.BlockSpec((tb, d), row), pl.BlockSpec((tb, LANES), row),
                      pl.BlockSpec((tb, LANES), row), pl.BlockSpec((tb, pd), row),
                      _resident((d, d)), _resident((pd, d)), _resident((1, d)), _resident((1, d)),
                      pl.BlockSpec(memory_space=pl.ANY)],
            out_specs=pl.BlockSpec((tb, d), row),
            scratch_shapes=[pltpu.VMEM((2, LOCAL_ROWS, d), BF16), pltpu.VMEM((d, d), BF16),
                            pltpu.VMEM((pd, d), BF16), pltpu.SemaphoreType.DMA((2,))]),
        out_shape=jax.ShapeDtypeStruct((t, d), F32),
        compiler_params=_params(("arbitrary",), blocks + temps),
        name="combine",
    )(plan["seg_len"], plan["seg_src"], plan["seg_dst"], plan["block_chunks"], x1, comb, sel, p2,
      w_pg, w_pp, ln_g, ln_b, y)


def kernel(x, p, w_in, w_pool, pool_scale, w_pu, w_au, w_o, ln1_g, ln1_b, w_rg, b_rg, w_re, b_re,
           w_eg, w_eu, w_ed, w_pg, w_pp, ln2_g, ln2_b):
    bsz, seq, d = x.shape
    depth = w_in.shape[0]
    t = bsz * seq
    de = w_eg.shape[-1]
    alpha = (2.0 * depth) ** 0.25
    assert w_rg.shape[2] == N_GROUPS and w_re.shape[1:] == (N_GROUPS, d, EXPERTS_PER_GROUP)
    assert w_in.shape[2] == 4 * d and w_pool.shape[1] == len(POOL_WINDOWS)
    assert t % ROUTE_BLOCK == 0 and N_EXPERTS + N_GROUPS <= LANES
    n_blocks = t // ROUTE_BLOCK
    n_tiles = -(-(2 * t + n_blocks * N_EXPERTS * (CHUNK - 1) + N_EXPERTS * (EXPERT_TILE - CHUNK))
                // EXPERT_TILE) + MACRO_TILES - 1

    x2 = x.reshape(t, d)
    for i in range(depth):
        pool_out, q, k, v, gates = _in_hbm(
            *_proj(x2, w_in[i], w_pool[i], pool_scale[i][None, :], seq, tm=512))
        attn_out, = _in_hbm(_attention(q, k, v, seq, tb=128))

        w_r = jnp.concatenate(
            [w_re[i].transpose(1, 0, 2).reshape(d, N_EXPERTS), w_rg[i]], axis=1)
        w_r = jnp.pad(w_r, ((0, 0), (0, LANES - w_r.shape[1])))
        b_r = jnp.pad(jnp.concatenate([b_re[i].reshape(-1), b_rg[i]]),
                      (0, LANES - N_GROUPS - N_EXPERTS))[None, :]

        x1, x1b, comb, sel, counts = _merge(
            x2, pool_out, attn_out, gates, *_in_hbm(w_pu[i], w_au[i], w_o[i]), ln1_g[i][None, :],
            ln1_b[i][None, :], w_r, b_r, alpha, tm=ROUTE_BLOCK)
        x1, x1b, comb, sel = _in_hbm(x1, x1b, comb, sel)

        plan = _routing_plan(counts)
        sorted_x, = _in_hbm(_scatter(x1b, sel, plan, n_tiles))
        y, = _in_hbm(_experts(sorted_x, plan, w_eg[i].reshape(N_EXPERTS, d, de),
                              w_eu[i].reshape(N_EXPERTS, d, de), w_ed[i].reshape(N_EXPERTS, de, d)))
        x2 = _combine(y, plan, x1, comb, sel, p[i].reshape(t, -1), *_in_hbm(w_pg[i], w_pp[i]),
                      ln2_g[i][None, :], ln2_b[i][None, :], alpha)
    return x2.reshape(bsz, seq, d)
```

```python
import functools
import math

import jax
import jax.numpy as jnp
from jax import lax
from jax.experimental import pallas as pl
from jax.experimental.pallas import tpu as pltpu

F32 = jnp.float32
BF16 = jnp.bfloat16

LANES = 128
POOL_WINDOWS = (2, 4, 8, 16)
POOL_HALO = 16
HEAD_DIM = 64
N_GROUPS = 4
EXPERTS_PER_GROUP = 8
N_EXPERTS = N_GROUPS * EXPERTS_PER_GROUP
LN_EPS = 1e-5
GROUP_LANE0 = N_EXPERTS
NEG_BIG = -1e30
ROUTE_BLOCK = 512
CHUNK = 16
EXPERT_TILE = 256
MACRO_TILES = 8
SORT_ROWS = 256
ROW_SUBS = 2
LOCAL_ROWS = -(-(2 * ROUTE_BLOCK + N_EXPERTS * (CHUNK - 1)) // SORT_ROWS) * SORT_ROWS
SORT_HEIGHTS = (LOCAL_ROWS - SORT_ROWS, LOCAL_ROWS - SORT_ROWS // 2, LOCAL_ROWS)
ATTN_DEAD_LOG2 = -160.0
VMEM_CAP_BYTES = 56 * 1024 * 1024


def _params(sem, vmem_bytes):
    return pltpu.CompilerParams(
        dimension_semantics=sem, vmem_limit_bytes=min(int(vmem_bytes), VMEM_CAP_BYTES))


def _layer_norm(h, g, b):
    mu = jnp.mean(h, axis=-1, keepdims=True)
    c = h - mu
    var = jnp.mean(c * c, axis=-1, keepdims=True)
    return c * lax.rsqrt(var + LN_EPS) * g + b


def _sigmoid(z):
    return 1.0 / (1.0 + jnp.exp(-z))


def _row_subs(rows):
    return [slice(k * (rows // ROW_SUBS), (k + 1) * (rows // ROW_SUBS)) for k in range(ROW_SUBS)]


def _skewed(subs, first, second):
    out, pending = [], None
    for r in subs:
        mid = first(r)
        if pending is not None:
            out.append(second(*pending))
        pending = (r, mid)
    out.append(second(*pending))
    return out


def _in_hbm(*arrays):
    if not all(isinstance(a, jax.core.Tracer) for a in arrays):
        return list(arrays)
    return [pltpu.with_memory_space_constraint(a, pltpu.HBM) for a in arrays]


def _resident(shape):
    return pl.BlockSpec(shape, lambda *_: (0,) * len(shape), pipeline_mode=pl.Buffered(1))


def _cast_once(w_ref, wb_ref):
    @pl.when(pl.program_id(0) == 0)
    def _():
        wb_ref[...] = w_ref[...].astype(BF16)


def _proj_kernel(x_ref, w_ref, wp_ref, sc_ref, po_ref, q_ref, k_ref, v_ref, g_ref, wb_ref, halo_ref,
                 *, width, tiles_per_seq):
    q_scale = math.log2(math.e) / math.sqrt(HEAD_DIM)
    tile_in_seq = pl.program_id(0) % tiles_per_seq
    _cast_once(w_ref, wb_ref)
    xb = x_ref[...].astype(BF16)

    def mm(lo):
        return jnp.dot(xb, wb_ref[:, lo:lo + width], preferred_element_type=F32)

    @pl.when(tile_in_seq == 0)
    def _():
        halo_ref[...] = jnp.zeros_like(halo_ref)

    u = mm(0)
    tm = u.shape[0]
    gd = wp_ref.shape[1]
    pos = tile_in_seq * tm + lax.broadcasted_iota(jnp.int32, (tm, gd), 0)

    def pool_group(g, w):
        cols = slice(g * gd, (g + 1) * gd)
        ug = u[:, cols]
        s = jnp.concatenate([halo_ref[:, cols], ug], axis=0)
        sh = 1
        while sh < w:
            s = s + pltpu.roll(s, sh, axis=0)
            sh *= 2
        cnt = jnp.minimum(pos + 1, w).astype(F32)
        pooled = s[POOL_HALO:, :] / cnt - ug
        mixed = jnp.dot(pooled.astype(BF16), wp_ref[g].astype(BF16), preferred_element_type=F32)
        po_ref[:, cols] = (mixed * sc_ref[:, cols]).astype(BF16)

    def write_q():
        q_ref[...] = (mm(width) * q_scale).astype(BF16)

    def write_k():
        k_ref[...] = mm(2 * width).astype(BF16)

    def write_v():
        v_ref[...] = mm(3 * width).astype(BF16)

    def gate_chunk(c):
        g_ref[:, c * width:(c + 1) * width] = _sigmoid(mm((4 + c) * width)).astype(BF16)

    matmuls = [write_q, write_k, write_v] + [
        functools.partial(gate_chunk, c) for c in range(g_ref.shape[1] // width)]
    for n, matmul in enumerate(matmuls):
        matmul()
        if n < len(POOL_WINDOWS):
            pool_group(n, POOL_WINDOWS[n])
    halo_ref[...] = u[tm - POOL_HALO:, :]


def _proj(x2, w_in, w_pool, pool_scale, seq, tm):
    t, d = x2.shape
    n = w_in.shape[1]
    width = d // 2
    gate_w = n - 4 * width
    groups, gd, _ = w_pool.shape
    assert seq % tm == 0 and groups * gd == width and max(POOL_WINDOWS) <= POOL_HALO + 1
    row = lambda i: (i, 0)
    blocks = (2 * (tm * d * 4 + tm * width * 4 * 2 + tm * gate_w * 2)
              + d * n * (4 + 2) + groups * gd * gd * 4 + width * 4 + POOL_HALO * width * 4)
    temps = tm * d * 2 + 3 * tm * width * 4 + 6 * (tm + POOL_HALO) * gd * 4
    return pl.pallas_call(
        functools.partial(_proj_kernel, width=width, tiles_per_seq=seq // tm),
        grid=(t // tm,),
        in_specs=[pl.BlockSpec((tm, d), row), _resident((d, n)), _resident((groups, gd, gd)),
                  _resident((1, width))],
        out_specs=[pl.BlockSpec((tm, width), row)] * 4 + [pl.BlockSpec((tm, gate_w), row)],
        out_shape=[jax.ShapeDtypeStruct((t, width), BF16)] * 4
        + [jax.ShapeDtypeStruct((t, gate_w), BF16)],
        scratch_shapes=[pltpu.VMEM((d, n), BF16), pltpu.VMEM((POOL_HALO, width), F32)],
        compiler_params=_params(("arbitrary",), blocks + temps),
        name="proj",
    )(x2, w_in, w_pool, pool_scale)


def _attn_kernel(q_ref, k_ref, v_ref, o_ref, rem_ref, acc_ref, *, tb):
    qi = pl.program_id(1)
    pairs = q_ref.shape[1] // LANES
    col_blocks = [slice(p * LANES, (p + 1) * LANES) for p in range(pairs)]
    first_head = lax.broadcasted_iota(jnp.int32, (tb, LANES), 1) < HEAD_DIM
    r = lax.broadcasted_iota(jnp.int32, (2 * tb, 2 * tb), 0)
    c = lax.broadcasted_iota(jnp.int32, (2 * tb, 2 * tb), 1)
    same_head = (r >= tb) == (c >= tb)
    cum = -jnp.concatenate([(same_head & (r >= c)).astype(BF16), same_head.astype(BF16)], axis=1)
    kcol = lax.broadcasted_iota(jnp.int32, (tb, 2 * tb), 1)
    kcol = jnp.where(kcol >= tb, kcol - tb, kcol)
    causal = kcol < lax.broadcasted_iota(jnp.int32, (tb, 2 * tb), 0)

    def stack_heads(blk):
        zero = jnp.zeros_like(blk)
        return jnp.concatenate(
            [jnp.where(first_head, blk, zero), jnp.where(first_head, zero, blk)], axis=0)

    def scores(j, rows, diagonal):
        ks = pl.multiple_of(j * tb, tb)
        zs = [lax.dot_general(q_ref[rows, cols], stack_heads(k_ref[pl.ds(ks, tb), cols]),
                              (((1,), (1,)), ((), ())), preferred_element_type=F32)
              for cols in col_blocks]
        sums = []
        for z in zs:
            softplus = jnp.maximum(z, 0.0) + jnp.log2(1.0 + jnp.exp2(-jnp.abs(z)))
            if diagonal:
                softplus = jnp.where(causal, softplus, 0.0)
            sums.append(jnp.dot(softplus.astype(BF16), cum, preferred_element_type=F32))
        return zs, sums

    def weighted_values(j, cols, z, later):
        ks = pl.multiple_of(j * tb, tb)
        a = jnp.exp2(z + later)
        return jnp.dot(a.astype(BF16), stack_heads(v_ref[pl.ds(ks, tb), cols]),
                       preferred_element_type=F32)

    def first_step(n_before):
        full, half = slice(0, tb), slice(0, tb // 2)
        z_d, sums_d = scores(qi, full, diagonal=True)
        if n_before >= 1:
            z_1, sums_1 = scores(qi - 1, full, diagonal=False)
        if n_before >= 2:
            z_2, sums_2 = scores(qi - 2, half, diagonal=False)
        owed = []
        for p, cols in enumerate(col_blocks):
            ks = pl.multiple_of(qi * tb, tb)
            a = jnp.where(causal, jnp.exp2(z_d[p] + sums_d[p][:, :2 * tb]), 0.0)
            out = jnp.dot(a.astype(BF16), stack_heads(v_ref[pl.ds(ks, tb), cols]),
                          preferred_element_type=F32)
            total = sums_d[p][:, 2 * tb:]
            if n_before >= 1:
                out = out + weighted_values(qi - 1, cols, z_1[p], sums_1[p][:, :2 * tb] + total)
                total = total + sums_1[p][:, 2 * tb:]
            acc_ref[p] = out
            rem_ref[p] = total
            if n_before >= 2:
                owed.append(total[tb // 2:, :])
                later = sums_2[p][:, :2 * tb] + total[half, :]
                acc_ref[p, half, :] += weighted_values(qi - 2, cols, z_2[p], later)
                rem_ref[p, half, :] += sums_2[p][:, 2 * tb:]
        if n_before < 2:
            return None
        return jnp.max(functools.reduce(jnp.maximum, owed)) > ATTN_DEAD_LOG2

    def step(j, rows):
        zs, sums = scores(j, rows, diagonal=False)
        for p, cols in enumerate(col_blocks):
            later = sums[p][:, :2 * tb] + rem_ref[p, rows, :]
            acc_ref[p, rows, :] += weighted_values(j, cols, zs[p], later)
            rem_ref[p, rows, :] += sums[p][:, 2 * tb:]

    def live_rows():
        worst = functools.reduce(jnp.maximum, [rem_ref[p] for p in range(pairs)])
        live = jnp.max(worst, axis=1, keepdims=True) > ATTN_DEAD_LOG2
        row = lax.broadcasted_iota(jnp.int32, (tb, 1), 0)
        return jnp.max(jnp.where(live, row + 1, 0))

    for n_before in (0, 1):
        @pl.when(qi == n_before)
        def _():
            first_step(n_before)

    @pl.when(qi >= 2)
    def _():
        second_half_owed = first_step(2)

        @pl.when(second_half_owed)
        def _():
            step(qi - 2, slice(tb // 2, tb))

    row_counts = (tb, tb // 2, tb // 4)

    def cond(carry):
        j, n_live = carry
        return (j >= 0) & (n_live > 0)

    def body(carry):
        j, n_live = carry
        for rows, fewer in zip(row_counts, row_counts[1:] + (0,)):
            @pl.when((n_live > fewer) & (n_live <= rows))
            def _():
                step(j, slice(0, rows))

        return j - 1, live_rows()

    lax.while_loop(cond, body, (qi - 3, live_rows()))
    for p, cols in enumerate(col_blocks):
        o_ref[:, cols] = acc_ref[p].astype(BF16)


def _attention(q, k, v, seq, tb):
    t, width = q.shape
    pairs = width // LANES
    blocks = 2 * (2 * tb * width * 2 + 2 * seq * width * 2) + pairs * tb * (2 * tb + LANES) * 4
    temps = pairs * 10 * tb * 2 * tb * 4 + 2 * tb * 4 * tb * 2
    qspec = pl.BlockSpec((tb, width), lambda b, i: (b * (seq // tb) + i, 0))
    kvspec = pl.BlockSpec((seq, width), lambda b, i: (b, 0))
    return pl.pallas_call(
        functools.partial(_attn_kernel, tb=tb),
        grid=(t // seq, seq // tb),
        in_specs=[qspec, kvspec, kvspec],
        out_specs=qspec,
        out_shape=jax.ShapeDtypeStruct((t, width), BF16),
        scratch_shapes=[pltpu.VMEM((pairs, tb, 2 * tb), F32), pltpu.VMEM((pairs, tb, LANES), F32)],
        compiler_params=_params(("parallel", "parallel"), blocks + temps),
        name="attn",
    )(q, k, v)


def _route(logits):
    lane = lax.broadcasted_iota(jnp.int32, logits.shape, 1)

    def first_max(vals):
        m = jnp.max(vals, axis=1, keepdims=True)
        idx = jnp.min(jnp.where(vals == m, lane, LANES), axis=1, keepdims=True)
        return m, idx

    is_group = (lane >= GROUP_LANE0) & (lane < GROUP_LANE0 + N_GROUPS)
    gm, g_lane = first_max(jnp.where(is_group, logits, NEG_BIG))
    g_prob = 1.0 / jnp.sum(jnp.where(is_group, jnp.exp(logits - gm), 0.0), axis=1, keepdims=True)
    lo = EXPERTS_PER_GROUP * (g_lane - GROUP_LANE0)
    in_group = jnp.where((lane >= lo) & (lane < lo + EXPERTS_PER_GROUP), logits, NEG_BIG)
    m1, i1 = first_max(in_group)
    m2, i2 = first_max(jnp.where(lane == i1, NEG_BIG, in_group))
    e21 = jnp.exp(m2 - m1)
    w1 = g_prob / (1.0 + e21)
    w2 = w1 * e21
    comb = jnp.where(lane == i1, w1, 0.0) + jnp.where(lane == i2, w2, 0.0)
    sel = ((lane == i1) | (lane == i2)).astype(F32)
    return comb, sel


def _merge_kernel(x_ref, po_ref, at_ref, g_ref, wpu_ref, wau_ref, wo_ref, lg_ref, lb_ref,
                  wr_ref, br_ref, x1_ref, x1b_ref, comb_ref, sel_ref, cnt_ref,
                  wpub_ref, waub_ref, wob_ref, wrb_ref, *, alpha):
    tm, d = x_ref.shape
    _cast_once(wpu_ref, wpub_ref)
    _cast_once(wau_ref, waub_ref)
    _cast_once(wo_ref, wob_ref)

    @pl.when(pl.program_id(0) == 0)
    def _():
        w_r = wr_ref[...]
        hi = w_r.astype(BF16)
        wrb_ref[:, :LANES] = hi
        wrb_ref[:, LANES:] = (w_r - hi.astype(F32)).astype(BF16)

    def mix(r):
        a = jnp.dot(po_ref[r, :], wpub_ref[...], preferred_element_type=F32)
        b = jnp.dot(at_ref[r, :], waub_ref[...], preferred_element_type=F32)
        merged = (g_ref[r, :d].astype(F32) * a + g_ref[r, d:].astype(F32) * b).astype(BF16)
        return alpha * x_ref[r, :] + jnp.dot(merged, wob_ref[...], preferred_element_type=F32)

    def norm_and_route(r, h):
        x1 = _layer_norm(h, lg_ref[...], lb_ref[...])
        xh = x1.astype(BF16)
        x1_ref[r, :] = x1
        x1b_ref[r, :] = xh
        xl = (x1 - xh.astype(F32)).astype(BF16)
        by_hi = jnp.dot(xh, wrb_ref[...], preferred_element_type=F32)
        by_lo = jnp.dot(xl, wrb_ref[:, :LANES], preferred_element_type=F32)
        comb, sel = _route(by_hi[:, :LANES] + by_hi[:, LANES:] + by_lo + br_ref[...])
        comb_ref[r, :] = comb
        sel_ref[r, :] = sel.astype(BF16)
        return jnp.sum(sel, axis=0, keepdims=True)

    cnt_ref[0] = sum(_skewed(_row_subs(tm), mix, norm_and_route))


def _merge(x2, pool_out, attn_out, gates, w_pu, w_au, w_o, ln_g, ln_b, w_r, b_r, alpha, tm):
    t, d = x2.shape
    width = pool_out.shape[1]
    row = lambda i: (i, 0)
    blocks = (2 * (tm * d * 4 + 2 * tm * width * 2 + tm * 2 * d * 2
                   + tm * d * 4 + tm * d * 2 + tm * LANES * (4 + 2) + LANES * 4)
              + (2 * width * d + d * d + d * LANES) * (4 + 2) + 2 * d * 4 + LANES * 4)
    temps = 5 * tm * d * 4
    return pl.pallas_call(
        functools.partial(_merge_kernel, alpha=alpha),
        grid=(t // tm,),
        in_specs=[pl.BlockSpec((tm, d), row), pl.BlockSpec((tm, width), row),
                  pl.BlockSpec((tm, width), row), pl.BlockSpec((tm, 2 * d), row),
                  _resident((width, d)), _resident((width, d)), _resident((d, d)),
                  _resident((1, d)), _resident((1, d)), _resident((d, LANES)),
                  _resident((1, LANES))],
        out_specs=[pl.BlockSpec((tm, d), row), pl.BlockSpec((tm, d), row),
                   pl.BlockSpec((tm, LANES), row), pl.BlockSpec((tm, LANES), row),
                   pl.BlockSpec((1, 1, LANES), lambda i: (i, 0, 0))],
        out_shape=[jax.ShapeDtypeStruct((t, d), F32), jax.ShapeDtypeStruct((t, d), BF16),
                   jax.ShapeDtypeStruct((t, LANES), F32), jax.ShapeDtypeStruct((t, LANES), BF16),
                   jax.ShapeDtypeStruct((t // tm, 1, LANES), F32)],
        scratch_shapes=[pltpu.VMEM((width, d), BF16), pltpu.VMEM((width, d), BF16),
                        pltpu.VMEM((d, d), BF16), pltpu.VMEM((d, 2 * LANES), BF16)],
        compiler_params=_params(("arbitrary",), blocks + temps),
        name="merge",
    )(x2, pool_out, attn_out, gates, w_pu, w_au, w_o, ln_g, ln_b, w_r, b_r)


def _routing_plan(counts):
    cnt = counts[:, 0, :N_EXPERTS].astype(jnp.int32)
    nch = (cnt + (CHUNK - 1)) // CHUNK
    local_end = jnp.cumsum(nch, axis=1)
    local_start = local_end - nch
    block_chunks = local_end[:, -1]
    before_block = jnp.cumsum(nch, axis=0) - nch
    expert_chunks = jnp.sum(nch, axis=0)
    chunks_per_tile = EXPERT_TILE // CHUNK
    expert_tiles = (expert_chunks + (chunks_per_tile - 1)) // chunks_per_tile
    tiles_end = jnp.cumsum(expert_tiles)
    region_start = (tiles_end - expert_tiles) * chunks_per_tile
    segment_dst = region_start[None, :] + before_block
    i32 = lambda a: a.astype(jnp.int32)
    return dict(seg_len=i32(nch), seg_src=i32(local_start), seg_dst=i32(segment_dst),
                block_chunks=i32(block_chunks),
                expert_tiles=i32(expert_tiles), tiles_used=i32(tiles_end[-1:]),
                pad_start=i32(region_start + expert_chunks),
                pad_count=i32(expert_tiles * chunks_per_tile - expert_chunks))


def _local_positions(sel):
    tb = sel.shape[0]
    earlier = (lax.broadcasted_iota(jnp.int32, (tb, tb), 1)
               < lax.broadcasted_iota(jnp.int32, (tb, tb), 0)).astype(BF16)
    rank = jnp.dot(earlier, sel, preferred_element_type=F32)
    cnt = jnp.sum(sel.astype(F32), axis=0, keepdims=True)
    nch = jnp.floor((cnt + (CHUNK - 1)) * (1.0 / CHUNK))
    lower = (lax.broadcasted_iota(jnp.int32, (LANES, LANES), 0)
             < lax.broadcasted_iota(jnp.int32, (LANES, LANES), 1)).astype(BF16)
    start = CHUNK * jnp.dot(jnp.broadcast_to(nch, (8, LANES)).astype(BF16), lower,
                            preferred_element_type=F32)[0:1]
    pos = rank + start
    chosen = sel > 0
    pos_lo = jnp.min(jnp.where(chosen, pos, float(LOCAL_ROWS)), axis=1, keepdims=True)
    pos_hi = jnp.max(jnp.where(chosen, pos, -1.0), axis=1, keepdims=True)
    return pos, pos_lo, pos_hi


def _for_each(n, fn):
    lax.fori_loop(0, n, lambda c, carry: (fn(c), carry)[1], 0)


def _segment_rows(len_ref, src_ref, dst_ref, blk, e):
    n = len_ref[blk, e] * CHUNK
    src = pl.multiple_of(src_ref[blk, e] * CHUNK, CHUNK)
    dst = pl.multiple_of(dst_ref[blk, e] * CHUNK, CHUNK)
    return n, src, dst


def _scatter_kernel(len_ref, src_ref, dst_ref, nchunk_ref, pad_start_ref, pad_count_ref, used_ref,
                    x_ref, sel_ref, g_ref, loc_ref, zero_ref, sems):
    b = pl.program_id(0)
    last = pl.num_programs(0) - 1
    slot = b % 2
    tb = x_ref.shape[0]

    def start_block(blk, slot):
        def start_segment(e):
            n, src, dst = _segment_rows(len_ref, src_ref, dst_ref, blk, e)

            @pl.when(n > 0)
            def _():
                pltpu.make_async_copy(loc_ref.at[slot, pl.ds(src, n)], g_ref.at[pl.ds(dst, n)],
                                      sems.at[slot]).start()

        _for_each(N_EXPERTS, start_segment)

    def wait_block(blk, slot):
        n = nchunk_ref[blk] * CHUNK
        pltpu.make_async_copy(loc_ref.at[slot, pl.ds(0, n)], g_ref.at[pl.ds(0, n)],
                              sems.at[slot]).wait()

    def pad_copy(e):
        n = pad_count_ref[e] * CHUNK
        dst = pl.multiple_of(pad_start_ref[e] * CHUNK, CHUNK)
        return n, pltpu.make_async_copy(zero_ref.at[pl.ds(0, n)], g_ref.at[pl.ds(dst, n)],
                                        sems.at[2])

    def unused_tile_copy(i):
        dst = pl.multiple_of(i * EXPERT_TILE, EXPERT_TILE)
        return pltpu.make_async_copy(zero_ref, g_ref.at[pl.ds(dst, EXPERT_TILE)], sems.at[2])

    @pl.when(b >= 2)
    def _():
        wait_block(b - 2, slot)

    _, pos_lo, pos_hi = _local_positions(sel_ref[...])
    lo = pos_lo.astype(jnp.int32)
    hi = pos_hi.astype(jnp.int32)
    x = x_ref[...]

    def sort_rows(height):
        r = lax.broadcasted_iota(jnp.int32, (tb, height), 1)
        perm = ((r == lo) | (r == hi)).astype(BF16)
        rows = lax.dot_general(perm, x, (((0,), (0,)), ((), ())), preferred_element_type=F32)
        loc_ref[slot, :height, :] = rows.astype(BF16)

    n_rows = nchunk_ref[b] * CHUNK
    for below, height in zip((0,) + SORT_HEIGHTS[:-1], SORT_HEIGHTS):
        @pl.when((n_rows > below) & (n_rows <= height))
        def _():
            sort_rows(height)

    start_block(b, slot)

    @pl.when(b == last)
    def _():
        zero_ref[...] = jnp.zeros_like(zero_ref)
        n_unused = g_ref.shape[0] // EXPERT_TILE - used_ref[0]

        def each_pad(act):
            def one(e):
                n, copy = pad_copy(e)

                @pl.when(n > 0)
                def _():
                    act(copy)

            _for_each(N_EXPERTS, one)

        each_pad(lambda copy: copy.start())
        _for_each(n_unused, lambda i: unused_tile_copy(used_ref[0] + i).start())
        each_pad(lambda copy: copy.wait())
        _for_each(n_unused, lambda i: unused_tile_copy(used_ref[0] + i).wait())

        @pl.when(b >= 1)
        def _():
            wait_block(b - 1, 1 - slot)

        wait_block(b, slot)


def _scatter(x1b, sel, plan, n_tiles):
    t, d = x1b.shape
    tb = ROUTE_BLOCK
    row = lambda b, *_: (b, 0)
    blocks = 2 * (tb * d * 2 + tb * LANES * 2) + 2 * LOCAL_ROWS * d * 2 + EXPERT_TILE * d * 2
    temps = tb * tb * 2 + 6 * tb * LANES * 4 + tb * LOCAL_ROWS * 6 + LOCAL_ROWS * d * 6
    return pl.pallas_call(
        _scatter_kernel,
        grid_spec=pltpu.PrefetchScalarGridSpec(
            num_scalar_prefetch=7,
            grid=(t // tb,),
            in_specs=[pl.BlockSpec((tb, d), row), pl.BlockSpec((tb, LANES), row)],
            out_specs=pl.BlockSpec(memory_space=pl.ANY),
            scratch_shapes=[pltpu.VMEM((2, LOCAL_ROWS, d), BF16),
                            pltpu.VMEM((EXPERT_TILE, d), BF16), pltpu.SemaphoreType.DMA((3,))]),
        out_shape=jax.ShapeDtypeStruct((n_tiles * EXPERT_TILE, d), BF16),
        compiler_params=_params(("arbitrary",), blocks + temps),
        name="scatter",
    )(plan["seg_len"], plan["seg_src"], plan["seg_dst"], plan["block_chunks"], plan["pad_start"],
      plan["pad_count"], plan["tiles_used"], x1b, sel)


def _experts_kernel(ntile_ref, used_ref, g_ref, wg_ref, wu_ref, wd_ref, y_ref,
                    x_buf, y_buf, wg_buf, wu_buf, wd_buf, wgb_ref, wub_ref, wdb_ref,
                    x_sems, y_sems, w_sems):
    n_experts = wg_ref.shape[0]
    tm = EXPERT_TILE
    used = used_ref[0]

    def weight_copies(e, slot):
        return [pltpu.make_async_copy(src.at[e], dst.at[slot], w_sems.at[slot])
                for src, dst in ((wg_ref, wg_buf), (wu_ref, wu_buf), (wd_ref, wd_buf))]

    def x_copy(t, slot):
        rows = pl.ds(pl.multiple_of(t * tm, tm), MACRO_TILES * tm)
        return pltpu.make_async_copy(g_ref.at[rows], x_buf.at[slot], x_sems.at[slot])

    def y_copy(t, k, slot):
        n = k * tm
        dst = pl.ds(pl.multiple_of(t * tm, tm), n)
        return pltpu.make_async_copy(y_buf.at[slot, pl.ds(0, n)], y_ref.at[dst], y_sems.at[slot])

    def wait_y(t, k, slot):
        @pl.when(k > 0)
        def _():
            y_copy(t, k, slot).wait()

    def mlp(slot, rows):
        x = x_buf[slot, :rows, :]
        hg = jnp.dot(x, wgb_ref[...], preferred_element_type=F32)
        hu = jnp.dot(x, wub_ref[...], preferred_element_type=F32)
        h = hg * _sigmoid(hg) * hu
        y_buf[slot, :rows, :] = jnp.dot(h.astype(BF16), wdb_ref[...],
                                        preferred_element_type=F32).astype(BF16)

    for c in weight_copies(0, 0):
        c.start()
    x_copy(0, 0).start()

    def run_expert(e, carry):
        wslot = e % 2
        for c in weight_copies(e, wslot):
            c.wait()

        @pl.when(e + 1 < n_experts)
        def _():
            for c in weight_copies(e + 1, 1 - wslot):
                c.start()

        @pl.when(ntile_ref[e] > 0)
        def _():
            wgb_ref[...] = wg_buf[wslot].astype(BF16)
            wub_ref[...] = wu_buf[wslot].astype(BF16)
            wdb_ref[...] = wd_buf[wslot].astype(BF16)

        def run_macro(m, carry):
            t, step, k1, t1, k2, t2 = carry
            k = jnp.minimum(MACRO_TILES, ntile_ref[e] - m * MACRO_TILES)
            slot = step % 2
            x_copy(t, slot).wait()

            @pl.when(t + k < used)
            def _():
                x_copy(t + k, 1 - slot).start()

            wait_y(t2, k2, slot)
            for tiles in range(1, MACRO_TILES + 1):
                @pl.when(k == tiles)
                def _():
                    mlp(slot, tiles * tm)

            y_copy(t, k, slot).start()
            return t + k, step + 1, k, t, k1, t1

        n_macro = (ntile_ref[e] + (MACRO_TILES - 1)) // MACRO_TILES
        return lax.fori_loop(0, n_macro, run_macro, carry)

    zero = jnp.int32(0)
    _, step, k1, t1, k2, t2 = lax.fori_loop(0, n_experts, run_expert, (zero,) * 6)
    wait_y(t2, k2, step % 2)
    wait_y(t1, k1, (step + 1) % 2)
    y_buf[0, :tm, :] = jnp.zeros((tm, y_buf.shape[2]), BF16)
    n_unused = y_ref.shape[0] // tm - used
    _for_each(n_unused, lambda i: y_copy(used + i, 1, 0).start())
    _for_each(n_unused, lambda i: y_copy(used + i, 1, 0).wait())


def _experts(sorted_x, plan, w_eg, w_eu, w_ed):
    rows, d = sorted_x.shape
    _, _, de = w_eg.shape
    tm = EXPERT_TILE
    any_space = pl.BlockSpec(memory_space=pl.ANY)
    big = MACRO_TILES * tm
    scratch = 2 * 2 * big * d * 2 + 2 * 3 * d * de * 4 + 3 * d * de * 2
    temps = 3 * big * de * 4 + big * d * 4 + d * de * 4
    return pl.pallas_call(
        _experts_kernel,
        grid_spec=pltpu.PrefetchScalarGridSpec(
            num_scalar_prefetch=2,
            grid=(1,),
            in_specs=[any_space] * 4,
            out_specs=any_space,
            scratch_shapes=[pltpu.VMEM((2, big, d), BF16), pltpu.VMEM((2, big, d), BF16),
                            pltpu.VMEM((2, d, de), F32), pltpu.VMEM((2, d, de), F32),
                            pltpu.VMEM((2, de, d), F32),
                            pltpu.VMEM((d, de), BF16), pltpu.VMEM((d, de), BF16),
                            pltpu.VMEM((de, d), BF16),
                            pltpu.SemaphoreType.DMA((2,)), pltpu.SemaphoreType.DMA((2,)),
                            pltpu.SemaphoreType.DMA((2,))]),
        out_shape=jax.ShapeDtypeStruct((rows, d), BF16),
        compiler_params=_params(("arbitrary",), scratch + temps),
        name="experts",
    )(plan["expert_tiles"], plan["tiles_used"], sorted_x, w_eg, w_eu, w_ed)


def _combine_kernel(len_ref, src_ref, dst_ref, nchunk_ref, x1_ref, comb_ref, sel_ref, p_ref,
                    wpg_ref, wpp_ref,
                    lg_ref, lb_ref, y_ref, o_ref, loc_ref, wpgb_ref, wppb_ref, sems, *, alpha):
    b = pl.program_id(0)
    nb = pl.num_programs(0)
    slot = b % 2
    tb, d = x1_ref.shape
    _cast_once(wpg_ref, wpgb_ref)
    _cast_once(wpp_ref, wppb_ref)

    def start_block(blk, slot):
        def start_segment(e):
            n, local, sorted_at = _segment_rows(len_ref, src_ref, dst_ref, blk, e)

            @pl.when(n > 0)
            def _():
                pltpu.make_async_copy(y_ref.at[pl.ds(sorted_at, n)],
                                      loc_ref.at[slot, pl.ds(local, n)], sems.at[slot]).start()

        _for_each(N_EXPERTS, start_segment)

    @pl.when(b == 0)
    def _():
        loc_ref[...] = jnp.zeros_like(loc_ref)
        start_block(0, 0)

    @pl.when(b + 1 < nb)
    def _():
        start_block(b + 1, 1 - slot)

    n_rows = nchunk_ref[b] * CHUNK
    pltpu.make_async_copy(y_ref.at[pl.ds(0, n_rows)], loc_ref.at[slot, pl.ds(0, n_rows)],
                          sems.at[slot]).wait()

    sel = sel_ref[...]
    pos, pos_lo, pos_hi = _local_positions(sel)
    chosen = sel > 0
    comb = comb_ref[...]
    w_lo = jnp.sum(jnp.where(chosen & (pos == pos_lo), comb, 0.0), axis=1, keepdims=True)
    w_hi = jnp.sum(jnp.where(chosen & (pos == pos_hi), comb, 0.0), axis=1, keepdims=True)
    lo = pos_lo.astype(jnp.int32)
    hi = pos_hi.astype(jnp.int32)

    sorted_row = lax.broadcasted_iota(jnp.int32, (tb // ROW_SUBS, LOCAL_ROWS), 1)

    def branches(r):
        weights = (jnp.where(sorted_row == lo[r], w_lo[r], 0.0)
                   + jnp.where(sorted_row == hi[r], w_hi[r], 0.0)).astype(BF16)
        moe = jnp.dot(weights, loc_ref[slot], preferred_element_type=F32)
        gate = jnp.dot(x1_ref[r, :].astype(BF16), wpgb_ref[...], preferred_element_type=F32)
        emb = jnp.dot(p_ref[r, :].astype(BF16), wppb_ref[...], preferred_element_type=F32)
        return moe, gate, emb

    def finish(r, parts):
        moe, gate, emb = parts
        h = alpha * x1_ref[r, :] + moe + _sigmoid(gate) * emb
        o_ref[r, :] = _layer_norm(h, lg_ref[...], lb_ref[...])

    _skewed(_row_subs(tb), branches, finish)


def _combine(y, plan, x1, comb, sel, p2, w_pg, w_pp, ln_g, ln_b, alpha):
    t, d = x1.shape
    pd = p2.shape[1]
    tb = ROUTE_BLOCK
    row = lambda b, *_: (b, 0)
    blocks = (2 * (2 * tb * d * 4 + tb * LANES * (4 + 2) + tb * pd * 4)
              + (d * d + pd * d) * (4 + 2) + 2 * d * 4 + 2 * LOCAL_ROWS * d * 2)
    temps = tb * tb * 2 + 8 * tb * LANES * 4 + tb * LOCAL_ROWS * 10 + 5 * tb * d * 4
    return pl.pallas_call(
        functools.partial(_combine_kernel, alpha=alpha),
        grid_spec=pltpu.PrefetchScalarGridSpec(
            num_scalar_prefetch=4,
            grid=(t // tb,),
            in_specs=[pl.BlockSpec((tb, d), row), pl.BlockSpec((tb, LANES), row),
                      pl.BlockSpec((tb, LANES), row), pl.BlockSpec((tb, pd), row),
                      _resident((d, d)), _resident((pd, d)), _resident((1, d)), _resident((1, d)),
                      pl.BlockSpec(memory_space=pl.ANY)],
            out_specs=pl.BlockSpec((tb, d), row),
            scratch_shapes=[pltpu.VMEM((2, LOCAL_ROWS, d), BF16), pltpu.VMEM((d, d), BF16),
                            pltpu.VMEM((pd, d), BF16), pltpu.SemaphoreType.DMA((2,))]),
        out_shape=jax.ShapeDtypeStruct((t, d), F32),
        compiler_params=_params(("arbitrary",), blocks + temps),
        name="combine",
    )(plan["seg_len"], plan["seg_src"], plan["seg_dst"], plan["block_chunks"], x1, comb, sel, p2,
      w_pg, w_pp, ln_g, ln_b, y)


def kernel(x, p, w_in, w_pool, pool_scale, w_pu, w_au, w_o, ln1_g, ln1_b, w_rg, b_rg, w_re, b_re,
           w_eg, w_eu, w_ed, w_pg, w_pp, ln2_g, ln2_b):
    bsz, seq, d = x.shape
    depth = w_in.shape[0]
    t = bsz * seq
    de = w_eg.shape[-1]
    alpha = (2.0 * depth) ** 0.25
    assert w_rg.shape[2] == N_GROUPS and w_re.shape[1:] == (N_GROUPS, d, EXPERTS_PER_GROUP)
    assert w_in.shape[2] == 4 * d and w_pool.shape[1] == len(POOL_WINDOWS)
    assert t % ROUTE_BLOCK == 0 and N_EXPERTS + N_GROUPS <= LANES
    n_blocks = t // ROUTE_BLOCK
    n_tiles = -(-(2 * t + n_blocks * N_EXPERTS * (CHUNK - 1) + N_EXPERTS * (EXPERT_TILE - CHUNK))
                // EXPERT_TILE) + MACRO_TILES - 1

    x2 = x.reshape(t, d)
    for i in range(depth):
        pool_out, q, k, v, gates = _in_hbm(
            *_proj(x2, w_in[i], w_pool[i], pool_scale[i][None, :], seq, tm=512))
        attn_out, = _in_hbm(_attention(q, k, v, seq, tb=128))

        w_r = jnp.concatenate(
            [w_re[i].transpose(1, 0, 2).reshape(d, N_EXPERTS), w_rg[i]], axis=1)
        w_r = jnp.pad(w_r, ((0, 0), (0, LANES - w_r.shape[1])))
        b_r = jnp.pad(jnp.concatenate([b_re[i].reshape(-1), b_rg[i]]),
                      (0, LANES - N_GROUPS - N_EXPERTS))[None, :]

        x1, x1b, comb, sel, counts = _merge(
            x2, pool_out, attn_out, gates, *_in_hbm(w_pu[i], w_au[i], w_o[i]), ln1_g[i][None, :],
            ln1_b[i][None, :], w_r, b_r, alpha, tm=ROUTE_BLOCK)
        x1, x1b, comb, sel = _in_hbm(x1, x1b, comb, sel)

        plan = _routing_plan(counts)
        sorted_x, = _in_hbm(_scatter(x1b, sel, plan, n_tiles))
        y, = _in_hbm(_experts(sorted_x, plan, w_eg[i].reshape(N_EXPERTS, d, de),
                              w_eu[i].reshape(N_EXPERTS, d, de), w_ed[i].reshape(N_EXPERTS, de, d)))
        x2 = _combine(y, plan, x1, comb, sel, p[i].reshape(t, -1), *_in_hbm(w_pg[i], w_pp[i]),
                      ln2_g[i][None, :], ln2_b[i][None, :], alpha)
    return x2.reshape(bsz, seq, d)
```

```python
import functools
import math

import jax
import jax.numpy as jnp
from jax import lax
from jax.experimental import pallas as pl
from jax.experimental.pallas import tpu as pltpu

F32 = jnp.float32
BF16 = jnp.bfloat16

LANES = 128
POOL_WINDOWS = (2, 4, 8, 16)
POOL_HALO = 16
HEAD_DIM = 64
N_GROUPS = 4
EXPERTS_PER_GROUP = 8
N_EXPERTS = N_GROUPS * EXPERTS_PER_GROUP
LN_EPS = 1e-5
GROUP_LANE0 = N_EXPERTS
NEG_BIG = -1e30
ROUTE_BLOCK = 512
CHUNK = 16
EXPERT_TILE = 256
MACRO_TILES = 6
SORT_ROWS = 256
ROW_SUBS = 2
LOCAL_ROWS = -(-(2 * ROUTE_BLOCK + N_EXPERTS * (CHUNK - 1)) // SORT_ROWS) * SORT_ROWS
SORT_HEIGHTS = (LOCAL_ROWS - SORT_ROWS, LOCAL_ROWS - SORT_ROWS // 2, LOCAL_ROWS)
ATTN_DEAD_LOG2 = -160.0
VMEM_CAP_BYTES = 56 * 1024 * 1024


def _params(sem, vmem_bytes):
    return pltpu.CompilerParams(
        dimension_semantics=sem, vmem_limit_bytes=min(int(vmem_bytes), VMEM_CAP_BYTES))


def _layer_norm(h, g, b):
    mu = jnp.mean(h, axis=-1, keepdims=True)
    c = h - mu
    var = jnp.mean(c * c, axis=-1, keepdims=True)
    return c * lax.rsqrt(var + LN_EPS) * g + b


def _sigmoid(z):
    return 1.0 / (1.0 + jnp.exp(-z))


def _row_subs(rows):
    return [slice(k * (rows // ROW_SUBS), (k + 1) * (rows // ROW_SUBS)) for k in range(ROW_SUBS)]


def _skewed(subs, first, second):
    out, pending = [], None
    for r in subs:
        mid = first(r)
        if pending is not None:
            out.append(second(*pending))
        pending = (r, mid)
    out.append(second(*pending))
    return out


def _in_hbm(*arrays):
    if not all(isinstance(a, jax.core.Tracer) for a in arrays):
        return list(arrays)
    return [pltpu.with_memory_space_constraint(a, pltpu.HBM) for a in arrays]


def _resident(shape):
    return pl.BlockSpec(shape, lambda *_: (0,) * len(shape), pipeline_mode=pl.Buffered(1))


def _cast_once(w_ref, wb_ref):
    @pl.when(pl.program_id(0) == 0)
    def _():
        wb_ref[...] = w_ref[...].astype(BF16)


def _proj_kernel(x_ref, w_ref, wp_ref, sc_ref, po_ref, q_ref, k_ref, v_ref, g_ref, wb_ref, halo_ref,
                 *, width, tiles_per_seq):
    q_scale = math.log2(math.e) / math.sqrt(HEAD_DIM)
    tile_in_seq = pl.program_id(0) % tiles_per_seq
    _cast_once(w_ref, wb_ref)
    xb = x_ref[...].astype(BF16)

    def mm(lo):
        return jnp.dot(xb, wb_ref[:, lo:lo + width], preferred_element_type=F32)

    @pl.when(tile_in_seq == 0)
    def _():
        halo_ref[...] = jnp.zeros_like(halo_ref)

    u = mm(0)
    tm = u.shape[0]
    gd = wp_ref.shape[1]
    pos = tile_in_seq * tm + lax.broadcasted_iota(jnp.int32, (tm, gd), 0)

    def pool_group(g, w):
        cols = slice(g * gd, (g + 1) * gd)
        ug = u[:, cols]
        s = jnp.concatenate([halo_ref[:, cols], ug], axis=0)
        sh = 1
        while sh < w:
            s = s + pltpu.roll(s, sh, axis=0)
            sh *= 2
        cnt = jnp.minimum(pos + 1, w).astype(F32)
        pooled = s[POOL_HALO:, :] / cnt - ug
        mixed = jnp.dot(pooled.astype(BF16), wp_ref[g].astype(BF16), preferred_element_type=F32)
        po_ref[:, cols] = (mixed * sc_ref[:, cols]).astype(BF16)

    def write_q():
        q_ref[...] = (mm(width) * q_scale).astype(BF16)

    def write_k():
        k_ref[...] = mm(2 * width).astype(BF16)

    def write_v():
        v_ref[...] = mm(3 * width).astype(BF16)

    def gate_chunk(c):
        g_ref[:, c * width:(c + 1) * width] = _sigmoid(mm((4 + c) * width)).astype(BF16)

    matmuls = [write_q, write_k, write_v] + [
        functools.partial(gate_chunk, c) for c in range(g_ref.shape[1] // width)]
    for n, matmul in enumerate(matmuls):
        matmul()
        if n < len(POOL_WINDOWS):
            pool_group(n, POOL_WINDOWS[n])
    halo_ref[...] = u[tm - POOL_HALO:, :]


def _proj(x2, w_in, w_pool, pool_scale, seq, tm):
    t, d = x2.shape
    n = w_in.shape[1]
    width = d // 2
    gate_w = n - 4 * width
    groups, gd, _ = w_pool.shape
    assert seq % tm == 0 and groups * gd == width and max(POOL_WINDOWS) <= POOL_HALO + 1
    row = lambda i: (i, 0)
    blocks = (2 * (tm * d * 4 + tm * width * 4 * 2 + tm * gate_w * 2)
              + d * n * (4 + 2) + groups * gd * gd * 4 + width * 4 + POOL_HALO * width * 4)
    temps = tm * d * 2 + 3 * tm * width * 4 + 6 * (tm + POOL_HALO) * gd * 4
    return pl.pallas_call(
        functools.partial(_proj_kernel, width=width, tiles_per_seq=seq // tm),
        grid=(t // tm,),
        in_specs=[pl.BlockSpec((tm, d), row), _resident((d, n)), _resident((groups, gd, gd)),
                  _resident((1, width))],
        out_specs=[pl.BlockSpec((tm, width), row)] * 4 + [pl.BlockSpec((tm, gate_w), row)],
        out_shape=[jax.ShapeDtypeStruct((t, width), BF16)] * 4
        + [jax.ShapeDtypeStruct((t, gate_w), BF16)],
        scratch_shapes=[pltpu.VMEM((d, n), BF16), pltpu.VMEM((POOL_HALO, width), F32)],
        compiler_params=_params(("arbitrary",), blocks + temps),
        name="proj",
    )(x2, w_in, w_pool, pool_scale)


def _attn_kernel(q_ref, k_ref, v_ref, o_ref, rem_ref, acc_ref, *, tb):
    qi = pl.program_id(1)
    pairs = q_ref.shape[1] // LANES
    col_blocks = [slice(p * LANES, (p + 1) * LANES) for p in range(pairs)]
    first_head = lax.broadcasted_iota(jnp.int32, (tb, LANES), 1) < HEAD_DIM
    r = lax.broadcasted_iota(jnp.int32, (2 * tb, 2 * tb), 0)
    c = lax.broadcasted_iota(jnp.int32, (2 * tb, 2 * tb), 1)
    same_head = (r >= tb) == (c >= tb)
    cum = -jnp.concatenate([(same_head & (r >= c)).astype(BF16), same_head.astype(BF16)], axis=1)
    kcol = lax.broadcasted_iota(jnp.int32, (tb, 2 * tb), 1)
    kcol = jnp.where(kcol >= tb, kcol - tb, kcol)
    causal = kcol < lax.broadcasted_iota(jnp.int32, (tb, 2 * tb), 0)

    def stack_heads(blk):
        zero = jnp.zeros_like(blk)
        return jnp.concatenate(
            [jnp.where(first_head, blk, zero), jnp.where(first_head, zero, blk)], axis=0)

    def scores(j, rows, diagonal):
        ks = pl.multiple_of(j * tb, tb)
        zs = [lax.dot_general(q_ref[rows, cols], stack_heads(k_ref[pl.ds(ks, tb), cols]),
                              (((1,), (1,)), ((), ())), preferred_element_type=F32)
              for cols in col_blocks]
        sums = []
        for z in zs:
            softplus = jnp.maximum(z, 0.0) + jnp.log2(1.0 + jnp.exp2(-jnp.abs(z)))
            if diagonal:
                softplus = jnp.where(causal, softplus, 0.0)
            sums.append(jnp.dot(softplus.astype(BF16), cum, preferred_element_type=F32))
        return zs, sums

    def weighted_values(j, cols, z, later):
        ks = pl.multiple_of(j * tb, tb)
        a = jnp.exp2(z + later)
        return jnp.dot(a.astype(BF16), stack_heads(v_ref[pl.ds(ks, tb), cols]),
                       preferred_element_type=F32)

    def first_step(n_before):
        full, half = slice(0, tb), slice(0, tb // 2)
        z_d, sums_d = scores(qi, full, diagonal=True)
        if n_before >= 1:
            z_1, sums_1 = scores(qi - 1, full, diagonal=False)
        if n_before >= 2:
            z_2, sums_2 = scores(qi - 2, half, diagonal=False)
        owed = []
        for p, cols in enumerate(col_blocks):
            ks = pl.multiple_of(qi * tb, tb)
            a = jnp.where(causal, jnp.exp2(z_d[p] + sums_d[p][:, :2 * tb]), 0.0)
            out = jnp.dot(a.astype(BF16), stack_heads(v_ref[pl.ds(ks, tb), cols]),
                          preferred_element_type=F32)
            total = sums_d[p][:, 2 * tb:]
            if n_before >= 1:
                out = out + weighted_values(qi - 1, cols, z_1[p], sums_1[p][:, :2 * tb] + total)
                total = total + sums_1[p][:, 2 * tb:]
            acc_ref[p] = out
            rem_ref[p] = total
            if n_before >= 2:
                owed.append(total[tb // 2:, :])
                later = sums_2[p][:, :2 * tb] + total[half, :]
                acc_ref[p, half, :] += weighted_values(qi - 2, cols, z_2[p], later)
                rem_ref[p, half, :] += sums_2[p][:, 2 * tb:]
        if n_before < 2:
            return None
        return jnp.max(functools.reduce(jnp.maximum, owed)) > ATTN_DEAD_LOG2

    def step(j, rows):
        zs, sums = scores(j, rows, diagonal=False)
        for p, cols in enumerate(col_blocks):
            later = sums[p][:, :2 * tb] + rem_ref[p, rows, :]
            acc_ref[p, rows, :] += weighted_values(j, cols, zs[p], later)
            rem_ref[p, rows, :] += sums[p][:, 2 * tb:]

    def live_rows():
        worst = functools.reduce(jnp.maximum, [rem_ref[p] for p in range(pairs)])
        live = jnp.max(worst, axis=1, keepdims=True) > ATTN_DEAD_LOG2
        row = lax.broadcasted_iota(jnp.int32, (tb, 1), 0)
        return jnp.max(jnp.where(live, row + 1, 0))

    for n_before in (0, 1):
        @pl.when(qi == n_before)
        def _():
            first_step(n_before)

    @pl.when(qi >= 2)
    def _():
        second_half_owed = first_step(2)

        @pl.when(second_half_owed)
        def _():
            step(qi - 2, slice(tb // 2, tb))

    row_counts = (tb, tb // 2, tb // 4)

    def cond(carry):
        j, n_live = carry
        return (j >= 0) & (n_live > 0)

    def body(carry):
        j, n_live = carry
        for rows, fewer in zip(row_counts, row_counts[1:] + (0,)):
            @pl.when((n_live > fewer) & (n_live <= rows))
            def _():
                step(j, slice(0, rows))

        return j - 1, live_rows()

    lax.while_loop(cond, body, (qi - 3, live_rows()))
    for p, cols in enumerate(col_blocks):
        o_ref[:, cols] = acc_ref[p].astype(BF16)


def _attention(q, k, v, seq, tb):
    t, width = q.shape
    pairs = width // LANES
    blocks = 2 * (2 * tb * width * 2 + 2 * seq * width * 2) + pairs * tb * (2 * tb + LANES) * 4
    temps = pairs * 10 * tb * 2 * tb * 4 + 2 * tb * 4 * tb * 2
    qspec = pl.BlockSpec((tb, width), lambda b, i: (b * (seq // tb) + i, 0))
    kvspec = pl.BlockSpec((seq, width), lambda b, i: (b, 0))
    return pl.pallas_call(
        functools.partial(_attn_kernel, tb=tb),
        grid=(t // seq, seq // tb),
        in_specs=[qspec, kvspec, kvspec],
        out_specs=qspec,
        out_shape=jax.ShapeDtypeStruct((t, width), BF16),
        scratch_shapes=[pltpu.VMEM((pairs, tb, 2 * tb), F32), pltpu.VMEM((pairs, tb, LANES), F32)],
        compiler_params=_params(("parallel", "parallel"), blocks + temps),
        name="attn",
    )(q, k, v)


def _route(logits):
    lane = lax.broadcasted_iota(jnp.int32, logits.shape, 1)

    def first_max(vals):
        m = jnp.max(vals, axis=1, keepdims=True)
        idx = jnp.min(jnp.where(vals == m, lane, LANES), axis=1, keepdims=True)
        return m, idx

    is_group = (lane >= GROUP_LANE0) & (lane < GROUP_LANE0 + N_GROUPS)
    gm, g_lane = first_max(jnp.where(is_group, logits, NEG_BIG))
    g_prob = 1.0 / jnp.sum(jnp.where(is_group, jnp.exp(logits - gm), 0.0), axis=1, keepdims=True)
    lo = EXPERTS_PER_GROUP * (g_lane - GROUP_LANE0)
    in_group = jnp.where((lane >= lo) & (lane < lo + EXPERTS_PER_GROUP), logits, NEG_BIG)
    m1, i1 = first_max(in_group)
    m2, i2 = first_max(jnp.where(lane == i1, NEG_BIG, in_group))
    e21 = jnp.exp(m2 - m1)
    w1 = g_prob / (1.0 + e21)
    w2 = w1 * e21
    comb = jnp.where(lane == i1, w1, 0.0) + jnp.where(lane == i2, w2, 0.0)
    sel = ((lane == i1) | (lane == i2)).astype(F32)
    return comb, sel


def _merge_kernel(x_ref, po_ref, at_ref, g_ref, wpu_ref, wau_ref, wo_ref, lg_ref, lb_ref,
                  wr_ref, br_ref, x1_ref, x1b_ref, comb_ref, sel_ref, cnt_ref,
                  wpub_ref, waub_ref, wob_ref, wrb_ref, *, alpha):
    tm, d = x_ref.shape
    _cast_once(wpu_ref, wpub_ref)
    _cast_once(wau_ref, waub_ref)
    _cast_once(wo_ref, wob_ref)

    @pl.when(pl.program_id(0) == 0)
    def _():
        w_r = wr_ref[...]
        hi = w_r.astype(BF16)
        wrb_ref[:, :LANES] = hi
        wrb_ref[:, LANES:] = (w_r - hi.astype(F32)).astype(BF16)

    def mix(r):
        a = jnp.dot(po_ref[r, :], wpub_ref[...], preferred_element_type=F32)
        b = jnp.dot(at_ref[r, :], waub_ref[...], preferred_element_type=F32)
        merged = (g_ref[r, :d].astype(F32) * a + g_ref[r, d:].astype(F32) * b).astype(BF16)
        return alpha * x_ref[r, :] + jnp.dot(merged, wob_ref[...], preferred_element_type=F32)

    def norm_and_route(r, h):
        x1 = _layer_norm(h, lg_ref[...], lb_ref[...])
        xh = x1.astype(BF16)
        x1_ref[r, :] = x1
        x1b_ref[r, :] = xh
        xl = (x1 - xh.astype(F32)).astype(BF16)
        by_hi = jnp.dot(xh, wrb_ref[...], preferred_element_type=F32)
        by_lo = jnp.dot(xl, wrb_ref[:, :LANES], preferred_element_type=F32)
        comb, sel = _route(by_hi[:, :LANES] + by_hi[:, LANES:] + by_lo + br_ref[...])
        comb_ref[r, :] = comb
        sel_ref[r, :] = sel.astype(BF16)
        return jnp.sum(sel, axis=0, keepdims=True)

    cnt_ref[0] = sum(_skewed(_row_subs(tm), mix, norm_and_route))


def _merge(x2, pool_out, attn_out, gates, w_pu, w_au, w_o, ln_g, ln_b, w_r, b_r, alpha, tm):
    t, d = x2.shape
    width = pool_out.shape[1]
    row = lambda i: (i, 0)
    blocks = (2 * (tm * d * 4 + 2 * tm * width * 2 + tm * 2 * d * 2
                   + tm * d * 4 + tm * d * 2 + tm * LANES * (4 + 2) + LANES * 4)
              + (2 * width * d + d * d + d * LANES) * (4 + 2) + 2 * d * 4 + LANES * 4)
    temps = 5 * tm * d * 4
    return pl.pallas_call(
        functools.partial(_merge_kernel, alpha=alpha),
        grid=(t // tm,),
        in_specs=[pl.BlockSpec((tm, d), row), pl.BlockSpec((tm, width), row),
                  pl.BlockSpec((tm, width), row), pl.BlockSpec((tm, 2 * d), row),
                  _resident((width, d)), _resident((width, d)), _resident((d, d)),
                  _resident((1, d)), _resident((1, d)), _resident((d, LANES)),
                  _resident((1, LANES))],
        out_specs=[pl.BlockSpec((tm, d), row), pl.BlockSpec((tm, d), row),
                   pl.BlockSpec((tm, LANES), row), pl.BlockSpec((tm, LANES), row),
                   pl.BlockSpec((1, 1, LANES), lambda i: (i, 0, 0))],
        out_shape=[jax.ShapeDtypeStruct((t, d), F32), jax.ShapeDtypeStruct((t, d), BF16),
                   jax.ShapeDtypeStruct((t, LANES), F32), jax.ShapeDtypeStruct((t, LANES), BF16),
                   jax.ShapeDtypeStruct((t // tm, 1, LANES), F32)],
        scratch_shapes=[pltpu.VMEM((width, d), BF16), pltpu.VMEM((width, d), BF16),
                        pltpu.VMEM((d, d), BF16), pltpu.VMEM((d, 2 * LANES), BF16)],
        compiler_params=_params(("arbitrary",), blocks + temps),
        name="merge",
    )(x2, pool_out, attn_out, gates, w_pu, w_au, w_o, ln_g, ln_b, w_r, b_r)


def _routing_plan(counts):
    cnt = counts[:, 0, :N_EXPERTS].astype(jnp.int32)
    nch = (cnt + (CHUNK - 1)) // CHUNK
    local_end = jnp.cumsum(nch, axis=1)
    local_start = local_end - nch
    block_chunks = local_end[:, -1]
    before_block = jnp.cumsum(nch, axis=0) - nch
    expert_chunks = jnp.sum(nch, axis=0)
    chunks_per_tile = EXPERT_TILE // CHUNK
    expert_tiles = (expert_chunks + (chunks_per_tile - 1)) // chunks_per_tile
    tiles_end = jnp.cumsum(expert_tiles)
    region_start = (tiles_end - expert_tiles) * chunks_per_tile
    segment_dst = region_start[None, :] + before_block
    i32 = lambda a: a.astype(jnp.int32)
    return dict(seg_len=i32(nch), seg_src=i32(local_start), seg_dst=i32(segment_dst),
                block_chunks=i32(block_chunks),
                expert_tiles=i32(expert_tiles), tiles_used=i32(tiles_end[-1:]),
                pad_start=i32(region_start + expert_chunks),
                pad_count=i32(expert_tiles * chunks_per_tile - expert_chunks))


def _local_positions(sel):
    tb = sel.shape[0]
    earlier = (lax.broadcasted_iota(jnp.int32, (tb, tb), 1)
               < lax.broadcasted_iota(jnp.int32, (tb, tb), 0)).astype(BF16)
    rank = jnp.dot(earlier, sel, preferred_element_type=F32)
    cnt = jnp.sum(sel.astype(F32), axis=0, keepdims=True)
    nch = jnp.floor((cnt + (CHUNK - 1)) * (1.0 / CHUNK))
    lower = (lax.broadcasted_iota(jnp.int32, (LANES, LANES), 0)
             < lax.broadcasted_iota(jnp.int32, (LANES, LANES), 1)).astype(BF16)
    start = CHUNK * jnp.dot(jnp.broadcast_to(nch, (8, LANES)).astype(BF16), lower,
                            preferred_element_type=F32)[0:1]
    pos = rank + start
    chosen = sel > 0
    pos_lo = jnp.min(jnp.where(chosen, pos, float(LOCAL_ROWS)), axis=1, keepdims=True)
    pos_hi = jnp.max(jnp.where(chosen, pos, -1.0), axis=1, keepdims=True)
    return pos, pos_lo, pos_hi


def _for_each(n, fn):
    lax.fori_loop(0, n, lambda c, carry: (fn(c), carry)[1], 0)


def _segment_rows(len_ref, src_ref, dst_ref, blk, e):
    n = len_ref[blk, e] * CHUNK
    src = pl.multiple_of(src_ref[blk, e] * CHUNK, CHUNK)
    dst = pl.multiple_of(dst_ref[blk, e] * CHUNK, CHUNK)
    return n, src, dst


def _scatter_kernel(len_ref, src_ref, dst_ref, nchunk_ref, pad_start_ref, pad_count_ref, used_ref,
                    x_ref, sel_ref, g_ref, loc_ref, zero_ref, sems):
    b = pl.program_id(0)
    last = pl.num_programs(0) - 1
    slot = b % 2
    tb = x_ref.shape[0]

    def start_block(blk, slot):
        def start_segment(e):
            n, src, dst = _segment_rows(len_ref, src_ref, dst_ref, blk, e)

            @pl.when(n > 0)
            def _():
                pltpu.make_async_copy(loc_ref.at[slot, pl.ds(src, n)], g_ref.at[pl.ds(dst, n)],
                                      sems.at[slot]).start()

        _for_each(N_EXPERTS, start_segment)

    def wait_block(blk, slot):
        n = nchunk_ref[blk] * CHUNK
        pltpu.make_async_copy(loc_ref.at[slot, pl.ds(0, n)], g_ref.at[pl.ds(0, n)],
                              sems.at[slot]).wait()

    def pad_copy(e):
        n = pad_count_ref[e] * CHUNK
        dst = pl.multiple_of(pad_start_ref[e] * CHUNK, CHUNK)
        return n, pltpu.make_async_copy(zero_ref.at[pl.ds(0, n)], g_ref.at[pl.ds(dst, n)],
                                        sems.at[2])

    def unused_tile_copy(i):
        dst = pl.multiple_of(i * EXPERT_TILE, EXPERT_TILE)
        return pltpu.make_async_copy(zero_ref, g_ref.at[pl.ds(dst, EXPERT_TILE)], sems.at[2])

    @pl.when(b >= 2)
    def _():
        wait_block(b - 2, slot)

    _, pos_lo, pos_hi = _local_positions(sel_ref[...])
    lo = pos_lo.astype(jnp.int32)
    hi = pos_hi.astype(jnp.int32)
    x = x_ref[...]

    def sort_rows(height):
        r = lax.broadcasted_iota(jnp.int32, (tb, height), 1)
        perm = ((r == lo) | (r == hi)).astype(BF16)
        rows = lax.dot_general(perm, x, (((0,), (0,)), ((), ())), preferred_element_type=F32)
        loc_ref[slot, :height, :] = rows.astype(BF16)

    n_rows = nchunk_ref[b] * CHUNK
    for below, height in zip((0,) + SORT_HEIGHTS[:-1], SORT_HEIGHTS):
        @pl.when((n_rows > below) & (n_rows <= height))
        def _():
            sort_rows(height)

    start_block(b, slot)

    @pl.when(b == last)
    def _():
        zero_ref[...] = jnp.zeros_like(zero_ref)
        n_unused = g_ref.shape[0] // EXPERT_TILE - used_ref[0]

        def each_pad(act):
            def one(e):
                n, copy = pad_copy(e)

                @pl.when(n > 0)
                def _():
                    act(copy)

            _for_each(N_EXPERTS, one)

        each_pad(lambda copy: copy.start())
        _for_each(n_unused, lambda i: unused_tile_copy(used_ref[0] + i).start())
        each_pad(lambda copy: copy.wait())
        _for_each(n_unused, lambda i: unused_tile_copy(used_ref[0] + i).wait())

        @pl.when(b >= 1)
        def _():
            wait_block(b - 1, 1 - slot)

        wait_block(b, slot)


def _scatter(x1b, sel, plan, n_tiles):
    t, d = x1b.shape
    tb = ROUTE_BLOCK
    row = lambda b, *_: (b, 0)
    blocks = 2 * (tb * d * 2 + tb * LANES * 2) + 2 * LOCAL_ROWS * d * 2 + EXPERT_TILE * d * 2
    temps = tb * tb * 2 + 6 * tb * LANES * 4 + tb * LOCAL_ROWS * 6 + LOCAL_ROWS * d * 6
    return pl.pallas_call(
        _scatter_kernel,
        grid_spec=pltpu.PrefetchScalarGridSpec(
            num_scalar_prefetch=7,
            grid=(t // tb,),
            in_specs=[pl.BlockSpec((tb, d), row), pl.BlockSpec((tb, LANES), row)],
            out_specs=pl.BlockSpec(memory_space=pl.ANY),
            scratch_shapes=[pltpu.VMEM((2, LOCAL_ROWS, d), BF16),
                            pltpu.VMEM((EXPERT_TILE, d), BF16), pltpu.SemaphoreType.DMA((3,))]),
        out_shape=jax.ShapeDtypeStruct((n_tiles * EXPERT_TILE, d), BF16),
        compiler_params=_params(("arbitrary",), blocks + temps),
        name="scatter",
    )(plan["seg_len"], plan["seg_src"], plan["seg_dst"], plan["block_chunks"], plan["pad_start"],
      plan["pad_count"], plan["tiles_used"], x1b, sel)


def _experts_kernel(ntile_ref, used_ref, g_ref, wg_ref, wu_ref, wd_ref, y_ref,
                    x_buf, y_buf, wg_buf, wu_buf, wd_buf, wgb_ref, wub_ref, wdb_ref,
                    x_sems, y_sems, w_sems):
    n_experts = wg_ref.shape[0]
    tm = EXPERT_TILE
    used = used_ref[0]

    def weight_copies(e, slot):
        return [pltpu.make_async_copy(src.at[e], dst.at[slot], w_sems.at[slot])
                for src, dst in ((wg_ref, wg_buf), (wu_ref, wu_buf), (wd_ref, wd_buf))]

    def x_copy(t, slot):
        rows = pl.ds(pl.multiple_of(t * tm, tm), MACRO_TILES * tm)
        return pltpu.make_async_copy(g_ref.at[rows], x_buf.at[slot], x_sems.at[slot])

    def y_copy(t, k, slot):
        n = k * tm
        dst = pl.ds(pl.multiple_of(t * tm, tm), n)
        return pltpu.make_async_copy(y_buf.at[slot, pl.ds(0, n)], y_ref.at[dst], y_sems.at[slot])

    def wait_y(t, k, slot):
        @pl.when(k > 0)
        def _():
            y_copy(t, k, slot).wait()

    def mlp(slot, rows):
        x = x_buf[slot, :rows, :]
        hg = jnp.dot(x, wgb_ref[...], preferred_element_type=F32)
        hu = jnp.dot(x, wub_ref[...], preferred_element_type=F32)
        h = hg * _sigmoid(hg) * hu
        y_buf[slot, :rows, :] = jnp.dot(h.astype(BF16), wdb_ref[...],
                                        preferred_element_type=F32).astype(BF16)

    for c in weight_copies(0, 0):
        c.start()
    x_copy(0, 0).start()

    def run_expert(e, carry):
        wslot = e % 2
        for c in weight_copies(e, wslot):
            c.wait()

        @pl.when(e + 1 < n_experts)
        def _():
            for c in weight_copies(e + 1, 1 - wslot):
                c.start()

        @pl.when(ntile_ref[e] > 0)
        def _():
            wgb_ref[...] = wg_buf[wslot].astype(BF16)
            wub_ref[...] = wu_buf[wslot].astype(BF16)
            wdb_ref[...] = wd_buf[wslot].astype(BF16)

        def run_macro(m, carry):
            t, step, k1, t1, k2, t2 = carry
            k = jnp.minimum(MACRO_TILES, ntile_ref[e] - m * MACRO_TILES)
            slot = step % 2
            x_copy(t, slot).wait()

            @pl.when(t + k < used)
            def _():
                x_copy(t + k, 1 - slot).start()

            wait_y(t2, k2, slot)
            for tiles in range(1, MACRO_TILES + 1):
                @pl.when(k == tiles)
                def _():
                    mlp(slot, tiles * tm)

            y_copy(t, k, slot).start()
            return t + k, step + 1, k, t, k1, t1

        n_macro = (ntile_ref[e] + (MACRO_TILES - 1)) // MACRO_TILES
        return lax.fori_loop(0, n_macro, run_macro, carry)

    zero = jnp.int32(0)
    _, step, k1, t1, k2, t2 = lax.fori_loop(0, n_experts, run_expert, (zero,) * 6)
    wait_y(t2, k2, step % 2)
    wait_y(t1, k1, (step + 1) % 2)
    y_buf[0, :tm, :] = jnp.zeros((tm, y_buf.shape[2]), BF16)
    n_unused = y_ref.shape[0] // tm - used
    _for_each(n_unused, lambda i: y_copy(used + i, 1, 0).start())
    _for_each(n_unused, lambda i: y_copy(used + i, 1, 0).wait())


def _experts(sorted_x, plan, w_eg, w_eu, w_ed):
    rows, d = sorted_x.shape
    _, _, de = w_eg.shape
    tm = EXPERT_TILE
    any_space = pl.BlockSpec(memory_space=pl.ANY)
    big = MACRO_TILES * tm
    scratch = 2 * 2 * big * d * 2 + 2 * 3 * d * de * 4 + 3 * d * de * 2
    temps = 3 * big * de * 4 + big * d * 4 + d * de * 4
    return pl.pallas_call(
        _experts_kernel,
        grid_spec=pltpu.PrefetchScalarGridSpec(
            num_scalar_prefetch=2,
            grid=(1,),
            in_specs=[any_space] * 4,
            out_specs=any_space,
            scratch_shapes=[pltpu.VMEM((2, big, d), BF16), pltpu.VMEM((2, big, d), BF16),
                            pltpu.VMEM((2, d, de), F32), pltpu.VMEM((2, d, de), F32),
                            pltpu.VMEM((2, de, d), F32),
                            pltpu.VMEM((d, de), BF16), pltpu.VMEM((d, de), BF16),
                            pltpu.VMEM((de, d), BF16),
                            pltpu.SemaphoreType.DMA((2,)), pltpu.SemaphoreType.DMA((2,)),
                            pltpu.SemaphoreType.DMA((2,))]),
        out_shape=jax.ShapeDtypeStruct((rows, d), BF16),
        compiler_params=_params(("arbitrary",), scratch + temps),
        name="experts",
    )(plan["expert_tiles"], plan["tiles_used"], sorted_x, w_eg, w_eu, w_ed)


def _combine_kernel(len_ref, src_ref, dst_ref, nchunk_ref, x1_ref, comb_ref, sel_ref, p_ref,
                    wpg_ref, wpp_ref,
                    lg_ref, lb_ref, y_ref, o_ref, loc_ref, wpgb_ref, wppb_ref, sems, *, alpha):
    b = pl.program_id(0)
    nb = pl.num_programs(0)
    slot = b % 2
    tb, d = x1_ref.shape
    _cast_once(wpg_ref, wpgb_ref)
    _cast_once(wpp_ref, wppb_ref)

    def start_block(blk, slot):
        def start_segment(e):
            n, local, sorted_at = _segment_rows(len_ref, src_ref, dst_ref, blk, e)

            @pl.when(n > 0)
            def _():
                pltpu.make_async_copy(y_ref.at[pl.ds(sorted_at, n)],
                                      loc_ref.at[slot, pl.ds(local, n)], sems.at[slot]).start()

        _for_each(N_EXPERTS, start_segment)

    @pl.when(b == 0)
    def _():
        loc_ref[...] = jnp.zeros_like(loc_ref)
        start_block(0, 0)

    @pl.when(b + 1 < nb)
    def _():
        start_block(b + 1, 1 - slot)

    n_rows = nchunk_ref[b] * CHUNK
    pltpu.make_async_copy(y_ref.at[pl.ds(0, n_rows)], loc_ref.at[slot, pl.ds(0, n_rows)],
                          sems.at[slot]).wait()

    sel = sel_ref[...]
    pos, pos_lo, pos_hi = _local_positions(sel)
    chosen = sel > 0
    comb = comb_ref[...]
    w_lo = jnp.sum(jnp.where(chosen & (pos == pos_lo), comb, 0.0), axis=1, keepdims=True)
    w_hi = jnp.sum(jnp.where(chosen & (pos == pos_hi), comb, 0.0), axis=1, keepdims=True)
    lo = pos_lo.astype(jnp.int32)
    hi = pos_hi.astype(jnp.int32)

    sorted_row = lax.broadcasted_iota(jnp.int32, (tb // ROW_SUBS, LOCAL_ROWS), 1)

    def branches(r):
        weights = (jnp.where(sorted_row == lo[r], w_lo[r], 0.0)
                   + jnp.where(sorted_row == hi[r], w_hi[r], 0.0)).astype(BF16)
        moe = jnp.dot(weights, loc_ref[slot], preferred_element_type=F32)
        gate = jnp.dot(x1_ref[r, :].astype(BF16), wpgb_ref[...], preferred_element_type=F32)
        emb = jnp.dot(p_ref[r, :].astype(BF16), wppb_ref[...], preferred_element_type=F32)
        return moe, gate, emb

    def finish(r, parts):
        moe, gate, emb = parts
        h = alpha * x1_ref[r, :] + moe + _sigmoid(gate) * emb
        o_ref[r, :] = _layer_norm(h, lg_ref[...], lb_ref[...])

    _skewed(_row_subs(tb), branches, finish)


def _combine(y, plan, x1, comb, sel, p2, w_pg, w_pp, ln_g, ln_b, alpha):
    t, d = x1.shape
    pd = p2.shape[1]
    tb = ROUTE_BLOCK
    row = lambda b, *_: (b, 0)
    blocks = (2 * (2 * tb * d * 4 + tb * LANES * (4 + 2) + tb * pd * 4)
              + (d * d + pd * d) * (4 + 2) + 2 * d * 4 + 2 * LOCAL_ROWS * d * 2)
    temps = tb * tb * 2 + 8 * tb * LANES * 4 + tb * LOCAL_ROWS * 10 + 5 * tb * d * 4
    return pl.pallas_call(
        functools.partial(_combine_kernel, alpha=alpha),
        grid_spec=pltpu.PrefetchScalarGridSpec(
            num_scalar_prefetch=4,
            grid=(t // tb,),
            in_specs=[pl.BlockSpec((tb, d), row), pl.BlockSpec((tb, LANES), row),
                      pl.BlockSpec((tb, LANES), row), pl.BlockSpec((tb, pd), row),
                      _resident((d, d)), _resident((pd, d)), _resident((1, d)), _resident((1, d)),
                      pl.BlockSpec(memory_space=pl.ANY)],
            out_specs=pl.BlockSpec((tb, d), row),
            scratch_shapes=[pltpu.VMEM((2, LOCAL_ROWS, d), BF16), pltpu.VMEM((d, d), BF16),
                            pltpu.VMEM((pd, d), BF16), pltpu.SemaphoreType.DMA((2,))]),
        out_shape=jax.ShapeDtypeStruct((t, d), F32),
        compiler_params=_params(("arbitrary",), blocks + temps),
        name="combine",
    )(plan["seg_len"], plan["seg_src"], plan["seg_dst"], plan["block_chunks"], x1, comb, sel, p2,
      w_pg, w_pp, ln_g, ln_b, y)


def kernel(x, p, w_in, w_pool, pool_scale, w_pu, w_au, w_o, ln1_g, ln1_b, w_rg, b_rg, w_re, b_re,
           w_eg, w_eu, w_ed, w_pg, w_pp, ln2_g, ln2_b):
    bsz, seq, d = x.shape
    depth = w_in.shape[0]
    t = bsz * seq
    de = w_eg.shape[-1]
    alpha = (2.0 * depth) ** 0.25
    assert w_rg.shape[2] == N_GROUPS and w_re.shape[1:] == (N_GROUPS, d, EXPERTS_PER_GROUP)
    assert w_in.shape[2] == 4 * d and w_pool.shape[1] == len(POOL_WINDOWS)
    assert t % ROUTE_BLOCK == 0 and N_EXPERTS + N_GROUPS <= LANES
    n_blocks = t // ROUTE_BLOCK
    n_tiles = -(-(2 * t + n_blocks * N_EXPERTS * (CHUNK - 1) + N_EXPERTS * (EXPERT_TILE - CHUNK))
                // EXPERT_TILE) + MACRO_TILES - 1

    x2 = x.reshape(t, d)
    for i in range(depth):
        pool_out, q, k, v, gates = _in_hbm(
            *_proj(x2, w_in[i], w_pool[i], pool_scale[i][None, :], seq, tm=512))
        attn_out, = _in_hbm(_attention(q, k, v, seq, tb=128))

        w_r = jnp.concatenate(
            [w_re[i].transpose(1, 0, 2).reshape(d, N_EXPERTS), w_rg[i]], axis=1)
        w_r = jnp.pad(w_r, ((0, 0), (0, LANES - w_r.shape[1])))
        b_r = jnp.pad(jnp.concatenate([b_re[i].reshape(-1), b_rg[i]]),
                      (0, LANES - N_GROUPS - N_EXPERTS))[None, :]

        x1, x1b, comb, sel, counts = _merge(
            x2, pool_out, attn_out, gates, *_in_hbm(w_pu[i], w_au[i], w_o[i]), ln1_g[i][None, :],
            ln1_b[i][None, :], w_r, b_r, alpha, tm=ROUTE_BLOCK)
        x1, x1b, comb, sel = _in_hbm(x1, x1b, comb, sel)

        plan = _routing_plan(counts)
        sorted_x, = _in_hbm(_scatter(x1b, sel, plan, n_tiles))
        y, = _in_hbm(_experts(sorted_x, plan, w_eg[i].reshape(N_EXPERTS, d, de),
                              w_eu[i].reshape(N_EXPERTS, d, de), w_ed[i].reshape(N_EXPERTS, de, d)))
        x2 = _combine(y, plan, x1, comb, sel, p[i].reshape(t, -1), *_in_hbm(w_pg[i], w_pp[i]),
                      ln2_g[i][None, :], ln2_b[i][None, :], alpha)
    return x2.reshape(bsz, seq, d)
```

```python
import functools
import math

import jax
import jax.numpy as jnp
from jax import lax
from jax.experimental import pallas as pl
from jax.experimental.pallas import tpu as pltpu

F32 = jnp.float32
BF16 = jnp.bfloat16

LANES = 128
POOL_WINDOWS = (2, 4, 8, 16)
POOL_HALO = 16
HEAD_DIM = 64
N_GROUPS = 4
EXPERTS_PER_GROUP = 8
N_EXPERTS = N_GROUPS * EXPERTS_PER_GROUP
LN_EPS = 1e-5
GROUP_LANE0 = N_EXPERTS
NEG_BIG = -1e30
ROUTE_BLOCK = 512
CHUNK = 16
EXPERT_TILE = 256
MACRO_TILES = 6
SORT_ROWS = 256
ROW_SUBS = 2
LOCAL_ROWS = -(-(2 * ROUTE_BLOCK + N_EXPERTS * (CHUNK - 1)) // SORT_ROWS) * SORT_ROWS
SORT_HEIGHTS = (LOCAL_ROWS - SORT_ROWS, LOCAL_ROWS - SORT_ROWS // 2, LOCAL_ROWS)
ATTN_DEAD_LOG2 = -160.0
VMEM_CAP_BYTES = 56 * 1024 * 1024


def _params(sem, vmem_bytes):
    return pltpu.CompilerParams(
        dimension_semantics=sem, vmem_limit_bytes=min(int(vmem_bytes), VMEM_CAP_BYTES))


def _layer_norm(h, g, b):
    mu = jnp.mean(h, axis=-1, keepdims=True)
    c = h - mu
    var = jnp.mean(c * c, axis=-1, keepdims=True)
    return c * lax.rsqrt(var + LN_EPS) * g + b


def _sigmoid(z):
    return 1.0 / (1.0 + jnp.exp(-z))


def _row_subs(rows):
    return [slice(k * (rows // ROW_SUBS), (k + 1) * (rows // ROW_SUBS)) for k in range(ROW_SUBS)]


def _skewed(subs, first, second):
    out, pending = [], None
    for r in subs:
        mid = first(r)
        if pending is not None:
            out.append(second(*pending))
        pending = (r, mid)
    out.append(second(*pending))
    return out


def _in_hbm(*arrays):
    if not all(isinstance(a, jax.core.Tracer) for a in arrays):
        return list(arrays)
    return [pltpu.with_memory_space_constraint(a, pltpu.HBM) for a in arrays]


def _resident(shape):
    return pl.BlockSpec(shape, lambda *_: (0,) * len(shape), pipeline_mode=pl.Buffered(1))


def _cast_once(w_ref, wb_ref):
    @pl.when(pl.program_id(0) == 0)
    def _():
        wb_ref[...] = w_ref[...].astype(BF16)


def _proj_kernel(x_ref, w_ref, wp_ref, sc_ref, po_ref, q_ref, k_ref, v_ref, g_ref, wb_ref, halo_ref,
                 *, width, tiles_per_seq):
    q_scale = math.log2(math.e) / math.sqrt(HEAD_DIM)
    tile_in_seq = pl.program_id(0) % tiles_per_seq
    _cast_once(w_ref, wb_ref)
    xb = x_ref[...].astype(BF16)

    def mm(lo):
        return jnp.dot(xb, wb_ref[:, lo:lo + width], preferred_element_type=F32)

    @pl.when(tile_in_seq == 0)
    def _():
        halo_ref[...] = jnp.zeros_like(halo_ref)

    u = mm(0)
    tm = u.shape[0]
    gd = wp_ref.shape[1]
    pos = tile_in_seq * tm + lax.broadcasted_iota(jnp.int32, (tm, gd), 0)

    def pool_group(g, w):
        cols = slice(g * gd, (g + 1) * gd)
        ug = u[:, cols]
        s = jnp.concatenate([halo_ref[:, cols], ug], axis=0)
        sh = 1
        while sh < w:
            s = s + pltpu.roll(s, sh, axis=0)
            sh *= 2
        cnt = jnp.minimum(pos + 1, w).astype(F32)
        pooled = s[POOL_HALO:, :] / cnt - ug
        mixed = jnp.dot(pooled.astype(BF16), wp_ref[g].astype(BF16), preferred_element_type=F32)
        po_ref[:, cols] = (mixed * sc_ref[:, cols]).astype(BF16)

    def write_q():
        q_ref[...] = (mm(width) * q_scale).astype(BF16)

    def write_k():
        k_ref[...] = mm(2 * width).astype(BF16)

    def write_v():
        v_ref[...] = mm(3 * width).astype(BF16)

    def gate_chunk(c):
        g_ref[:, c * width:(c + 1) * width] = _sigmoid(mm((4 + c) * width)).astype(BF16)

    matmuls = [write_q, write_k, write_v] + [
        functools.partial(gate_chunk, c) for c in range(g_ref.shape[1] // width)]
    for n, matmul in enumerate(matmuls):
        matmul()
        if n < len(POOL_WINDOWS):
            pool_group(n, POOL_WINDOWS[n])
    halo_ref[...] = u[tm - POOL_HALO:, :]


def _proj(x2, w_in, w_pool, pool_scale, seq, tm):
    t, d = x2.shape
    n = w_in.shape[1]
    width = d // 2
    gate_w = n - 4 * width
    groups, gd, _ = w_pool.shape
    assert seq % tm == 0 and groups * gd == width and max(POOL_WINDOWS) <= POOL_HALO + 1
    row = lambda i: (i, 0)
    blocks = (2 * (tm * d * 4 + tm * width * 4 * 2 + tm * gate_w * 2)
              + d * n * (4 + 2) + groups * gd * gd * 4 + width * 4 + POOL_HALO * width * 4)
    temps = tm * d * 2 + 3 * tm * width * 4 + 6 * (tm + POOL_HALO) * gd * 4
    return pl.pallas_call(
        functools.partial(_proj_kernel, width=width, tiles_per_seq=seq // tm),
        grid=(t // tm,),
        in_specs=[pl.BlockSpec((tm, d), row), _resident((d, n)), _resident((groups, gd, gd)),
                  _resident((1, width))],
        out_specs=[pl.BlockSpec((tm, width), row)] * 4 + [pl.BlockSpec((tm, gate_w), row)],
        out_shape=[jax.ShapeDtypeStruct((t, width), BF16)] * 4
        + [jax.ShapeDtypeStruct((t, gate_w), BF16)],
        scratch_shapes=[pltpu.VMEM((d, n), BF16), pltpu.VMEM((POOL_HALO, width), F32)],
        compiler_params=_params(("arbitrary",), blocks + temps),
        name="proj",
    )(x2, w_in, w_pool, pool_scale)


def _attn_kernel(q_ref, k_ref, v_ref, o_ref, rem_ref, acc_ref, *, tb):
    pairs = q_ref.shape[1] // LANES
    blocks_per_step = q_ref.shape[0] // tb
    col_blocks = [slice(p * LANES, (p + 1) * LANES) for p in range(pairs)]
    first_head = lax.broadcasted_iota(jnp.int32, (tb, LANES), 1) < HEAD_DIM
    r = lax.broadcasted_iota(jnp.int32, (2 * tb, 2 * tb), 0)
    c = lax.broadcasted_iota(jnp.int32, (2 * tb, 2 * tb), 1)
    same_head = (r >= tb) == (c >= tb)
    cum = -jnp.concatenate([(same_head & (r >= c)).astype(BF16), same_head.astype(BF16)], axis=1)
    kcol = lax.broadcasted_iota(jnp.int32, (tb, 2 * tb), 1)
    kcol = jnp.where(kcol >= tb, kcol - tb, kcol)
    causal = kcol < lax.broadcasted_iota(jnp.int32, (tb, 2 * tb), 0)

    def stack_heads(blk):
        zero = jnp.zeros_like(blk)
        return jnp.concatenate(
            [jnp.where(first_head, blk, zero), jnp.where(first_head, zero, blk)], axis=0)

    def scores(q0, j, rows, diagonal):
        ks = pl.multiple_of(j * tb, tb)
        q_rows = pl.ds(q0 + rows.start, rows.stop - rows.start)
        zs = [lax.dot_general(q_ref[q_rows, cols], stack_heads(k_ref[pl.ds(ks, tb), cols]),
                              (((1,), (1,)), ((), ())), preferred_element_type=F32)
              for cols in col_blocks]
        sums = []
        for z in zs:
            softplus = jnp.maximum(z, 0.0) + jnp.log2(1.0 + jnp.exp2(-jnp.abs(z)))
            if diagonal:
                softplus = jnp.where(causal, softplus, 0.0)
            sums.append(jnp.dot(softplus.astype(BF16), cum, preferred_element_type=F32))
        return zs, sums

    def weighted_values(j, cols, z, later):
        ks = pl.multiple_of(j * tb, tb)
        a = jnp.exp2(z + later)
        return jnp.dot(a.astype(BF16), stack_heads(v_ref[pl.ds(ks, tb), cols]),
                       preferred_element_type=F32)

    def first_step(qi, q0, n_before):
        full, half = slice(0, tb), slice(0, tb // 2)
        z_d, sums_d = scores(q0, qi, full, diagonal=True)
        if n_before >= 1:
            z_1, sums_1 = scores(q0, qi - 1, full, diagonal=False)
        if n_before >= 2:
            z_2, sums_2 = scores(q0, qi - 2, half, diagonal=False)
        owed = []
        for p, cols in enumerate(col_blocks):
            ks = pl.multiple_of(qi * tb, tb)
            a = jnp.where(causal, jnp.exp2(z_d[p] + sums_d[p][:, :2 * tb]), 0.0)
            out = jnp.dot(a.astype(BF16), stack_heads(v_ref[pl.ds(ks, tb), cols]),
                          preferred_element_type=F32)
            total = sums_d[p][:, 2 * tb:]
            if n_before >= 1:
                out = out + weighted_values(qi - 1, cols, z_1[p], sums_1[p][:, :2 * tb] + total)
                total = total + sums_1[p][:, 2 * tb:]
            acc_ref[p] = out
            rem_ref[p] = total
            if n_before >= 2:
                owed.append(total[tb // 2:, :])
                later = sums_2[p][:, :2 * tb] + total[half, :]
                acc_ref[p, half, :] += weighted_values(qi - 2, cols, z_2[p], later)
                rem_ref[p, half, :] += sums_2[p][:, 2 * tb:]
        if n_before < 2:
            return None
        return jnp.max(functools.reduce(jnp.maximum, owed)) > ATTN_DEAD_LOG2

    def step(q0, j, rows):
        zs, sums = scores(q0, j, rows, diagonal=False)
        for p, cols in enumerate(col_blocks):
            later = sums[p][:, :2 * tb] + rem_ref[p, rows, :]
            acc_ref[p, rows, :] += weighted_values(j, cols, zs[p], later)
            rem_ref[p, rows, :] += sums[p][:, 2 * tb:]

    def live_rows():
        worst = functools.reduce(jnp.maximum, [rem_ref[p] for p in range(pairs)])
        live = jnp.max(worst, axis=1, keepdims=True) > ATTN_DEAD_LOG2
        row = lax.broadcasted_iota(jnp.int32, (tb, 1), 0)
        return jnp.max(jnp.where(live, row + 1, 0))

    row_counts = (tb, tb // 2, tb // 4)

    def one_block(s, carry):
        qi = pl.program_id(1) * blocks_per_step + s
        q0 = pl.multiple_of(s * tb, tb)
        for n_before in (0, 1):
            @pl.when(qi == n_before)
            def _():
                first_step(qi, q0, n_before)

        @pl.when(qi >= 2)
        def _():
            second_half_owed = first_step(qi, q0, 2)

            @pl.when(second_half_owed)
            def _():
                step(q0, qi - 2, slice(tb // 2, tb))

        def cond(state):
            j, n_live = state
            return (j >= 0) & (n_live > 0)

        def body(state):
            j, n_live = state
            for rows, fewer in zip(row_counts, row_counts[1:] + (0,)):
                @pl.when((n_live > fewer) & (n_live <= rows))
                def _():
                    step(q0, j, slice(0, rows))

            return j - 1, live_rows()

        lax.while_loop(cond, body, (qi - 3, live_rows()))
        for p, cols in enumerate(col_blocks):
            o_ref[pl.ds(q0, tb), cols] = acc_ref[p].astype(BF16)
        return carry

    lax.fori_loop(0, blocks_per_step, one_block, 0)


def _attention(q, k, v, seq, tb, blocks_per_step):
    t, width = q.shape
    pairs = width // LANES
    tq = tb * blocks_per_step
    blocks = 2 * (2 * tq * width * 2 + 2 * seq * width * 2) + pairs * tb * (2 * tb + LANES) * 4
    temps = pairs * 10 * tb * 2 * tb * 4 + 2 * tb * 4 * tb * 2
    qspec = pl.BlockSpec((tq, width), lambda b, i: (b * (seq // tq) + i, 0))
    kvspec = pl.BlockSpec((seq, width), lambda b, i: (b, 0))
    return pl.pallas_call(
        functools.partial(_attn_kernel, tb=tb),
        grid=(t // seq, seq // tq),
        in_specs=[qspec, kvspec, kvspec],
        out_specs=qspec,
        out_shape=jax.ShapeDtypeStruct((t, width), BF16),
        scratch_shapes=[pltpu.VMEM((pairs, tb, 2 * tb), F32), pltpu.VMEM((pairs, tb, LANES), F32)],
        compiler_params=_params(("parallel", "parallel"), blocks + temps),
        name="attn",
    )(q, k, v)


def _route(logits):
    lane = lax.broadcasted_iota(jnp.int32, logits.shape, 1)

    def first_max(vals):
        m = jnp.max(vals, axis=1, keepdims=True)
        idx = jnp.min(jnp.where(vals == m, lane, LANES), axis=1, keepdims=True)
        return m, idx

    is_group = (lane >= GROUP_LANE0) & (lane < GROUP_LANE0 + N_GROUPS)
    gm, g_lane = first_max(jnp.where(is_group, logits, NEG_BIG))
    g_prob = 1.0 / jnp.sum(jnp.where(is_group, jnp.exp(logits - gm), 0.0), axis=1, keepdims=True)
    lo = EXPERTS_PER_GROUP * (g_lane - GROUP_LANE0)
    in_group = jnp.where((lane >= lo) & (lane < lo + EXPERTS_PER_GROUP), logits, NEG_BIG)
    m1, i1 = first_max(in_group)
    m2, i2 = first_max(jnp.where(lane == i1, NEG_BIG, in_group))
    e21 = jnp.exp(m2 - m1)
    w1 = g_prob / (1.0 + e21)
    w2 = w1 * e21
    comb = jnp.where(lane == i1, w1, 0.0) + jnp.where(lane == i2, w2, 0.0)
    sel = ((lane == i1) | (lane == i2)).astype(F32)
    return comb, sel


def _merge_kernel(x_ref, po_ref, at_ref, g_ref, wpu_ref, wau_ref, wo_ref, lg_ref, lb_ref,
                  wr_ref, br_ref, x1_ref, x1b_ref, comb_ref, sel_ref, cnt_ref,
                  wpub_ref, waub_ref, wob_ref, wrb_ref, *, alpha):
    tm, d = x_ref.shape
    _cast_once(wpu_ref, wpub_ref)
    _cast_once(wau_ref, waub_ref)
    _cast_once(wo_ref, wob_ref)

    @pl.when(pl.program_id(0) == 0)
    def _():
        w_r = wr_ref[...]
        hi = w_r.astype(BF16)
        wrb_ref[:, :LANES] = hi
        wrb_ref[:, LANES:] = (w_r - hi.astype(F32)).astype(BF16)

    def mix(r):
        a = jnp.dot(po_ref[r, :], wpub_ref[...], preferred_element_type=F32)
        b = jnp.dot(at_ref[r, :], waub_ref[...], preferred_element_type=F32)
        merged = (g_ref[r, :d].astype(F32) * a + g_ref[r, d:].astype(F32) * b).astype(BF16)
        return alpha * x_ref[r, :] + jnp.dot(merged, wob_ref[...], preferred_element_type=F32)

    def norm_and_route(r, h):
        x1 = _layer_norm(h, lg_ref[...], lb_ref[...])
        xh = x1.astype(BF16)
        x1_ref[r, :] = x1
        x1b_ref[r, :] = xh
        xl = (x1 - xh.astype(F32)).astype(BF16)
        by_hi = jnp.dot(xh, wrb_ref[...], preferred_element_type=F32)
        by_lo = jnp.dot(xl, wrb_ref[:, :LANES], preferred_element_type=F32)
        comb, sel = _route(by_hi[:, :LANES] + by_hi[:, LANES:] + by_lo + br_ref[...])
        comb_ref[r, :] = comb
        sel_ref[r, :] = sel.astype(BF16)
        return jnp.sum(sel, axis=0, keepdims=True)

    cnt_ref[0] = sum(_skewed(_row_subs(tm), mix, norm_and_route))


def _merge(x2, pool_out, attn_out, gates, w_pu, w_au, w_o, ln_g, ln_b, w_r, b_r, alpha, tm):
    t, d = x2.shape
    width = pool_out.shape[1]
    row = lambda i: (i, 0)
    blocks = (2 * (tm * d * 4 + 2 * tm * width * 2 + tm * 2 * d * 2
                   + tm * d * 4 + tm * d * 2 + tm * LANES * (4 + 2) + LANES * 4)
              + (2 * width * d + d * d + d * LANES) * (4 + 2) + 2 * d * 4 + LANES * 4)
    temps = 5 * tm * d * 4
    return pl.pallas_call(
        functools.partial(_merge_kernel, alpha=alpha),
        grid=(t // tm,),
        in_specs=[pl.BlockSpec((tm, d), row), pl.BlockSpec((tm, width), row),
                  pl.BlockSpec((tm, width), row), pl.BlockSpec((tm, 2 * d), row),
                  _resident((width, d)), _resident((width, d)), _resident((d, d)),
                  _resident((1, d)), _resident((1, d)), _resident((d, LANES)),
                  _resident((1, LANES))],
        out_specs=[pl.BlockSpec((tm, d), row), pl.BlockSpec((tm, d), row),
                   pl.BlockSpec((tm, LANES), row), pl.BlockSpec((tm, LANES), row),
                   pl.BlockSpec((1, 1, LANES), lambda i: (i, 0, 0))],
        out_shape=[jax.ShapeDtypeStruct((t, d), F32), jax.ShapeDtypeStruct((t, d), BF16),
                   jax.ShapeDtypeStruct((t, LANES), F32), jax.ShapeDtypeStruct((t, LANES), BF16),
                   jax.ShapeDtypeStruct((t // tm, 1, LANES), F32)],
        scratch_shapes=[pltpu.VMEM((width, d), BF16), pltpu.VMEM((width, d), BF16),
                        pltpu.VMEM((d, d), BF16), pltpu.VMEM((d, 2 * LANES), BF16)],
        compiler_params=_params(("arbitrary",), blocks + temps),
        name="merge",
    )(x2, pool_out, attn_out, gates, w_pu, w_au, w_o, ln_g, ln_b, w_r, b_r)


def _routing_plan(counts):
    cnt = counts[:, 0, :N_EXPERTS].astype(jnp.int32)
    nch = (cnt + (CHUNK - 1)) // CHUNK
    local_end = jnp.cumsum(nch, axis=1)
    local_start = local_end - nch
    block_chunks = local_end[:, -1]
    before_block = jnp.cumsum(nch, axis=0) - nch
    expert_chunks = jnp.sum(nch, axis=0)
    chunks_per_tile = EXPERT_TILE // CHUNK
    expert_tiles = (expert_chunks + (chunks_per_tile - 1)) // chunks_per_tile
    tiles_end = jnp.cumsum(expert_tiles)
    region_start = (tiles_end - expert_tiles) * chunks_per_tile
    segment_dst = region_start[None, :] + before_block
    i32 = lambda a: a.astype(jnp.int32)
    return dict(seg_len=i32(nch), seg_src=i32(local_start), seg_dst=i32(segment_dst),
                block_chunks=i32(block_chunks),
                expert_tiles=i32(expert_tiles), tiles_used=i32(tiles_end[-1:]),
                pad_start=i32(region_start + expert_chunks),
                pad_count=i32(expert_tiles * chunks_per_tile - expert_chunks))


def _local_positions(sel):
    tb = sel.shape[0]
    earlier = (lax.broadcasted_iota(jnp.int32, (tb, tb), 1)
               < lax.broadcasted_iota(jnp.int32, (tb, tb), 0)).astype(BF16)
    rank = jnp.dot(earlier, sel, preferred_element_type=F32)
    cnt = jnp.sum(sel.astype(F32), axis=0, keepdims=True)
    nch = jnp.floor((cnt + (CHUNK - 1)) * (1.0 / CHUNK))
    lower = (lax.broadcasted_iota(jnp.int32, (LANES, LANES), 0)
             < lax.broadcasted_iota(jnp.int32, (LANES, LANES), 1)).astype(BF16)
    start = CHUNK * jnp.dot(jnp.broadcast_to(nch, (8, LANES)).astype(BF16), lower,
                            preferred_element_type=F32)[0:1]
    pos = rank + start
    chosen = sel > 0
    pos_lo = jnp.min(jnp.where(chosen, pos, float(LOCAL_ROWS)), axis=1, keepdims=True)
    pos_hi = jnp.max(jnp.where(chosen, pos, -1.0), axis=1, keepdims=True)
    return pos, pos_lo, pos_hi


def _for_each(n, fn):
    lax.fori_loop(0, n, lambda c, carry: (fn(c), carry)[1], 0)


def _segment_rows(len_ref, src_ref, dst_ref, blk, e):
    n = len_ref[blk, e] * CHUNK
    src = pl.multiple_of(src_ref[blk, e] * CHUNK, CHUNK)
    dst = pl.multiple_of(dst_ref[blk, e] * CHUNK, CHUNK)
    return n, src, dst


def _scatter_kernel(len_ref, src_ref, dst_ref, nchunk_ref, pad_start_ref, pad_count_ref, used_ref,
                    x_ref, sel_ref, g_ref, loc_ref, zero_ref, sems):
    b = pl.program_id(0)
    last = pl.num_programs(0) - 1
    slot = b % 2
    tb = x_ref.shape[0]

    def start_block(blk, slot):
        def start_segment(e):
            n, src, dst = _segment_rows(len_ref, src_ref, dst_ref, blk, e)

            @pl.when(n > 0)
            def _():
                pltpu.make_async_copy(loc_ref.at[slot, pl.ds(src, n)], g_ref.at[pl.ds(dst, n)],
                                      sems.at[slot]).start()

        _for_each(N_EXPERTS, start_segment)

    def wait_block(blk, slot):
        n = nchunk_ref[blk] * CHUNK
        pltpu.make_async_copy(loc_ref.at[slot, pl.ds(0, n)], g_ref.at[pl.ds(0, n)],
                              sems.at[slot]).wait()

    def pad_copy(e):
        n = pad_count_ref[e] * CHUNK
        dst = pl.multiple_of(pad_start_ref[e] * CHUNK, CHUNK)
        return n, pltpu.make_async_copy(zero_ref.at[pl.ds(0, n)], g_ref.at[pl.ds(dst, n)],
                                        sems.at[2])

    def unused_tile_copy(i):
        dst = pl.multiple_of(i * EXPERT_TILE, EXPERT_TILE)
        return pltpu.make_async_copy(zero_ref, g_ref.at[pl.ds(dst, EXPERT_TILE)], sems.at[2])

    @pl.when(b >= 2)
    def _():
        wait_block(b - 2, slot)

    _, pos_lo, pos_hi = _local_positions(sel_ref[...])
    lo = pos_lo.astype(jnp.int32)
    hi = pos_hi.astype(jnp.int32)
    x = x_ref[...]

    def sort_rows(height):
        r = lax.broadcasted_iota(jnp.int32, (tb, height), 1)
        perm = ((r == lo) | (r == hi)).astype(BF16)
        rows = lax.dot_general(perm, x, (((0,), (0,)), ((), ())), preferred_element_type=F32)
        loc_ref[slot, :height, :] = rows.astype(BF16)

    n_rows = nchunk_ref[b] * CHUNK
    for below, height in zip((0,) + SORT_HEIGHTS[:-1], SORT_HEIGHTS):
        @pl.when((n_rows > below) & (n_rows <= height))
        def _():
            sort_rows(height)

    start_block(b, slot)

    @pl.when(b == last)
    def _():
        zero_ref[...] = jnp.zeros_like(zero_ref)
        n_unused = g_ref.shape[0] // EXPERT_TILE - used_ref[0]

        def each_pad(act):
            def one(e):
                n, copy = pad_copy(e)

                @pl.when(n > 0)
                def _():
                    act(copy)

            _for_each(N_EXPERTS, one)

        each_pad(lambda copy: copy.start())
        _for_each(n_unused, lambda i: unused_tile_copy(used_ref[0] + i).start())
        each_pad(lambda copy: copy.wait())
        _for_each(n_unused, lambda i: unused_tile_copy(used_ref[0] + i).wait())

        @pl.when(b >= 1)
        def _():
            wait_block(b - 1, 1 - slot)

        wait_block(b, slot)


def _scatter(x1b, sel, plan, n_tiles):
    t, d = x1b.shape
    tb = ROUTE_BLOCK
    row = lambda b, *_: (b, 0)
    blocks = 2 * (tb * d * 2 + tb * LANES * 2) + 2 * LOCAL_ROWS * d * 2 + EXPERT_TILE * d * 2
    temps = tb * tb * 2 + 6 * tb * LANES * 4 + tb * LOCAL_ROWS * 6 + LOCAL_ROWS * d * 6
    return pl.pallas_call(
        _scatter_kernel,
        grid_spec=pltpu.PrefetchScalarGridSpec(
            num_scalar_prefetch=7,
            grid=(t // tb,),
            in_specs=[pl.BlockSpec((tb, d), row), pl.BlockSpec((tb, LANES), row)],
            out_specs=pl.BlockSpec(memory_space=pl.ANY),
            scratch_shapes=[pltpu.VMEM((2, LOCAL_ROWS, d), BF16),
                            pltpu.VMEM((EXPERT_TILE, d), BF16), pltpu.SemaphoreType.DMA((3,))]),
        out_shape=jax.ShapeDtypeStruct((n_tiles * EXPERT_TILE, d), BF16),
        compiler_params=_params(("arbitrary",), blocks + temps),
        name="scatter",
    )(plan["seg_len"], plan["seg_src"], plan["seg_dst"], plan["block_chunks"], plan["pad_start"],
      plan["pad_count"], plan["tiles_used"], x1b, sel)


def _experts_kernel(ntile_ref, used_ref, g_ref, wg_ref, wu_ref, wd_ref, y_ref,
                    x_buf, y_buf, wg_buf, wu_buf, wd_buf, wgb_ref, wub_ref, wdb_ref,
                    x_sems, y_sems, w_sems):
    n_experts = wg_ref.shape[0]
    tm = EXPERT_TILE
    used = used_ref[0]

    def weight_copies(e, slot):
        return [pltpu.make_async_copy(src.at[e], dst.at[slot], w_sems.at[slot])
                for src, dst in ((wg_ref, wg_buf), (wu_ref, wu_buf), (wd_ref, wd_buf))]

    def x_copy(t, slot):
        rows = pl.ds(pl.multiple_of(t * tm, tm), MACRO_TILES * tm)
        return pltpu.make_async_copy(g_ref.at[rows], x_buf.at[slot], x_sems.at[slot])

    def y_copy(t, k, slot):
        n = k * tm
        dst = pl.ds(pl.multiple_of(t * tm, tm), n)
        return pltpu.make_async_copy(y_buf.at[slot, pl.ds(0, n)], y_ref.at[dst], y_sems.at[slot])

    def wait_y(t, k, slot):
        @pl.when(k > 0)
        def _():
            y_copy(t, k, slot).wait()

    def mlp(slot, rows):
        x = x_buf[slot, :rows, :]
        hg = jnp.dot(x, wgb_ref[...], preferred_element_type=F32)
        hu = jnp.dot(x, wub_ref[...], preferred_element_type=F32)
        h = hg * _sigmoid(hg) * hu
        y_buf[slot, :rows, :] = jnp.dot(h.astype(BF16), wdb_ref[...],
                                        preferred_element_type=F32).astype(BF16)

    for c in weight_copies(0, 0):
        c.start()
    x_copy(0, 0).start()

    def run_expert(e, carry):
        wslot = e % 2
        for c in weight_copies(e, wslot):
            c.wait()

        @pl.when(e + 1 < n_experts)
        def _():
            for c in weight_copies(e + 1, 1 - wslot):
                c.start()

        @pl.when(ntile_ref[e] > 0)
        def _():
            wgb_ref[...] = wg_buf[wslot].astype(BF16)
            wub_ref[...] = wu_buf[wslot].astype(BF16)
            wdb_ref[...] = wd_buf[wslot].astype(BF16)

        def run_macro(m, carry):
            t, step, k1, t1, k2, t2 = carry
            k = jnp.minimum(MACRO_TILES, ntile_ref[e] - m * MACRO_TILES)
            slot = step % 2
            x_copy(t, slot).wait()

            @pl.when(t + k < used)
            def _():
                x_copy(t + k, 1 - slot).start()

            wait_y(t2, k2, slot)
            for tiles in range(1, MACRO_TILES + 1):
                @pl.when(k == tiles)
                def _():
                    mlp(slot, tiles * tm)

            y_copy(t, k, slot).start()
            return t + k, step + 1, k, t, k1, t1

        n_macro = (ntile_ref[e] + (MACRO_TILES - 1)) // MACRO_TILES
        return lax.fori_loop(0, n_macro, run_macro, carry)

    zero = jnp.int32(0)
    _, step, k1, t1, k2, t2 = lax.fori_loop(0, n_experts, run_expert, (zero,) * 6)
    wait_y(t2, k2, step % 2)
    wait_y(t1, k1, (step + 1) % 2)
    y_buf[0, :tm, :] = jnp.zeros((tm, y_buf.shape[2]), BF16)
    n_unused = y_ref.shape[0] // tm - used
    _for_each(n_unused, lambda i: y_copy(used + i, 1, 0).start())
    _for_each(n_unused, lambda i: y_copy(used + i, 1, 0).wait())


def _experts(sorted_x, plan, w_eg, w_eu, w_ed):
    rows, d = sorted_x.shape
    _, _, de = w_eg.shape
    tm = EXPERT_TILE
    any_space = pl.BlockSpec(memory_space=pl.ANY)
    big = MACRO_TILES * tm
    scratch = 2 * 2 * big * d * 2 + 2 * 3 * d * de * 4 + 3 * d * de * 2
    temps = 3 * big * de * 4 + big * d * 4 + d * de * 4
    return pl.pallas_call(
        _experts_kernel,
        grid_spec=pltpu.PrefetchScalarGridSpec(
            num_scalar_prefetch=2,
            grid=(1,),
            in_specs=[any_space] * 4,
            out_specs=any_space,
            scratch_shapes=[pltpu.VMEM((2, big, d), BF16), pltpu.VMEM((2, big, d), BF16),
                            pltpu.VMEM((2, d, de), F32), pltpu.VMEM((2, d, de), F32),
                            pltpu.VMEM((2, de, d), F32),
                            pltpu.VMEM((d, de), BF16), pltpu.VMEM((d, de), BF16),
                            pltpu.VMEM((de, d), BF16),
                            pltpu.SemaphoreType.DMA((2,)), pltpu.SemaphoreType.DMA((2,)),
                            pltpu.SemaphoreType.DMA((2,))]),
        out_shape=jax.ShapeDtypeStruct((rows, d), BF16),
        compiler_params=_params(("arbitrary",), scratch + temps),
        name="experts",
    )(plan["expert_tiles"], plan["tiles_used"], sorted_x, w_eg, w_eu, w_ed)


def _combine_kernel(len_ref, src_ref, dst_ref, nchunk_ref, x1_ref, comb_ref, sel_ref, p_ref,
                    wpg_ref, wpp_ref,
                    lg_ref, lb_ref, y_ref, o_ref, loc_ref, wpgb_ref, wppb_ref, sems, *, alpha):
    b = pl.program_id(0)
    nb = pl.num_programs(0)
    slot = b % 2
    tb, d = x1_ref.shape
    _cast_once(wpg_ref, wpgb_ref)
    _cast_once(wpp_ref, wppb_ref)

    def start_block(blk, slot):
        def start_segment(e):
            n, local, sorted_at = _segment_rows(len_ref, src_ref, dst_ref, blk, e)

            @pl.when(n > 0)
            def _():
                pltpu.make_async_copy(y_ref.at[pl.ds(sorted_at, n)],
                                      loc_ref.at[slot, pl.ds(local, n)], sems.at[slot]).start()

        _for_each(N_EXPERTS, start_segment)

    @pl.when(b == 0)
    def _():
        loc_ref[...] = jnp.zeros_like(loc_ref)
        start_block(0, 0)

    @pl.when(b + 1 < nb)
    def _():
        start_block(b + 1, 1 - slot)

    n_rows = nchunk_ref[b] * CHUNK
    pltpu.make_async_copy(y_ref.at[pl.ds(0, n_rows)], loc_ref.at[slot, pl.ds(0, n_rows)],
                          sems.at[slot]).wait()

    sel = sel_ref[...]
    pos, pos_lo, pos_hi = _local_positions(sel)
    chosen = sel > 0
    comb = comb_ref[...]
    w_lo = jnp.sum(jnp.where(chosen & (pos == pos_lo), comb, 0.0), axis=1, keepdims=True)
    w_hi = jnp.sum(jnp.where(chosen & (pos == pos_hi), comb, 0.0), axis=1, keepdims=True)
    lo = pos_lo.astype(jnp.int32)
    hi = pos_hi.astype(jnp.int32)

    def finish_block(height):
        sorted_row = lax.broadcasted_iota(jnp.int32, (tb // ROW_SUBS, height), 1)

        def branches(r):
            weights = (jnp.where(sorted_row == lo[r], w_lo[r], 0.0)
                       + jnp.where(sorted_row == hi[r], w_hi[r], 0.0)).astype(BF16)
            moe = jnp.dot(weights, loc_ref[slot, :height, :], preferred_element_type=F32)
            gate = jnp.dot(x1_ref[r, :].astype(BF16), wpgb_ref[...], preferred_element_type=F32)
            emb = jnp.dot(p_ref[r, :].astype(BF16), wppb_ref[...], preferred_element_type=F32)
            return moe, gate, emb

        def finish(r, parts):
            moe, gate, emb = parts
            h = alpha * x1_ref[r, :] + moe + _sigmoid(gate) * emb
            o_ref[r, :] = _layer_norm(h, lg_ref[...], lb_ref[...])

        _skewed(_row_subs(tb), branches, finish)

    for below, height in zip((0,) + SORT_HEIGHTS[:-1], SORT_HEIGHTS):
        @pl.when((n_rows > below) & (n_rows <= height))
        def _():
            finish_block(height)


def _combine(y, plan, x1, comb, sel, p2, w_pg, w_pp, ln_g, ln_b, alpha):
    t, d = x1.shape
    pd = p2.shape[1]
    tb = ROUTE_BLOCK
    row = lambda b, *_: (b, 0)
    blocks = (2 * (2 * tb * d * 4 + tb * LANES * (4 + 2) + tb * pd * 4)
              + (d * d + pd * d) * (4 + 2) + 2 * d * 4 + 2 * LOCAL_ROWS * d * 2)
    temps = tb * tb * 2 + 8 * tb * LANES * 4 + tb * LOCAL_ROWS * 10 + 5 * tb * d * 4
    return pl.pallas_call(
        functools.partial(_combine_kernel, alpha=alpha),
        grid_spec=pltpu.PrefetchScalarGridSpec(
            num_scalar_prefetch=4,
            grid=(t // tb,),
            in_specs=[pl.BlockSpec((tb, d), row), pl.BlockSpec((tb, LANES), row),
                      pl.BlockSpec((tb, LANES), row), pl.BlockSpec((tb, pd), row),
                      _resident((d, d)), _resident((pd, d)), _resident((1, d)), _resident((1, d)),
                      pl.BlockSpec(memory_space=pl.ANY)],
            out_specs=pl.BlockSpec((tb, d), row),
            scratch_shapes=[pltpu.VMEM((2, LOCAL_ROWS, d), BF16), pltpu.VMEM((d, d), BF16),
                            pltpu.VMEM((pd, d), BF16), pltpu.SemaphoreType.DMA((2,))]),
        out_shape=jax.ShapeDtypeStruct((t, d), F32),
        compiler_params=_params(("arbitrary",), blocks + temps),
        name="combine",
    )(plan["seg_len"], plan["seg_src"], plan["seg_dst"], plan["block_chunks"], x1, comb, sel, p2,
      w_pg, w_pp, ln_g, ln_b, y)


def kernel(x, p, w_in, w_pool, pool_scale, w_pu, w_au, w_o, ln1_g, ln1_b, w_rg, b_rg, w_re, b_re,
           w_eg, w_eu, w_ed, w_pg, w_pp, ln2_g, ln2_b):
    bsz, seq, d = x.shape
    depth = w_in.shape[0]
    t = bsz * seq
    de = w_eg.shape[-1]
    alpha = (2.0 * depth) ** 0.25
    assert w_rg.shape[2] == N_GROUPS and w_re.shape[1:] == (N_GROUPS, d, EXPERTS_PER_GROUP)
    assert w_in.shape[2] == 4 * d and w_pool.shape[1] == len(POOL_WINDOWS)
    assert t % ROUTE_BLOCK == 0 and N_EXPERTS + N_GROUPS <= LANES
    n_blocks = t // ROUTE_BLOCK
    n_tiles = -(-(2 * t + n_blocks * N_EXPERTS * (CHUNK - 1) + N_EXPERTS * (EXPERT_TILE - CHUNK))
                // EXPERT_TILE) + MACRO_TILES - 1

    x2 = x.reshape(t, d)
    for i in range(depth):
        pool_out, q, k, v, gates = _in_hbm(
            *_proj(x2, w_in[i], w_pool[i], pool_scale[i][None, :], seq, tm=512))
        attn_out, = _in_hbm(_attention(q, k, v, seq, tb=128, blocks_per_step=4))

        w_r = jnp.concatenate(
            [w_re[i].transpose(1, 0, 2).reshape(d, N_EXPERTS), w_rg[i]], axis=1)
        w_r = jnp.pad(w_r, ((0, 0), (0, LANES - w_r.shape[1])))
        b_r = jnp.pad(jnp.concatenate([b_re[i].reshape(-1), b_rg[i]]),
                      (0, LANES - N_GROUPS - N_EXPERTS))[None, :]

        x1, x1b, comb, sel, counts = _merge(
            x2, pool_out, attn_out, gates, *_in_hbm(w_pu[i], w_au[i], w_o[i]), ln1_g[i][None, :],
            ln1_b[i][None, :], w_r, b_r, alpha, tm=ROUTE_BLOCK)
        x1, x1b, comb, sel = _in_hbm(x1, x1b, comb, sel)

        plan = _routing_plan(counts)
        sorted_x, = _in_hbm(_scatter(x1b, sel, plan, n_tiles))
        y, = _in_hbm(_experts(sorted_x, plan, w_eg[i].reshape(N_EXPERTS, d, de),
                              w_eu[i].reshape(N_EXPERTS, d, de), w_ed[i].reshape(N_EXPERTS, de, d)))
        x2 = _combine(y, plan, x1, comb, sel, p[i].reshape(t, -1), *_in_hbm(w_pg[i], w_pp[i]),
                      ln2_g[i][None, :], ln2_b[i][None, :], alpha)
    return x2.reshape(bsz, seq, d)
```

```python
import functools
import math

import jax
import jax.numpy as jnp
from jax import lax
from jax.experimental import pallas as pl
from jax.experimental.pallas import tpu as pltpu

F32 = jnp.float32
BF16 = jnp.bfloat16

LANES = 128
POOL_WINDOWS = (2, 4, 8, 16)
POOL_HALO = 16
HEAD_DIM = 64
N_GROUPS = 4
EXPERTS_PER_GROUP = 8
N_EXPERTS = N_GROUPS * EXPERTS_PER_GROUP
LN_EPS = 1e-5
GROUP_LANE0 = N_EXPERTS
NEG_BIG = -1e30
ROUTE_BLOCK = 512
CHUNK = 16
EXPERT_TILE = 256
MACRO_TILES = 6
SORT_ROWS = 256
ROW_SUBS = 2
LOCAL_ROWS = -(-(2 * ROUTE_BLOCK + N_EXPERTS * (CHUNK - 1)) // SORT_ROWS) * SORT_ROWS
SORT_HEIGHTS = (LOCAL_ROWS - SORT_ROWS, LOCAL_ROWS - SORT_ROWS // 2, LOCAL_ROWS)
ATTN_DEAD_LOG2 = -160.0
VMEM_CAP_BYTES = 56 * 1024 * 1024


def _params(sem, vmem_bytes):
    return pltpu.CompilerParams(
        dimension_semantics=sem, vmem_limit_bytes=min(int(vmem_bytes), VMEM_CAP_BYTES))


def _layer_norm(h, g, b):
    mu = jnp.mean(h, axis=-1, keepdims=True)
    c = h - mu
    var = jnp.mean(c * c, axis=-1, keepdims=True)
    return c * lax.rsqrt(var + LN_EPS) * g + b


def _sigmoid(z):
    return 1.0 / (1.0 + jnp.exp(-z))


def _row_subs(rows):
    return [slice(k * (rows // ROW_SUBS), (k + 1) * (rows // ROW_SUBS)) for k in range(ROW_SUBS)]


def _skewed(subs, first, second):
    out, pending = [], None
    for r in subs:
        mid = first(r)
        if pending is not None:
            out.append(second(*pending))
        pending = (r, mid)
    out.append(second(*pending))
    return out


def _in_hbm(*arrays):
    if not all(isinstance(a, jax.core.Tracer) for a in arrays):
        return list(arrays)
    return [pltpu.with_memory_space_constraint(a, pltpu.HBM) for a in arrays]


def _resident(shape):
    return pl.BlockSpec(shape, lambda *_: (0,) * len(shape), pipeline_mode=pl.Buffered(1))


def _cast_once(w_ref, wb_ref):
    @pl.when(pl.program_id(0) == 0)
    def _():
        wb_ref[...] = w_ref[...].astype(BF16)


def _proj_kernel(x_ref, w_ref, wp_ref, sc_ref, po_ref, q_ref, k_ref, v_ref, g_ref, wb_ref, halo_ref,
                 *, width, tiles_per_seq):
    q_scale = math.log2(math.e) / math.sqrt(HEAD_DIM)
    tile_in_seq = pl.program_id(0) % tiles_per_seq
    _cast_once(w_ref, wb_ref)
    xb = x_ref[...].astype(BF16)

    def mm(lo):
        return jnp.dot(xb, wb_ref[:, lo:lo + width], preferred_element_type=F32)

    @pl.when(tile_in_seq == 0)
    def _():
        halo_ref[...] = jnp.zeros_like(halo_ref)

    u = mm(0)
    tm = u.shape[0]
    gd = wp_ref.shape[1]
    pos = tile_in_seq * tm + lax.broadcasted_iota(jnp.int32, (tm, gd), 0)

    def pool_group(g, w):
        cols = slice(g * gd, (g + 1) * gd)
        ug = u[:, cols]
        s = jnp.concatenate([halo_ref[:, cols], ug], axis=0)
        sh = 1
        while sh < w:
            s = s + pltpu.roll(s, sh, axis=0)
            sh *= 2
        cnt = jnp.minimum(pos + 1, w).astype(F32)
        pooled = s[POOL_HALO:, :] / cnt - ug
        mixed = jnp.dot(pooled.astype(BF16), wp_ref[g].astype(BF16), preferred_element_type=F32)
        po_ref[:, cols] = (mixed * sc_ref[:, cols]).astype(BF16)

    def write_q():
        q_ref[...] = (mm(width) * q_scale).astype(BF16)

    def write_k():
        k_ref[...] = mm(2 * width).astype(BF16)

    def write_v():
        v_ref[...] = mm(3 * width).astype(BF16)

    def gate_chunk(c):
        g_ref[:, c * width:(c + 1) * width] = _sigmoid(mm((4 + c) * width)).astype(BF16)

    matmuls = [write_q, write_k, write_v] + [
        functools.partial(gate_chunk, c) for c in range(g_ref.shape[1] // width)]
    for n, matmul in enumerate(matmuls):
        matmul()
        if n < len(POOL_WINDOWS):
            pool_group(n, POOL_WINDOWS[n])
    halo_ref[...] = u[tm - POOL_HALO:, :]


def _proj(x2, w_in, w_pool, pool_scale, seq, tm):
    t, d = x2.shape
    n = w_in.shape[1]
    width = d // 2
    gate_w = n - 4 * width
    groups, gd, _ = w_pool.shape
    assert seq % tm == 0 and groups * gd == width and max(POOL_WINDOWS) <= POOL_HALO + 1
    row = lambda i: (i, 0)
    blocks = (2 * (tm * d * 4 + tm * width * 4 * 2 + tm * gate_w * 2)
              + d * n * (4 + 2) + groups * gd * gd * 4 + width * 4 + POOL_HALO * width * 4)
    temps = tm * d * 2 + 3 * tm * width * 4 + 6 * (tm + POOL_HALO) * gd * 4
    return pl.pallas_call(
        functools.partial(_proj_kernel, width=width, tiles_per_seq=seq // tm),
        grid=(t // tm,),
        in_specs=[pl.BlockSpec((tm, d), row), _resident((d, n)), _resident((groups, gd, gd)),
                  _resident((1, width))],
        out_specs=[pl.BlockSpec((tm, width), row)] * 4 + [pl.BlockSpec((tm, gate_w), row)],
        out_shape=[jax.ShapeDtypeStruct((t, width), BF16)] * 4
        + [jax.ShapeDtypeStruct((t, gate_w), BF16)],
        scratch_shapes=[pltpu.VMEM((d, n), BF16), pltpu.VMEM((POOL_HALO, width), F32)],
        compiler_params=_params(("arbitrary",), blocks + temps),
        name="proj",
    )(x2, w_in, w_pool, pool_scale)


def _attn_kernel(q_ref, k_ref, v_ref, o_ref, rem_ref, acc_ref, *, tb):
    pairs = q_ref.shape[1] // LANES
    blocks_per_step = q_ref.shape[0] // tb
    col_blocks = [slice(p * LANES, (p + 1) * LANES) for p in range(pairs)]
    first_head = lax.broadcasted_iota(jnp.int32, (tb, LANES), 1) < HEAD_DIM
    r = lax.broadcasted_iota(jnp.int32, (2 * tb, 2 * tb), 0)
    c = lax.broadcasted_iota(jnp.int32, (2 * tb, 2 * tb), 1)
    same_head = (r >= tb) == (c >= tb)
    cum = -jnp.concatenate([(same_head & (r >= c)).astype(BF16), same_head.astype(BF16)], axis=1)
    kcol = lax.broadcasted_iota(jnp.int32, (tb, 2 * tb), 1)
    kcol = jnp.where(kcol >= tb, kcol - tb, kcol)
    causal = kcol < lax.broadcasted_iota(jnp.int32, (tb, 2 * tb), 0)

    def stack_heads(blk):
        zero = jnp.zeros_like(blk)
        return jnp.concatenate(
            [jnp.where(first_head, blk, zero), jnp.where(first_head, zero, blk)], axis=0)

    def scores(q0, j, rows, diagonal):
        ks = pl.multiple_of(j * tb, tb)
        q_rows = pl.ds(q0 + rows.start, rows.stop - rows.start)
        zs = [lax.dot_general(q_ref[q_rows, cols], stack_heads(k_ref[pl.ds(ks, tb), cols]),
                              (((1,), (1,)), ((), ())), preferred_element_type=F32)
              for cols in col_blocks]
        sums = []
        for z in zs:
            softplus = jnp.maximum(z, 0.0) + jnp.log2(1.0 + jnp.exp2(-jnp.abs(z)))
            if diagonal:
                softplus = jnp.where(causal, softplus, 0.0)
            sums.append(jnp.dot(softplus.astype(BF16), cum, preferred_element_type=F32))
        return zs, sums

    def weighted_values(j, cols, z, later):
        ks = pl.multiple_of(j * tb, tb)
        a = jnp.exp2(z + later)
        return jnp.dot(a.astype(BF16), stack_heads(v_ref[pl.ds(ks, tb), cols]),
                       preferred_element_type=F32)

    def first_step(qi, q0, n_before):
        full, half = slice(0, tb), slice(0, tb // 2)
        z_d, sums_d = scores(q0, qi, full, diagonal=True)
        if n_before >= 1:
            z_1, sums_1 = scores(q0, qi - 1, full, diagonal=False)
        if n_before >= 2:
            z_2, sums_2 = scores(q0, qi - 2, half, diagonal=False)
        owed = []
        for p, cols in enumerate(col_blocks):
            ks = pl.multiple_of(qi * tb, tb)
            a = jnp.where(causal, jnp.exp2(z_d[p] + sums_d[p][:, :2 * tb]), 0.0)
            out = jnp.dot(a.astype(BF16), stack_heads(v_ref[pl.ds(ks, tb), cols]),
                          preferred_element_type=F32)
            total = sums_d[p][:, 2 * tb:]
            if n_before >= 1:
                out = out + weighted_values(qi - 1, cols, z_1[p], sums_1[p][:, :2 * tb] + total)
                total = total + sums_1[p][:, 2 * tb:]
            acc_ref[p] = out
            rem_ref[p] = total
            if n_before >= 2:
                owed.append(total[tb // 2:, :])
                later = sums_2[p][:, :2 * tb] + total[half, :]
                acc_ref[p, half, :] += weighted_values(qi - 2, cols, z_2[p], later)
                rem_ref[p, half, :] += sums_2[p][:, 2 * tb:]
        if n_before < 2:
            return None
        return jnp.max(functools.reduce(jnp.maximum, owed)) > ATTN_DEAD_LOG2

    def step(q0, j, rows):
        zs, sums = scores(q0, j, rows, diagonal=False)
        for p, cols in enumerate(col_blocks):
            later = sums[p][:, :2 * tb] + rem_ref[p, rows, :]
            acc_ref[p, rows, :] += weighted_values(j, cols, zs[p], later)
            rem_ref[p, rows, :] += sums[p][:, 2 * tb:]

    def live_rows():
        worst = functools.reduce(jnp.maximum, [rem_ref[p] for p in range(pairs)])
        live = jnp.max(worst, axis=1, keepdims=True) > ATTN_DEAD_LOG2
        row = lax.broadcasted_iota(jnp.int32, (tb, 1), 0)
        return jnp.max(jnp.where(live, row + 1, 0))

    row_counts = (tb, tb // 2, tb // 4)

    def one_block(s, carry):
        qi = pl.program_id(1) * blocks_per_step + s
        q0 = pl.multiple_of(s * tb, tb)
        for n_before in (0, 1):
            @pl.when(qi == n_before)
            def _():
                first_step(qi, q0, n_before)

        @pl.when(qi >= 2)
        def _():
            second_half_owed = first_step(qi, q0, 2)

            @pl.when(second_half_owed)
            def _():
                step(q0, qi - 2, slice(tb // 2, tb))

        def cond(state):
            j, n_live = state
            return (j >= 0) & (n_live > 0)

        def body(state):
            j, n_live = state
            for rows, fewer in zip(row_counts, row_counts[1:] + (0,)):
                @pl.when((n_live > fewer) & (n_live <= rows))
                def _():
                    step(q0, j, slice(0, rows))

            return j - 1, live_rows()

        lax.while_loop(cond, body, (qi - 3, live_rows()))
        for p, cols in enumerate(col_blocks):
            o_ref[pl.ds(q0, tb), cols] = acc_ref[p].astype(BF16)
        return carry

    lax.fori_loop(0, blocks_per_step, one_block, 0)


def _attention(q, k, v, seq, tb, blocks_per_step):
    t, width = q.shape
    pairs = width // LANES
    tq = tb * blocks_per_step
    blocks = 2 * (2 * tq * width * 2 + 2 * seq * width * 2) + pairs * tb * (2 * tb + LANES) * 4
    temps = pairs * 10 * tb * 2 * tb * 4 + 2 * tb * 4 * tb * 2
    qspec = pl.BlockSpec((tq, width), lambda b, i: (b * (seq // tq) + i, 0))
    kvspec = pl.BlockSpec((seq, width), lambda b, i: (b, 0))
    return pl.pallas_call(
        functools.partial(_attn_kernel, tb=tb),
        grid=(t // seq, seq // tq),
        in_specs=[qspec, kvspec, kvspec],
        out_specs=qspec,
        out_shape=jax.ShapeDtypeStruct((t, width), BF16),
        scratch_shapes=[pltpu.VMEM((pairs, tb, 2 * tb), F32), pltpu.VMEM((pairs, tb, LANES), F32)],
        input_output_aliases={0: 0},
        compiler_params=_params(("parallel", "parallel"), blocks + temps),
        name="attn",
    )(q, k, v)


def _route(logits):
    lane = lax.broadcasted_iota(jnp.int32, logits.shape, 1)

    def first_max(vals):
        m = jnp.max(vals, axis=1, keepdims=True)
        idx = jnp.min(jnp.where(vals == m, lane, LANES), axis=1, keepdims=True)
        return m, idx

    is_group = (lane >= GROUP_LANE0) & (lane < GROUP_LANE0 + N_GROUPS)
    gm, g_lane = first_max(jnp.where(is_group, logits, NEG_BIG))
    g_prob = 1.0 / jnp.sum(jnp.where(is_group, jnp.exp(logits - gm), 0.0), axis=1, keepdims=True)
    lo = EXPERTS_PER_GROUP * (g_lane - GROUP_LANE0)
    in_group = jnp.where((lane >= lo) & (lane < lo + EXPERTS_PER_GROUP), logits, NEG_BIG)
    m1, i1 = first_max(in_group)
    m2, i2 = first_max(jnp.where(lane == i1, NEG_BIG, in_group))
    e21 = jnp.exp(m2 - m1)
    w1 = g_prob / (1.0 + e21)
    w2 = w1 * e21
    comb = jnp.where(lane == i1, w1, 0.0) + jnp.where(lane == i2, w2, 0.0)
    sel = ((lane == i1) | (lane == i2)).astype(F32)
    return comb, sel


def _merge_kernel(x_ref, po_ref, at_ref, g_ref, wpu_ref, wau_ref, wo_ref, lg_ref, lb_ref,
                  wr_ref, br_ref, x1_ref, x1b_ref, comb_ref, sel_ref, cnt_ref,
                  wpub_ref, waub_ref, wob_ref, wrb_ref, *, alpha):
    tm, d = x_ref.shape
    _cast_once(wpu_ref, wpub_ref)
    _cast_once(wau_ref, waub_ref)
    _cast_once(wo_ref, wob_ref)

    @pl.when(pl.program_id(0) == 0)
    def _():
        w_r = wr_ref[...]
        hi = w_r.astype(BF16)
        wrb_ref[:, :LANES] = hi
        wrb_ref[:, LANES:] = (w_r - hi.astype(F32)).astype(BF16)

    def mix(r):
        a = jnp.dot(po_ref[r, :], wpub_ref[...], preferred_element_type=F32)
        b = jnp.dot(at_ref[r, :], waub_ref[...], preferred_element_type=F32)
        merged = (g_ref[r, :d].astype(F32) * a + g_ref[r, d:].astype(F32) * b).astype(BF16)
        return alpha * x_ref[r, :] + jnp.dot(merged, wob_ref[...], preferred_element_type=F32)

    def norm_and_route(r, h):
        x1 = _layer_norm(h, lg_ref[...], lb_ref[...])
        xh = x1.astype(BF16)
        x1_ref[r, :] = x1
        x1b_ref[r, :] = xh
        xl = (x1 - xh.astype(F32)).astype(BF16)
        by_hi = jnp.dot(xh, wrb_ref[...], preferred_element_type=F32)
        by_lo = jnp.dot(xl, wrb_ref[:, :LANES], preferred_element_type=F32)
        comb, sel = _route(by_hi[:, :LANES] + by_hi[:, LANES:] + by_lo + br_ref[...])
        comb_ref[r, :] = comb
        sel_ref[r, :] = sel.astype(BF16)
        return jnp.sum(sel, axis=0, keepdims=True)

    cnt_ref[0] = sum(_skewed(_row_subs(tm), mix, norm_and_route))


def _merge(x2, pool_out, attn_out, gates, w_pu, w_au, w_o, ln_g, ln_b, w_r, b_r, alpha, tm):
    t, d = x2.shape
    width = pool_out.shape[1]
    row = lambda i: (i, 0)
    blocks = (2 * (tm * d * 4 + 2 * tm * width * 2 + tm * 2 * d * 2
                   + tm * d * 4 + tm * d * 2 + tm * LANES * (4 + 2) + LANES * 4)
              + (2 * width * d + d * d + d * LANES) * (4 + 2) + 2 * d * 4 + LANES * 4)
    temps = 5 * tm * d * 4
    return pl.pallas_call(
        functools.partial(_merge_kernel, alpha=alpha),
        grid=(t // tm,),
        in_specs=[pl.BlockSpec((tm, d), row), pl.BlockSpec((tm, width), row),
                  pl.BlockSpec((tm, width), row), pl.BlockSpec((tm, 2 * d), row),
                  _resident((width, d)), _resident((width, d)), _resident((d, d)),
                  _resident((1, d)), _resident((1, d)), _resident((d, LANES)),
                  _resident((1, LANES))],
        out_specs=[pl.BlockSpec((tm, d), row), pl.BlockSpec((tm, d), row),
                   pl.BlockSpec((tm, LANES), row), pl.BlockSpec((tm, LANES), row),
                   pl.BlockSpec((1, 1, LANES), lambda i: (i, 0, 0))],
        out_shape=[jax.ShapeDtypeStruct((t, d), F32), jax.ShapeDtypeStruct((t, d), BF16),
                   jax.ShapeDtypeStruct((t, LANES), F32), jax.ShapeDtypeStruct((t, LANES), BF16),
                   jax.ShapeDtypeStruct((t // tm, 1, LANES), F32)],
        scratch_shapes=[pltpu.VMEM((width, d), BF16), pltpu.VMEM((width, d), BF16),
                        pltpu.VMEM((d, d), BF16), pltpu.VMEM((d, 2 * LANES), BF16)],
        compiler_params=_params(("arbitrary",), blocks + temps),
        name="merge",
    )(x2, pool_out, attn_out, gates, w_pu, w_au, w_o, ln_g, ln_b, w_r, b_r)


def _routing_plan(counts):
    cnt = counts[:, 0, :N_EXPERTS].astype(jnp.int32)
    nch = (cnt + (CHUNK - 1)) // CHUNK
    local_end = jnp.cumsum(nch, axis=1)
    local_start = local_end - nch
    block_chunks = local_end[:, -1]
    before_block = jnp.cumsum(nch, axis=0) - nch
    expert_chunks = jnp.sum(nch, axis=0)
    chunks_per_tile = EXPERT_TILE // CHUNK
    expert_tiles = (expert_chunks + (chunks_per_tile - 1)) // chunks_per_tile
    tiles_end = jnp.cumsum(expert_tiles)
    region_start = (tiles_end - expert_tiles) * chunks_per_tile
    segment_dst = region_start[None, :] + before_block
    i32 = lambda a: a.astype(jnp.int32)
    return dict(seg_len=i32(nch), seg_src=i32(local_start), seg_dst=i32(segment_dst),
                block_chunks=i32(block_chunks),
                expert_tiles=i32(expert_tiles), tiles_used=i32(tiles_end[-1:]),
                pad_start=i32(region_start + expert_chunks),
                pad_count=i32(expert_tiles * chunks_per_tile - expert_chunks))


def _local_positions(sel):
    tb = sel.shape[0]
    earlier = (lax.broadcasted_iota(jnp.int32, (tb, tb), 1)
               < lax.broadcasted_iota(jnp.int32, (tb, tb), 0)).astype(BF16)
    rank = jnp.dot(earlier, sel, preferred_element_type=F32)
    cnt = jnp.sum(sel.astype(F32), axis=0, keepdims=True)
    nch = jnp.floor((cnt + (CHUNK - 1)) * (1.0 / CHUNK))
    lower = (lax.broadcasted_iota(jnp.int32, (LANES, LANES), 0)
             < lax.broadcasted_iota(jnp.int32, (LANES, LANES), 1)).astype(BF16)
    start = CHUNK * jnp.dot(jnp.broadcast_to(nch, (8, LANES)).astype(BF16), lower,
                            preferred_element_type=F32)[0:1]
    pos = rank + start
    chosen = sel > 0
    pos_lo = jnp.min(jnp.where(chosen, pos, float(LOCAL_ROWS)), axis=1, keepdims=True)
    pos_hi = jnp.max(jnp.where(chosen, pos, -1.0), axis=1, keepdims=True)
    return pos, pos_lo, pos_hi


def _for_each(n, fn):
    lax.fori_loop(0, n, lambda c, carry: (fn(c), carry)[1], 0)


def _segment_rows(len_ref, src_ref, dst_ref, blk, e):
    n = len_ref[blk, e] * CHUNK
    src = pl.multiple_of(src_ref[blk, e] * CHUNK, CHUNK)
    dst = pl.multiple_of(dst_ref[blk, e] * CHUNK, CHUNK)
    return n, src, dst


def _scatter_kernel(len_ref, src_ref, dst_ref, nchunk_ref, pad_start_ref, pad_count_ref, used_ref,
                    x_ref, sel_ref, g_ref, loc_ref, zero_ref, sems):
    b = pl.program_id(0)
    last = pl.num_programs(0) - 1
    slot = b % 2
    tb = x_ref.shape[0]

    def start_block(blk, slot):
        def start_segment(e):
            n, src, dst = _segment_rows(len_ref, src_ref, dst_ref, blk, e)

            @pl.when(n > 0)
            def _():
                pltpu.make_async_copy(loc_ref.at[slot, pl.ds(src, n)], g_ref.at[pl.ds(dst, n)],
                                      sems.at[slot]).start()

        _for_each(N_EXPERTS, start_segment)

    def wait_block(blk, slot):
        n = nchunk_ref[blk] * CHUNK
        pltpu.make_async_copy(loc_ref.at[slot, pl.ds(0, n)], g_ref.at[pl.ds(0, n)],
                              sems.at[slot]).wait()

    def pad_copy(e):
        n = pad_count_ref[e] * CHUNK
        dst = pl.multiple_of(pad_start_ref[e] * CHUNK, CHUNK)
        return n, pltpu.make_async_copy(zero_ref.at[pl.ds(0, n)], g_ref.at[pl.ds(dst, n)],
                                        sems.at[2])

    def unused_tile_copy(i):
        dst = pl.multiple_of(i * EXPERT_TILE, EXPERT_TILE)
        return pltpu.make_async_copy(zero_ref, g_ref.at[pl.ds(dst, EXPERT_TILE)], sems.at[2])

    @pl.when(b >= 2)
    def _():
        wait_block(b - 2, slot)

    _, pos_lo, pos_hi = _local_positions(sel_ref[...])
    lo = pos_lo.astype(jnp.int32)
    hi = pos_hi.astype(jnp.int32)
    x = x_ref[...]

    def sort_rows(height):
        r = lax.broadcasted_iota(jnp.int32, (tb, height), 1)
        perm = ((r == lo) | (r == hi)).astype(BF16)
        rows = lax.dot_general(perm, x, (((0,), (0,)), ((), ())), preferred_element_type=F32)
        loc_ref[slot, :height, :] = rows.astype(BF16)

    n_rows = nchunk_ref[b] * CHUNK
    for below, height in zip((0,) + SORT_HEIGHTS[:-1], SORT_HEIGHTS):
        @pl.when((n_rows > below) & (n_rows <= height))
        def _():
            sort_rows(height)

    start_block(b, slot)

    @pl.when(b == last)
    def _():
        zero_ref[...] = jnp.zeros_like(zero_ref)
        n_unused = g_ref.shape[0] // EXPERT_TILE - used_ref[0]

        def each_pad(act):
            def one(e):
                n, copy = pad_copy(e)

                @pl.when(n > 0)
                def _():
                    act(copy)

            _for_each(N_EXPERTS, one)

        each_pad(lambda copy: copy.start())
        _for_each(n_unused, lambda i: unused_tile_copy(used_ref[0] + i).start())
        each_pad(lambda copy: copy.wait())
        _for_each(n_unused, lambda i: unused_tile_copy(used_ref[0] + i).wait())

        @pl.when(b >= 1)
        def _():
            wait_block(b - 1, 1 - slot)

        wait_block(b, slot)


def _scatter(x1b, sel, plan, n_tiles):
    t, d = x1b.shape
    tb = ROUTE_BLOCK
    row = lambda b, *_: (b, 0)
    blocks = 2 * (tb * d * 2 + tb * LANES * 2) + 2 * LOCAL_ROWS * d * 2 + EXPERT_TILE * d * 2
    temps = tb * tb * 2 + 6 * tb * LANES * 4 + tb * LOCAL_ROWS * 6 + LOCAL_ROWS * d * 6
    return pl.pallas_call(
        _scatter_kernel,
        grid_spec=pltpu.PrefetchScalarGridSpec(
            num_scalar_prefetch=7,
            grid=(t // tb,),
            in_specs=[pl.BlockSpec((tb, d), row), pl.BlockSpec((tb, LANES), row)],
            out_specs=pl.BlockSpec(memory_space=pl.ANY),
            scratch_shapes=[pltpu.VMEM((2, LOCAL_ROWS, d), BF16),
                            pltpu.VMEM((EXPERT_TILE, d), BF16), pltpu.SemaphoreType.DMA((3,))]),
        out_shape=jax.ShapeDtypeStruct((n_tiles * EXPERT_TILE, d), BF16),
        compiler_params=_params(("arbitrary",), blocks + temps),
        name="scatter",
    )(plan["seg_len"], plan["seg_src"], plan["seg_dst"], plan["block_chunks"], plan["pad_start"],
      plan["pad_count"], plan["tiles_used"], x1b, sel)


def _experts_kernel(ntile_ref, used_ref, g_ref, wg_ref, wu_ref, wd_ref, y_ref,
                    x_buf, y_buf, wg_buf, wu_buf, wd_buf, wgb_ref, wub_ref, wdb_ref,
                    x_sems, y_sems, w_sems):
    n_experts = wg_ref.shape[0]
    tm = EXPERT_TILE
    used = used_ref[0]

    def weight_copies(e, slot):
        return [pltpu.make_async_copy(src.at[e], dst.at[slot], w_sems.at[slot])
                for src, dst in ((wg_ref, wg_buf), (wu_ref, wu_buf), (wd_ref, wd_buf))]

    def x_copy(t, slot):
        rows = pl.ds(pl.multiple_of(t * tm, tm), MACRO_TILES * tm)
        return pltpu.make_async_copy(g_ref.at[rows], x_buf.at[slot], x_sems.at[slot])

    def y_copy(t, k, slot):
        n = k * tm
        dst = pl.ds(pl.multiple_of(t * tm, tm), n)
        return pltpu.make_async_copy(y_buf.at[slot, pl.ds(0, n)], y_ref.at[dst], y_sems.at[slot])

    def wait_y(t, k, slot):
        @pl.when(k > 0)
        def _():
            y_copy(t, k, slot).wait()

    def mlp(slot, rows):
        x = x_buf[slot, :rows, :]
        hg = jnp.dot(x, wgb_ref[...], preferred_element_type=F32)
        hu = jnp.dot(x, wub_ref[...], preferred_element_type=F32)
        h = hg * _sigmoid(hg) * hu
        y_buf[slot, :rows, :] = jnp.dot(h.astype(BF16), wdb_ref[...],
                                        preferred_element_type=F32).astype(BF16)

    for c in weight_copies(0, 0):
        c.start()
    x_copy(0, 0).start()

    def run_expert(e, carry):
        wslot = e % 2
        for c in weight_copies(e, wslot):
            c.wait()

        @pl.when(e + 1 < n_experts)
        def _():
            for c in weight_copies(e + 1, 1 - wslot):
                c.start()

        @pl.when(ntile_ref[e] > 0)
        def _():
            wgb_ref[...] = wg_buf[wslot].astype(BF16)
            wub_ref[...] = wu_buf[wslot].astype(BF16)
            wdb_ref[...] = wd_buf[wslot].astype(BF16)

        def run_macro(m, carry):
            t, step, k1, t1, k2, t2 = carry
            k = jnp.minimum(MACRO_TILES, ntile_ref[e] - m * MACRO_TILES)
            slot = step % 2
            x_copy(t, slot).wait()

            @pl.when(t + k < used)
            def _():
                x_copy(t + k, 1 - slot).start()

            wait_y(t2, k2, slot)
            for tiles in range(1, MACRO_TILES + 1):
                @pl.when(k == tiles)
                def _():
                    mlp(slot, tiles * tm)

            y_copy(t, k, slot).start()
            return t + k, step + 1, k, t, k1, t1

        n_macro = (ntile_ref[e] + (MACRO_TILES - 1)) // MACRO_TILES
        return lax.fori_loop(0, n_macro, run_macro, carry)

    zero = jnp.int32(0)
    _, step, k1, t1, k2, t2 = lax.fori_loop(0, n_experts, run_expert, (zero,) * 6)
    wait_y(t2, k2, step % 2)
    wait_y(t1, k1, (step + 1) % 2)
    y_buf[0, :tm, :] = jnp.zeros((tm, y_buf.shape[2]), BF16)
    n_unused = y_ref.shape[0] // tm - used
    _for_each(n_unused, lambda i: y_copy(used + i, 1, 0).start())
    _for_each(n_unused, lambda i: y_copy(used + i, 1, 0).wait())


def _experts(sorted_x, plan, w_eg, w_eu, w_ed):
    rows, d = sorted_x.shape
    _, _, de = w_eg.shape
    tm = EXPERT_TILE
    any_space = pl.BlockSpec(memory_space=pl.ANY)
    big = MACRO_TILES * tm
    scratch = 2 * 2 * big * d * 2 + 2 * 3 * d * de * 4 + 3 * d * de * 2
    temps = 3 * big * de * 4 + big * d * 4 + d * de * 4
    return pl.pallas_call(
        _experts_kernel,
        grid_spec=pltpu.PrefetchScalarGridSpec(
            num_scalar_prefetch=2,
            grid=(1,),
            in_specs=[any_space] * 4,
            out_specs=any_space,
            scratch_shapes=[pltpu.VMEM((2, big, d), BF16), pltpu.VMEM((2, big, d), BF16),
                            pltpu.VMEM((2, d, de), F32), pltpu.VMEM((2, d, de), F32),
                            pltpu.VMEM((2, de, d), F32),
                            pltpu.VMEM((d, de), BF16), pltpu.VMEM((d, de), BF16),
                            pltpu.VMEM((de, d), BF16),
                            pltpu.SemaphoreType.DMA((2,)), pltpu.SemaphoreType.DMA((2,)),
                            pltpu.SemaphoreType.DMA((2,))]),
        out_shape=jax.ShapeDtypeStruct((rows, d), BF16),
        compiler_params=_params(("arbitrary",), scratch + temps),
        name="experts",
    )(plan["expert_tiles"], plan["tiles_used"], sorted_x, w_eg, w_eu, w_ed)


def _combine_kernel(len_ref, src_ref, dst_ref, nchunk_ref, x1_ref, comb_ref, sel_ref, p_ref,
                    wpg_ref, wpp_ref,
                    lg_ref, lb_ref, y_ref, o_ref, loc_ref, wpgb_ref, wppb_ref, sems, *, alpha):
    b = pl.program_id(0)
    nb = pl.num_programs(0)
    slot = b % 2
    tb, d = x1_ref.shape
    _cast_once(wpg_ref, wpgb_ref)
    _cast_once(wpp_ref, wppb_ref)

    def start_block(blk, slot):
        def start_segment(e):
            n, local, sorted_at = _segment_rows(len_ref, src_ref, dst_ref, blk, e)

            @pl.when(n > 0)
            def _():
                pltpu.make_async_copy(y_ref.at[pl.ds(sorted_at, n)],
                                      loc_ref.at[slot, pl.ds(local, n)], sems.at[slot]).start()

        _for_each(N_EXPERTS, start_segment)

    @pl.when(b == 0)
    def _():
        loc_ref[...] = jnp.zeros_like(loc_ref)
        start_block(0, 0)

    @pl.when(b + 1 < nb)
    def _():
        start_block(b + 1, 1 - slot)

    n_rows = nchunk_ref[b] * CHUNK
    pltpu.make_async_copy(y_ref.at[pl.ds(0, n_rows)], loc_ref.at[slot, pl.ds(0, n_rows)],
                          sems.at[slot]).wait()

    sel = sel_ref[...]
    pos, pos_lo, pos_hi = _local_positions(sel)
    chosen = sel > 0
    comb = comb_ref[...]
    w_lo = jnp.sum(jnp.where(chosen & (pos == pos_lo), comb, 0.0), axis=1, keepdims=True)
    w_hi = jnp.sum(jnp.where(chosen & (pos == pos_hi), comb, 0.0), axis=1, keepdims=True)
    lo = pos_lo.astype(jnp.int32)
    hi = pos_hi.astype(jnp.int32)

    def finish_block(height):
        sorted_row = lax.broadcasted_iota(jnp.int32, (tb // ROW_SUBS, height), 1)

        def branches(r):
            weights = (jnp.where(sorted_row == lo[r], w_lo[r], 0.0)
                       + jnp.where(sorted_row == hi[r], w_hi[r], 0.0)).astype(BF16)
            moe = jnp.dot(weights, loc_ref[slot, :height, :], preferred_element_type=F32)
            gate = jnp.dot(x1_ref[r, :].astype(BF16), wpgb_ref[...], preferred_element_type=F32)
            emb = jnp.dot(p_ref[r, :].astype(BF16), wppb_ref[...], preferred_element_type=F32)
            return moe, gate, emb

        def finish(r, parts):
            moe, gate, emb = parts
            h = alpha * x1_ref[r, :] + moe + _sigmoid(gate) * emb
            o_ref[r, :] = _layer_norm(h, lg_ref[...], lb_ref[...])

        _skewed(_row_subs(tb), branches, finish)

    for below, height in zip((0,) + SORT_HEIGHTS[:-1], SORT_HEIGHTS):
        @pl.when((n_rows > below) & (n_rows <= height))
        def _():
            finish_block(height)


def _combine(y, plan, x1, comb, sel, p2, w_pg, w_pp, ln_g, ln_b, alpha):
    t, d = x1.shape
    pd = p2.shape[1]
    tb = ROUTE_BLOCK
    row = lambda b, *_: (b, 0)
    blocks = (2 * (2 * tb * d * 4 + tb * LANES * (4 + 2) + tb * pd * 4)
              + (d * d + pd * d) * (4 + 2) + 2 * d * 4 + 2 * LOCAL_ROWS * d * 2)
    temps = tb * tb * 2 + 8 * tb * LANES * 4 + tb * LOCAL_ROWS * 10 + 5 * tb * d * 4
    return pl.pallas_call(
        functools.partial(_combine_kernel, alpha=alpha),
        grid_spec=pltpu.PrefetchScalarGridSpec(
            num_scalar_prefetch=4,
            grid=(t // tb,),
            in_specs=[pl.BlockSpec((tb, d), row), pl.BlockSpec((tb, LANES), row),
                      pl.BlockSpec((tb, LANES), row), pl.BlockSpec((tb, pd), row),
                      _resident((d, d)), _resident((pd, d)), _resident((1, d)), _resident((1, d)),
                      pl.BlockSpec(memory_space=pl.ANY)],
            out_specs=pl.BlockSpec((tb, d), row),
            scratch_shapes=[pltpu.VMEM((2, LOCAL_ROWS, d), BF16), pltpu.VMEM((d, d), BF16),
                            pltpu.VMEM((pd, d), BF16), pltpu.SemaphoreType.DMA((2,))]),
        out_shape=jax.ShapeDtypeStruct((t, d), F32),
        compiler_params=_params(("arbitrary",), blocks + temps),
        name="combine",
    )(plan["seg_len"], plan["seg_src"], plan["seg_dst"], plan["block_chunks"], x1, comb, sel, p2,
      w_pg, w_pp, ln_g, ln_b, y)


def kernel(x, p, w_in, w_pool, pool_scale, w_pu, w_au, w_o, ln1_g, ln1_b, w_rg, b_rg, w_re, b_re,
           w_eg, w_eu, w_ed, w_pg, w_pp, ln2_g, ln2_b):
    bsz, seq, d = x.shape
    depth = w_in.shape[0]
    t = bsz * seq
    de = w_eg.shape[-1]
    alpha = (2.0 * depth) ** 0.25
    assert w_rg.shape[2] == N_GROUPS and w_re.shape[1:] == (N_GROUPS, d, EXPERTS_PER_GROUP)
    assert w_in.shape[2] == 4 * d and w_pool.shape[1] == len(POOL_WINDOWS)
    assert t % ROUTE_BLOCK == 0 and N_EXPERTS + N_GROUPS <= LANES
    n_blocks = t // ROUTE_BLOCK
    n_tiles = -(-(2 * t + n_blocks * N_EXPERTS * (CHUNK - 1) + N_EXPERTS * (EXPERT_TILE - CHUNK))
                // EXPERT_TILE) + MACRO_TILES - 1

    x2 = x.reshape(t, d)
    for i in range(depth):
        pool_out, q, k, v, gates = _in_hbm(
            *_proj(x2, w_in[i], w_pool[i], pool_scale[i][None, :], seq, tm=512))
        attn_out, = _in_hbm(_attention(q, k, v, seq, tb=128, blocks_per_step=4))

        w_r = jnp.concatenate(
            [w_re[i].transpose(1, 0, 2).reshape(d, N_EXPERTS), w_rg[i]], axis=1)
        w_r = jnp.pad(w_r, ((0, 0), (0, LANES - w_r.shape[1])))
        b_r = jnp.pad(jnp.concatenate([b_re[i].reshape(-1), b_rg[i]]),
                      (0, LANES - N_GROUPS - N_EXPERTS))[None, :]

        x1, x1b, comb, sel, counts = _merge(
            x2, pool_out, attn_out, gates, *_in_hbm(w_pu[i], w_au[i], w_o[i]), ln1_g[i][None, :],
            ln1_b[i][None, :], w_r, b_r, alpha, tm=ROUTE_BLOCK)
        x1, x1b, comb, sel = _in_hbm(x1, x1b, comb, sel)

        plan = _routing_plan(counts)
        sorted_x, = _in_hbm(_scatter(x1b, sel, plan, n_tiles))
        y, = _in_hbm(_experts(sorted_x, plan, w_eg[i].reshape(N_EXPERTS, d, de),
                              w_eu[i].reshape(N_EXPERTS, d, de), w_ed[i].reshape(N_EXPERTS, de, d)))
        x2 = _combine(y, plan, x1, comb, sel, p[i].reshape(t, -1), *_in_hbm(w_pg[i], w_pp[i]),
                      ln2_g[i][None, :], ln2_b[i][None, :], alpha)
    return x2.reshape(bsz, seq, d)
```

```python
import functools
import math

import jax
import jax.numpy as jnp
from jax import lax
from jax.experimental import pallas as pl
from jax.experimental.pallas import tpu as pltpu

F32 = jnp.float32
BF16 = jnp.bfloat16

LANES = 128
POOL_WINDOWS = (2, 4, 8, 16)
POOL_HALO = 16
HEAD_DIM = 64
N_GROUPS = 4
EXPERTS_PER_GROUP = 8
N_EXPERTS = N_GROUPS * EXPERTS_PER_GROUP
LN_EPS = 1e-5
GROUP_LANE0 = N_EXPERTS
NEG_BIG = -1e30
ROUTE_BLOCK = 512
CHUNK = 16
EXPERT_TILE = 256
MACRO_TILES = 6
SORT_ROWS = 256
ROW_SUBS = 2
ROUTE_POS_LO, ROUTE_POS_HI, ROUTE_W_LO, ROUTE_W_HI = range(4)
LOCAL_ROWS = -(-(2 * ROUTE_BLOCK + N_EXPERTS * (CHUNK - 1)) // SORT_ROWS) * SORT_ROWS
SORT_HEIGHTS = (LOCAL_ROWS - SORT_ROWS, LOCAL_ROWS - SORT_ROWS // 2, LOCAL_ROWS)
ATTN_DEAD_LOG2 = -160.0
VMEM_CAP_BYTES = 56 * 1024 * 1024


def _params(sem, vmem_bytes):
    return pltpu.CompilerParams(
        dimension_semantics=sem, vmem_limit_bytes=min(int(vmem_bytes), VMEM_CAP_BYTES))


def _layer_norm(h, g, b):
    mu = jnp.mean(h, axis=-1, keepdims=True)
    c = h - mu
    var = jnp.mean(c * c, axis=-1, keepdims=True)
    return c * lax.rsqrt(var + LN_EPS) * g + b


def _sigmoid(z):
    return 1.0 / (1.0 + jnp.exp(-z))


def _row_subs(rows):
    return [slice(k * (rows // ROW_SUBS), (k + 1) * (rows // ROW_SUBS)) for k in range(ROW_SUBS)]


def _skewed(subs, first, second):
    out, pending = [], None
    for r in subs:
        mid = first(r)
        if pending is not None:
            out.append(second(*pending))
        pending = (r, mid)
    out.append(second(*pending))
    return out


def _in_hbm(*arrays):
    if not all(isinstance(a, jax.core.Tracer) for a in arrays):
        return list(arrays)
    return [pltpu.with_memory_space_constraint(a, pltpu.HBM) for a in arrays]


def _resident(shape):
    return pl.BlockSpec(shape, lambda *_: (0,) * len(shape), pipeline_mode=pl.Buffered(1))


def _cast_once(w_ref, wb_ref):
    @pl.when(pl.program_id(0) == 0)
    def _():
        wb_ref[...] = w_ref[...].astype(BF16)


def _proj_kernel(x_ref, w_ref, wp_ref, sc_ref, po_ref, q_ref, k_ref, v_ref, g_ref, wb_ref, halo_ref,
                 *, width, tiles_per_seq):
    q_scale = math.log2(math.e) / math.sqrt(HEAD_DIM)
    tile_in_seq = pl.program_id(0) % tiles_per_seq
    _cast_once(w_ref, wb_ref)
    xb = x_ref[...].astype(BF16)

    def mm(lo):
        return jnp.dot(xb, wb_ref[:, lo:lo + width], preferred_element_type=F32)

    @pl.when(tile_in_seq == 0)
    def _():
        halo_ref[...] = jnp.zeros_like(halo_ref)

    u = mm(0)
    tm = u.shape[0]
    gd = wp_ref.shape[1]
    pos = tile_in_seq * tm + lax.broadcasted_iota(jnp.int32, (tm, gd), 0)

    def pool_group(g, w):
        cols = slice(g * gd, (g + 1) * gd)
        ug = u[:, cols]
        s = jnp.concatenate([halo_ref[:, cols], ug], axis=0)
        sh = 1
        while sh < w:
            s = s + pltpu.roll(s, sh, axis=0)
            sh *= 2
        cnt = jnp.minimum(pos + 1, w).astype(F32)
        pooled = s[POOL_HALO:, :] / cnt - ug
        mixed = jnp.dot(pooled.astype(BF16), wp_ref[g].astype(BF16), preferred_element_type=F32)
        po_ref[:, cols] = (mixed * sc_ref[:, cols]).astype(BF16)

    def write_q():
        q_ref[...] = (mm(width) * q_scale).astype(BF16)

    def write_k():
        k_ref[...] = mm(2 * width).astype(BF16)

    def write_v():
        v_ref[...] = mm(3 * width).astype(BF16)

    def gate_chunk(c):
        g_ref[:, c * width:(c + 1) * width] = _sigmoid(mm((4 + c) * width)).astype(BF16)

    matmuls = [write_q, write_k, write_v] + [
        functools.partial(gate_chunk, c) for c in range(g_ref.shape[1] // width)]
    for n, matmul in enumerate(matmuls):
        matmul()
        if n < len(POOL_WINDOWS):
            pool_group(n, POOL_WINDOWS[n])
    halo_ref[...] = u[tm - POOL_HALO:, :]


def _proj(x2, w_in, w_pool, pool_scale, seq, tm):
    t, d = x2.shape
    n = w_in.shape[1]
    width = d // 2
    gate_w = n - 4 * width
    groups, gd, _ = w_pool.shape
    assert seq % tm == 0 and groups * gd == width and max(POOL_WINDOWS) <= POOL_HALO + 1
    row = lambda i: (i, 0)
    blocks = (2 * (tm * d * 4 + tm * width * 4 * 2 + tm * gate_w * 2)
              + d * n * (4 + 2) + groups * gd * gd * 4 + width * 4 + POOL_HALO * width * 4)
    temps = tm * d * 2 + 3 * tm * width * 4 + 6 * (tm + POOL_HALO) * gd * 4
    return pl.pallas_call(
        functools.partial(_proj_kernel, width=width, tiles_per_seq=seq // tm),
        grid=(t // tm,),
        in_specs=[pl.BlockSpec((tm, d), row), _resident((d, n)), _resident((groups, gd, gd)),
                  _resident((1, width))],
        out_specs=[pl.BlockSpec((tm, width), row)] * 4 + [pl.BlockSpec((tm, gate_w), row)],
        out_shape=[jax.ShapeDtypeStruct((t, width), BF16)] * 4
        + [jax.ShapeDtypeStruct((t, gate_w), BF16)],
        scratch_shapes=[pltpu.VMEM((d, n), BF16), pltpu.VMEM((POOL_HALO, width), F32)],
        compiler_params=_params(("arbitrary",), blocks + temps),
        name="proj",
    )(x2, w_in, w_pool, pool_scale)


def _attn_kernel(q_ref, k_ref, v_ref, o_ref, rem_ref, acc_ref, *, tb):
    pairs = q_ref.shape[1] // LANES
    blocks_per_step = q_ref.shape[0] // tb
    col_blocks = [slice(p * LANES, (p + 1) * LANES) for p in range(pairs)]
    first_head = lax.broadcasted_iota(jnp.int32, (tb, LANES), 1) < HEAD_DIM
    r = lax.broadcasted_iota(jnp.int32, (2 * tb, 2 * tb), 0)
    c = lax.broadcasted_iota(jnp.int32, (2 * tb, 2 * tb), 1)
    same_head = (r >= tb) == (c >= tb)
    cum = -jnp.concatenate([(same_head & (r >= c)).astype(BF16), same_head.astype(BF16)], axis=1)
    kcol = lax.broadcasted_iota(jnp.int32, (tb, 2 * tb), 1)
    kcol = jnp.where(kcol >= tb, kcol - tb, kcol)
    causal = kcol < lax.broadcasted_iota(jnp.int32, (tb, 2 * tb), 0)

    def stack_heads(blk):
        zero = jnp.zeros_like(blk)
        return jnp.concatenate(
            [jnp.where(first_head, blk, zero), jnp.where(first_head, zero, blk)], axis=0)

    def scores(q0, j, rows, diagonal):
        ks = pl.multiple_of(j * tb, tb)
        q_rows = pl.ds(q0 + rows.start, rows.stop - rows.start)
        zs = [lax.dot_general(q_ref[q_rows, cols], stack_heads(k_ref[pl.ds(ks, tb), cols]),
                              (((1,), (1,)), ((), ())), preferred_element_type=F32)
              for cols in col_blocks]
        sums = []
        for z in zs:
            softplus = jnp.maximum(z, 0.0) + jnp.log2(1.0 + jnp.exp2(-jnp.abs(z)))
            if diagonal:
                softplus = jnp.where(causal, softplus, 0.0)
            sums.append(jnp.dot(softplus.astype(BF16), cum, preferred_element_type=F32))
        return zs, sums

    def weighted_values(j, cols, z, later):
        ks = pl.multiple_of(j * tb, tb)
        a = jnp.exp2(z + later)
        return jnp.dot(a.astype(BF16), stack_heads(v_ref[pl.ds(ks, tb), cols]),
                       preferred_element_type=F32)

    def first_step(qi, q0, n_before):
        full, half = slice(0, tb), slice(0, tb // 2)
        z_d, sums_d = scores(q0, qi, full, diagonal=True)
        if n_before >= 1:
            z_1, sums_1 = scores(q0, qi - 1, full, diagonal=False)
        if n_before >= 2:
            z_2, sums_2 = scores(q0, qi - 2, half, diagonal=False)
        owed = []
        for p, cols in enumerate(col_blocks):
            ks = pl.multiple_of(qi * tb, tb)
            a = jnp.where(causal, jnp.exp2(z_d[p] + sums_d[p][:, :2 * tb]), 0.0)
            out = jnp.dot(a.astype(BF16), stack_heads(v_ref[pl.ds(ks, tb), cols]),
                          preferred_element_type=F32)
            total = sums_d[p][:, 2 * tb:]
            if n_before >= 1:
                out = out + weighted_values(qi - 1, cols, z_1[p], sums_1[p][:, :2 * tb] + total)
                total = total + sums_1[p][:, 2 * tb:]
            acc_ref[p] = out
            rem_ref[p] = total
            if n_before >= 2:
                owed.append(total[tb // 2:, :])
                later = sums_2[p][:, :2 * tb] + total[half, :]
                acc_ref[p, half, :] += weighted_values(qi - 2, cols, z_2[p], later)
                rem_ref[p, half, :] += sums_2[p][:, 2 * tb:]
        if n_before < 2:
            return None
        return jnp.max(functools.reduce(jnp.maximum, owed)) > ATTN_DEAD_LOG2

    def step(q0, j, rows):
        zs, sums = scores(q0, j, rows, diagonal=False)
        for p, cols in enumerate(col_blocks):
            later = sums[p][:, :2 * tb] + rem_ref[p, rows, :]
            acc_ref[p, rows, :] += weighted_values(j, cols, zs[p], later)
            rem_ref[p, rows, :] += sums[p][:, 2 * tb:]

    def live_rows():
        worst = functools.reduce(jnp.maximum, [rem_ref[p] for p in range(pairs)])
        live = jnp.max(worst, axis=1, keepdims=True) > ATTN_DEAD_LOG2
        row = lax.broadcasted_iota(jnp.int32, (tb, 1), 0)
        return jnp.max(jnp.where(live, row + 1, 0))

    row_counts = (tb, tb // 2, tb // 4)

    def one_block(s, carry):
        qi = pl.program_id(1) * blocks_per_step + s
        q0 = pl.multiple_of(s * tb, tb)
        for n_before in (0, 1):
            @pl.when(qi == n_before)
            def _():
                first_step(qi, q0, n_before)

        @pl.when(qi >= 2)
        def _():
            second_half_owed = first_step(qi, q0, 2)

            @pl.when(second_half_owed)
            def _():
                step(q0, qi - 2, slice(tb // 2, tb))

        def cond(state):
            j, n_live = state
            return (j >= 0) & (n_live > 0)

        def body(state):
            j, n_live = state
            for rows, fewer in zip(row_counts, row_counts[1:] + (0,)):
                @pl.when((n_live > fewer) & (n_live <= rows))
                def _():
                    step(q0, j, slice(0, rows))

            return j - 1, live_rows()

        lax.while_loop(cond, body, (qi - 3, live_rows()))
        for p, cols in enumerate(col_blocks):
            o_ref[pl.ds(q0, tb), cols] = acc_ref[p].astype(BF16)
        return carry

    lax.fori_loop(0, blocks_per_step, one_block, 0)


def _attention(q, k, v, seq, tb, blocks_per_step):
    t, width = q.shape
    pairs = width // LANES
    tq = tb * blocks_per_step
    blocks = 2 * (2 * tq * width * 2 + 2 * seq * width * 2) + pairs * tb * (2 * tb + LANES) * 4
    temps = pairs * 10 * tb * 2 * tb * 4 + 2 * tb * 4 * tb * 2
    qspec = pl.BlockSpec((tq, width), lambda b, i: (b * (seq // tq) + i, 0))
    kvspec = pl.BlockSpec((seq, width), lambda b, i: (b, 0))
    return pl.pallas_call(
        functools.partial(_attn_kernel, tb=tb),
        grid=(t // seq, seq // tq),
        in_specs=[qspec, kvspec, kvspec],
        out_specs=qspec,
        out_shape=jax.ShapeDtypeStruct((t, width), BF16),
        scratch_shapes=[pltpu.VMEM((pairs, tb, 2 * tb), F32), pltpu.VMEM((pairs, tb, LANES), F32)],
        input_output_aliases={0: 0},
        compiler_params=_params(("parallel", "parallel"), blocks + temps),
        name="attn",
    )(q, k, v)


def _route(logits):
    lane = lax.broadcasted_iota(jnp.int32, logits.shape, 1)

    def first_max(vals):
        m = jnp.max(vals, axis=1, keepdims=True)
        idx = jnp.min(jnp.where(vals == m, lane, LANES), axis=1, keepdims=True)
        return m, idx

    is_group = (lane >= GROUP_LANE0) & (lane < GROUP_LANE0 + N_GROUPS)
    gm, g_lane = first_max(jnp.where(is_group, logits, NEG_BIG))
    g_prob = 1.0 / jnp.sum(jnp.where(is_group, jnp.exp(logits - gm), 0.0), axis=1, keepdims=True)
    lo = EXPERTS_PER_GROUP * (g_lane - GROUP_LANE0)
    in_group = jnp.where((lane >= lo) & (lane < lo + EXPERTS_PER_GROUP), logits, NEG_BIG)
    m1, i1 = first_max(in_group)
    m2, i2 = first_max(jnp.where(lane == i1, NEG_BIG, in_group))
    e21 = jnp.exp(m2 - m1)
    w1 = g_prob / (1.0 + e21)
    w2 = w1 * e21
    comb = jnp.where(lane == i1, w1, 0.0) + jnp.where(lane == i2, w2, 0.0)
    sel = ((lane == i1) | (lane == i2)).astype(F32)
    return comb, sel


def _merge_kernel(x_ref, po_ref, at_ref, g_ref, wpu_ref, wau_ref, wo_ref, lg_ref, lb_ref,
                  wr_ref, br_ref, x1_ref, x1b_ref, comb_ref, sel_ref, cnt_ref,
                  wpub_ref, waub_ref, wob_ref, wrb_ref, *, alpha):
    tm, d = x_ref.shape
    _cast_once(wpu_ref, wpub_ref)
    _cast_once(wau_ref, waub_ref)
    _cast_once(wo_ref, wob_ref)

    @pl.when(pl.program_id(0) == 0)
    def _():
        w_r = wr_ref[...]
        hi = w_r.astype(BF16)
        wrb_ref[:, :LANES] = hi
        wrb_ref[:, LANES:] = (w_r - hi.astype(F32)).astype(BF16)

    def mix(r):
        a = jnp.dot(po_ref[r, :], wpub_ref[...], preferred_element_type=F32)
        b = jnp.dot(at_ref[r, :], waub_ref[...], preferred_element_type=F32)
        merged = (g_ref[r, :d].astype(F32) * a + g_ref[r, d:].astype(F32) * b).astype(BF16)
        return alpha * x_ref[r, :] + jnp.dot(merged, wob_ref[...], preferred_element_type=F32)

    def norm_and_route(r, h):
        x1 = _layer_norm(h, lg_ref[...], lb_ref[...])
        xh = x1.astype(BF16)
        x1_ref[r, :] = x1
        x1b_ref[r, :] = xh
        xl = (x1 - xh.astype(F32)).astype(BF16)
        by_hi = jnp.dot(xh, wrb_ref[...], preferred_element_type=F32)
        by_lo = jnp.dot(xl, wrb_ref[:, :LANES], preferred_element_type=F32)
        comb, sel = _route(by_hi[:, :LANES] + by_hi[:, LANES:] + by_lo + br_ref[...])
        comb_ref[r, :] = comb
        sel_ref[r, :] = sel.astype(BF16)
        return jnp.sum(sel, axis=0, keepdims=True)

    cnt_ref[0] = sum(_skewed(_row_subs(tm), mix, norm_and_route))


def _merge(x2, pool_out, attn_out, gates, w_pu, w_au, w_o, ln_g, ln_b, w_r, b_r, alpha, tm):
    t, d = x2.shape
    width = pool_out.shape[1]
    row = lambda i: (i, 0)
    blocks = (2 * (tm * d * 4 + 2 * tm * width * 2 + tm * 2 * d * 2
                   + tm * d * 4 + tm * d * 2 + tm * LANES * (4 + 2) + LANES * 4)
              + (2 * width * d + d * d + d * LANES) * (4 + 2) + 2 * d * 4 + LANES * 4)
    temps = 5 * tm * d * 4
    return pl.pallas_call(
        functools.partial(_merge_kernel, alpha=alpha),
        grid=(t // tm,),
        in_specs=[pl.BlockSpec((tm, d), row), pl.BlockSpec((tm, width), row),
                  pl.BlockSpec((tm, width), row), pl.BlockSpec((tm, 2 * d), row),
                  _resident((width, d)), _resident((width, d)), _resident((d, d)),
                  _resident((1, d)), _resident((1, d)), _resident((d, LANES)),
                  _resident((1, LANES))],
        out_specs=[pl.BlockSpec((tm, d), row), pl.BlockSpec((tm, d), row),
                   pl.BlockSpec((tm, LANES), row), pl.BlockSpec((tm, LANES), row),
                   pl.BlockSpec((1, 1, LANES), lambda i: (i, 0, 0))],
        out_shape=[jax.ShapeDtypeStruct((t, d), F32), jax.ShapeDtypeStruct((t, d), BF16),
                   jax.ShapeDtypeStruct((t, LANES), F32), jax.ShapeDtypeStruct((t, LANES), BF16),
                   jax.ShapeDtypeStruct((t // tm, 1, LANES), F32)],
        scratch_shapes=[pltpu.VMEM((width, d), BF16), pltpu.VMEM((width, d), BF16),
                        pltpu.VMEM((d, d), BF16), pltpu.VMEM((d, 2 * LANES), BF16)],
        compiler_params=_params(("arbitrary",), blocks + temps),
        name="merge",
    )(x2, pool_out, attn_out, gates, w_pu, w_au, w_o, ln_g, ln_b, w_r, b_r)


def _routing_plan(counts):
    cnt = counts[:, 0, :N_EXPERTS].astype(jnp.int32)
    nch = (cnt + (CHUNK - 1)) // CHUNK
    local_end = jnp.cumsum(nch, axis=1)
    local_start = local_end - nch
    block_chunks = local_end[:, -1]
    before_block = jnp.cumsum(nch, axis=0) - nch
    expert_chunks = jnp.sum(nch, axis=0)
    chunks_per_tile = EXPERT_TILE // CHUNK
    expert_tiles = (expert_chunks + (chunks_per_tile - 1)) // chunks_per_tile
    tiles_end = jnp.cumsum(expert_tiles)
    region_start = (tiles_end - expert_tiles) * chunks_per_tile
    segment_dst = region_start[None, :] + before_block
    i32 = lambda a: a.astype(jnp.int32)
    return dict(seg_len=i32(nch), seg_src=i32(local_start), seg_dst=i32(segment_dst),
                block_chunks=i32(block_chunks),
                expert_tiles=i32(expert_tiles), tiles_used=i32(tiles_end[-1:]),
                pad_start=i32(region_start + expert_chunks),
                pad_count=i32(expert_tiles * chunks_per_tile - expert_chunks))


def _local_positions(sel):
    tb = sel.shape[0]
    earlier = (lax.broadcasted_iota(jnp.int32, (tb, tb), 1)
               < lax.broadcasted_iota(jnp.int32, (tb, tb), 0)).astype(BF16)
    rank = jnp.dot(earlier, sel, preferred_element_type=F32)
    cnt = jnp.sum(sel.astype(F32), axis=0, keepdims=True)
    nch = jnp.floor((cnt + (CHUNK - 1)) * (1.0 / CHUNK))
    lower = (lax.broadcasted_iota(jnp.int32, (LANES, LANES), 0)
             < lax.broadcasted_iota(jnp.int32, (LANES, LANES), 1)).astype(BF16)
    start = CHUNK * jnp.dot(jnp.broadcast_to(nch, (8, LANES)).astype(BF16), lower,
                            preferred_element_type=F32)[0:1]
    pos = rank + start
    chosen = sel > 0
    pos_lo = jnp.min(jnp.where(chosen, pos, float(LOCAL_ROWS)), axis=1, keepdims=True)
    pos_hi = jnp.max(jnp.where(chosen, pos, -1.0), axis=1, keepdims=True)
    return pos, pos_lo, pos_hi


def _for_each(n, fn):
    lax.fori_loop(0, n, lambda c, carry: (fn(c), carry)[1], 0)


def _segment_rows(len_ref, src_ref, dst_ref, blk, e):
    n = len_ref[blk, e] * CHUNK
    src = pl.multiple_of(src_ref[blk, e] * CHUNK, CHUNK)
    dst = pl.multiple_of(dst_ref[blk, e] * CHUNK, CHUNK)
    return n, src, dst


def _scatter_kernel(len_ref, src_ref, dst_ref, nchunk_ref, pad_start_ref, pad_count_ref, used_ref,
                    x_ref, sel_ref, comb_ref, g_ref, route_ref, loc_ref, zero_ref, sems):
    b = pl.program_id(0)
    last = pl.num_programs(0) - 1
    slot = b % 2
    tb = x_ref.shape[0]

    def start_block(blk, slot):
        def start_segment(e):
            n, src, dst = _segment_rows(len_ref, src_ref, dst_ref, blk, e)

            @pl.when(n > 0)
            def _():
                pltpu.make_async_copy(loc_ref.at[slot, pl.ds(src, n)], g_ref.at[pl.ds(dst, n)],
                                      sems.at[slot]).start()

        _for_each(N_EXPERTS, start_segment)

    def wait_block(blk, slot):
        n = nchunk_ref[blk] * CHUNK
        pltpu.make_async_copy(loc_ref.at[slot, pl.ds(0, n)], g_ref.at[pl.ds(0, n)],
                              sems.at[slot]).wait()

    def pad_copy(e):
        n = pad_count_ref[e] * CHUNK
        dst = pl.multiple_of(pad_start_ref[e] * CHUNK, CHUNK)
        return n, pltpu.make_async_copy(zero_ref.at[pl.ds(0, n)], g_ref.at[pl.ds(dst, n)],
                                        sems.at[2])

    def unused_tile_copy(i):
        dst = pl.multiple_of(i * EXPERT_TILE, EXPERT_TILE)
        return pltpu.make_async_copy(zero_ref, g_ref.at[pl.ds(dst, EXPERT_TILE)], sems.at[2])

    @pl.when(b >= 2)
    def _():
        wait_block(b - 2, slot)

    sel = sel_ref[...]
    pos, pos_lo, pos_hi = _local_positions(sel)
    lo = pos_lo.astype(jnp.int32)
    hi = pos_hi.astype(jnp.int32)
    x = x_ref[...]
    chosen = sel > 0
    comb = comb_ref[...]
    w_lo = jnp.sum(jnp.where(chosen & (pos == pos_lo), comb, 0.0), axis=1, keepdims=True)
    w_hi = jnp.sum(jnp.where(chosen & (pos == pos_hi), comb, 0.0), axis=1, keepdims=True)
    lane = lax.broadcasted_iota(jnp.int32, comb.shape, 1)
    route_ref[...] = jnp.where(
        lane == ROUTE_POS_LO, pos_lo, jnp.where(
            lane == ROUTE_POS_HI, pos_hi, jnp.where(
                lane == ROUTE_W_LO, w_lo, jnp.where(lane == ROUTE_W_HI, w_hi, 0.0))))

    def sort_rows(height):
        r = lax.broadcasted_iota(jnp.int32, (tb, height), 1)
        perm = ((r == lo) | (r == hi)).astype(BF16)
        rows = lax.dot_general(perm, x, (((0,), (0,)), ((), ())), preferred_element_type=F32)
        loc_ref[slot, :height, :] = rows.astype(BF16)

    n_rows = nchunk_ref[b] * CHUNK
    for below, height in zip((0,) + SORT_HEIGHTS[:-1], SORT_HEIGHTS):
        @pl.when((n_rows > below) & (n_rows <= height))
        def _():
            sort_rows(height)

    start_block(b, slot)

    @pl.when(b == last)
    def _():
        zero_ref[...] = jnp.zeros_like(zero_ref)
        n_unused = g_ref.shape[0] // EXPERT_TILE - used_ref[0]

        def each_pad(act):
            def one(e):
                n, copy = pad_copy(e)

                @pl.when(n > 0)
                def _():
                    act(copy)

            _for_each(N_EXPERTS, one)

        each_pad(lambda copy: copy.start())
        _for_each(n_unused, lambda i: unused_tile_copy(used_ref[0] + i).start())
        each_pad(lambda copy: copy.wait())
        _for_each(n_unused, lambda i: unused_tile_copy(used_ref[0] + i).wait())

        @pl.when(b >= 1)
        def _():
            wait_block(b - 1, 1 - slot)

        wait_block(b, slot)


def _scatter(x1b, sel, comb, plan, n_tiles):
    t, d = x1b.shape
    tb = ROUTE_BLOCK
    row = lambda b, *_: (b, 0)
    blocks = (2 * (tb * d * 2 + tb * LANES * (2 + 4 + 4)) + 2 * LOCAL_ROWS * d * 2
              + EXPERT_TILE * d * 2)
    temps = tb * tb * 2 + 6 * tb * LANES * 4 + tb * LOCAL_ROWS * 6 + LOCAL_ROWS * d * 6
    return pl.pallas_call(
        _scatter_kernel,
        grid_spec=pltpu.PrefetchScalarGridSpec(
            num_scalar_prefetch=7,
            grid=(t // tb,),
            in_specs=[pl.BlockSpec((tb, d), row), pl.BlockSpec((tb, LANES), row),
                      pl.BlockSpec((tb, LANES), row)],
            out_specs=[pl.BlockSpec(memory_space=pl.ANY), pl.BlockSpec((tb, LANES), row)],
            scratch_shapes=[pltpu.VMEM((2, LOCAL_ROWS, d), BF16),
                            pltpu.VMEM((EXPERT_TILE, d), BF16), pltpu.SemaphoreType.DMA((3,))]),
        out_shape=[jax.ShapeDtypeStruct((n_tiles * EXPERT_TILE, d), BF16),
                   jax.ShapeDtypeStruct((t, LANES), F32)],
        compiler_params=_params(("arbitrary",), blocks + temps),
        name="scatter",
    )(plan["seg_len"], plan["seg_src"], plan["seg_dst"], plan["block_chunks"], plan["pad_start"],
      plan["pad_count"], plan["tiles_used"], x1b, sel, comb)


def _experts_kernel(ntile_ref, used_ref, g_ref, wg_ref, wu_ref, wd_ref, y_ref,
                    x_buf, y_buf, wg_buf, wu_buf, wd_buf, wgb_ref, wub_ref, wdb_ref,
                    x_sems, y_sems, w_sems):
    n_experts = wg_ref.shape[0]
    tm = EXPERT_TILE
    used = used_ref[0]

    def weight_copies(e, slot):
        return [pltpu.make_async_copy(src.at[e], dst.at[slot], w_sems.at[slot])
                for src, dst in ((wg_ref, wg_buf), (wu_ref, wu_buf), (wd_ref, wd_buf))]

    def x_copy(t, slot):
        rows = pl.ds(pl.multiple_of(t * tm, tm), MACRO_TILES * tm)
        return pltpu.make_async_copy(g_ref.at[rows], x_buf.at[slot], x_sems.at[slot])

    def y_copy(t, k, slot):
        n = k * tm
        dst = pl.ds(pl.multiple_of(t * tm, tm), n)
        return pltpu.make_async_copy(y_buf.at[slot, pl.ds(0, n)], y_ref.at[dst], y_sems.at[slot])

    def wait_y(t, k, slot):
        @pl.when(k > 0)
        def _():
            y_copy(t, k, slot).wait()

    def mlp(slot, rows):
        x = x_buf[slot, :rows, :]
        hg = jnp.dot(x, wgb_ref[...], preferred_element_type=F32)
        hu = jnp.dot(x, wub_ref[...], preferred_element_type=F32)
        h = hg * _sigmoid(hg) * hu
        y_buf[slot, :rows, :] = jnp.dot(h.astype(BF16), wdb_ref[...],
                                        preferred_element_type=F32).astype(BF16)

    for c in weight_copies(0, 0):
        c.start()
    x_copy(0, 0).start()

    def run_expert(e, carry):
        wslot = e % 2
        for c in weight_copies(e, wslot):
            c.wait()

        @pl.when(e + 1 < n_experts)
        def _():
            for c in weight_copies(e + 1, 1 - wslot):
                c.start()

        @pl.when(ntile_ref[e] > 0)
        def _():
            wgb_ref[...] = wg_buf[wslot].astype(BF16)
            wub_ref[...] = wu_buf[wslot].astype(BF16)
            wdb_ref[...] = wd_buf[wslot].astype(BF16)

        def run_macro(m, carry):
            t, step, k1, t1, k2, t2 = carry
            k = jnp.minimum(MACRO_TILES, ntile_ref[e] - m * MACRO_TILES)
            slot = step % 2
            x_copy(t, slot).wait()

            @pl.when(t + k < used)
            def _():
                x_copy(t + k, 1 - slot).start()

            wait_y(t2, k2, slot)
            for tiles in range(1, MACRO_TILES + 1):
                @pl.when(k == tiles)
                def _():
                    mlp(slot, tiles * tm)

            y_copy(t, k, slot).start()
            return t + k, step + 1, k, t, k1, t1

        n_macro = (ntile_ref[e] + (MACRO_TILES - 1)) // MACRO_TILES
        return lax.fori_loop(0, n_macro, run_macro, carry)

    zero = jnp.int32(0)
    _, step, k1, t1, k2, t2 = lax.fori_loop(0, n_experts, run_expert, (zero,) * 6)
    wait_y(t2, k2, step % 2)
    wait_y(t1, k1, (step + 1) % 2)
    y_buf[0, :tm, :] = jnp.zeros((tm, y_buf.shape[2]), BF16)
    n_unused = y_ref.shape[0] // tm - used
    _for_each(n_unused, lambda i: y_copy(used + i, 1, 0).start())
    _for_each(n_unused, lambda i: y_copy(used + i, 1, 0).wait())


def _experts(sorted_x, plan, w_eg, w_eu, w_ed):
    rows, d = sorted_x.shape
    _, _, de = w_eg.shape
    tm = EXPERT_TILE
    any_space = pl.BlockSpec(memory_space=pl.ANY)
    big = MACRO_TILES * tm
    scratch = 2 * 2 * big * d * 2 + 2 * 3 * d * de * 4 + 3 * d * de * 2
    temps = 3 * big * de * 4 + big * d * 4 + d * de * 4
    return pl.pallas_call(
        _experts_kernel,
        grid_spec=pltpu.PrefetchScalarGridSpec(
            num_scalar_prefetch=2,
            grid=(1,),
            in_specs=[any_space] * 4,
            out_specs=any_space,
            scratch_shapes=[pltpu.VMEM((2, big, d), BF16), pltpu.VMEM((2, big, d), BF16),
                            pltpu.VMEM((2, d, de), F32), pltpu.VMEM((2, d, de), F32),
                            pltpu.VMEM((2, de, d), F32),
                            pltpu.VMEM((d, de), BF16), pltpu.VMEM((d, de), BF16),
                            pltpu.VMEM((de, d), BF16),
                            pltpu.SemaphoreType.DMA((2,)), pltpu.SemaphoreType.DMA((2,)),
                            pltpu.SemaphoreType.DMA((2,))]),
        out_shape=jax.ShapeDtypeStruct((rows, d), BF16),
        compiler_params=_params(("arbitrary",), scratch + temps),
        name="experts",
    )(plan["expert_tiles"], plan["tiles_used"], sorted_x, w_eg, w_eu, w_ed)


def _combine_kernel(len_ref, src_ref, dst_ref, nchunk_ref, x1_ref, route_ref, p_ref,
                    wpg_ref, wpp_ref,
                    lg_ref, lb_ref, y_ref, o_ref, loc_ref, wpgb_ref, wppb_ref, sems, *, alpha):
    b = pl.program_id(0)
    nb = pl.num_programs(0)
    slot = b % 2
    tb, d = x1_ref.shape
    _cast_once(wpg_ref, wpgb_ref)
    _cast_once(wpp_ref, wppb_ref)

    def start_block(blk, slot):
        def start_segment(e):
            n, local, sorted_at = _segment_rows(len_ref, src_ref, dst_ref, blk, e)

            @pl.when(n > 0)
            def _():
                pltpu.make_async_copy(y_ref.at[pl.ds(sorted_at, n)],
                                      loc_ref.at[slot, pl.ds(local, n)], sems.at[slot]).start()

        _for_each(N_EXPERTS, start_segment)

    @pl.when(b == 0)
    def _():
        loc_ref[...] = jnp.zeros_like(loc_ref)
        start_block(0, 0)

    @pl.when(b + 1 < nb)
    def _():
        start_block(b + 1, 1 - slot)

    n_rows = nchunk_ref[b] * CHUNK
    pltpu.make_async_copy(y_ref.at[pl.ds(0, n_rows)], loc_ref.at[slot, pl.ds(0, n_rows)],
                          sems.at[slot]).wait()

    lo = route_ref[:, ROUTE_POS_LO:ROUTE_POS_LO + 1].astype(jnp.int32)
    hi = route_ref[:, ROUTE_POS_HI:ROUTE_POS_HI + 1].astype(jnp.int32)
    w_lo = route_ref[:, ROUTE_W_LO:ROUTE_W_LO + 1]
    w_hi = route_ref[:, ROUTE_W_HI:ROUTE_W_HI + 1]

    def finish_block(height):
        sorted_row = lax.broadcasted_iota(jnp.int32, (tb // ROW_SUBS, height), 1)

        def branches(r):
            weights = (jnp.where(sorted_row == lo[r], w_lo[r], 0.0)
                       + jnp.where(sorted_row == hi[r], w_hi[r], 0.0)).astype(BF16)
            moe = jnp.dot(weights, loc_ref[slot, :height, :], preferred_element_type=F32)
            gate = jnp.dot(x1_ref[r, :].astype(BF16), wpgb_ref[...], preferred_element_type=F32)
            emb = jnp.dot(p_ref[r, :].astype(BF16), wppb_ref[...], preferred_element_type=F32)
            return moe, gate, emb

        def finish(r, parts):
            moe, gate, emb = parts
            h = alpha * x1_ref[r, :] + moe + _sigmoid(gate) * emb
            o_ref[r, :] = _layer_norm(h, lg_ref[...], lb_ref[...])

        _skewed(_row_subs(tb), branches, finish)

    for below, height in zip((0,) + SORT_HEIGHTS[:-1], SORT_HEIGHTS):
        @pl.when((n_rows > below) & (n_rows <= height))
        def _():
            finish_block(height)


def _combine(y, plan, x1, route, p2, w_pg, w_pp, ln_g, ln_b, alpha):
    t, d = x1.shape
    pd = p2.shape[1]
    tb = ROUTE_BLOCK
    row = lambda b, *_: (b, 0)
    blocks = (2 * (2 * tb * d * 4 + tb * LANES * 4 + tb * pd * 4)
              + (d * d + pd * d) * (4 + 2) + 2 * d * 4 + 2 * LOCAL_ROWS * d * 2)
    temps = tb * LOCAL_ROWS * 10 + 5 * tb * d * 4
    return pl.pallas_call(
        functools.partial(_combine_kernel, alpha=alpha),
        grid_spec=pltpu.PrefetchScalarGridSpec(
            num_scalar_prefetch=4,
            grid=(t // tb,),
            in_specs=[pl.BlockSpec((tb, d), row), pl.BlockSpec((tb, LANES), row),
                      pl.BlockSpec((tb, pd), row),
                      _resident((d, d)), _resident((pd, d)), _resident((1, d)), _resident((1, d)),
                      pl.BlockSpec(memory_space=pl.ANY)],
            out_specs=pl.BlockSpec((tb, d), row),
            scratch_shapes=[pltpu.VMEM((2, LOCAL_ROWS, d), BF16), pltpu.VMEM((d, d), BF16),
                            pltpu.VMEM((pd, d), BF16), pltpu.SemaphoreType.DMA((2,))]),
        out_shape=jax.ShapeDtypeStruct((t, d), F32),
        compiler_params=_params(("arbitrary",), blocks + temps),
        name="combine",
    )(plan["seg_len"], plan["seg_src"], plan["seg_dst"], plan["block_chunks"], x1, route, p2,
      w_pg, w_pp, ln_g, ln_b, y)


def kernel(x, p, w_in, w_pool, pool_scale, w_pu, w_au, w_o, ln1_g, ln1_b, w_rg, b_rg, w_re, b_re,
           w_eg, w_eu, w_ed, w_pg, w_pp, ln2_g, ln2_b):
    bsz, seq, d = x.shape
    depth = w_in.shape[0]
    t = bsz * seq
    de = w_eg.shape[-1]
    alpha = (2.0 * depth) ** 0.25
    assert w_rg.shape[2] == N_GROUPS and w_re.shape[1:] == (N_GROUPS, d, EXPERTS_PER_GROUP)
    assert w_in.shape[2] == 4 * d and w_pool.shape[1] == len(POOL_WINDOWS)
    assert t % ROUTE_BLOCK == 0 and N_EXPERTS + N_GROUPS <= LANES
    n_blocks = t // ROUTE_BLOCK
    n_tiles = -(-(2 * t + n_blocks * N_EXPERTS * (CHUNK - 1) + N_EXPERTS * (EXPERT_TILE - CHUNK))
                // EXPERT_TILE) + MACRO_TILES - 1

    x2 = x.reshape(t, d)
    for i in range(depth):
        pool_out, q, k, v, gates = _in_hbm(
            *_proj(x2, w_in[i], w_pool[i], pool_scale[i][None, :], seq, tm=512))
        attn_out, = _in_hbm(_attention(q, k, v, seq, tb=128, blocks_per_step=4))

        w_r = jnp.concatenate(
            [w_re[i].transpose(1, 0, 2).reshape(d, N_EXPERTS), w_rg[i]], axis=1)
        w_r = jnp.pad(w_r, ((0, 0), (0, LANES - w_r.shape[1])))
        b_r = jnp.pad(jnp.concatenate([b_re[i].reshape(-1), b_rg[i]]),
                      (0, LANES - N_GROUPS - N_EXPERTS))[None, :]

        x1, x1b, comb, sel, counts = _merge(
            x2, pool_out, attn_out, gates, *_in_hbm(w_pu[i], w_au[i], w_o[i]), ln1_g[i][None, :],
            ln1_b[i][None, :], w_r, b_r, alpha, tm=ROUTE_BLOCK)
        x1, x1b, comb, sel = _in_hbm(x1, x1b, comb, sel)

        plan = _routing_plan(counts)
        sorted_x, route = _in_hbm(*_scatter(x1b, sel, comb, plan, n_tiles))
        y, = _in_hbm(_experts(sorted_x, plan, w_eg[i].reshape(N_EXPERTS, d, de),
                              w_eu[i].reshape(N_EXPERTS, d, de), w_ed[i].reshape(N_EXPERTS, de, d)))
        x2 = _combine(y, plan, x1, route, p[i].reshape(t, -1), *_in_hbm(w_pg[i], w_pp[i]),
                      ln2_g[i][None, :], ln2_b[i][None, :], alpha)
    return x2.reshape(bsz, seq, d)
```

```python
import functools
import math

import jax
import jax.numpy as jnp
from jax import lax
from jax.experimental import pallas as pl
from jax.experimental.pallas import tpu as pltpu

F32 = jnp.float32
BF16 = jnp.bfloat16

LANES = 128
POOL_WINDOWS = (2, 4, 8, 16)
POOL_HALO = 16
HEAD_DIM = 64
N_GROUPS = 4
EXPERTS_PER_GROUP = 8
N_EXPERTS = N_GROUPS * EXPERTS_PER_GROUP
LN_EPS = 1e-5
GROUP_LANE0 = N_EXPERTS
NEG_BIG = -1e30
ROUTE_BLOCK = 512
CHUNK = 16
EXPERT_TILE = 256
MACRO_TILES = 6
SORT_ROWS = 256
ROW_SUBS = 2
ROUTE_POS_LO, ROUTE_POS_HI, ROUTE_W_LO, ROUTE_W_HI = range(4)
LOCAL_ROWS = -(-(2 * ROUTE_BLOCK + N_EXPERTS * (CHUNK - 1)) // SORT_ROWS) * SORT_ROWS
SORT_HEIGHTS = (LOCAL_ROWS - SORT_ROWS, LOCAL_ROWS - SORT_ROWS // 2, LOCAL_ROWS)
ATTN_DEAD_LOG2 = -160.0
VMEM_CAP_BYTES = 56 * 1024 * 1024


def _params(sem, vmem_bytes):
    return pltpu.CompilerParams(
        dimension_semantics=sem, vmem_limit_bytes=min(int(vmem_bytes), VMEM_CAP_BYTES))


def _layer_norm(h, g, b):
    mu = jnp.mean(h, axis=-1, keepdims=True)
    c = h - mu
    var = jnp.mean(c * c, axis=-1, keepdims=True)
    return c * lax.rsqrt(var + LN_EPS) * g + b


def _sigmoid(z):
    return 1.0 / (1.0 + jnp.exp(-z))


def _row_subs(rows):
    return [slice(k * (rows // ROW_SUBS), (k + 1) * (rows // ROW_SUBS)) for k in range(ROW_SUBS)]


def _skewed(subs, first, second):
    out, pending = [], None
    for r in subs:
        mid = first(r)
        if pending is not None:
            out.append(second(*pending))
        pending = (r, mid)
    out.append(second(*pending))
    return out


def _in_hbm(*arrays):
    if not all(isinstance(a, jax.core.Tracer) for a in arrays):
        return list(arrays)
    return [pltpu.with_memory_space_constraint(a, pltpu.HBM) for a in arrays]


def _resident(shape):
    return pl.BlockSpec(shape, lambda *_: (0,) * len(shape), pipeline_mode=pl.Buffered(1))


def _cast_once(w_ref, wb_ref):
    @pl.when(pl.program_id(0) == 0)
    def _():
        wb_ref[...] = w_ref[...].astype(BF16)


def _proj_kernel(x_ref, w_ref, wp_ref, sc_ref, po_ref, q_ref, k_ref, v_ref, g_ref, wb_ref, halo_ref,
                 *, width, tiles_per_seq):
    q_scale = math.log2(math.e) / math.sqrt(HEAD_DIM)
    tile_in_seq = pl.program_id(0) % tiles_per_seq
    _cast_once(w_ref, wb_ref)
    xb = x_ref[...].astype(BF16)

    def mm(lo):
        return jnp.dot(xb, wb_ref[:, lo:lo + width], preferred_element_type=F32)

    @pl.when(tile_in_seq == 0)
    def _():
        halo_ref[...] = jnp.zeros_like(halo_ref)

    u = mm(0)
    tm = u.shape[0]
    gd = wp_ref.shape[1]
    pos = tile_in_seq * tm + lax.broadcasted_iota(jnp.int32, (tm, gd), 0)

    def pool_group(g, w):
        cols = slice(g * gd, (g + 1) * gd)
        ug = u[:, cols]
        s = jnp.concatenate([halo_ref[:, cols], ug], axis=0)
        sh = 1
        while sh < w:
            s = s + pltpu.roll(s, sh, axis=0)
            sh *= 2
        cnt = jnp.minimum(pos + 1, w).astype(F32)
        pooled = s[POOL_HALO:, :] / cnt - ug
        mixed = jnp.dot(pooled.astype(BF16), wp_ref[g].astype(BF16), preferred_element_type=F32)
        po_ref[:, cols] = (mixed * sc_ref[:, cols]).astype(BF16)

    def write_q():
        q_ref[...] = (mm(width) * q_scale).astype(BF16)

    def write_k():
        k_ref[...] = mm(2 * width).astype(BF16)

    def write_v():
        v_ref[...] = mm(3 * width).astype(BF16)

    def gate_chunk(c):
        g_ref[:, c * width:(c + 1) * width] = _sigmoid(mm((4 + c) * width)).astype(BF16)

    matmuls = [write_q, write_k, write_v] + [
        functools.partial(gate_chunk, c) for c in range(g_ref.shape[1] // width)]
    for n, matmul in enumerate(matmuls):
        matmul()
        if n < len(POOL_WINDOWS):
            pool_group(n, POOL_WINDOWS[n])
    halo_ref[...] = u[tm - POOL_HALO:, :]


def _proj(x2, w_in, w_pool, pool_scale, seq, tm):
    t, d = x2.shape
    n = w_in.shape[1]
    width = d // 2
    gate_w = n - 4 * width
    groups, gd, _ = w_pool.shape
    assert seq % tm == 0 and groups * gd == width and max(POOL_WINDOWS) <= POOL_HALO + 1
    row = lambda i: (i, 0)
    blocks = (2 * (tm * d * 4 + tm * width * 4 * 2 + tm * gate_w * 2)
              + d * n * (4 + 2) + groups * gd * gd * 4 + width * 4 + POOL_HALO * width * 4)
    temps = tm * d * 2 + 3 * tm * width * 4 + 6 * (tm + POOL_HALO) * gd * 4
    return pl.pallas_call(
        functools.partial(_proj_kernel, width=width, tiles_per_seq=seq // tm),
        grid=(t // tm,),
        in_specs=[pl.BlockSpec((tm, d), row), _resident((d, n)), _resident((groups, gd, gd)),
                  _resident((1, width))],
        out_specs=[pl.BlockSpec((tm, width), row)] * 4 + [pl.BlockSpec((tm, gate_w), row)],
        out_shape=[jax.ShapeDtypeStruct((t, width), BF16)] * 4
        + [jax.ShapeDtypeStruct((t, gate_w), BF16)],
        scratch_shapes=[pltpu.VMEM((d, n), BF16), pltpu.VMEM((POOL_HALO, width), F32)],
        compiler_params=_params(("arbitrary",), blocks + temps),
        name="proj",
    )(x2, w_in, w_pool, pool_scale)


def _attn_kernel(q_ref, k_ref, v_ref, o_ref, rem_ref, acc_ref, *, tb):
    pairs = q_ref.shape[1] // LANES
    blocks_per_step = q_ref.shape[0] // tb
    col_blocks = [slice(p * LANES, (p + 1) * LANES) for p in range(pairs)]
    first_head = lax.broadcasted_iota(jnp.int32, (tb, LANES), 1) < HEAD_DIM
    r = lax.broadcasted_iota(jnp.int32, (2 * tb, 2 * tb), 0)
    c = lax.broadcasted_iota(jnp.int32, (2 * tb, 2 * tb), 1)
    same_head = (r >= tb) == (c >= tb)
    cum = -jnp.concatenate([(same_head & (r >= c)).astype(BF16), same_head.astype(BF16)], axis=1)
    kcol = lax.broadcasted_iota(jnp.int32, (tb, 2 * tb), 1)
    kcol = jnp.where(kcol >= tb, kcol - tb, kcol)
    causal = kcol < lax.broadcasted_iota(jnp.int32, (tb, 2 * tb), 0)

    def stack_heads(blk):
        zero = jnp.zeros_like(blk)
        return jnp.concatenate(
            [jnp.where(first_head, blk, zero), jnp.where(first_head, zero, blk)], axis=0)

    def scores(q0, j, rows, diagonal):
        ks = pl.multiple_of(j * tb, tb)
        q_rows = pl.ds(q0 + rows.start, rows.stop - rows.start)
        zs = [lax.dot_general(q_ref[q_rows, cols], stack_heads(k_ref[pl.ds(ks, tb), cols]),
                              (((1,), (1,)), ((), ())), preferred_element_type=F32)
              for cols in col_blocks]
        sums = []
        for z in zs:
            softplus = jnp.maximum(z, 0.0) + jnp.log2(1.0 + jnp.exp2(-jnp.abs(z)))
            if diagonal:
                softplus = jnp.where(causal, softplus, 0.0)
            sums.append(jnp.dot(softplus.astype(BF16), cum, preferred_element_type=F32))
        return zs, sums

    def weighted_values(j, cols, z, later):
        ks = pl.multiple_of(j * tb, tb)
        a = jnp.exp2(z + later)
        return jnp.dot(a.astype(BF16), stack_heads(v_ref[pl.ds(ks, tb), cols]),
                       preferred_element_type=F32)

    def first_step(qi, q0, n_before):
        full, half = slice(0, tb), slice(0, tb // 2)
        z_d, sums_d = scores(q0, qi, full, diagonal=True)
        if n_before >= 1:
            z_1, sums_1 = scores(q0, qi - 1, full, diagonal=False)
        if n_before >= 2:
            z_2, sums_2 = scores(q0, qi - 2, half, diagonal=False)
        owed = []
        for p, cols in enumerate(col_blocks):
            ks = pl.multiple_of(qi * tb, tb)
            a = jnp.where(causal, jnp.exp2(z_d[p] + sums_d[p][:, :2 * tb]), 0.0)
            out = jnp.dot(a.astype(BF16), stack_heads(v_ref[pl.ds(ks, tb), cols]),
                          preferred_element_type=F32)
            total = sums_d[p][:, 2 * tb:]
            if n_before >= 1:
                out = out + weighted_values(qi - 1, cols, z_1[p], sums_1[p][:, :2 * tb] + total)
                total = total + sums_1[p][:, 2 * tb:]
            acc_ref[p] = out
            rem_ref[p] = total
            if n_before >= 2:
                owed.append(total[tb // 2:, :])
                later = sums_2[p][:, :2 * tb] + total[half, :]
                acc_ref[p, half, :] += weighted_values(qi - 2, cols, z_2[p], later)
                rem_ref[p, half, :] += sums_2[p][:, 2 * tb:]
        if n_before < 2:
            return None
        return jnp.max(functools.reduce(jnp.maximum, owed)) > ATTN_DEAD_LOG2

    def step(q0, j, rows):
        zs, sums = scores(q0, j, rows, diagonal=False)
        for p, cols in enumerate(col_blocks):
            later = sums[p][:, :2 * tb] + rem_ref[p, rows, :]
            acc_ref[p, rows, :] += weighted_values(j, cols, zs[p], later)
            rem_ref[p, rows, :] += sums[p][:, 2 * tb:]

    def live_rows():
        worst = functools.reduce(jnp.maximum, [rem_ref[p] for p in range(pairs)])
        live = jnp.max(worst, axis=1, keepdims=True) > ATTN_DEAD_LOG2
        row = lax.broadcasted_iota(jnp.int32, (tb, 1), 0)
        return jnp.max(jnp.where(live, row + 1, 0))

    row_counts = (tb, tb // 2, tb // 4)

    def one_block(s, carry):
        qi = pl.program_id(1) * blocks_per_step + s
        q0 = pl.multiple_of(s * tb, tb)
        for n_before in (0, 1):
            @pl.when(qi == n_before)
            def _():
                first_step(qi, q0, n_before)

        @pl.when(qi >= 2)
        def _():
            second_half_owed = first_step(qi, q0, 2)

            @pl.when(second_half_owed)
            def _():
                step(q0, qi - 2, slice(tb // 2, tb))

        def cond(state):
            j, n_live = state
            return (j >= 0) & (n_live > 0)

        def body(state):
            j, n_live = state
            for rows, fewer in zip(row_counts, row_counts[1:] + (0,)):
                @pl.when((n_live > fewer) & (n_live <= rows))
                def _():
                    step(q0, j, slice(0, rows))

            return j - 1, live_rows()

        lax.while_loop(cond, body, (qi - 3, live_rows()))
        for p, cols in enumerate(col_blocks):
            o_ref[pl.ds(q0, tb), cols] = acc_ref[p].astype(BF16)
        return carry

    lax.fori_loop(0, blocks_per_step, one_block, 0)


def _attention(q, k, v, seq, tb, blocks_per_step):
    t, width = q.shape
    pairs = width // LANES
    tq = tb * blocks_per_step
    blocks = 2 * (2 * tq * width * 2 + 2 * seq * width * 2) + pairs * tb * (2 * tb + LANES) * 4
    temps = pairs * 10 * tb * 2 * tb * 4 + 2 * tb * 4 * tb * 2
    qspec = pl.BlockSpec((tq, width), lambda b, i: (b * (seq // tq) + i, 0))
    kvspec = pl.BlockSpec((seq, width), lambda b, i: (b, 0))
    return pl.pallas_call(
        functools.partial(_attn_kernel, tb=tb),
        grid=(t // seq, seq // tq),
        in_specs=[qspec, kvspec, kvspec],
        out_specs=qspec,
        out_shape=jax.ShapeDtypeStruct((t, width), BF16),
        scratch_shapes=[pltpu.VMEM((pairs, tb, 2 * tb), F32), pltpu.VMEM((pairs, tb, LANES), F32)],
        input_output_aliases={0: 0},
        compiler_params=_params(("parallel", "parallel"), blocks + temps),
        name="attn",
    )(q, k, v)


def _route(logits):
    lane = lax.broadcasted_iota(jnp.int32, logits.shape, 1)

    def first_max(vals):
        m = jnp.max(vals, axis=1, keepdims=True)
        idx = jnp.min(jnp.where(vals == m, lane, LANES), axis=1, keepdims=True)
        return m, idx

    is_group = (lane >= GROUP_LANE0) & (lane < GROUP_LANE0 + N_GROUPS)
    gm, g_lane = first_max(jnp.where(is_group, logits, NEG_BIG))
    g_prob = 1.0 / jnp.sum(jnp.where(is_group, jnp.exp(logits - gm), 0.0), axis=1, keepdims=True)
    lo = EXPERTS_PER_GROUP * (g_lane - GROUP_LANE0)
    in_group = jnp.where((lane >= lo) & (lane < lo + EXPERTS_PER_GROUP), logits, NEG_BIG)
    m1, i1 = first_max(in_group)
    m2, i2 = first_max(jnp.where(lane == i1, NEG_BIG, in_group))
    e21 = jnp.exp(m2 - m1)
    w1 = g_prob / (1.0 + e21)
    w2 = w1 * e21
    comb = jnp.where(lane == i1, w1, 0.0) + jnp.where(lane == i2, w2, 0.0)
    sel = ((lane == i1) | (lane == i2)).astype(F32)
    return comb, sel


def _merge_kernel(x_ref, po_ref, at_ref, g_ref, wpu_ref, wau_ref, wo_ref, lg_ref, lb_ref,
                  wr_ref, br_ref, x1_ref, x1b_ref, comb_ref, sel_ref, cnt_ref,
                  wpub_ref, waub_ref, wob_ref, wrb_ref, *, alpha):
    tm, d = x_ref.shape
    _cast_once(wpu_ref, wpub_ref)
    _cast_once(wau_ref, waub_ref)
    _cast_once(wo_ref, wob_ref)

    @pl.when(pl.program_id(0) == 0)
    def _():
        w_r = wr_ref[...]
        hi = w_r.astype(BF16)
        wrb_ref[:, :LANES] = hi
        wrb_ref[:, LANES:] = (w_r - hi.astype(F32)).astype(BF16)

    def mix(r):
        a = jnp.dot(po_ref[r, :], wpub_ref[...], preferred_element_type=F32)
        b = jnp.dot(at_ref[r, :], waub_ref[...], preferred_element_type=F32)
        merged = (g_ref[r, :d].astype(F32) * a + g_ref[r, d:].astype(F32) * b).astype(BF16)
        return alpha * x_ref[r, :] + jnp.dot(merged, wob_ref[...], preferred_element_type=F32)

    def norm_and_route(r, h):
        x1 = _layer_norm(h, lg_ref[...], lb_ref[...])
        xh = x1.astype(BF16)
        x1_ref[r, :] = x1
        x1b_ref[r, :] = xh
        xl = (x1 - xh.astype(F32)).astype(BF16)
        by_hi = jnp.dot(xh, wrb_ref[...], preferred_element_type=F32)
        by_lo = jnp.dot(xl, wrb_ref[:, :LANES], preferred_element_type=F32)
        comb, sel = _route(by_hi[:, :LANES] + by_hi[:, LANES:] + by_lo + br_ref[...])
        comb_ref[r, :] = comb
        sel_ref[r, :] = sel.astype(BF16)
        return jnp.sum(sel, axis=0, keepdims=True)

    cnt_ref[0] = sum(_skewed(_row_subs(tm), mix, norm_and_route))


def _merge(x2, pool_out, attn_out, gates, w_pu, w_au, w_o, ln_g, ln_b, w_r, b_r, alpha, tm):
    t, d = x2.shape
    width = pool_out.shape[1]
    row = lambda i: (i, 0)
    blocks = (2 * (tm * d * 4 + 2 * tm * width * 2 + tm * 2 * d * 2
                   + tm * d * 4 + tm * d * 2 + tm * LANES * (4 + 2) + LANES * 4)
              + (2 * width * d + d * d + d * LANES) * (4 + 2) + 2 * d * 4 + LANES * 4)
    temps = 5 * tm * d * 4
    return pl.pallas_call(
        functools.partial(_merge_kernel, alpha=alpha),
        grid=(t // tm,),
        in_specs=[pl.BlockSpec((tm, d), row), pl.BlockSpec((tm, width), row),
                  pl.BlockSpec((tm, width), row), pl.BlockSpec((tm, 2 * d), row),
                  _resident((width, d)), _resident((width, d)), _resident((d, d)),
                  _resident((1, d)), _resident((1, d)), _resident((d, LANES)),
                  _resident((1, LANES))],
        out_specs=[pl.BlockSpec((tm, d), row), pl.BlockSpec((tm, d), row),
                   pl.BlockSpec((tm, LANES), row), pl.BlockSpec((tm, LANES), row),
                   pl.BlockSpec((1, 1, LANES), lambda i: (i, 0, 0))],
        out_shape=[jax.ShapeDtypeStruct((t, d), F32), jax.ShapeDtypeStruct((t, d), BF16),
                   jax.ShapeDtypeStruct((t, LANES), F32), jax.ShapeDtypeStruct((t, LANES), BF16),
                   jax.ShapeDtypeStruct((t // tm, 1, LANES), F32)],
        scratch_shapes=[pltpu.VMEM((width, d), BF16), pltpu.VMEM((width, d), BF16),
                        pltpu.VMEM((d, d), BF16), pltpu.VMEM((d, 2 * LANES), BF16)],
        compiler_params=_params(("arbitrary",), blocks + temps),
        name="merge",
    )(x2, pool_out, attn_out, gates, w_pu, w_au, w_o, ln_g, ln_b, w_r, b_r)


def _routing_plan(counts):
    cnt = counts[:, 0, :N_EXPERTS].astype(jnp.int32)
    nch = (cnt + (CHUNK - 1)) // CHUNK
    local_end = jnp.cumsum(nch, axis=1)
    local_start = local_end - nch
    block_chunks = local_end[:, -1]
    before_block = jnp.cumsum(nch, axis=0) - nch
    expert_chunks = jnp.sum(nch, axis=0)
    chunks_per_tile = EXPERT_TILE // CHUNK
    expert_tiles = (expert_chunks + (chunks_per_tile - 1)) // chunks_per_tile
    tiles_end = jnp.cumsum(expert_tiles)
    region_start = (tiles_end - expert_tiles) * chunks_per_tile
    segment_dst = region_start[None, :] + before_block
    i32 = lambda a: a.astype(jnp.int32)
    return dict(seg_len=i32(nch), seg_src=i32(local_start), seg_dst=i32(segment_dst),
                block_chunks=i32(block_chunks),
                expert_tiles=i32(expert_tiles), tiles_used=i32(tiles_end[-1:]),
                pad_start=i32(region_start + expert_chunks),
                pad_count=i32(expert_tiles * chunks_per_tile - expert_chunks))


def _local_positions(sel):
    tb = sel.shape[0]
    earlier = (lax.broadcasted_iota(jnp.int32, (tb, tb), 1)
               < lax.broadcasted_iota(jnp.int32, (tb, tb), 0)).astype(BF16)
    rank = jnp.dot(earlier, sel, preferred_element_type=F32)
    cnt = jnp.sum(sel.astype(F32), axis=0, keepdims=True)
    nch = jnp.floor((cnt + (CHUNK - 1)) * (1.0 / CHUNK))
    lower = (lax.broadcasted_iota(jnp.int32, (LANES, LANES), 0)
             < lax.broadcasted_iota(jnp.int32, (LANES, LANES), 1)).astype(BF16)
    start = CHUNK * jnp.dot(jnp.broadcast_to(nch, (8, LANES)).astype(BF16), lower,
                            preferred_element_type=F32)[0:1]
    pos = rank + start
    chosen = sel > 0
    pos_lo = jnp.min(jnp.where(chosen, pos, float(LOCAL_ROWS)), axis=1, keepdims=True)
    pos_hi = jnp.max(jnp.where(chosen, pos, -1.0), axis=1, keepdims=True)
    return pos, pos_lo, pos_hi


def _for_each(n, fn):
    lax.fori_loop(0, n, lambda c, carry: (fn(c), carry)[1], 0)


def _segment_rows(len_ref, src_ref, dst_ref, blk, e):
    n = len_ref[blk, e] * CHUNK
    src = pl.multiple_of(src_ref[blk, e] * CHUNK, CHUNK)
    dst = pl.multiple_of(dst_ref[blk, e] * CHUNK, CHUNK)
    return n, src, dst


def _scatter_kernel(len_ref, src_ref, dst_ref, nchunk_ref, pad_start_ref, pad_count_ref, used_ref,
                    x_ref, sel_ref, comb_ref, g_ref, route_ref, loc_ref, zero_ref, sems):
    b = pl.program_id(0)
    last = pl.num_programs(0) - 1
    slot = b % 2
    tb = x_ref.shape[0]

    def start_block(blk, slot):
        def start_segment(e):
            n, src, dst = _segment_rows(len_ref, src_ref, dst_ref, blk, e)

            @pl.when(n > 0)
            def _():
                pltpu.make_async_copy(loc_ref.at[slot, pl.ds(src, n)], g_ref.at[pl.ds(dst, n)],
                                      sems.at[slot]).start()

        _for_each(N_EXPERTS, start_segment)

    def wait_block(blk, slot):
        n = nchunk_ref[blk] * CHUNK
        pltpu.make_async_copy(loc_ref.at[slot, pl.ds(0, n)], g_ref.at[pl.ds(0, n)],
                              sems.at[slot]).wait()

    def pad_copy(e):
        n = pad_count_ref[e] * CHUNK
        dst = pl.multiple_of(pad_start_ref[e] * CHUNK, CHUNK)
        return n, pltpu.make_async_copy(zero_ref.at[pl.ds(0, n)], g_ref.at[pl.ds(dst, n)],
                                        sems.at[2])

    def unused_tile_copy(i):
        dst = pl.multiple_of(i * EXPERT_TILE, EXPERT_TILE)
        return pltpu.make_async_copy(zero_ref, g_ref.at[pl.ds(dst, EXPERT_TILE)], sems.at[2])

    @pl.when(b >= 2)
    def _():
        wait_block(b - 2, slot)

    sel = sel_ref[...]
    pos, pos_lo, pos_hi = _local_positions(sel)
    lo = pos_lo.astype(jnp.int32)
    hi = pos_hi.astype(jnp.int32)
    x = x_ref[...]
    chosen = sel > 0
    comb = comb_ref[...]
    w_lo = jnp.sum(jnp.where(chosen & (pos == pos_lo), comb, 0.0), axis=1, keepdims=True)
    w_hi = jnp.sum(jnp.where(chosen & (pos == pos_hi), comb, 0.0), axis=1, keepdims=True)
    lane = lax.broadcasted_iota(jnp.int32, comb.shape, 1)
    route_ref[...] = jnp.where(
        lane == ROUTE_POS_LO, pos_lo, jnp.where(
            lane == ROUTE_POS_HI, pos_hi, jnp.where(
                lane == ROUTE_W_LO, w_lo, jnp.where(lane == ROUTE_W_HI, w_hi, 0.0))))

    def sort_rows(height):
        r = lax.broadcasted_iota(jnp.int32, (tb, height), 1)
        perm = ((r == lo) | (r == hi)).astype(BF16)
        rows = lax.dot_general(perm, x, (((0,), (0,)), ((), ())), preferred_element_type=F32)
        loc_ref[slot, :height, :] = rows.astype(BF16)

    n_rows = nchunk_ref[b] * CHUNK
    for below, height in zip((0,) + SORT_HEIGHTS[:-1], SORT_HEIGHTS):
        @pl.when((n_rows > below) & (n_rows <= height))
        def _():
            sort_rows(height)

    start_block(b, slot)

    @pl.when(b == last)
    def _():
        zero_ref[...] = jnp.zeros_like(zero_ref)
        n_unused = g_ref.shape[0] // EXPERT_TILE - used_ref[0]

        def each_pad(act):
            def one(e):
                n, copy = pad_copy(e)

                @pl.when(n > 0)
                def _():
                    act(copy)

            _for_each(N_EXPERTS, one)

        each_pad(lambda copy: copy.start())
        _for_each(n_unused, lambda i: unused_tile_copy(used_ref[0] + i).start())
        each_pad(lambda copy: copy.wait())
        _for_each(n_unused, lambda i: unused_tile_copy(used_ref[0] + i).wait())

        @pl.when(b >= 1)
        def _():
            wait_block(b - 1, 1 - slot)

        wait_block(b, slot)


def _scatter(x1b, sel, comb, plan, n_tiles):
    t, d = x1b.shape
    tb = ROUTE_BLOCK
    row = lambda b, *_: (b, 0)
    blocks = (2 * (tb * d * 2 + tb * LANES * (2 + 4 + 4)) + 2 * LOCAL_ROWS * d * 2
              + EXPERT_TILE * d * 2)
    temps = tb * tb * 2 + 6 * tb * LANES * 4 + tb * LOCAL_ROWS * 6 + LOCAL_ROWS * d * 6
    return pl.pallas_call(
        _scatter_kernel,
        grid_spec=pltpu.PrefetchScalarGridSpec(
            num_scalar_prefetch=7,
            grid=(t // tb,),
            in_specs=[pl.BlockSpec((tb, d), row), pl.BlockSpec((tb, LANES), row),
                      pl.BlockSpec((tb, LANES), row)],
            out_specs=[pl.BlockSpec(memory_space=pl.ANY), pl.BlockSpec((tb, LANES), row)],
            scratch_shapes=[pltpu.VMEM((2, LOCAL_ROWS, d), BF16),
                            pltpu.VMEM((EXPERT_TILE, d), BF16), pltpu.SemaphoreType.DMA((3,))]),
        out_shape=[jax.ShapeDtypeStruct((n_tiles * EXPERT_TILE, d), BF16),
                   jax.ShapeDtypeStruct((t, LANES), F32)],
        input_output_aliases={9: 1},
        compiler_params=_params(("arbitrary",), blocks + temps),
        name="scatter",
    )(plan["seg_len"], plan["seg_src"], plan["seg_dst"], plan["block_chunks"], plan["pad_start"],
      plan["pad_count"], plan["tiles_used"], x1b, sel, comb)


def _experts_kernel(ntile_ref, used_ref, g_ref, wg_ref, wu_ref, wd_ref, y_ref,
                    x_buf, y_buf, wg_buf, wu_buf, wd_buf, wgb_ref, wub_ref, wdb_ref,
                    x_sems, y_sems, w_sems):
    n_experts = wg_ref.shape[0]
    tm = EXPERT_TILE
    used = used_ref[0]

    def weight_copies(e, slot):
        return [pltpu.make_async_copy(src.at[e], dst.at[slot], w_sems.at[slot])
                for src, dst in ((wg_ref, wg_buf), (wu_ref, wu_buf), (wd_ref, wd_buf))]

    def x_copy(t, slot):
        rows = pl.ds(pl.multiple_of(t * tm, tm), MACRO_TILES * tm)
        return pltpu.make_async_copy(g_ref.at[rows], x_buf.at[slot], x_sems.at[slot])

    def y_copy(t, k, slot):
        n = k * tm
        dst = pl.ds(pl.multiple_of(t * tm, tm), n)
        return pltpu.make_async_copy(y_buf.at[slot, pl.ds(0, n)], y_ref.at[dst], y_sems.at[slot])

    def wait_y(t, k, slot):
        @pl.when(k > 0)
        def _():
            y_copy(t, k, slot).wait()

    def mlp(slot, rows):
        x = x_buf[slot, :rows, :]
        hg = jnp.dot(x, wgb_ref[...], preferred_element_type=F32)
        hu = jnp.dot(x, wub_ref[...], preferred_element_type=F32)
        h = hg * _sigmoid(hg) * hu
        y_buf[slot, :rows, :] = jnp.dot(h.astype(BF16), wdb_ref[...],
                                        preferred_element_type=F32).astype(BF16)

    for c in weight_copies(0, 0):
        c.start()
    x_copy(0, 0).start()

    def run_expert(e, carry):
        wslot = e % 2
        for c in weight_copies(e, wslot):
            c.wait()

        @pl.when(e + 1 < n_experts)
        def _():
            for c in weight_copies(e + 1, 1 - wslot):
                c.start()

        @pl.when(ntile_ref[e] > 0)
        def _():
            wgb_ref[...] = wg_buf[wslot].astype(BF16)
            wub_ref[...] = wu_buf[wslot].astype(BF16)
            wdb_ref[...] = wd_buf[wslot].astype(BF16)

        def run_macro(m, carry):
            t, step, k1, t1, k2, t2 = carry
            k = jnp.minimum(MACRO_TILES, ntile_ref[e] - m * MACRO_TILES)
            slot = step % 2
            x_copy(t, slot).wait()

            @pl.when(t + k < used)
            def _():
                x_copy(t + k, 1 - slot).start()

            wait_y(t2, k2, slot)
            for tiles in range(1, MACRO_TILES + 1):
                @pl.when(k == tiles)
                def _():
                    mlp(slot, tiles * tm)

            y_copy(t, k, slot).start()
            return t + k, step + 1, k, t, k1, t1

        n_macro = (ntile_ref[e] + (MACRO_TILES - 1)) // MACRO_TILES
        return lax.fori_loop(0, n_macro, run_macro, carry)

    zero = jnp.int32(0)
    _, step, k1, t1, k2, t2 = lax.fori_loop(0, n_experts, run_expert, (zero,) * 6)
    wait_y(t2, k2, step % 2)
    wait_y(t1, k1, (step + 1) % 2)
    y_buf[0, :tm, :] = jnp.zeros((tm, y_buf.shape[2]), BF16)
    n_unused = y_ref.shape[0] // tm - used
    _for_each(n_unused, lambda i: y_copy(used + i, 1, 0).start())
    _for_each(n_unused, lambda i: y_copy(used + i, 1, 0).wait())


def _experts(sorted_x, plan, w_eg, w_eu, w_ed):
    rows, d = sorted_x.shape
    _, _, de = w_eg.shape
    tm = EXPERT_TILE
    any_space = pl.BlockSpec(memory_space=pl.ANY)
    big = MACRO_TILES * tm
    scratch = 2 * 2 * big * d * 2 + 2 * 3 * d * de * 4 + 3 * d * de * 2
    temps = 3 * big * de * 4 + big * d * 4 + d * de * 4
    return pl.pallas_call(
        _experts_kernel,
        grid_spec=pltpu.PrefetchScalarGridSpec(
            num_scalar_prefetch=2,
            grid=(1,),
            in_specs=[any_space] * 4,
            out_specs=any_space,
            scratch_shapes=[pltpu.VMEM((2, big, d), BF16), pltpu.VMEM((2, big, d), BF16),
                            pltpu.VMEM((2, d, de), F32), pltpu.VMEM((2, d, de), F32),
                            pltpu.VMEM((2, de, d), F32),
                            pltpu.VMEM((d, de), BF16), pltpu.VMEM((d, de), BF16),
                            pltpu.VMEM((de, d), BF16),
                            pltpu.SemaphoreType.DMA((2,)), pltpu.SemaphoreType.DMA((2,)),
                            pltpu.SemaphoreType.DMA((2,))]),
        out_shape=jax.ShapeDtypeStruct((rows, d), BF16),
        compiler_params=_params(("arbitrary",), scratch + temps),
        name="experts",
    )(plan["expert_tiles"], plan["tiles_used"], sorted_x, w_eg, w_eu, w_ed)


def _combine_kernel(len_ref, src_ref, dst_ref, nchunk_ref, x1_ref, route_ref, p_ref,
                    wpg_ref, wpp_ref,
                    lg_ref, lb_ref, y_ref, o_ref, loc_ref, wpgb_ref, wppb_ref, sems, *, alpha):
    b = pl.program_id(0)
    nb = pl.num_programs(0)
    slot = b % 2
    tb, d = x1_ref.shape
    _cast_once(wpg_ref, wpgb_ref)
    _cast_once(wpp_ref, wppb_ref)

    def start_block(blk, slot):
        def start_segment(e):
            n, local, sorted_at = _segment_rows(len_ref, src_ref, dst_ref, blk, e)

            @pl.when(n > 0)
            def _():
                pltpu.make_async_copy(y_ref.at[pl.ds(sorted_at, n)],
                                      loc_ref.at[slot, pl.ds(local, n)], sems.at[slot]).start()

        _for_each(N_EXPERTS, start_segment)

    @pl.when(b == 0)
    def _():
        loc_ref[...] = jnp.zeros_like(loc_ref)
        start_block(0, 0)

    @pl.when(b + 1 < nb)
    def _():
        start_block(b + 1, 1 - slot)

    n_rows = nchunk_ref[b] * CHUNK
    pltpu.make_async_copy(y_ref.at[pl.ds(0, n_rows)], loc_ref.at[slot, pl.ds(0, n_rows)],
                          sems.at[slot]).wait()

    lo = route_ref[:, ROUTE_POS_LO:ROUTE_POS_LO + 1].astype(jnp.int32)
    hi = route_ref[:, ROUTE_POS_HI:ROUTE_POS_HI + 1].astype(jnp.int32)
    w_lo = route_ref[:, ROUTE_W_LO:ROUTE_W_LO + 1]
    w_hi = route_ref[:, ROUTE_W_HI:ROUTE_W_HI + 1]

    def finish_block(height):
        sorted_row = lax.broadcasted_iota(jnp.int32, (tb // ROW_SUBS, height), 1)

        def branches(r):
            weights = (jnp.where(sorted_row == lo[r], w_lo[r], 0.0)
                       + jnp.where(sorted_row == hi[r], w_hi[r], 0.0)).astype(BF16)
            moe = jnp.dot(weights, loc_ref[slot, :height, :], preferred_element_type=F32)
            gate = jnp.dot(x1_ref[r, :].astype(BF16), wpgb_ref[...], preferred_element_type=F32)
            emb = jnp.dot(p_ref[r, :].astype(BF16), wppb_ref[...], preferred_element_type=F32)
            return moe, gate, emb

        def finish(r, parts):
            moe, gate, emb = parts
            h = alpha * x1_ref[r, :] + moe + _sigmoid(gate) * emb
            o_ref[r, :] = _layer_norm(h, lg_ref[...], lb_ref[...])

        _skewed(_row_subs(tb), branches, finish)

    for below, height in zip((0,) + SORT_HEIGHTS[:-1], SORT_HEIGHTS):
        @pl.when((n_rows > below) & (n_rows <= height))
        def _():
            finish_block(height)


def _combine(y, plan, x1, route, p2, w_pg, w_pp, ln_g, ln_b, alpha):
    t, d = x1.shape
    pd = p2.shape[1]
    tb = ROUTE_BLOCK
    row = lambda b, *_: (b, 0)
    blocks = (2 * (2 * tb * d * 4 + tb * LANES * 4 + tb * pd * 4)
              + (d * d + pd * d) * (4 + 2) + 2 * d * 4 + 2 * LOCAL_ROWS * d * 2)
    temps = tb * LOCAL_ROWS * 10 + 5 * tb * d * 4
    return pl.pallas_call(
        functools.partial(_combine_kernel, alpha=alpha),
        grid_spec=pltpu.PrefetchScalarGridSpec(
            num_scalar_prefetch=4,
            grid=(t // tb,),
            in_specs=[pl.BlockSpec((tb, d), row), pl.BlockSpec((tb, LANES), row),
                      pl.BlockSpec((tb, pd), row),
                      _resident((d, d)), _resident((pd, d)), _resident((1, d)), _resident((1, d)),
                      pl.BlockSpec(memory_space=pl.ANY)],
            out_specs=pl.BlockSpec((tb, d), row),
            scratch_shapes=[pltpu.VMEM((2, LOCAL_ROWS, d), BF16), pltpu.VMEM((d, d), BF16),
                            pltpu.VMEM((pd, d), BF16), pltpu.SemaphoreType.DMA((2,))]),
        out_shape=jax.ShapeDtypeStruct((t, d), F32),
        compiler_params=_params(("arbitrary",), blocks + temps),
        name="combine",
    )(plan["seg_len"], plan["seg_src"], plan["seg_dst"], plan["block_chunks"], x1, route, p2,
      w_pg, w_pp, ln_g, ln_b, y)


def kernel(x, p, w_in, w_pool, pool_scale, w_pu, w_au, w_o, ln1_g, ln1_b, w_rg, b_rg, w_re, b_re,
           w_eg, w_eu, w_ed, w_pg, w_pp, ln2_g, ln2_b):
    bsz, seq, d = x.shape
    depth = w_in.shape[0]
    t = bsz * seq
    de = w_eg.shape[-1]
    alpha = (2.0 * depth) ** 0.25
    assert w_rg.shape[2] == N_GROUPS and w_re.shape[1:] == (N_GROUPS, d, EXPERTS_PER_GROUP)
    assert w_in.shape[2] == 4 * d and w_pool.shape[1] == len(POOL_WINDOWS)
    assert t % ROUTE_BLOCK == 0 and N_EXPERTS + N_GROUPS <= LANES
    n_blocks = t // ROUTE_BLOCK
    n_tiles = -(-(2 * t + n_blocks * N_EXPERTS * (CHUNK - 1) + N_EXPERTS * (EXPERT_TILE - CHUNK))
                // EXPERT_TILE) + MACRO_TILES - 1

    x2 = x.reshape(t, d)
    for i in range(depth):
        pool_out, q, k, v, gates = _in_hbm(
            *_proj(x2, w_in[i], w_pool[i], pool_scale[i][None, :], seq, tm=512))
        attn_out, = _in_hbm(_attention(q, k, v, seq, tb=128, blocks_per_step=8))

        w_r = jnp.concatenate(
            [w_re[i].transpose(1, 0, 2).reshape(d, N_EXPERTS), w_rg[i]], axis=1)
        w_r = jnp.pad(w_r, ((0, 0), (0, LANES - w_r.shape[1])))
        b_r = jnp.pad(jnp.concatenate([b_re[i].reshape(-1), b_rg[i]]),
                      (0, LANES - N_GROUPS - N_EXPERTS))[None, :]

        x1, x1b, comb, sel, counts = _merge(
            x2, pool_out, attn_out, gates, *_in_hbm(w_pu[i], w_au[i], w_o[i]), ln1_g[i][None, :],
            ln1_b[i][None, :], w_r, b_r, alpha, tm=ROUTE_BLOCK)
        x1, x1b, comb, sel = _in_hbm(x1, x1b, comb, sel)

        plan = _routing_plan(counts)
        sorted_x, route = _in_hbm(*_scatter(x1b, sel, comb, plan, n_tiles))
        y, = _in_hbm(_experts(sorted_x, plan, w_eg[i].reshape(N_EXPERTS, d, de),
                              w_eu[i].reshape(N_EXPERTS, d, de), w_ed[i].reshape(N_EXPERTS, de, d)))
        x2 = _combine(y, plan, x1, route, p[i].reshape(t, -1), *_in_hbm(w_pg[i], w_pp[i]),
                      ln2_g[i][None, :], ln2_b[i][None, :], alpha)
    return x2.reshape(bsz, seq, d)
```

```python
import functools
import math

import jax
import jax.numpy as jnp
from jax import lax
from jax.experimental import pallas as pl
from jax.experimental.pallas import tpu as pltpu

F32 = jnp.float32
BF16 = jnp.bfloat16

LANES = 128
POOL_WINDOWS = (2, 4, 8, 16)
POOL_HALO = 16
HEAD_DIM = 64
N_GROUPS = 4
EXPERTS_PER_GROUP = 8
N_EXPERTS = N_GROUPS * EXPERTS_PER_GROUP
LN_EPS = 1e-5
GROUP_LANE0 = N_EXPERTS
NEG_BIG = -1e30
ROUTE_BLOCK = 512
CHUNK = 16
EXPERT_TILE = 256
MACRO_TILES = 6
SORT_ROWS = 256
ROW_SUBS = 2
ROUTE_POS_LO, ROUTE_POS_HI, ROUTE_W_LO, ROUTE_W_HI = range(4)
LOCAL_ROWS = -(-(2 * ROUTE_BLOCK + N_EXPERTS * (CHUNK - 1)) // SORT_ROWS) * SORT_ROWS
SORT_HEIGHTS = (LOCAL_ROWS - SORT_ROWS, LOCAL_ROWS - SORT_ROWS // 2, LOCAL_ROWS)
ATTN_DEAD_LOG2 = -160.0
VMEM_CAP_BYTES = 56 * 1024 * 1024


def _params(sem, vmem_bytes):
    return pltpu.CompilerParams(
        dimension_semantics=sem, vmem_limit_bytes=min(int(vmem_bytes), VMEM_CAP_BYTES))


def _layer_norm(h, g, b):
    mu = jnp.mean(h, axis=-1, keepdims=True)
    c = h - mu
    var = jnp.mean(c * c, axis=-1, keepdims=True)
    return c * lax.rsqrt(var + LN_EPS) * g + b


def _sigmoid(z):
    return 1.0 / (1.0 + jnp.exp(-z))


def _row_subs(rows):
    return [slice(k * (rows // ROW_SUBS), (k + 1) * (rows // ROW_SUBS)) for k in range(ROW_SUBS)]


def _skewed(subs, first, second):
    out, pending = [], None
    for r in subs:
        mid = first(r)
        if pending is not None:
            out.append(second(*pending))
        pending = (r, mid)
    out.append(second(*pending))
    return out


def _in_hbm(*arrays):
    if not all(isinstance(a, jax.core.Tracer) for a in arrays):
        return list(arrays)
    return [pltpu.with_memory_space_constraint(a, pltpu.HBM) for a in arrays]


def _resident(shape):
    return pl.BlockSpec(shape, lambda *_: (0,) * len(shape), pipeline_mode=pl.Buffered(1))


def _resident_columns(rows, cols, block):
    return pl.BlockSpec((rows, cols), lambda *_: (0, block), pipeline_mode=pl.Buffered(1))


def _cast_once(w_ref, wb_ref):
    @pl.when(pl.program_id(0) == 0)
    def _():
        wb_ref[...] = w_ref[...].astype(BF16)


def _proj_kernel(x_ref, w_ref, wp_ref, sc_ref, po_ref, q_ref, k_ref, v_ref, wb_ref, halo_ref,
                 *, width, tiles_per_seq):
    q_scale = math.log2(math.e) / math.sqrt(HEAD_DIM)
    tile_in_seq = pl.program_id(0) % tiles_per_seq
    _cast_once(w_ref, wb_ref)
    xb = x_ref[...].astype(BF16)

    def mm(lo):
        return jnp.dot(xb, wb_ref[:, lo:lo + width], preferred_element_type=F32)

    @pl.when(tile_in_seq == 0)
    def _():
        halo_ref[...] = jnp.zeros_like(halo_ref)

    u = mm(0)
    tm = u.shape[0]
    gd = wp_ref.shape[1]
    pos = tile_in_seq * tm + lax.broadcasted_iota(jnp.int32, (tm, gd), 0)

    def pool_group(g, w):
        cols = slice(g * gd, (g + 1) * gd)
        ug = u[:, cols]
        s = jnp.concatenate([halo_ref[:, cols], ug], axis=0)
        sh = 1
        while sh < w:
            s = s + pltpu.roll(s, sh, axis=0)
            sh *= 2
        cnt = jnp.minimum(pos + 1, w).astype(F32)
        pooled = s[POOL_HALO:, :] / cnt - ug
        mixed = jnp.dot(pooled.astype(BF16), wp_ref[g].astype(BF16), preferred_element_type=F32)
        po_ref[:, cols] = (mixed * sc_ref[:, cols]).astype(BF16)

    def write_q():
        q_ref[...] = (mm(width) * q_scale).astype(BF16)

    def write_k():
        k_ref[...] = mm(2 * width).astype(BF16)

    def write_v():
        v_ref[...] = mm(3 * width).astype(BF16)

    pool_group(0, POOL_WINDOWS[0])
    for n, matmul in enumerate((write_q, write_k, write_v)):
        matmul()
        pool_group(n + 1, POOL_WINDOWS[n + 1])
    halo_ref[...] = u[tm - POOL_HALO:, :]


def _proj(x2, w_in, w_pool, pool_scale, seq, tm):
    t, d = x2.shape
    width = d // 2
    n = 4 * width
    groups, gd, _ = w_pool.shape
    assert seq % tm == 0 and groups * gd == width and max(POOL_WINDOWS) <= POOL_HALO + 1
    assert len(POOL_WINDOWS) == 4 and w_in.shape[1] % n == 0
    row = lambda i: (i, 0)
    blocks = (2 * (tm * d * 4 + tm * width * 4 * 2)
              + d * n * (4 + 2) + groups * gd * gd * 4 + width * 4 + POOL_HALO * width * 4)
    temps = tm * d * 2 + 3 * tm * width * 4 + 6 * (tm + POOL_HALO) * gd * 4
    return pl.pallas_call(
        functools.partial(_proj_kernel, width=width, tiles_per_seq=seq // tm),
        grid=(t // tm,),
        in_specs=[pl.BlockSpec((tm, d), row), _resident_columns(d, n, 0),
                  _resident((groups, gd, gd)), _resident((1, width))],
        out_specs=[pl.BlockSpec((tm, width), row)] * 4,
        out_shape=[jax.ShapeDtypeStruct((t, width), BF16)] * 4,
        scratch_shapes=[pltpu.VMEM((d, n), BF16), pltpu.VMEM((POOL_HALO, width), F32)],
        compiler_params=_params(("arbitrary",), blocks + temps),
        name="proj",
    )(x2, w_in, w_pool, pool_scale)


def _attn_kernel(q_ref, k_ref, v_ref, o_ref, rem_ref, acc_ref, *, tb):
    pairs = q_ref.shape[1] // LANES
    blocks_per_step = q_ref.shape[0] // tb
    col_blocks = [slice(p * LANES, (p + 1) * LANES) for p in range(pairs)]
    first_head = lax.broadcasted_iota(jnp.int32, (tb, LANES), 1) < HEAD_DIM
    r = lax.broadcasted_iota(jnp.int32, (2 * tb, 2 * tb), 0)
    c = lax.broadcasted_iota(jnp.int32, (2 * tb, 2 * tb), 1)
    same_head = (r >= tb) == (c >= tb)
    cum = -jnp.concatenate([(same_head & (r >= c)).astype(BF16), same_head.astype(BF16)], axis=1)
    kcol = lax.broadcasted_iota(jnp.int32, (tb, 2 * tb), 1)
    kcol = jnp.where(kcol >= tb, kcol - tb, kcol)
    causal = kcol < lax.broadcasted_iota(jnp.int32, (tb, 2 * tb), 0)

    def stack_heads(blk):
        zero = jnp.zeros_like(blk)
        return jnp.concatenate(
            [jnp.where(first_head, blk, zero), jnp.where(first_head, zero, blk)], axis=0)

    def scores(q0, j, rows, diagonal):
        ks = pl.multiple_of(j * tb, tb)
        q_rows = pl.ds(q0 + rows.start, rows.stop - rows.start)
        zs = [lax.dot_general(q_ref[q_rows, cols], stack_heads(k_ref[pl.ds(ks, tb), cols]),
                              (((1,), (1,)), ((), ())), preferred_element_type=F32)
              for cols in col_blocks]
        sums = []
        for z in zs:
            softplus = jnp.maximum(z, 0.0) + jnp.log2(1.0 + jnp.exp2(-jnp.abs(z)))
            if diagonal:
                softplus = jnp.where(causal, softplus, 0.0)
            sums.append(jnp.dot(softplus.astype(BF16), cum, preferred_element_type=F32))
        return zs, sums

    def weighted_values(j, cols, z, later):
        ks = pl.multiple_of(j * tb, tb)
        a = jnp.exp2(z + later)
        return jnp.dot(a.astype(BF16), stack_heads(v_ref[pl.ds(ks, tb), cols]),
                       preferred_element_type=F32)

    def first_step(qi, q0, n_before):
        full, half = slice(0, tb), slice(0, tb // 2)
        z_d, sums_d = scores(q0, qi, full, diagonal=True)
        if n_before >= 1:
            z_1, sums_1 = scores(q0, qi - 1, full, diagonal=False)
        if n_before >= 2:
            z_2, sums_2 = scores(q0, qi - 2, half, diagonal=False)
        owed = []
        for p, cols in enumerate(col_blocks):
            ks = pl.multiple_of(qi * tb, tb)
            a = jnp.where(causal, jnp.exp2(z_d[p] + sums_d[p][:, :2 * tb]), 0.0)
            out = jnp.dot(a.astype(BF16), stack_heads(v_ref[pl.ds(ks, tb), cols]),
                          preferred_element_type=F32)
            total = sums_d[p][:, 2 * tb:]
            if n_before >= 1:
                out = out + weighted_values(qi - 1, cols, z_1[p], sums_1[p][:, :2 * tb] + total)
                total = total + sums_1[p][:, 2 * tb:]
            acc_ref[p] = out
            rem_ref[p] = total
            if n_before >= 2:
                owed.append(total[tb // 2:, :])
                later = sums_2[p][:, :2 * tb] + total[half, :]
                acc_ref[p, half, :] += weighted_values(qi - 2, cols, z_2[p], later)
                rem_ref[p, half, :] += sums_2[p][:, 2 * tb:]
        if n_before < 2:
            return None
        return jnp.max(functools.reduce(jnp.maximum, owed)) > ATTN_DEAD_LOG2

    def step(q0, j, rows):
        zs, sums = scores(q0, j, rows, diagonal=False)
        for p, cols in enumerate(col_blocks):
            later = sums[p][:, :2 * tb] + rem_ref[p, rows, :]
            acc_ref[p, rows, :] += weighted_values(j, cols, zs[p], later)
            rem_ref[p, rows, :] += sums[p][:, 2 * tb:]

    def live_rows():
        worst = functools.reduce(jnp.maximum, [rem_ref[p] for p in range(pairs)])
        live = jnp.max(worst, axis=1, keepdims=True) > ATTN_DEAD_LOG2
        row = lax.broadcasted_iota(jnp.int32, (tb, 1), 0)
        return jnp.max(jnp.where(live, row + 1, 0))

    row_counts = (tb, tb // 2, tb // 4)

    def one_block(s, carry):
        qi = pl.program_id(1) * blocks_per_step + s
        q0 = pl.multiple_of(s * tb, tb)
        for n_before in (0, 1):
            @pl.when(qi == n_before)
            def _():
                first_step(qi, q0, n_before)

        @pl.when(qi >= 2)
        def _():
            second_half_owed = first_step(qi, q0, 2)

            @pl.when(second_half_owed)
            def _():
                step(q0, qi - 2, slice(tb // 2, tb))

        def cond(state):
            j, n_live = state
            return (j >= 0) & (n_live > 0)

        def body(state):
            j, n_live = state
            for rows, fewer in zip(row_counts, row_counts[1:] + (0,)):
                @pl.when((n_live > fewer) & (n_live <= rows))
                def _():
                    step(q0, j, slice(0, rows))

            return j - 1, live_rows()

        lax.while_loop(cond, body, (qi - 3, live_rows()))
        for p, cols in enumerate(col_blocks):
            o_ref[pl.ds(q0, tb), cols] = acc_ref[p].astype(BF16)
        return carry

    lax.fori_loop(0, blocks_per_step, one_block, 0)


def _attention(q, k, v, seq, tb, blocks_per_step):
    t, width = q.shape
    pairs = width // LANES
    tq = tb * blocks_per_step
    blocks = 2 * (2 * tq * width * 2 + 2 * seq * width * 2) + pairs * tb * (2 * tb + LANES) * 4
    temps = pairs * 10 * tb * 2 * tb * 4 + 2 * tb * 4 * tb * 2
    qspec = pl.BlockSpec((tq, width), lambda b, i: (b * (seq // tq) + i, 0))
    kvspec = pl.BlockSpec((seq, width), lambda b, i: (b, 0))
    return pl.pallas_call(
        functools.partial(_attn_kernel, tb=tb),
        grid=(t // seq, seq // tq),
        in_specs=[qspec, kvspec, kvspec],
        out_specs=qspec,
        out_shape=jax.ShapeDtypeStruct((t, width), BF16),
        scratch_shapes=[pltpu.VMEM((pairs, tb, 2 * tb), F32), pltpu.VMEM((pairs, tb, LANES), F32)],
        input_output_aliases={0: 0},
        compiler_params=_params(("parallel", "parallel"), blocks + temps),
        name="attn",
    )(q, k, v)


def _route(logits):
    lane = lax.broadcasted_iota(jnp.int32, logits.shape, 1)

    def first_max(vals):
        m = jnp.max(vals, axis=1, keepdims=True)
        idx = jnp.min(jnp.where(vals == m, lane, LANES), axis=1, keepdims=True)
        return m, idx

    is_group = (lane >= GROUP_LANE0) & (lane < GROUP_LANE0 + N_GROUPS)
    gm, g_lane = first_max(jnp.where(is_group, logits, NEG_BIG))
    g_prob = 1.0 / jnp.sum(jnp.where(is_group, jnp.exp(logits - gm), 0.0), axis=1, keepdims=True)
    lo = EXPERTS_PER_GROUP * (g_lane - GROUP_LANE0)
    in_group = jnp.where((lane >= lo) & (lane < lo + EXPERTS_PER_GROUP), logits, NEG_BIG)
    m1, i1 = first_max(in_group)
    m2, i2 = first_max(jnp.where(lane == i1, NEG_BIG, in_group))
    e21 = jnp.exp(m2 - m1)
    w1 = g_prob / (1.0 + e21)
    w2 = w1 * e21
    comb = jnp.where(lane == i1, w1, 0.0) + jnp.where(lane == i2, w2, 0.0)
    sel = ((lane == i1) | (lane == i2)).astype(F32)
    return comb, sel


def _merge_kernel(x_ref, po_ref, at_ref, wg_ref, wpu_ref, wau_ref, wo_ref, lg_ref, lb_ref,
                  wr_ref, br_ref, x1_ref, x1b_ref, comb_ref, sel_ref, cnt_ref,
                  wgb_ref, wpub_ref, waub_ref, wob_ref, wrb_ref, *, alpha):
    tm, d = x_ref.shape
    _cast_once(wg_ref, wgb_ref)
    _cast_once(wpu_ref, wpub_ref)
    _cast_once(wau_ref, waub_ref)
    _cast_once(wo_ref, wob_ref)

    @pl.when(pl.program_id(0) == 0)
    def _():
        w_r = wr_ref[...]
        hi = w_r.astype(BF16)
        wrb_ref[:, :LANES] = hi
        wrb_ref[:, LANES:] = (w_r - hi.astype(F32)).astype(BF16)

    def mix(r):
        x = x_ref[r, :]
        xb = x.astype(BF16)
        a = jnp.dot(po_ref[r, :], wpub_ref[...], preferred_element_type=F32)
        gate_a = _sigmoid(jnp.dot(xb, wgb_ref[:, :d], preferred_element_type=F32))
        b = jnp.dot(at_ref[r, :], waub_ref[...], preferred_element_type=F32)
        gate_b = _sigmoid(jnp.dot(xb, wgb_ref[:, d:], preferred_element_type=F32))
        merged = (gate_a * a + gate_b * b).astype(BF16)
        return alpha * x + jnp.dot(merged, wob_ref[...], preferred_element_type=F32)

    def norm_and_route(r, h):
        x1 = _layer_norm(h, lg_ref[...], lb_ref[...])
        xh = x1.astype(BF16)
        x1_ref[r, :] = x1
        x1b_ref[r, :] = xh
        xl = (x1 - xh.astype(F32)).astype(BF16)
        by_hi = jnp.dot(xh, wrb_ref[...], preferred_element_type=F32)
        by_lo = jnp.dot(xl, wrb_ref[:, :LANES], preferred_element_type=F32)
        comb, sel = _route(by_hi[:, :LANES] + by_hi[:, LANES:] + by_lo + br_ref[...])
        comb_ref[r, :] = comb
        sel_ref[r, :] = sel.astype(BF16)
        return jnp.sum(sel, axis=0, keepdims=True)

    cnt_ref[0] = sum(_skewed(_row_subs(tm), mix, norm_and_route))


def _merge(x2, pool_out, attn_out, w_in, w_pu, w_au, w_o, ln_g, ln_b, w_r, b_r, alpha, tm):
    t, d = x2.shape
    width = pool_out.shape[1]
    assert w_in.shape[1] == 4 * width + 2 * d
    row = lambda i: (i, 0)
    blocks = (2 * (tm * d * 4 + 2 * tm * width * 2
                   + tm * d * 4 + tm * d * 2 + tm * LANES * (4 + 2) + LANES * 4)
              + (2 * d * d + 2 * width * d + d * d + d * LANES) * (4 + 2) + 2 * d * 4 + LANES * 4)
    temps = 7 * tm * d * 4
    return pl.pallas_call(
        functools.partial(_merge_kernel, alpha=alpha),
        grid=(t // tm,),
        in_specs=[pl.BlockSpec((tm, d), row), pl.BlockSpec((tm, width), row),
                  pl.BlockSpec((tm, width), row), _resident_columns(d, 2 * d, 1),
                  _resident((width, d)), _resident((width, d)), _resident((d, d)),
                  _resident((1, d)), _resident((1, d)), _resident((d, LANES)),
                  _resident((1, LANES))],
        out_specs=[pl.BlockSpec((tm, d), row), pl.BlockSpec((tm, d), row),
                   pl.BlockSpec((tm, LANES), row), pl.BlockSpec((tm, LANES), row),
                   pl.BlockSpec((1, 1, LANES), lambda i: (i, 0, 0))],
        out_shape=[jax.ShapeDtypeStruct((t, d), F32), jax.ShapeDtypeStruct((t, d), BF16),
                   jax.ShapeDtypeStruct((t, LANES), F32), jax.ShapeDtypeStruct((t, LANES), BF16),
                   jax.ShapeDtypeStruct((t // tm, 1, LANES), F32)],
        scratch_shapes=[pltpu.VMEM((d, 2 * d), BF16), pltpu.VMEM((width, d), BF16),
                        pltpu.VMEM((width, d), BF16), pltpu.VMEM((d, d), BF16),
                        pltpu.VMEM((d, 2 * LANES), BF16)],
        compiler_params=_params(("arbitrary",), blocks + temps),
        name="merge",
    )(x2, pool_out, attn_out, w_in, w_pu, w_au, w_o, ln_g, ln_b, w_r, b_r)


def _routing_plan(counts):
    cnt = counts[:, 0, :N_EXPERTS].astype(jnp.int32)
    nch = (cnt + (CHUNK - 1)) // CHUNK
    local_end = jnp.cumsum(nch, axis=1)
    local_start = local_end - nch
    block_chunks = local_end[:, -1]
    before_block = jnp.cumsum(nch, axis=0) - nch
    expert_chunks = jnp.sum(nch, axis=0)
    chunks_per_tile = EXPERT_TILE // CHUNK
    expert_tiles = (expert_chunks + (chunks_per_tile - 1)) // chunks_per_tile
    tiles_end = jnp.cumsum(expert_tiles)
    region_start = (tiles_end - expert_tiles) * chunks_per_tile
    segment_dst = region_start[None, :] + before_block
    i32 = lambda a: a.astype(jnp.int32)
    return dict(seg_len=i32(nch), seg_src=i32(local_start), seg_dst=i32(segment_dst),
                block_chunks=i32(block_chunks),
                expert_tiles=i32(expert_tiles), tiles_used=i32(tiles_end[-1:]),
                pad_start=i32(region_start + expert_chunks),
                pad_count=i32(expert_tiles * chunks_per_tile - expert_chunks))


def _local_positions(sel):
    tb = sel.shape[0]
    earlier = (lax.broadcasted_iota(jnp.int32, (tb, tb), 1)
               < lax.broadcasted_iota(jnp.int32, (tb, tb), 0)).astype(BF16)
    rank = jnp.dot(earlier, sel, preferred_element_type=F32)
    cnt = jnp.sum(sel.astype(F32), axis=0, keepdims=True)
    nch = jnp.floor((cnt + (CHUNK - 1)) * (1.0 / CHUNK))
    lower = (lax.broadcasted_iota(jnp.int32, (LANES, LANES), 0)
             < lax.broadcasted_iota(jnp.int32, (LANES, LANES), 1)).astype(BF16)
    start = CHUNK * jnp.dot(jnp.broadcast_to(nch, (8, LANES)).astype(BF16), lower,
                            preferred_element_type=F32)[0:1]
    pos = rank + start
    chosen = sel > 0
    pos_lo = jnp.min(jnp.where(chosen, pos, float(LOCAL_ROWS)), axis=1, keepdims=True)
    pos_hi = jnp.max(jnp.where(chosen, pos, -1.0), axis=1, keepdims=True)
    return pos, pos_lo, pos_hi


def _for_each(n, fn):
    lax.fori_loop(0, n, lambda c, carry: (fn(c), carry)[1], 0)


def _segment_rows(len_ref, src_ref, dst_ref, blk, e):
    n = len_ref[blk, e] * CHUNK
    src = pl.multiple_of(src_ref[blk, e] * CHUNK, CHUNK)
    dst = pl.multiple_of(dst_ref[blk, e] * CHUNK, CHUNK)
    return n, src, dst


def _scatter_kernel(len_ref, src_ref, dst_ref, nchunk_ref, pad_start_ref, pad_count_ref, used_ref,
                    x_ref, sel_ref, comb_ref, g_ref, route_ref, loc_ref, zero_ref, sems):
    b = pl.program_id(0)
    last = pl.num_programs(0) - 1
    slot = b % 2
    tb = x_ref.shape[0]

    def start_block(blk, slot):
        def start_segment(e):
            n, src, dst = _segment_rows(len_ref, src_ref, dst_ref, blk, e)

            @pl.when(n > 0)
            def _():
                pltpu.make_async_copy(loc_ref.at[slot, pl.ds(src, n)], g_ref.at[pl.ds(dst, n)],
                                      sems.at[slot]).start()

        _for_each(N_EXPERTS, start_segment)

    def wait_block(blk, slot):
        n = nchunk_ref[blk] * CHUNK
        pltpu.make_async_copy(loc_ref.at[slot, pl.ds(0, n)], g_ref.at[pl.ds(0, n)],
                              sems.at[slot]).wait()

    def pad_copy(e):
        n = pad_count_ref[e] * CHUNK
        dst = pl.multiple_of(pad_start_ref[e] * CHUNK, CHUNK)
        return n, pltpu.make_async_copy(zero_ref.at[pl.ds(0, n)], g_ref.at[pl.ds(dst, n)],
                                        sems.at[2])

    def unused_tile_copy(i):
        dst = pl.multiple_of(i * EXPERT_TILE, EXPERT_TILE)
        return pltpu.make_async_copy(zero_ref, g_ref.at[pl.ds(dst, EXPERT_TILE)], sems.at[2])

    @pl.when(b >= 2)
    def _():
        wait_block(b - 2, slot)

    sel = sel_ref[...]
    pos, pos_lo, pos_hi = _local_positions(sel)
    lo = pos_lo.astype(jnp.int32)
    hi = pos_hi.astype(jnp.int32)
    x = x_ref[...]
    chosen = sel > 0
    comb = comb_ref[...]
    w_lo = jnp.sum(jnp.where(chosen & (pos == pos_lo), comb, 0.0), axis=1, keepdims=True)
    w_hi = jnp.sum(jnp.where(chosen & (pos == pos_hi), comb, 0.0), axis=1, keepdims=True)
    lane = lax.broadcasted_iota(jnp.int32, comb.shape, 1)
    route_ref[...] = jnp.where(
        lane == ROUTE_POS_LO, pos_lo, jnp.where(
            lane == ROUTE_POS_HI, pos_hi, jnp.where(
                lane == ROUTE_W_LO, w_lo, jnp.where(lane == ROUTE_W_HI, w_hi, 0.0))))

    def sort_rows(height):
        r = lax.broadcasted_iota(jnp.int32, (tb, height), 1)
        perm = ((r == lo) | (r == hi)).astype(BF16)
        rows = lax.dot_general(perm, x, (((0,), (0,)), ((), ())), preferred_element_type=F32)
        loc_ref[slot, :height, :] = rows.astype(BF16)

    n_rows = nchunk_ref[b] * CHUNK
    for below, height in zip((0,) + SORT_HEIGHTS[:-1], SORT_HEIGHTS):
        @pl.when((n_rows > below) & (n_rows <= height))
        def _():
            sort_rows(height)

    start_block(b, slot)

    @pl.when(b == last)
    def _():
        zero_ref[...] = jnp.zeros_like(zero_ref)
        n_unused = g_ref.shape[0] // EXPERT_TILE - used_ref[0]

        def each_pad(act):
            def one(e):
                n, copy = pad_copy(e)

                @pl.when(n > 0)
                def _():
                    act(copy)

            _for_each(N_EXPERTS, one)

        each_pad(lambda copy: copy.start())
        _for_each(n_unused, lambda i: unused_tile_copy(used_ref[0] + i).start())
        each_pad(lambda copy: copy.wait())
        _for_each(n_unused, lambda i: unused_tile_copy(used_ref[0] + i).wait())

        @pl.when(b >= 1)
        def _():
            wait_block(b - 1, 1 - slot)

        wait_block(b, slot)


def _scatter(x1b, sel, comb, plan, n_tiles):
    t, d = x1b.shape
    tb = ROUTE_BLOCK
    row = lambda b, *_: (b, 0)
    blocks = (2 * (tb * d * 2 + tb * LANES * (2 + 4 + 4)) + 2 * LOCAL_ROWS * d * 2
              + EXPERT_TILE * d * 2)
    temps = tb * tb * 2 + 6 * tb * LANES * 4 + tb * LOCAL_ROWS * 6 + LOCAL_ROWS * d * 6
    return pl.pallas_call(
        _scatter_kernel,
        grid_spec=pltpu.PrefetchScalarGridSpec(
            num_scalar_prefetch=7,
            grid=(t // tb,),
            in_specs=[pl.BlockSpec((tb, d), row), pl.BlockSpec((tb, LANES), row),
                      pl.BlockSpec((tb, LANES), row)],
            out_specs=[pl.BlockSpec(memory_space=pl.ANY), pl.BlockSpec((tb, LANES), row)],
            scratch_shapes=[pltpu.VMEM((2, LOCAL_ROWS, d), BF16),
                            pltpu.VMEM((EXPERT_TILE, d), BF16), pltpu.SemaphoreType.DMA((3,))]),
        out_shape=[jax.ShapeDtypeStruct((n_tiles * EXPERT_TILE, d), BF16),
                   jax.ShapeDtypeStruct((t, LANES), F32)],
        input_output_aliases={9: 1},
        compiler_params=_params(("arbitrary",), blocks + temps),
        name="scatter",
    )(plan["seg_len"], plan["seg_src"], plan["seg_dst"], plan["block_chunks"], plan["pad_start"],
      plan["pad_count"], plan["tiles_used"], x1b, sel, comb)


def _experts_kernel(ntile_ref, used_ref, g_ref, wg_ref, wu_ref, wd_ref, y_ref,
                    x_buf, y_buf, wg_buf, wu_buf, wd_buf, wgb_ref, wub_ref, wdb_ref,
                    x_sems, y_sems, w_sems):
    n_experts = wg_ref.shape[0]
    tm = EXPERT_TILE
    used = used_ref[0]

    def weight_copies(e, slot):
        return [pltpu.make_async_copy(src.at[e], dst.at[slot], w_sems.at[slot])
                for src, dst in ((wg_ref, wg_buf), (wu_ref, wu_buf), (wd_ref, wd_buf))]

    def x_copy(t, slot):
        rows = pl.ds(pl.multiple_of(t * tm, tm), MACRO_TILES * tm)
        return pltpu.make_async_copy(g_ref.at[rows], x_buf.at[slot], x_sems.at[slot])

    def y_copy(t, k, slot):
        n = k * tm
        dst = pl.ds(pl.multiple_of(t * tm, tm), n)
        return pltpu.make_async_copy(y_buf.at[slot, pl.ds(0, n)], y_ref.at[dst], y_sems.at[slot])

    def wait_y(t, k, slot):
        @pl.when(k > 0)
        def _():
            y_copy(t, k, slot).wait()

    def mlp(slot, rows):
        x = x_buf[slot, :rows, :]
        hg = jnp.dot(x, wgb_ref[...], preferred_element_type=F32)
        hu = jnp.dot(x, wub_ref[...], preferred_element_type=F32)
        h = hg * _sigmoid(hg) * hu
        y_buf[slot, :rows, :] = jnp.dot(h.astype(BF16), wdb_ref[...],
                                        preferred_element_type=F32).astype(BF16)

    for c in weight_copies(0, 0):
        c.start()
    x_copy(0, 0).start()

    def run_expert(e, carry):
        wslot = e % 2
        for c in weight_copies(e, wslot):
            c.wait()

        @pl.when(e + 1 < n_experts)
        def _():
            for c in weight_copies(e + 1, 1 - wslot):
                c.start()

        @pl.when(ntile_ref[e] > 0)
        def _():
            wgb_ref[...] = wg_buf[wslot].astype(BF16)
            wub_ref[...] = wu_buf[wslot].astype(BF16)
            wdb_ref[...] = wd_buf[wslot].astype(BF16)

        def run_macro(m, carry):
            t, step, k1, t1, k2, t2 = carry
            k = jnp.minimum(MACRO_TILES, ntile_ref[e] - m * MACRO_TILES)
            slot = step % 2
            x_copy(t, slot).wait()

            @pl.when(t + k < used)
            def _():
                x_copy(t + k, 1 - slot).start()

            wait_y(t2, k2, slot)
            for tiles in range(1, MACRO_TILES + 1):
                @pl.when(k == tiles)
                def _():
                    mlp(slot, tiles * tm)

            y_copy(t, k, slot).start()
            return t + k, step + 1, k, t, k1, t1

        n_macro = (ntile_ref[e] + (MACRO_TILES - 1)) // MACRO_TILES
        return lax.fori_loop(0, n_macro, run_macro, carry)

    zero = jnp.int32(0)
    _, step, k1, t1, k2, t2 = lax.fori_loop(0, n_experts, run_expert, (zero,) * 6)
    wait_y(t2, k2, step % 2)
    wait_y(t1, k1, (step + 1) % 2)
    y_buf[0, :tm, :] = jnp.zeros((tm, y_buf.shape[2]), BF16)
    n_unused = y_ref.shape[0] // tm - used
    _for_each(n_unused, lambda i: y_copy(used + i, 1, 0).start())
    _for_each(n_unused, lambda i: y_copy(used + i, 1, 0).wait())


def _experts(sorted_x, plan, w_eg, w_eu, w_ed):
    rows, d = sorted_x.shape
    _, _, de = w_eg.shape
    tm = EXPERT_TILE
    any_space = pl.BlockSpec(memory_space=pl.ANY)
    big = MACRO_TILES * tm
    scratch = 2 * 2 * big * d * 2 + 2 * 3 * d * de * 4 + 3 * d * de * 2
    temps = 3 * big * de * 4 + big * d * 4 + d * de * 4
    return pl.pallas_call(
        _experts_kernel,
        grid_spec=pltpu.PrefetchScalarGridSpec(
            num_scalar_prefetch=2,
            grid=(1,),
            in_specs=[any_space] * 4,
            out_specs=any_space,
            scratch_shapes=[pltpu.VMEM((2, big, d), BF16), pltpu.VMEM((2, big, d), BF16),
                            pltpu.VMEM((2, d, de), F32), pltpu.VMEM((2, d, de), F32),
                            pltpu.VMEM((2, de, d), F32),
                            pltpu.VMEM((d, de), BF16), pltpu.VMEM((d, de), BF16),
                            pltpu.VMEM((de, d), BF16),
                            pltpu.SemaphoreType.DMA((2,)), pltpu.SemaphoreType.DMA((2,)),
                            pltpu.SemaphoreType.DMA((2,))]),
        out_shape=jax.ShapeDtypeStruct((rows, d), BF16),
        compiler_params=_params(("arbitrary",), scratch + temps),
        name="experts",
    )(plan["expert_tiles"], plan["tiles_used"], sorted_x, w_eg, w_eu, w_ed)


def _combine_kernel(len_ref, src_ref, dst_ref, nchunk_ref, x1_ref, route_ref, p_ref,
                    wpg_ref, wpp_ref,
                    lg_ref, lb_ref, y_ref, o_ref, loc_ref, wpgb_ref, wppb_ref, sems, *, alpha):
    b = pl.program_id(0)
    nb = pl.num_programs(0)
    slot = b % 2
    tb, d = x1_ref.shape
    _cast_once(wpg_ref, wpgb_ref)
    _cast_once(wpp_ref, wppb_ref)

    def start_block(blk, slot):
        def start_segment(e):
            n, local, sorted_at = _segment_rows(len_ref, src_ref, dst_ref, blk, e)

            @pl.when(n > 0)
            def _():
                pltpu.make_async_copy(y_ref.at[pl.ds(sorted_at, n)],
                                      loc_ref.at[slot, pl.ds(local, n)], sems.at[slot]).start()

        _for_each(N_EXPERTS, start_segment)

    @pl.when(b == 0)
    def _():
        loc_ref[...] = jnp.zeros_like(loc_ref)
        start_block(0, 0)

    @pl.when(b + 1 < nb)
    def _():
        start_block(b + 1, 1 - slot)

    n_rows = nchunk_ref[b] * CHUNK
    pltpu.make_async_copy(y_ref.at[pl.ds(0, n_rows)], loc_ref.at[slot, pl.ds(0, n_rows)],
                          sems.at[slot]).wait()

    lo = route_ref[:, ROUTE_POS_LO:ROUTE_POS_LO + 1].astype(jnp.int32)
    hi = route_ref[:, ROUTE_POS_HI:ROUTE_POS_HI + 1].astype(jnp.int32)
    w_lo = route_ref[:, ROUTE_W_LO:ROUTE_W_LO + 1]
    w_hi = route_ref[:, ROUTE_W_HI:ROUTE_W_HI + 1]

    def finish_block(height):
        sorted_row = lax.broadcasted_iota(jnp.int32, (tb // ROW_SUBS, height), 1)

        def branches(r):
            weights = (jnp.where(sorted_row == lo[r], w_lo[r], 0.0)
                       + jnp.where(sorted_row == hi[r], w_hi[r], 0.0)).astype(BF16)
            moe = jnp.dot(weights, loc_ref[slot, :height, :], preferred_element_type=F32)
            gate = jnp.dot(x1_ref[r, :].astype(BF16), wpgb_ref[...], preferred_element_type=F32)
            emb = jnp.dot(p_ref[r, :].astype(BF16), wppb_ref[...], preferred_element_type=F32)
            return moe, gate, emb

        def finish(r, parts):
            moe, gate, emb = parts
            h = alpha * x1_ref[r, :] + moe + _sigmoid(gate) * emb
            o_ref[r, :] = _layer_norm(h, lg_ref[...], lb_ref[...])

        _skewed(_row_subs(tb), branches, finish)

    for below, height in zip((0,) + SORT_HEIGHTS[:-1], SORT_HEIGHTS):
        @pl.when((n_rows > below) & (n_rows <= height))
        def _():
            finish_block(height)


def _combine(y, plan, x1, route, p2, w_pg, w_pp, ln_g, ln_b, alpha):
    t, d = x1.shape
    pd = p2.shape[1]
    tb = ROUTE_BLOCK
    row = lambda b, *_: (b, 0)
    blocks = (2 * (2 * tb * d * 4 + tb * LANES * 4 + tb * pd * 4)
              + (d * d + pd * d) * (4 + 2) + 2 * d * 4 + 2 * LOCAL_ROWS * d * 2)
    temps = tb * LOCAL_ROWS * 10 + 5 * tb * d * 4
    return pl.pallas_call(
        functools.partial(_combine_kernel, alpha=alpha),
        grid_spec=pltpu.PrefetchScalarGridSpec(
            num_scalar_prefetch=4,
            grid=(t // tb,),
            in_specs=[pl.BlockSpec((tb, d), row), pl.BlockSpec((tb, LANES), row),
                      pl.BlockSpec((tb, pd), row),
                      _resident((d, d)), _resident((pd, d)), _resident((1, d)), _resident((1, d)),
                      pl.BlockSpec(memory_space=pl.ANY)],
            out_specs=pl.BlockSpec((tb, d), row),
            scratch_shapes=[pltpu.VMEM((2, LOCAL_ROWS, d), BF16), pltpu.VMEM((d, d), BF16),
                            pltpu.VMEM((pd, d), BF16), pltpu.SemaphoreType.DMA((2,))]),
        out_shape=jax.ShapeDtypeStruct((t, d), F32),
        compiler_params=_params(("arbitrary",), blocks + temps),
        name="combine",
    )(plan["seg_len"], plan["seg_src"], plan["seg_dst"], plan["block_chunks"], x1, route, p2,
      w_pg, w_pp, ln_g, ln_b, y)


def kernel(x, p, w_in, w_pool, pool_scale, w_pu, w_au, w_o, ln1_g, ln1_b, w_rg, b_rg, w_re, b_re,
           w_eg, w_eu, w_ed, w_pg, w_pp, ln2_g, ln2_b):
    bsz, seq, d = x.shape
    depth = w_in.shape[0]
    t = bsz * seq
    de = w_eg.shape[-1]
    alpha = (2.0 * depth) ** 0.25
    assert w_rg.shape[2] == N_GROUPS and w_re.shape[1:] == (N_GROUPS, d, EXPERTS_PER_GROUP)
    assert w_in.shape[2] == 4 * d and w_pool.shape[1] == len(POOL_WINDOWS)
    assert t % ROUTE_BLOCK == 0 and N_EXPERTS + N_GROUPS <= LANES
    n_blocks = t // ROUTE_BLOCK
    n_tiles = -(-(2 * t + n_blocks * N_EXPERTS * (CHUNK - 1) + N_EXPERTS * (EXPERT_TILE - CHUNK))
                // EXPERT_TILE) + MACRO_TILES - 1

    x2 = x.reshape(t, d)
    for i in range(depth):
        pool_out, q, k, v = _in_hbm(
            *_proj(x2, w_in[i], w_pool[i], pool_scale[i][None, :], seq, tm=512))
        attn_out, = _in_hbm(_attention(q, k, v, seq, tb=128, blocks_per_step=8))

        w_r = jnp.concatenate(
            [w_re[i].transpose(1, 0, 2).reshape(d, N_EXPERTS), w_rg[i]], axis=1)
        w_r = jnp.pad(w_r, ((0, 0), (0, LANES - w_r.shape[1])))
        b_r = jnp.pad(jnp.concatenate([b_re[i].reshape(-1), b_rg[i]]),
                      (0, LANES - N_GROUPS - N_EXPERTS))[None, :]

        x1, x1b, comb, sel, counts = _merge(
            x2, pool_out, attn_out, w_in[i], *_in_hbm(w_pu[i], w_au[i], w_o[i]), ln1_g[i][None, :],
            ln1_b[i][None, :], w_r, b_r, alpha, tm=ROUTE_BLOCK)
        x1, x1b, comb, sel = _in_hbm(x1, x1b, comb, sel)

        plan = _routing_plan(counts)
        sorted_x, route = _in_hbm(*_scatter(x1b, sel, comb, plan, n_tiles))
        y, = _in_hbm(_experts(sorted_x, plan, w_eg[i].reshape(N_EXPERTS, d, de),
                              w_eu[i].reshape(N_EXPERTS, d, de), w_ed[i].reshape(N_EXPERTS, de, d)))
        x2 = _combine(y, plan, x1, route, p[i].reshape(t, -1), *_in_hbm(w_pg[i], w_pp[i]),
                      ln2_g[i][None, :], ln2_b[i][None, :], alpha)
    return x2.reshape(bsz, seq, d)
```

```python
import functools
import math

import jax
import jax.numpy as jnp
from jax import lax
from jax.experimental import pallas as pl
from jax.experimental.pallas import tpu as pltpu

F32 = jnp.float32
BF16 = jnp.bfloat16

LANES = 128
POOL_WINDOWS = (2, 4, 8, 16)
POOL_HALO = 16
HEAD_DIM = 64
N_GROUPS = 4
EXPERTS_PER_GROUP = 8
N_EXPERTS = N_GROUPS * EXPERTS_PER_GROUP
LN_EPS = 1e-5
GROUP_LANE0 = N_EXPERTS
NEG_BIG = -1e30
ROUTE_BLOCK = 512
CHUNK = 16
EXPERT_TILE = 256
MACRO_TILES = 6
SORT_ROWS = 256
ROW_SUBS = 2
ROUTE_POS_LO, ROUTE_POS_HI, ROUTE_W_LO, ROUTE_W_HI = range(4)
LOCAL_ROWS = -(-(2 * ROUTE_BLOCK + N_EXPERTS * (CHUNK - 1)) // SORT_ROWS) * SORT_ROWS
SORT_HEIGHTS = (LOCAL_ROWS - SORT_ROWS, LOCAL_ROWS - SORT_ROWS // 2, LOCAL_ROWS)
ATTN_DEAD_LOG2 = -160.0
VMEM_CAP_BYTES = 56 * 1024 * 1024


def _params(sem, vmem_bytes):
    return pltpu.CompilerParams(
        dimension_semantics=sem, vmem_limit_bytes=min(int(vmem_bytes), VMEM_CAP_BYTES))


def _layer_norm(h, g, b):
    mu = jnp.mean(h, axis=-1, keepdims=True)
    c = h - mu
    var = jnp.mean(c * c, axis=-1, keepdims=True)
    return c * lax.rsqrt(var + LN_EPS) * g + b


def _sigmoid(z):
    return 1.0 / (1.0 + jnp.exp(-z))


def _row_subs(rows):
    return [slice(k * (rows // ROW_SUBS), (k + 1) * (rows // ROW_SUBS)) for k in range(ROW_SUBS)]


def _skewed(subs, first, second):
    out, pending = [], None
    for r in subs:
        mid = first(r)
        if pending is not None:
            out.append(second(*pending))
        pending = (r, mid)
    out.append(second(*pending))
    return out


def _in_hbm(*arrays):
    if not all(isinstance(a, jax.core.Tracer) for a in arrays):
        return list(arrays)
    return [pltpu.with_memory_space_constraint(a, pltpu.HBM) for a in arrays]


def _resident(shape):
    return pl.BlockSpec(shape, lambda *_: (0,) * len(shape), pipeline_mode=pl.Buffered(1))


def _resident_columns(rows, cols, block):
    return pl.BlockSpec((rows, cols), lambda *_: (0, block), pipeline_mode=pl.Buffered(1))


def _cast_once(w_ref, wb_ref):
    @pl.when(pl.program_id(0) == 0)
    def _():
        wb_ref[...] = w_ref[...].astype(BF16)


def _proj_kernel(x_ref, w_ref, wp_ref, sc_ref, po_ref, q_ref, k_ref, v_ref, wb_ref, halo_ref,
                 *, width, tiles_per_seq):
    q_scale = math.log2(math.e) / math.sqrt(HEAD_DIM)
    tile_in_seq = pl.program_id(0) % tiles_per_seq
    _cast_once(w_ref, wb_ref)
    xb = x_ref[...].astype(BF16)

    def mm(lo):
        return jnp.dot(xb, wb_ref[:, lo:lo + width], preferred_element_type=F32)

    @pl.when(tile_in_seq == 0)
    def _():
        halo_ref[...] = jnp.zeros_like(halo_ref)

    u = mm(0)
    tm = u.shape[0]
    gd = wp_ref.shape[1]
    pos = tile_in_seq * tm + lax.broadcasted_iota(jnp.int32, (tm, gd), 0)

    def pool_group(g, w):
        cols = slice(g * gd, (g + 1) * gd)
        ug = u[:, cols]
        s = jnp.concatenate([halo_ref[:, cols], ug], axis=0)
        sh = 1
        while sh < w:
            s = s + pltpu.roll(s, sh, axis=0)
            sh *= 2
        cnt = jnp.minimum(pos + 1, w).astype(F32)
        pooled = s[POOL_HALO:, :] / cnt - ug
        mixed = jnp.dot(pooled.astype(BF16), wp_ref[g].astype(BF16), preferred_element_type=F32)
        po_ref[:, cols] = (mixed * sc_ref[:, cols]).astype(BF16)

    def write_q():
        q_ref[...] = (mm(width) * q_scale).astype(BF16)

    def write_k():
        k_ref[...] = mm(2 * width).astype(BF16)

    def write_v():
        v_ref[...] = mm(3 * width).astype(BF16)

    pool_group(0, POOL_WINDOWS[0])
    for n, matmul in enumerate((write_q, write_k, write_v)):
        matmul()
        pool_group(n + 1, POOL_WINDOWS[n + 1])
    halo_ref[...] = u[tm - POOL_HALO:, :]


def _proj(x2, w_in, w_pool, pool_scale, seq, tm):
    t, d = x2.shape
    width = d // 2
    n = 4 * width
    groups, gd, _ = w_pool.shape
    assert seq % tm == 0 and groups * gd == width and max(POOL_WINDOWS) <= POOL_HALO + 1
    assert len(POOL_WINDOWS) == 4 and w_in.shape[1] % n == 0
    row = lambda i: (i, 0)
    blocks = (2 * (tm * d * 4 + tm * width * 4 * 2)
              + d * n * (4 + 2) + groups * gd * gd * 4 + width * 4 + POOL_HALO * width * 4)
    temps = tm * d * 2 + 3 * tm * width * 4 + 6 * (tm + POOL_HALO) * gd * 4
    return pl.pallas_call(
        functools.partial(_proj_kernel, width=width, tiles_per_seq=seq // tm),
        grid=(t // tm,),
        in_specs=[pl.BlockSpec((tm, d), row), _resident_columns(d, n, 0),
                  _resident((groups, gd, gd)), _resident((1, width))],
        out_specs=[pl.BlockSpec((tm, width), row)] * 4,
        out_shape=[jax.ShapeDtypeStruct((t, width), BF16)] * 4,
        scratch_shapes=[pltpu.VMEM((d, n), BF16), pltpu.VMEM((POOL_HALO, width), F32)],
        compiler_params=_params(("arbitrary",), blocks + temps),
        name="proj",
    )(x2, w_in, w_pool, pool_scale)


def _attn_kernel(q_ref, k_ref, v_ref, o_ref, rem_ref, acc_ref, *, tb):
    pairs = q_ref.shape[1] // LANES
    blocks_per_step = q_ref.shape[0] // tb
    col_blocks = [slice(p * LANES, (p + 1) * LANES) for p in range(pairs)]
    first_head = lax.broadcasted_iota(jnp.int32, (tb, LANES), 1) < HEAD_DIM
    r = lax.broadcasted_iota(jnp.int32, (2 * tb, 2 * tb), 0)
    c = lax.broadcasted_iota(jnp.int32, (2 * tb, 2 * tb), 1)
    same_head = (r >= tb) == (c >= tb)
    cum = -jnp.concatenate([(same_head & (r >= c)).astype(BF16), same_head.astype(BF16)], axis=1)
    kcol = lax.broadcasted_iota(jnp.int32, (tb, 2 * tb), 1)
    kcol = jnp.where(kcol >= tb, kcol - tb, kcol)
    causal = kcol < lax.broadcasted_iota(jnp.int32, (tb, 2 * tb), 0)

    def stack_heads(blk):
        zero = jnp.zeros_like(blk)
        return jnp.concatenate(
            [jnp.where(first_head, blk, zero), jnp.where(first_head, zero, blk)], axis=0)

    def scores(q0, j, rows, diagonal):
        ks = pl.multiple_of(j * tb, tb)
        q_rows = pl.ds(q0 + rows.start, rows.stop - rows.start)
        zs = [lax.dot_general(q_ref[q_rows, cols], stack_heads(k_ref[pl.ds(ks, tb), cols]),
                              (((1,), (1,)), ((), ())), preferred_element_type=F32)
              for cols in col_blocks]
        sums = []
        for z in zs:
            softplus = jnp.maximum(z, 0.0) + jnp.log2(1.0 + jnp.exp2(-jnp.abs(z)))
            if diagonal:
                softplus = jnp.where(causal, softplus, 0.0)
            sums.append(jnp.dot(softplus.astype(BF16), cum, preferred_element_type=F32))
        return zs, sums

    def weighted_values(j, cols, z, later):
        ks = pl.multiple_of(j * tb, tb)
        a = jnp.exp2(z + later)
        return jnp.dot(a.astype(BF16), stack_heads(v_ref[pl.ds(ks, tb), cols]),
                       preferred_element_type=F32)

    def first_step(qi, q0, n_before):
        full, half = slice(0, tb), slice(0, tb // 2)
        z_d, sums_d = scores(q0, qi, full, diagonal=True)
        if n_before >= 1:
            z_1, sums_1 = scores(q0, qi - 1, full, diagonal=False)
        if n_before >= 2:
            z_2, sums_2 = scores(q0, qi - 2, half, diagonal=False)
        owed = []
        for p, cols in enumerate(col_blocks):
            ks = pl.multiple_of(qi * tb, tb)
            a = jnp.where(causal, jnp.exp2(z_d[p] + sums_d[p][:, :2 * tb]), 0.0)
            out = jnp.dot(a.astype(BF16), stack_heads(v_ref[pl.ds(ks, tb), cols]),
                          preferred_element_type=F32)
            total = sums_d[p][:, 2 * tb:]
            if n_before >= 1:
                out = out + weighted_values(qi - 1, cols, z_1[p], sums_1[p][:, :2 * tb] + total)
                total = total + sums_1[p][:, 2 * tb:]
            acc_ref[p] = out
            rem_ref[p] = total
            if n_before >= 2:
                owed.append(total[tb // 2:, :])
                later = sums_2[p][:, :2 * tb] + total[half, :]
                acc_ref[p, half, :] += weighted_values(qi - 2, cols, z_2[p], later)
                rem_ref[p, half, :] += sums_2[p][:, 2 * tb:]
        if n_before < 2:
            return None
        return jnp.max(functools.reduce(jnp.maximum, owed)) > ATTN_DEAD_LOG2

    def step(q0, j, rows):
        zs, sums = scores(q0, j, rows, diagonal=False)
        for p, cols in enumerate(col_blocks):
            later = sums[p][:, :2 * tb] + rem_ref[p, rows, :]
            acc_ref[p, rows, :] += weighted_values(j, cols, zs[p], later)
            rem_ref[p, rows, :] += sums[p][:, 2 * tb:]

    def live_rows():
        worst = functools.reduce(jnp.maximum, [rem_ref[p] for p in range(pairs)])
        live = jnp.max(worst, axis=1, keepdims=True) > ATTN_DEAD_LOG2
        row = lax.broadcasted_iota(jnp.int32, (tb, 1), 0)
        return jnp.max(jnp.where(live, row + 1, 0))

    row_counts = (tb, tb // 2, tb // 4)

    def one_block(s, carry):
        qi = pl.program_id(1) * blocks_per_step + s
        q0 = pl.multiple_of(s * tb, tb)
        for n_before in (0, 1):
            @pl.when(qi == n_before)
            def _():
                first_step(qi, q0, n_before)

        @pl.when(qi >= 2)
        def _():
            second_half_owed = first_step(qi, q0, 2)

            @pl.when(second_half_owed)
            def _():
                step(q0, qi - 2, slice(tb // 2, tb))

        def cond(state):
            j, n_live = state
            return (j >= 0) & (n_live > 0)

        def body(state):
            j, n_live = state
            for rows, fewer in zip(row_counts, row_counts[1:] + (0,)):
                @pl.when((n_live > fewer) & (n_live <= rows))
                def _():
                    step(q0, j, slice(0, rows))

            return j - 1, live_rows()

        lax.while_loop(cond, body, (qi - 3, live_rows()))
        for p, cols in enumerate(col_blocks):
            o_ref[pl.ds(q0, tb), cols] = acc_ref[p].astype(BF16)
        return carry

    lax.fori_loop(0, blocks_per_step, one_block, 0)


def _attention(q, k, v, seq, tb, blocks_per_step):
    t, width = q.shape
    pairs = width // LANES
    tq = tb * blocks_per_step
    blocks = 2 * (2 * tq * width * 2 + 2 * seq * width * 2) + pairs * tb * (2 * tb + LANES) * 4
    temps = pairs * 10 * tb * 2 * tb * 4 + 2 * tb * 4 * tb * 2
    qspec = pl.BlockSpec((tq, width), lambda b, i: (b * (seq // tq) + i, 0))
    kvspec = pl.BlockSpec((seq, width), lambda b, i: (b, 0))
    return pl.pallas_call(
        functools.partial(_attn_kernel, tb=tb),
        grid=(t // seq, seq // tq),
        in_specs=[qspec, kvspec, kvspec],
        out_specs=qspec,
        out_shape=jax.ShapeDtypeStruct((t, width), BF16),
        scratch_shapes=[pltpu.VMEM((pairs, tb, 2 * tb), F32), pltpu.VMEM((pairs, tb, LANES), F32)],
        input_output_aliases={0: 0},
        compiler_params=_params(("parallel", "parallel"), blocks + temps),
        name="attn",
    )(q, k, v)


def _route(logits):
    lane = lax.broadcasted_iota(jnp.int32, logits.shape, 1)

    def first_max(vals):
        m = jnp.max(vals, axis=1, keepdims=True)
        idx = jnp.min(jnp.where(vals == m, lane, LANES), axis=1, keepdims=True)
        return m, idx

    is_group = (lane >= GROUP_LANE0) & (lane < GROUP_LANE0 + N_GROUPS)
    gm, g_lane = first_max(jnp.where(is_group, logits, NEG_BIG))
    g_prob = 1.0 / jnp.sum(jnp.where(is_group, jnp.exp(logits - gm), 0.0), axis=1, keepdims=True)
    lo = EXPERTS_PER_GROUP * (g_lane - GROUP_LANE0)
    in_group = jnp.where((lane >= lo) & (lane < lo + EXPERTS_PER_GROUP), logits, NEG_BIG)
    m1, i1 = first_max(in_group)
    m2, i2 = first_max(jnp.where(lane == i1, NEG_BIG, in_group))
    e21 = jnp.exp(m2 - m1)
    w1 = g_prob / (1.0 + e21)
    w2 = w1 * e21
    comb = jnp.where(lane == i1, w1, 0.0) + jnp.where(lane == i2, w2, 0.0)
    sel = ((lane == i1) | (lane == i2)).astype(F32)
    return comb, sel


def _merge_kernel(x_ref, po_ref, at_ref, wg_ref, wpu_ref, wau_ref, wo_ref, lg_ref, lb_ref,
                  wr_ref, br_ref, x1_ref, x1b_ref, comb_ref, sel_ref, cnt_ref,
                  wgb_ref, wpub_ref, waub_ref, wob_ref, wrb_ref, *, alpha):
    tm, d = x_ref.shape
    _cast_once(wg_ref, wgb_ref)
    _cast_once(wpu_ref, wpub_ref)
    _cast_once(wau_ref, waub_ref)
    _cast_once(wo_ref, wob_ref)

    @pl.when(pl.program_id(0) == 0)
    def _():
        w_r = wr_ref[...]
        hi = w_r.astype(BF16)
        wrb_ref[:, :LANES] = hi
        wrb_ref[:, LANES:] = (w_r - hi.astype(F32)).astype(BF16)

    def mix(r):
        x = x_ref[r, :]
        xb = x.astype(BF16)
        a = jnp.dot(po_ref[r, :], wpub_ref[...], preferred_element_type=F32)
        gate_a = _sigmoid(jnp.dot(xb, wgb_ref[:, :d], preferred_element_type=F32))
        b = jnp.dot(at_ref[r, :], waub_ref[...], preferred_element_type=F32)
        gate_b = _sigmoid(jnp.dot(xb, wgb_ref[:, d:], preferred_element_type=F32))
        merged = (gate_a * a + gate_b * b).astype(BF16)
        return alpha * x + jnp.dot(merged, wob_ref[...], preferred_element_type=F32)

    def norm_and_route(r, h):
        x1 = _layer_norm(h, lg_ref[...], lb_ref[...])
        xh = x1.astype(BF16)
        x1_ref[r, :] = x1
        x1b_ref[r, :] = xh
        xl = (x1 - xh.astype(F32)).astype(BF16)
        by_hi = jnp.dot(xh, wrb_ref[...], preferred_element_type=F32)
        by_lo = jnp.dot(xl, wrb_ref[:, :LANES], preferred_element_type=F32)
        comb, sel = _route(by_hi[:, :LANES] + by_hi[:, LANES:] + by_lo + br_ref[...])
        comb_ref[r, :] = comb
        sel_ref[r, :] = sel.astype(BF16)
        return jnp.sum(sel, axis=0, keepdims=True)

    cnt_ref[0] = sum(_skewed(_row_subs(tm), mix, norm_and_route))


def _merge(x2, pool_out, attn_out, w_in, w_pu, w_au, w_o, ln_g, ln_b, w_r, b_r, alpha, tm):
    t, d = x2.shape
    width = pool_out.shape[1]
    assert w_in.shape[1] == 4 * width + 2 * d
    row = lambda i: (i, 0)
    blocks = (2 * (tm * d * 4 + 2 * tm * width * 2
                   + tm * d * 4 + tm * d * 2 + tm * LANES * (4 + 2) + LANES * 4)
              + (2 * d * d + 2 * width * d + d * d + d * LANES) * (4 + 2) + 2 * d * 4 + LANES * 4)
    temps = 7 * tm * d * 4
    return pl.pallas_call(
        functools.partial(_merge_kernel, alpha=alpha),
        grid=(t // tm,),
        in_specs=[pl.BlockSpec((tm, d), row), pl.BlockSpec((tm, width), row),
                  pl.BlockSpec((tm, width), row), _resident_columns(d, 2 * d, 1),
                  _resident((width, d)), _resident((width, d)), _resident((d, d)),
                  _resident((1, d)), _resident((1, d)), _resident((d, LANES)),
                  _resident((1, LANES))],
        out_specs=[pl.BlockSpec((tm, d), row), pl.BlockSpec((tm, d), row),
                   pl.BlockSpec((tm, LANES), row), pl.BlockSpec((tm, LANES), row),
                   pl.BlockSpec((1, 1, LANES), lambda i: (i, 0, 0))],
        out_shape=[jax.ShapeDtypeStruct((t, d), F32), jax.ShapeDtypeStruct((t, d), BF16),
                   jax.ShapeDtypeStruct((t, LANES), F32), jax.ShapeDtypeStruct((t, LANES), BF16),
                   jax.ShapeDtypeStruct((t // tm, 1, LANES), F32)],
        scratch_shapes=[pltpu.VMEM((d, 2 * d), BF16), pltpu.VMEM((width, d), BF16),
                        pltpu.VMEM((width, d), BF16), pltpu.VMEM((d, d), BF16),
                        pltpu.VMEM((d, 2 * LANES), BF16)],
        compiler_params=_params(("arbitrary",), blocks + temps),
        name="merge",
    )(x2, pool_out, attn_out, w_in, w_pu, w_au, w_o, ln_g, ln_b, w_r, b_r)


def _routing_plan(counts):
    cnt = counts[:, 0, :N_EXPERTS].astype(jnp.int32)
    nch = (cnt + (CHUNK - 1)) // CHUNK
    local_end = jnp.cumsum(nch, axis=1)
    local_start = local_end - nch
    block_chunks = local_end[:, -1]
    before_block = jnp.cumsum(nch, axis=0) - nch
    expert_chunks = jnp.sum(nch, axis=0)
    chunks_per_tile = EXPERT_TILE // CHUNK
    expert_tiles = (expert_chunks + (chunks_per_tile - 1)) // chunks_per_tile
    tiles_end = jnp.cumsum(expert_tiles)
    region_start = (tiles_end - expert_tiles) * chunks_per_tile
    segment_dst = region_start[None, :] + before_block
    i32 = lambda a: a.astype(jnp.int32)
    return dict(seg_len=i32(nch), seg_src=i32(local_start), seg_dst=i32(segment_dst),
                block_chunks=i32(block_chunks),
                expert_tiles=i32(expert_tiles), tiles_used=i32(tiles_end[-1:]),
                pad_start=i32(region_start + expert_chunks),
                pad_count=i32(expert_tiles * chunks_per_tile - expert_chunks))


def _local_positions(sel):
    tb = sel.shape[0]
    earlier = (lax.broadcasted_iota(jnp.int32, (tb, tb), 1)
               < lax.broadcasted_iota(jnp.int32, (tb, tb), 0)).astype(BF16)
    rank = jnp.dot(earlier, sel, preferred_element_type=F32)
    cnt = jnp.sum(sel.astype(F32), axis=0, keepdims=True)
    nch = jnp.floor((cnt + (CHUNK - 1)) * (1.0 / CHUNK))
    lower = (lax.broadcasted_iota(jnp.int32, (LANES, LANES), 0)
             < lax.broadcasted_iota(jnp.int32, (LANES, LANES), 1)).astype(BF16)
    start = CHUNK * jnp.dot(jnp.broadcast_to(nch, (8, LANES)).astype(BF16), lower,
                            preferred_element_type=F32)[0:1]
    pos = rank + start
    chosen = sel > 0
    pos_lo = jnp.min(jnp.where(chosen, pos, float(LOCAL_ROWS)), axis=1, keepdims=True)
    pos_hi = jnp.max(jnp.where(chosen, pos, -1.0), axis=1, keepdims=True)
    return pos, pos_lo, pos_hi


def _for_each(n, fn):
    lax.fori_loop(0, n, lambda c, carry: (fn(c), carry)[1], 0)


def _segment_rows(len_ref, src_ref, dst_ref, blk, e):
    n = len_ref[blk, e] * CHUNK
    src = pl.multiple_of(src_ref[blk, e] * CHUNK, CHUNK)
    dst = pl.multiple_of(dst_ref[blk, e] * CHUNK, CHUNK)
    return n, src, dst


def _scatter_kernel(len_ref, src_ref, dst_ref, nchunk_ref, pad_start_ref, pad_count_ref, used_ref,
                    x_ref, sel_ref, comb_ref, g_ref, route_ref, loc_ref, zero_ref, sems):
    b = pl.program_id(0)
    last = pl.num_programs(0) - 1
    slot = b % 2
    tb = x_ref.shape[0]

    def start_block(blk, slot):
        def start_segment(e):
            n, src, dst = _segment_rows(len_ref, src_ref, dst_ref, blk, e)

            @pl.when(n > 0)
            def _():
                pltpu.make_async_copy(loc_ref.at[slot, pl.ds(src, n)], g_ref.at[pl.ds(dst, n)],
                                      sems.at[slot]).start()

        _for_each(N_EXPERTS, start_segment)

    def wait_block(blk, slot):
        n = nchunk_ref[blk] * CHUNK
        pltpu.make_async_copy(loc_ref.at[slot, pl.ds(0, n)], g_ref.at[pl.ds(0, n)],
                              sems.at[slot]).wait()

    def pad_copy(e):
        n = pad_count_ref[e] * CHUNK
        dst = pl.multiple_of(pad_start_ref[e] * CHUNK, CHUNK)
        return n, pltpu.make_async_copy(zero_ref.at[pl.ds(0, n)], g_ref.at[pl.ds(dst, n)],
                                        sems.at[2])

    def unused_tile_copy(i):
        dst = pl.multiple_of(i * EXPERT_TILE, EXPERT_TILE)
        return pltpu.make_async_copy(zero_ref, g_ref.at[pl.ds(dst, EXPERT_TILE)], sems.at[2])

    @pl.when(b >= 2)
    def _():
        wait_block(b - 2, slot)

    sel = sel_ref[...]
    pos, pos_lo, pos_hi = _local_positions(sel)
    lo = pos_lo.astype(jnp.int32)
    hi = pos_hi.astype(jnp.int32)
    x = x_ref[...]
    chosen = sel > 0
    comb = comb_ref[...]
    w_lo = jnp.sum(jnp.where(chosen & (pos == pos_lo), comb, 0.0), axis=1, keepdims=True)
    w_hi = jnp.sum(jnp.where(chosen & (pos == pos_hi), comb, 0.0), axis=1, keepdims=True)
    lane = lax.broadcasted_iota(jnp.int32, comb.shape, 1)
    route_ref[...] = jnp.where(
        lane == ROUTE_POS_LO, pos_lo, jnp.where(
            lane == ROUTE_POS_HI, pos_hi, jnp.where(
                lane == ROUTE_W_LO, w_lo, jnp.where(lane == ROUTE_W_HI, w_hi, 0.0))))

    def sort_rows(height):
        r = lax.broadcasted_iota(jnp.int32, (tb, height), 1)
        perm = ((r == lo) | (r == hi)).astype(BF16)
        rows = lax.dot_general(perm, x, (((0,), (0,)), ((), ())), preferred_element_type=F32)
        loc_ref[slot, :height, :] = rows.astype(BF16)

    n_rows = nchunk_ref[b] * CHUNK
    for below, height in zip((0,) + SORT_HEIGHTS[:-1], SORT_HEIGHTS):
        @pl.when((n_rows > below) & (n_rows <= height))
        def _():
            sort_rows(height)

    start_block(b, slot)

    @pl.when(b == last)
    def _():
        zero_ref[...] = jnp.zeros_like(zero_ref)
        n_unused = g_ref.shape[0] // EXPERT_TILE - used_ref[0]

        def each_pad(act):
            def one(e):
                n, copy = pad_copy(e)

                @pl.when(n > 0)
                def _():
                    act(copy)

            _for_each(N_EXPERTS, one)

        each_pad(lambda copy: copy.start())
        _for_each(n_unused, lambda i: unused_tile_copy(used_ref[0] + i).start())
        each_pad(lambda copy: copy.wait())
        _for_each(n_unused, lambda i: unused_tile_copy(used_ref[0] + i).wait())

        @pl.when(b >= 1)
        def _():
            wait_block(b - 1, 1 - slot)

        wait_block(b, slot)


def _scatter(x1b, sel, comb, plan, n_tiles):
    t, d = x1b.shape
    tb = ROUTE_BLOCK
    row = lambda b, *_: (b, 0)
    blocks = (2 * (tb * d * 2 + tb * LANES * (2 + 4 + 4)) + 2 * LOCAL_ROWS * d * 2
              + EXPERT_TILE * d * 2)
    temps = tb * tb * 2 + 6 * tb * LANES * 4 + tb * LOCAL_ROWS * 6 + LOCAL_ROWS * d * 6
    return pl.pallas_call(
        _scatter_kernel,
        grid_spec=pltpu.PrefetchScalarGridSpec(
            num_scalar_prefetch=7,
            grid=(t // tb,),
            in_specs=[pl.BlockSpec((tb, d), row), pl.BlockSpec((tb, LANES), row),
                      pl.BlockSpec((tb, LANES), row)],
            out_specs=[pl.BlockSpec(memory_space=pl.ANY), pl.BlockSpec((tb, LANES), row)],
            scratch_shapes=[pltpu.VMEM((2, LOCAL_ROWS, d), BF16),
                            pltpu.VMEM((EXPERT_TILE, d), BF16), pltpu.SemaphoreType.DMA((3,))]),
        out_shape=[jax.ShapeDtypeStruct((n_tiles * EXPERT_TILE, d), BF16),
                   jax.ShapeDtypeStruct((t, LANES), F32)],
        input_output_aliases={9: 1},
        compiler_params=_params(("arbitrary",), blocks + temps),
        name="scatter",
    )(plan["seg_len"], plan["seg_src"], plan["seg_dst"], plan["block_chunks"], plan["pad_start"],
      plan["pad_count"], plan["tiles_used"], x1b, sel, comb)


def _experts_kernel(ntile_ref, used_ref, g_ref, wg_ref, wu_ref, wd_ref, y_ref,
                    x_buf, y_buf, wg_buf, wu_buf, wd_buf, wgb_ref, wub_ref, wdb_ref,
                    x_sems, y_sems, w_sems):
    n_experts = wg_ref.shape[0]
    tm = EXPERT_TILE
    used = used_ref[0]

    def weight_copies(e, slot):
        return [pltpu.make_async_copy(src.at[e], dst.at[slot], w_sems.at[slot])
                for src, dst in ((wg_ref, wg_buf), (wu_ref, wu_buf), (wd_ref, wd_buf))]

    def x_copy(t, slot):
        rows = pl.ds(pl.multiple_of(t * tm, tm), MACRO_TILES * tm)
        return pltpu.make_async_copy(g_ref.at[rows], x_buf.at[slot], x_sems.at[slot])

    def y_copy(t, k, slot):
        n = k * tm
        dst = pl.ds(pl.multiple_of(t * tm, tm), n)
        return pltpu.make_async_copy(y_buf.at[slot, pl.ds(0, n)], y_ref.at[dst], y_sems.at[slot])

    def wait_y(t, k, slot):
        @pl.when(k > 0)
        def _():
            y_copy(t, k, slot).wait()

    def mlp(slot, rows):
        x = x_buf[slot, :rows, :]
        hg = jnp.dot(x, wgb_ref[...], preferred_element_type=F32)
        hu = jnp.dot(x, wub_ref[...], preferred_element_type=F32)
        h = hg * _sigmoid(hg) * hu
        y_buf[slot, :rows, :] = jnp.dot(h.astype(BF16), wdb_ref[...],
                                        preferred_element_type=F32).astype(BF16)

    for c in weight_copies(0, 0):
        c.start()
    x_copy(0, 0).start()

    def run_expert(e, carry):
        wslot = e % 2
        for c in weight_copies(e, wslot):
            c.wait()

        @pl.when(e + 1 < n_experts)
        def _():
            for c in weight_copies(e + 1, 1 - wslot):
                c.start()

        @pl.when(ntile_ref[e] > 0)
        def _():
            wgb_ref[...] = wg_buf[wslot].astype(BF16)
            wub_ref[...] = wu_buf[wslot].astype(BF16)
            wdb_ref[...] = wd_buf[wslot].astype(BF16)

        def run_macro(m, carry):
            t, step, k1, t1, k2, t2 = carry
            k = jnp.minimum(MACRO_TILES, ntile_ref[e] - m * MACRO_TILES)
            slot = step % 2
            x_copy(t, slot).wait()

            @pl.when(t + k < used)
            def _():
                x_copy(t + k, 1 - slot).start()

            wait_y(t2, k2, slot)
            for tiles in range(1, MACRO_TILES + 1):
                @pl.when(k == tiles)
                def _():
                    mlp(slot, tiles * tm)

            y_copy(t, k, slot).start()
            return t + k, step + 1, k, t, k1, t1

        n_macro = (ntile_ref[e] + (MACRO_TILES - 1)) // MACRO_TILES
        return lax.fori_loop(0, n_macro, run_macro, carry)

    zero = jnp.int32(0)
    _, step, k1, t1, k2, t2 = lax.fori_loop(0, n_experts, run_expert, (zero,) * 6)
    wait_y(t2, k2, step % 2)
    wait_y(t1, k1, (step + 1) % 2)
    y_buf[0, :tm, :] = jnp.zeros((tm, y_buf.shape[2]), BF16)
    n_unused = y_ref.shape[0] // tm - used
    _for_each(n_unused, lambda i: y_copy(used + i, 1, 0).start())
    _for_each(n_unused, lambda i: y_copy(used + i, 1, 0).wait())


def _experts(sorted_x, plan, w_eg, w_eu, w_ed):
    rows, d = sorted_x.shape
    _, _, de = w_eg.shape
    tm = EXPERT_TILE
    any_space = pl.BlockSpec(memory_space=pl.ANY)
    big = MACRO_TILES * tm
    scratch = 2 * 2 * big * d * 2 + 2 * 3 * d * de * 4 + 3 * d * de * 2
    temps = 3 * big * de * 4 + big * d * 4 + d * de * 4
    return pl.pallas_call(
        _experts_kernel,
        grid_spec=pltpu.PrefetchScalarGridSpec(
            num_scalar_prefetch=2,
            grid=(1,),
            in_specs=[any_space] * 4,
            out_specs=any_space,
            scratch_shapes=[pltpu.VMEM((2, big, d), BF16), pltpu.VMEM((2, big, d), BF16),
                            pltpu.VMEM((2, d, de), F32), pltpu.VMEM((2, d, de), F32),
                            pltpu.VMEM((2, de, d), F32),
                            pltpu.VMEM((d, de), BF16), pltpu.VMEM((d, de), BF16),
                            pltpu.VMEM((de, d), BF16),
                            pltpu.SemaphoreType.DMA((2,)), pltpu.SemaphoreType.DMA((2,)),
                            pltpu.SemaphoreType.DMA((2,))]),
        out_shape=jax.ShapeDtypeStruct((rows, d), BF16),
        compiler_params=_params(("arbitrary",), scratch + temps),
        name="experts",
    )(plan["expert_tiles"], plan["tiles_used"], sorted_x, w_eg, w_eu, w_ed)


def _combine_kernel(len_ref, src_ref, dst_ref, nchunk_ref, x1_ref, route_ref, p_ref,
                    wpg_ref, wpp_ref,
                    lg_ref, lb_ref, y_ref, o_ref, loc_ref, wpgb_ref, wppb_ref, sems, *, alpha):
    b = pl.program_id(0)
    nb = pl.num_programs(0)
    slot = b % 2
    tb, d = x1_ref.shape
    _cast_once(wpg_ref, wpgb_ref)
    _cast_once(wpp_ref, wppb_ref)

    def start_block(blk, slot):
        def start_segment(e):
            n, local, sorted_at = _segment_rows(len_ref, src_ref, dst_ref, blk, e)

            @pl.when(n > 0)
            def _():
                pltpu.make_async_copy(y_ref.at[pl.ds(sorted_at, n)],
                                      loc_ref.at[slot, pl.ds(local, n)], sems.at[slot]).start()

        _for_each(N_EXPERTS, start_segment)

    @pl.when(b == 0)
    def _():
        loc_ref[...] = jnp.zeros_like(loc_ref)
        start_block(0, 0)

    @pl.when(b + 1 < nb)
    def _():
        start_block(b + 1, 1 - slot)

    n_rows = nchunk_ref[b] * CHUNK
    pltpu.make_async_copy(y_ref.at[pl.ds(0, n_rows)], loc_ref.at[slot, pl.ds(0, n_rows)],
                          sems.at[slot]).wait()

    lo = route_ref[:, ROUTE_POS_LO:ROUTE_POS_LO + 1].astype(jnp.int32)
    hi = route_ref[:, ROUTE_POS_HI:ROUTE_POS_HI + 1].astype(jnp.int32)
    w_lo = route_ref[:, ROUTE_W_LO:ROUTE_W_LO + 1]
    w_hi = route_ref[:, ROUTE_W_HI:ROUTE_W_HI + 1]

    def finish_block(height):
        sorted_row = lax.broadcasted_iota(jnp.int32, (tb // ROW_SUBS, height), 1)

        def branches(r):
            weights = (jnp.where(sorted_row == lo[r], w_lo[r], 0.0)
                       + jnp.where(sorted_row == hi[r], w_hi[r], 0.0)).astype(BF16)
            moe = jnp.dot(weights, loc_ref[slot, :height, :], preferred_element_type=F32)
            gate = jnp.dot(x1_ref[r, :].astype(BF16), wpgb_ref[...], preferred_element_type=F32)
            emb = jnp.dot(p_ref[r, :].astype(BF16), wppb_ref[...], preferred_element_type=F32)
            return moe, gate, emb

        def finish(r, parts):
            moe, gate, emb = parts
            h = alpha * x1_ref[r, :] + moe + _sigmoid(gate) * emb
            o_ref[r, :] = _layer_norm(h, lg_ref[...], lb_ref[...])

        _skewed(_row_subs(tb), branches, finish)

    for below, height in zip((0,) + SORT_HEIGHTS[:-1], SORT_HEIGHTS):
        @pl.when((n_rows > below) & (n_rows <= height))
        def _():
            finish_block(height)


def _combine(y, plan, x1, route, p2, w_pg, w_pp, ln_g, ln_b, alpha):
    t, d = x1.shape
    pd = p2.shape[1]
    tb = ROUTE_BLOCK
    row = lambda b, *_: (b, 0)
    blocks = (2 * (2 * tb * d * 4 + tb * LANES * 4 + tb * pd * 4)
              + (d * d + pd * d) * (4 + 2) + 2 * d * 4 + 2 * LOCAL_ROWS * d * 2)
    temps = tb * LOCAL_ROWS * 10 + 5 * tb * d * 4
    return pl.pallas_call(
        functools.partial(_combine_kernel, alpha=alpha),
        grid_spec=pltpu.PrefetchScalarGridSpec(
            num_scalar_prefetch=4,
            grid=(t // tb,),
            in_specs=[pl.BlockSpec((tb, d), row), pl.BlockSpec((tb, LANES), row),
                      pl.BlockSpec((tb, pd), row),
                      _resident((d, d)), _resident((pd, d)), _resident((1, d)), _resident((1, d)),
                      pl.BlockSpec(memory_space=pl.ANY)],
            out_specs=pl.BlockSpec((tb, d), row),
            scratch_shapes=[pltpu.VMEM((2, LOCAL_ROWS, d), BF16), pltpu.VMEM((d, d), BF16),
                            pltpu.VMEM((pd, d), BF16), pltpu.SemaphoreType.DMA((2,))]),
        out_shape=jax.ShapeDtypeStruct((t, d), F32),
        compiler_params=_params(("arbitrary",), blocks + temps),
        name="combine",
    )(plan["seg_len"], plan["seg_src"], plan["seg_dst"], plan["block_chunks"], x1, route, p2,
      w_pg, w_pp, ln_g, ln_b, y)


def kernel(x, p, w_in, w_pool, pool_scale, w_pu, w_au, w_o, ln1_g, ln1_b, w_rg, b_rg, w_re, b_re,
           w_eg, w_eu, w_ed, w_pg, w_pp, ln2_g, ln2_b):
    bsz, seq, d = x.shape
    depth = w_in.shape[0]
    t = bsz * seq
    de = w_eg.shape[-1]
    alpha = (2.0 * depth) ** 0.25
    assert w_rg.shape[2] == N_GROUPS and w_re.shape[1:] == (N_GROUPS, d, EXPERTS_PER_GROUP)
    assert w_in.shape[2] == 4 * d and w_pool.shape[1] == len(POOL_WINDOWS)
    assert t % ROUTE_BLOCK == 0 and N_EXPERTS + N_GROUPS <= LANES
    n_blocks = t // ROUTE_BLOCK
    n_tiles = -(-(2 * t + n_blocks * N_EXPERTS * (CHUNK - 1) + N_EXPERTS * (EXPERT_TILE - CHUNK))
                // EXPERT_TILE) + MACRO_TILES - 1

    x2 = x.reshape(t, d)
    for i in range(depth):
        pool_out, q, k, v = _in_hbm(
            *_proj(x2, w_in[i], w_pool[i], pool_scale[i][None, :], seq, tm=1024))
        attn_out, = _in_hbm(_attention(q, k, v, seq, tb=128, blocks_per_step=8))

        w_r = jnp.concatenate(
            [w_re[i].transpose(1, 0, 2).reshape(d, N_EXPERTS), w_rg[i]], axis=1)
        w_r = jnp.pad(w_r, ((0, 0), (0, LANES - w_r.shape[1])))
        b_r = jnp.pad(jnp.concatenate([b_re[i].reshape(-1), b_rg[i]]),
                      (0, LANES - N_GROUPS - N_EXPERTS))[None, :]

        x1, x1b, comb, sel, counts = _merge(
            x2, pool_out, attn_out, w_in[i], *_in_hbm(w_pu[i], w_au[i], w_o[i]), ln1_g[i][None, :],
            ln1_b[i][None, :], w_r, b_r, alpha, tm=ROUTE_BLOCK)
        x1, x1b, comb, sel = _in_hbm(x1, x1b, comb, sel)

        plan = _routing_plan(counts)
        sorted_x, route = _in_hbm(*_scatter(x1b, sel, comb, plan, n_tiles))
        y, = _in_hbm(_experts(sorted_x, plan, w_eg[i].reshape(N_EXPERTS, d, de),
                              w_eu[i].reshape(N_EXPERTS, d, de), w_ed[i].reshape(N_EXPERTS, de, d)))
        x2 = _combine(y, plan, x1, route, p[i].reshape(t, -1), *_in_hbm(w_pg[i], w_pp[i]),
                      ln2_g[i][None, :], ln2_b[i][None, :], alpha)
    return x2.reshape(bsz, seq, d)
```

```python
import functools
import math

import jax
import jax.numpy as jnp
from jax import lax
from jax.experimental import pallas as pl
from jax.experimental.pallas import tpu as pltpu

F32 = jnp.float32
BF16 = jnp.bfloat16

LANES = 128
POOL_WINDOWS = (2, 4, 8, 16)
POOL_HALO = 16
HEAD_DIM = 64
N_GROUPS = 4
EXPERTS_PER_GROUP = 8
N_EXPERTS = N_GROUPS * EXPERTS_PER_GROUP
LN_EPS = 1e-5
GROUP_LANE0 = N_EXPERTS
NEG_BIG = -1e30
ROUTE_BLOCK = 512
CHUNK = 16
EXPERT_TILE = 256
MACRO_TILES = 6
SORT_ROWS = 256
ROW_SUBS = 2
ROUTE_POS_LO, ROUTE_POS_HI, ROUTE_W_LO, ROUTE_W_HI = range(4)
LOCAL_ROWS = -(-(2 * ROUTE_BLOCK + N_EXPERTS * (CHUNK - 1)) // SORT_ROWS) * SORT_ROWS
SORT_HEIGHTS = (LOCAL_ROWS - SORT_ROWS, LOCAL_ROWS - SORT_ROWS // 2, LOCAL_ROWS)
ATTN_DEAD_LOG2 = -160.0
VMEM_CAP_BYTES = 56 * 1024 * 1024


def _params(sem, vmem_bytes):
    return pltpu.CompilerParams(
        dimension_semantics=sem, vmem_limit_bytes=min(int(vmem_bytes), VMEM_CAP_BYTES))


def _layer_norm(h, g, b):
    mu = jnp.mean(h, axis=-1, keepdims=True)
    c = h - mu
    var = jnp.mean(c * c, axis=-1, keepdims=True)
    return c * lax.rsqrt(var + LN_EPS) * g + b


def _sigmoid(z):
    return 1.0 / (1.0 + jnp.exp(-z))


def _row_subs(rows):
    return [slice(k * (rows // ROW_SUBS), (k + 1) * (rows // ROW_SUBS)) for k in range(ROW_SUBS)]


def _skewed(subs, first, second):
    out, pending = [], None
    for r in subs:
        mid = first(r)
        if pending is not None:
            out.append(second(*pending))
        pending = (r, mid)
    out.append(second(*pending))
    return out


def _in_hbm(*arrays):
    if not all(isinstance(a, jax.core.Tracer) for a in arrays):
        return list(arrays)
    return [pltpu.with_memory_space_constraint(a, pltpu.HBM) for a in arrays]


def _resident(shape):
    return pl.BlockSpec(shape, lambda *_: (0,) * len(shape), pipeline_mode=pl.Buffered(1))


def _resident_columns(rows, cols, block):
    return pl.BlockSpec((rows, cols), lambda *_: (0, block), pipeline_mode=pl.Buffered(1))


def _cast_once(w_ref, wb_ref):
    @pl.when(pl.program_id(0) == 0)
    def _():
        wb_ref[...] = w_ref[...].astype(BF16)


def _proj_kernel(x_ref, w_ref, wp_ref, sc_ref, po_ref, q_ref, k_ref, v_ref, wb_ref, halo_ref,
                 *, width, tiles_per_seq):
    q_scale = math.log2(math.e) / math.sqrt(HEAD_DIM)
    tile_in_seq = pl.program_id(0) % tiles_per_seq
    _cast_once(w_ref, wb_ref)
    xb = x_ref[...].astype(BF16)

    def mm(lo):
        return jnp.dot(xb, wb_ref[:, lo:lo + width], preferred_element_type=F32)

    @pl.when(tile_in_seq == 0)
    def _():
        halo_ref[...] = jnp.zeros_like(halo_ref)

    u = mm(0)
    tm = u.shape[0]
    gd = wp_ref.shape[1]
    pos = tile_in_seq * tm + lax.broadcasted_iota(jnp.int32, (tm, gd), 0)

    def pool_group(g, w):
        cols = slice(g * gd, (g + 1) * gd)
        ug = u[:, cols]
        s = jnp.concatenate([halo_ref[:, cols], ug], axis=0)
        sh = 1
        while sh < w:
            s = s + pltpu.roll(s, sh, axis=0)
            sh *= 2
        cnt = jnp.minimum(pos + 1, w).astype(F32)
        pooled = s[POOL_HALO:, :] / cnt - ug
        mixed = jnp.dot(pooled.astype(BF16), wp_ref[g].astype(BF16), preferred_element_type=F32)
        po_ref[:, cols] = (mixed * sc_ref[:, cols]).astype(BF16)

    def write_q():
        q_ref[...] = (mm(width) * q_scale).astype(BF16)

    def write_k():
        k_ref[...] = mm(2 * width).astype(BF16)

    def write_v():
        v_ref[...] = mm(3 * width).astype(BF16)

    pool_group(0, POOL_WINDOWS[0])
    for n, matmul in enumerate((write_q, write_k, write_v)):
        matmul()
        pool_group(n + 1, POOL_WINDOWS[n + 1])
    halo_ref[...] = u[tm - POOL_HALO:, :]


def _proj(x2, w_in, w_pool, pool_scale, seq, tm):
    t, d = x2.shape
    width = d // 2
    n = 4 * width
    groups, gd, _ = w_pool.shape
    assert seq % tm == 0 and groups * gd == width and max(POOL_WINDOWS) <= POOL_HALO + 1
    assert len(POOL_WINDOWS) == 4 and w_in.shape[1] % n == 0
    row = lambda i: (i, 0)
    blocks = (2 * (tm * d * 4 + tm * width * 4 * 2)
              + d * n * (4 + 2) + groups * gd * gd * 4 + width * 4 + POOL_HALO * width * 4)
    temps = tm * d * 2 + 3 * tm * width * 4 + 6 * (tm + POOL_HALO) * gd * 4
    return pl.pallas_call(
        functools.partial(_proj_kernel, width=width, tiles_per_seq=seq // tm),
        grid=(t // tm,),
        in_specs=[pl.BlockSpec((tm, d), row), _resident_columns(d, n, 0),
                  _resident((groups, gd, gd)), _resident((1, width))],
        out_specs=[pl.BlockSpec((tm, width), row)] * 4,
        out_shape=[jax.ShapeDtypeStruct((t, width), BF16)] * 4,
        scratch_shapes=[pltpu.VMEM((d, n), BF16), pltpu.VMEM((POOL_HALO, width), F32)],
        compiler_params=_params(("arbitrary",), blocks + temps),
        name="proj",
    )(x2, w_in, w_pool, pool_scale)


def _attn_kernel(q_ref, k_ref, v_ref, o_ref, rem_ref, acc_ref, *, tb):
    pairs = q_ref.shape[1] // LANES
    blocks_per_step = q_ref.shape[0] // tb
    col_blocks = [slice(p * LANES, (p + 1) * LANES) for p in range(pairs)]
    first_head = lax.broadcasted_iota(jnp.int32, (tb, LANES), 1) < HEAD_DIM
    r = lax.broadcasted_iota(jnp.int32, (2 * tb, 2 * tb), 0)
    c = lax.broadcasted_iota(jnp.int32, (2 * tb, 2 * tb), 1)
    same_head = (r >= tb) == (c >= tb)
    cum = -jnp.concatenate([(same_head & (r >= c)).astype(BF16), same_head.astype(BF16)], axis=1)
    kcol = lax.broadcasted_iota(jnp.int32, (tb, 2 * tb), 1)
    kcol = jnp.where(kcol >= tb, kcol - tb, kcol)
    causal = kcol < lax.broadcasted_iota(jnp.int32, (tb, 2 * tb), 0)

    def stack_heads(blk):
        zero = jnp.zeros_like(blk)
        return jnp.concatenate(
            [jnp.where(first_head, blk, zero), jnp.where(first_head, zero, blk)], axis=0)

    def scores(q0, j, rows, diagonal):
        ks = pl.multiple_of(j * tb, tb)
        q_rows = pl.ds(q0 + rows.start, rows.stop - rows.start)
        zs = [lax.dot_general(q_ref[q_rows, cols], stack_heads(k_ref[pl.ds(ks, tb), cols]),
                              (((1,), (1,)), ((), ())), preferred_element_type=F32)
              for cols in col_blocks]
        sums = []
        for z in zs:
            softplus = jnp.maximum(z, 0.0) + jnp.log2(1.0 + jnp.exp2(-jnp.abs(z)))
            if diagonal:
                softplus = jnp.where(causal, softplus, 0.0)
            sums.append(jnp.dot(softplus.astype(BF16), cum, preferred_element_type=F32))
        return zs, sums

    def weighted_values(j, cols, z, later):
        ks = pl.multiple_of(j * tb, tb)
        a = jnp.exp2(z + later)
        return jnp.dot(a.astype(BF16), stack_heads(v_ref[pl.ds(ks, tb), cols]),
                       preferred_element_type=F32)

    def first_step(w, qi, q0, n_before):
        full, half = slice(0, tb), slice(0, tb // 2)
        z_d, sums_d = scores(q0, qi, full, diagonal=True)
        if n_before >= 1:
            z_1, sums_1 = scores(q0, qi - 1, full, diagonal=False)
        if n_before >= 2:
            z_2, sums_2 = scores(q0, qi - 2, half, diagonal=False)
        owed = []
        for p, cols in enumerate(col_blocks):
            ks = pl.multiple_of(qi * tb, tb)
            a = jnp.where(causal, jnp.exp2(z_d[p] + sums_d[p][:, :2 * tb]), 0.0)
            out = jnp.dot(a.astype(BF16), stack_heads(v_ref[pl.ds(ks, tb), cols]),
                          preferred_element_type=F32)
            total = sums_d[p][:, 2 * tb:]
            if n_before >= 1:
                out = out + weighted_values(qi - 1, cols, z_1[p], sums_1[p][:, :2 * tb] + total)
                total = total + sums_1[p][:, 2 * tb:]
            acc_ref[w * pairs + p] = out
            rem_ref[w * pairs + p] = total
            if n_before >= 2:
                owed.append(total[tb // 2:, :])
                later = sums_2[p][:, :2 * tb] + total[half, :]
                acc_ref[w * pairs + p, half, :] += weighted_values(qi - 2, cols, z_2[p], later)
                rem_ref[w * pairs + p, half, :] += sums_2[p][:, 2 * tb:]
        if n_before < 2:
            return None
        return jnp.max(functools.reduce(jnp.maximum, owed)) > ATTN_DEAD_LOG2

    def step(w, q0, j, rows):
        zs, sums = scores(q0, j, rows, diagonal=False)
        for p, cols in enumerate(col_blocks):
            later = sums[p][:, :2 * tb] + rem_ref[w * pairs + p, rows, :]
            acc_ref[w * pairs + p, rows, :] += weighted_values(j, cols, zs[p], later)
            rem_ref[w * pairs + p, rows, :] += sums[p][:, 2 * tb:]

    def live_rows(w):
        worst = functools.reduce(jnp.maximum, [rem_ref[w * pairs + p] for p in range(pairs)])
        live = jnp.max(worst, axis=1, keepdims=True) > ATTN_DEAD_LOG2
        row = lax.broadcasted_iota(jnp.int32, (tb, 1), 0)
        return jnp.max(jnp.where(live, row + 1, 0))

    row_counts = (tb, tb // 2, tb // 4)

    def finish_block(w, qi, q0):
        def cond(state):
            j, n_live = state
            return (j >= 0) & (n_live > 0)

        def body(state):
            j, n_live = state
            for rows, fewer in zip(row_counts, row_counts[1:] + (0,)):
                @pl.when((n_live > fewer) & (n_live <= rows))
                def _():
                    step(w, q0, j, slice(0, rows))

            return j - 1, live_rows(w)

        lax.while_loop(cond, body, (qi - 3, live_rows(w)))
        for p, cols in enumerate(col_blocks):
            o_ref[pl.ds(q0, tb), cols] = acc_ref[w * pairs + p].astype(BF16)

    def two_blocks(s, carry):
        qi = (pl.program_id(1) * blocks_per_step + 2 * s, pl.program_id(1) * blocks_per_step + 2 * s + 1)
        q0 = (pl.multiple_of(2 * s * tb, tb), pl.multiple_of(2 * s * tb, tb) + tb)

        @pl.when(qi[0] == 0)
        def _():
            first_step(0, qi[0], q0[0], 0)
            first_step(1, qi[1], q0[1], 1)

        @pl.when(qi[0] >= 2)
        def _():
            owed = [first_step(w, qi[w], q0[w], 2) for w in (0, 1)]
            for w in (0, 1):
                @pl.when(owed[w])
                def _():
                    step(w, q0[w], qi[w] - 2, slice(tb // 2, tb))

        for w in (0, 1):
            finish_block(w, qi[w], q0[w])
        return carry

    assert blocks_per_step % 2 == 0
    lax.fori_loop(0, blocks_per_step // 2, two_blocks, 0)


def _attention(q, k, v, seq, tb, blocks_per_step):
    t, width = q.shape
    pairs = width // LANES
    tq = tb * blocks_per_step
    blocks = 2 * (2 * tq * width * 2 + 2 * seq * width * 2) + 2 * pairs * tb * (2 * tb + LANES) * 4
    temps = 2 * pairs * 10 * tb * 2 * tb * 4 + 2 * tb * 4 * tb * 2
    qspec = pl.BlockSpec((tq, width), lambda b, i: (b * (seq // tq) + i, 0))
    kvspec = pl.BlockSpec((seq, width), lambda b, i: (b, 0))
    return pl.pallas_call(
        functools.partial(_attn_kernel, tb=tb),
        grid=(t // seq, seq // tq),
        in_specs=[qspec, kvspec, kvspec],
        out_specs=qspec,
        out_shape=jax.ShapeDtypeStruct((t, width), BF16),
        scratch_shapes=[pltpu.VMEM((2 * pairs, tb, 2 * tb), F32),
                        pltpu.VMEM((2 * pairs, tb, LANES), F32)],
        input_output_aliases={0: 0},
        compiler_params=_params(("parallel", "parallel"), blocks + temps),
        name="attn",
    )(q, k, v)


def _route(logits):
    lane = lax.broadcasted_iota(jnp.int32, logits.shape, 1)

    def first_max(vals):
        m = jnp.max(vals, axis=1, keepdims=True)
        idx = jnp.min(jnp.where(vals == m, lane, LANES), axis=1, keepdims=True)
        return m, idx

    is_group = (lane >= GROUP_LANE0) & (lane < GROUP_LANE0 + N_GROUPS)
    gm, g_lane = first_max(jnp.where(is_group, logits, NEG_BIG))
    g_prob = 1.0 / jnp.sum(jnp.where(is_group, jnp.exp(logits - gm), 0.0), axis=1, keepdims=True)
    lo = EXPERTS_PER_GROUP * (g_lane - GROUP_LANE0)
    in_group = jnp.where((lane >= lo) & (lane < lo + EXPERTS_PER_GROUP), logits, NEG_BIG)
    m1, i1 = first_max(in_group)
    m2, i2 = first_max(jnp.where(lane == i1, NEG_BIG, in_group))
    e21 = jnp.exp(m2 - m1)
    w1 = g_prob / (1.0 + e21)
    w2 = w1 * e21
    comb = jnp.where(lane == i1, w1, 0.0) + jnp.where(lane == i2, w2, 0.0)
    sel = ((lane == i1) | (lane == i2)).astype(F32)
    return comb, sel


def _merge_kernel(x_ref, po_ref, at_ref, wg_ref, wpu_ref, wau_ref, wo_ref, lg_ref, lb_ref,
                  wr_ref, br_ref, x1_ref, x1b_ref, comb_ref, sel_ref, cnt_ref,
                  wgb_ref, wpub_ref, waub_ref, wob_ref, wrb_ref, *, alpha):
    tm, d = x_ref.shape
    _cast_once(wg_ref, wgb_ref)
    _cast_once(wpu_ref, wpub_ref)
    _cast_once(wau_ref, waub_ref)
    _cast_once(wo_ref, wob_ref)

    @pl.when(pl.program_id(0) == 0)
    def _():
        w_r = wr_ref[...]
        hi = w_r.astype(BF16)
        wrb_ref[:, :LANES] = hi
        wrb_ref[:, LANES:] = (w_r - hi.astype(F32)).astype(BF16)

    def mix(r):
        x = x_ref[r, :]
        xb = x.astype(BF16)
        a = jnp.dot(po_ref[r, :], wpub_ref[...], preferred_element_type=F32)
        gate_a = _sigmoid(jnp.dot(xb, wgb_ref[:, :d], preferred_element_type=F32))
        b = jnp.dot(at_ref[r, :], waub_ref[...], preferred_element_type=F32)
        gate_b = _sigmoid(jnp.dot(xb, wgb_ref[:, d:], preferred_element_type=F32))
        merged = (gate_a * a + gate_b * b).astype(BF16)
        return alpha * x + jnp.dot(merged, wob_ref[...], preferred_element_type=F32)

    def norm_and_route(r, h):
        x1 = _layer_norm(h, lg_ref[...], lb_ref[...])
        xh = x1.astype(BF16)
        x1_ref[r, :] = x1
        x1b_ref[r, :] = xh
        xl = (x1 - xh.astype(F32)).astype(BF16)
        by_hi = jnp.dot(xh, wrb_ref[...], preferred_element_type=F32)
        by_lo = jnp.dot(xl, wrb_ref[:, :LANES], preferred_element_type=F32)
        comb, sel = _route(by_hi[:, :LANES] + by_hi[:, LANES:] + by_lo + br_ref[...])
        comb_ref[r, :] = comb
        sel_ref[r, :] = sel.astype(BF16)
        return jnp.sum(sel, axis=0, keepdims=True)

    cnt_ref[0] = sum(_skewed(_row_subs(tm), mix, norm_and_route))


def _merge(x2, pool_out, attn_out, w_in, w_pu, w_au, w_o, ln_g, ln_b, w_r, b_r, alpha, tm):
    t, d = x2.shape
    width = pool_out.shape[1]
    assert w_in.shape[1] == 4 * width + 2 * d
    row = lambda i: (i, 0)
    blocks = (2 * (tm * d * 4 + 2 * tm * width * 2
                   + tm * d * 4 + tm * d * 2 + tm * LANES * (4 + 2) + LANES * 4)
              + (2 * d * d + 2 * width * d + d * d + d * LANES) * (4 + 2) + 2 * d * 4 + LANES * 4)
    temps = 7 * tm * d * 4
    return pl.pallas_call(
        functools.partial(_merge_kernel, alpha=alpha),
        grid=(t // tm,),
        in_specs=[pl.BlockSpec((tm, d), row), pl.BlockSpec((tm, width), row),
                  pl.BlockSpec((tm, width), row), _resident_columns(d, 2 * d, 1),
                  _resident((width, d)), _resident((width, d)), _resident((d, d)),
                  _resident((1, d)), _resident((1, d)), _resident((d, LANES)),
                  _resident((1, LANES))],
        out_specs=[pl.BlockSpec((tm, d), row), pl.BlockSpec((tm, d), row),
                   pl.BlockSpec((tm, LANES), row), pl.BlockSpec((tm, LANES), row),
                   pl.BlockSpec((1, 1, LANES), lambda i: (i, 0, 0))],
        out_shape=[jax.ShapeDtypeStruct((t, d), F32), jax.ShapeDtypeStruct((t, d), BF16),
                   jax.ShapeDtypeStruct((t, LANES), F32), jax.ShapeDtypeStruct((t, LANES), BF16),
                   jax.ShapeDtypeStruct((t // tm, 1, LANES), F32)],
        scratch_shapes=[pltpu.VMEM((d, 2 * d), BF16), pltpu.VMEM((width, d), BF16),
                        pltpu.VMEM((width, d), BF16), pltpu.VMEM((d, d), BF16),
                        pltpu.VMEM((d, 2 * LANES), BF16)],
        compiler_params=_params(("arbitrary",), blocks + temps),
        name="merge",
    )(x2, pool_out, attn_out, w_in, w_pu, w_au, w_o, ln_g, ln_b, w_r, b_r)


def _routing_plan(counts):
    cnt = counts[:, 0, :N_EXPERTS].astype(jnp.int32)
    nch = (cnt + (CHUNK - 1)) // CHUNK
    local_end = jnp.cumsum(nch, axis=1)
    local_start = local_end - nch
    block_chunks = local_end[:, -1]
    before_block = jnp.cumsum(nch, axis=0) - nch
    expert_chunks = jnp.sum(nch, axis=0)
    chunks_per_tile = EXPERT_TILE // CHUNK
    expert_tiles = (expert_chunks + (chunks_per_tile - 1)) // chunks_per_tile
    tiles_end = jnp.cumsum(expert_tiles)
    region_start = (tiles_end - expert_tiles) * chunks_per_tile
    segment_dst = region_start[None, :] + before_block
    i32 = lambda a: a.astype(jnp.int32)
    return dict(seg_len=i32(nch), seg_src=i32(local_start), seg_dst=i32(segment_dst),
                block_chunks=i32(block_chunks),
                expert_tiles=i32(expert_tiles), tiles_used=i32(tiles_end[-1:]),
                pad_start=i32(region_start + expert_chunks),
                pad_count=i32(expert_tiles * chunks_per_tile - expert_chunks))


def _local_positions(sel):
    tb = sel.shape[0]
    earlier = (lax.broadcasted_iota(jnp.int32, (tb, tb), 1)
               < lax.broadcasted_iota(jnp.int32, (tb, tb), 0)).astype(BF16)
    rank = jnp.dot(earlier, sel, preferred_element_type=F32)
    cnt = jnp.sum(sel.astype(F32), axis=0, keepdims=True)
    nch = jnp.floor((cnt + (CHUNK - 1)) * (1.0 / CHUNK))
    lower = (lax.broadcasted_iota(jnp.int32, (LANES, LANES), 0)
             < lax.broadcasted_iota(jnp.int32, (LANES, LANES), 1)).astype(BF16)
    start = CHUNK * jnp.dot(jnp.broadcast_to(nch, (8, LANES)).astype(BF16), lower,
                            preferred_element_type=F32)[0:1]
    pos = rank + start
    chosen = sel > 0
    pos_lo = jnp.min(jnp.where(chosen, pos, float(LOCAL_ROWS)), axis=1, keepdims=True)
    pos_hi = jnp.max(jnp.where(chosen, pos, -1.0), axis=1, keepdims=True)
    return pos, pos_lo, pos_hi


def _for_each(n, fn):
    lax.fori_loop(0, n, lambda c, carry: (fn(c), carry)[1], 0)


def _segment_rows(len_ref, src_ref, dst_ref, blk, e):
    n = len_ref[blk, e] * CHUNK
    src = pl.multiple_of(src_ref[blk, e] * CHUNK, CHUNK)
    dst = pl.multiple_of(dst_ref[blk, e] * CHUNK, CHUNK)
    return n, src, dst


def _scatter_kernel(len_ref, src_ref, dst_ref, nchunk_ref, pad_start_ref, pad_count_ref, used_ref,
                    x_ref, sel_ref, comb_ref, g_ref, route_ref, loc_ref, zero_ref, sems):
    b = pl.program_id(0)
    last = pl.num_programs(0) - 1
    slot = b % 2
    tb = x_ref.shape[0]

    def start_block(blk, slot):
        def start_segment(e):
            n, src, dst = _segment_rows(len_ref, src_ref, dst_ref, blk, e)

            @pl.when(n > 0)
            def _():
                pltpu.make_async_copy(loc_ref.at[slot, pl.ds(src, n)], g_ref.at[pl.ds(dst, n)],
                                      sems.at[slot]).start()

        _for_each(N_EXPERTS, start_segment)

    def wait_block(blk, slot):
        n = nchunk_ref[blk] * CHUNK
        pltpu.make_async_copy(loc_ref.at[slot, pl.ds(0, n)], g_ref.at[pl.ds(0, n)],
                              sems.at[slot]).wait()

    def pad_copy(e):
        n = pad_count_ref[e] * CHUNK
        dst = pl.multiple_of(pad_start_ref[e] * CHUNK, CHUNK)
        return n, pltpu.make_async_copy(zero_ref.at[pl.ds(0, n)], g_ref.at[pl.ds(dst, n)],
                                        sems.at[2])

    def unused_tile_copy(i):
        dst = pl.multiple_of(i * EXPERT_TILE, EXPERT_TILE)
        return pltpu.make_async_copy(zero_ref, g_ref.at[pl.ds(dst, EXPERT_TILE)], sems.at[2])

    @pl.when(b >= 2)
    def _():
        wait_block(b - 2, slot)

    sel = sel_ref[...]
    pos, pos_lo, pos_hi = _local_positions(sel)
    lo = pos_lo.astype(jnp.int32)
    hi = pos_hi.astype(jnp.int32)
    x = x_ref[...]
    chosen = sel > 0
    comb = comb_ref[...]
    w_lo = jnp.sum(jnp.where(chosen & (pos == pos_lo), comb, 0.0), axis=1, keepdims=True)
    w_hi = jnp.sum(jnp.where(chosen & (pos == pos_hi), comb, 0.0), axis=1, keepdims=True)
    lane = lax.broadcasted_iota(jnp.int32, comb.shape, 1)
    route_ref[...] = jnp.where(
        lane == ROUTE_POS_LO, pos_lo, jnp.where(
            lane == ROUTE_POS_HI, pos_hi, jnp.where(
                lane == ROUTE_W_LO, w_lo, jnp.where(lane == ROUTE_W_HI, w_hi, 0.0))))

    def sort_rows(height):
        r = lax.broadcasted_iota(jnp.int32, (tb, height), 1)
        perm = ((r == lo) | (r == hi)).astype(BF16)
        rows = lax.dot_general(perm, x, (((0,), (0,)), ((), ())), preferred_element_type=F32)
        loc_ref[slot, :height, :] = rows.astype(BF16)

    n_rows = nchunk_ref[b] * CHUNK
    for below, height in zip((0,) + SORT_HEIGHTS[:-1], SORT_HEIGHTS):
        @pl.when((n_rows > below) & (n_rows <= height))
        def _():
            sort_rows(height)

    start_block(b, slot)

    @pl.when(b == last)
    def _():
        zero_ref[...] = jnp.zeros_like(zero_ref)
        n_unused = g_ref.shape[0] // EXPERT_TILE - used_ref[0]

        def each_pad(act):
            def one(e):
                n, copy = pad_copy(e)

                @pl.when(n > 0)
                def _():
                    act(copy)

            _for_each(N_EXPERTS, one)

        each_pad(lambda copy: copy.start())
        _for_each(n_unused, lambda i: unused_tile_copy(used_ref[0] + i).start())
        each_pad(lambda copy: copy.wait())
        _for_each(n_unused, lambda i: unused_tile_copy(used_ref[0] + i).wait())

        @pl.when(b >= 1)
        def _():
            wait_block(b - 1, 1 - slot)

        wait_block(b, slot)


def _scatter(x1b, sel, comb, plan, n_tiles):
    t, d = x1b.shape
    tb = ROUTE_BLOCK
    row = lambda b, *_: (b, 0)
    blocks = (2 * (tb * d * 2 + tb * LANES * (2 + 4 + 4)) + 2 * LOCAL_ROWS * d * 2
              + EXPERT_TILE * d * 2)
    temps = tb * tb * 2 + 6 * tb * LANES * 4 + tb * LOCAL_ROWS * 6 + LOCAL_ROWS * d * 6
    return pl.pallas_call(
        _scatter_kernel,
        grid_spec=pltpu.PrefetchScalarGridSpec(
            num_scalar_prefetch=7,
            grid=(t // tb,),
            in_specs=[pl.BlockSpec((tb, d), row), pl.BlockSpec((tb, LANES), row),
                      pl.BlockSpec((tb, LANES), row)],
            out_specs=[pl.BlockSpec(memory_space=pl.ANY), pl.BlockSpec((tb, LANES), row)],
            scratch_shapes=[pltpu.VMEM((2, LOCAL_ROWS, d), BF16),
                            pltpu.VMEM((EXPERT_TILE, d), BF16), pltpu.SemaphoreType.DMA((3,))]),
        out_shape=[jax.ShapeDtypeStruct((n_tiles * EXPERT_TILE, d), BF16),
                   jax.ShapeDtypeStruct((t, LANES), F32)],
        input_output_aliases={9: 1},
        compiler_params=_params(("arbitrary",), blocks + temps),
        name="scatter",
    )(plan["seg_len"], plan["seg_src"], plan["seg_dst"], plan["block_chunks"], plan["pad_start"],
      plan["pad_count"], plan["tiles_used"], x1b, sel, comb)


def _experts_kernel(ntile_ref, used_ref, g_ref, wg_ref, wu_ref, wd_ref, y_ref,
                    x_buf, y_buf, wg_buf, wu_buf, wd_buf, wgb_ref, wub_ref, wdb_ref,
                    x_sems, y_sems, w_sems):
    n_experts = wg_ref.shape[0]
    tm = EXPERT_TILE
    used = used_ref[0]

    def weight_copies(e, slot):
        return [pltpu.make_async_copy(src.at[e], dst.at[slot], w_sems.at[slot])
                for src, dst in ((wg_ref, wg_buf), (wu_ref, wu_buf), (wd_ref, wd_buf))]

    def x_copy(t, slot):
        rows = pl.ds(pl.multiple_of(t * tm, tm), MACRO_TILES * tm)
        return pltpu.make_async_copy(g_ref.at[rows], x_buf.at[slot], x_sems.at[slot])

    def y_copy(t, k, slot):
        n = k * tm
        dst = pl.ds(pl.multiple_of(t * tm, tm), n)
        return pltpu.make_async_copy(y_buf.at[slot, pl.ds(0, n)], y_ref.at[dst], y_sems.at[slot])

    def wait_y(t, k, slot):
        @pl.when(k > 0)
        def _():
            y_copy(t, k, slot).wait()

    def mlp(slot, rows):
        x = x_buf[slot, :rows, :]
        hg = jnp.dot(x, wgb_ref[...], preferred_element_type=F32)
        hu = jnp.dot(x, wub_ref[...], preferred_element_type=F32)
        h = hg * _sigmoid(hg) * hu
        y_buf[slot, :rows, :] = jnp.dot(h.astype(BF16), wdb_ref[...],
                                        preferred_element_type=F32).astype(BF16)

    for c in weight_copies(0, 0):
        c.start()
    x_copy(0, 0).start()

    def run_expert(e, carry):
        wslot = e % 2
        for c in weight_copies(e, wslot):
            c.wait()

        @pl.when(e + 1 < n_experts)
        def _():
            for c in weight_copies(e + 1, 1 - wslot):
                c.start()

        @pl.when(ntile_ref[e] > 0)
        def _():
            wgb_ref[...] = wg_buf[wslot].astype(BF16)
            wub_ref[...] = wu_buf[wslot].astype(BF16)
            wdb_ref[...] = wd_buf[wslot].astype(BF16)

        def run_macro(m, carry):
            t, step, k1, t1, k2, t2 = carry
            k = jnp.minimum(MACRO_TILES, ntile_ref[e] - m * MACRO_TILES)
            slot = step % 2
            x_copy(t, slot).wait()

            @pl.when(t + k < used)
            def _():
                x_copy(t + k, 1 - slot).start()

            wait_y(t2, k2, slot)
            for tiles in range(1, MACRO_TILES + 1):
                @pl.when(k == tiles)
                def _():
                    mlp(slot, tiles * tm)

            y_copy(t, k, slot).start()
            return t + k, step + 1, k, t, k1, t1

        n_macro = (ntile_ref[e] + (MACRO_TILES - 1)) // MACRO_TILES
        return lax.fori_loop(0, n_macro, run_macro, carry)

    zero = jnp.int32(0)
    _, step, k1, t1, k2, t2 = lax.fori_loop(0, n_experts, run_expert, (zero,) * 6)
    wait_y(t2, k2, step % 2)
    wait_y(t1, k1, (step + 1) % 2)
    y_buf[0, :tm, :] = jnp.zeros((tm, y_buf.shape[2]), BF16)
    n_unused = y_ref.shape[0] // tm - used
    _for_each(n_unused, lambda i: y_copy(used + i, 1, 0).start())
    _for_each(n_unused, lambda i: y_copy(used + i, 1, 0).wait())


def _experts(sorted_x, plan, w_eg, w_eu, w_ed):
    rows, d = sorted_x.shape
    _, _, de = w_eg.shape
    tm = EXPERT_TILE
    any_space = pl.BlockSpec(memory_space=pl.ANY)
    big = MACRO_TILES * tm
    scratch = 2 * 2 * big * d * 2 + 2 * 3 * d * de * 4 + 3 * d * de * 2
    temps = 3 * big * de * 4 + big * d * 4 + d * de * 4
    return pl.pallas_call(
        _experts_kernel,
        grid_spec=pltpu.PrefetchScalarGridSpec(
            num_scalar_prefetch=2,
            grid=(1,),
            in_specs=[any_space] * 4,
            out_specs=any_space,
            scratch_shapes=[pltpu.VMEM((2, big, d), BF16), pltpu.VMEM((2, big, d), BF16),
                            pltpu.VMEM((2, d, de), F32), pltpu.VMEM((2, d, de), F32),
                            pltpu.VMEM((2, de, d), F32),
                            pltpu.VMEM((d, de), BF16), pltpu.VMEM((d, de), BF16),
                            pltpu.VMEM((de, d), BF16),
                            pltpu.SemaphoreType.DMA((2,)), pltpu.SemaphoreType.DMA((2,)),
                            pltpu.SemaphoreType.DMA((2,))]),
        out_shape=jax.ShapeDtypeStruct((rows, d), BF16),
        compiler_params=_params(("arbitrary",), scratch + temps),
        name="experts",
    )(plan["expert_tiles"], plan["tiles_used"], sorted_x, w_eg, w_eu, w_ed)


def _combine_kernel(len_ref, src_ref, dst_ref, nchunk_ref, x1_ref, route_ref, p_ref,
                    wpg_ref, wpp_ref,
                    lg_ref, lb_ref, y_ref, o_ref, loc_ref, wpgb_ref, wppb_ref, sems, *, alpha):
    b = pl.program_id(0)
    nb = pl.num_programs(0)
    slot = b % 2
    tb, d = x1_ref.shape
    _cast_once(wpg_ref, wpgb_ref)
    _cast_once(wpp_ref, wppb_ref)

    def start_block(blk, slot):
        def start_segment(e):
            n, local, sorted_at = _segment_rows(len_ref, src_ref, dst_ref, blk, e)

            @pl.when(n > 0)
            def _():
                pltpu.make_async_copy(y_ref.at[pl.ds(sorted_at, n)],
                                      loc_ref.at[slot, pl.ds(local, n)], sems.at[slot]).start()

        _for_each(N_EXPERTS, start_segment)

    @pl.when(b == 0)
    def _():
        loc_ref[...] = jnp.zeros_like(loc_ref)
        start_block(0, 0)

    @pl.when(b + 1 < nb)
    def _():
        start_block(b + 1, 1 - slot)

    n_rows = nchunk_ref[b] * CHUNK
    pltpu.make_async_copy(y_ref.at[pl.ds(0, n_rows)], loc_ref.at[slot, pl.ds(0, n_rows)],
                          sems.at[slot]).wait()

    lo = route_ref[:, ROUTE_POS_LO:ROUTE_POS_LO + 1].astype(jnp.int32)
    hi = route_ref[:, ROUTE_POS_HI:ROUTE_POS_HI + 1].astype(jnp.int32)
    w_lo = route_ref[:, ROUTE_W_LO:ROUTE_W_LO + 1]
    w_hi = route_ref[:, ROUTE_W_HI:ROUTE_W_HI + 1]

    def finish_block(height):
        sorted_row = lax.broadcasted_iota(jnp.int32, (tb // ROW_SUBS, height), 1)

        def branches(r):
            weights = (jnp.where(sorted_row == lo[r], w_lo[r], 0.0)
                       + jnp.where(sorted_row == hi[r], w_hi[r], 0.0)).astype(BF16)
            moe = jnp.dot(weights, loc_ref[slot, :height, :], preferred_element_type=F32)
            gate = jnp.dot(x1_ref[r, :].astype(BF16), wpgb_ref[...], preferred_element_type=F32)
            emb = jnp.dot(p_ref[r, :].astype(BF16), wppb_ref[...], preferred_element_type=F32)
            return moe, gate, emb

        def finish(r, parts):
            moe, gate, emb = parts
            h = alpha * x1_ref[r, :] + moe + _sigmoid(gate) * emb
            o_ref[r, :] = _layer_norm(h, lg_ref[...], lb_ref[...])

        _skewed(_row_subs(tb), branches, finish)

    for below, height in zip((0,) + SORT_HEIGHTS[:-1], SORT_HEIGHTS):
        @pl.when((n_rows > below) & (n_rows <= height))
        def _():
            finish_block(height)


def _combine(y, plan, x1, route, p2, w_pg, w_pp, ln_g, ln_b, alpha):
    t, d = x1.shape
    pd = p2.shape[1]
    tb = ROUTE_BLOCK
    row = lambda b, *_: (b, 0)
    blocks = (2 * (2 * tb * d * 4 + tb * LANES * 4 + tb * pd * 4)
              + (d * d + pd * d) * (4 + 2) + 2 * d * 4 + 2 * LOCAL_ROWS * d * 2)
    temps = tb * LOCAL_ROWS * 10 + 5 * tb * d * 4
    return pl.pallas_call(
        functools.partial(_combine_kernel, alpha=alpha),
        grid_spec=pltpu.PrefetchScalarGridSpec(
            num_scalar_prefetch=4,
            grid=(t // tb,),
            in_specs=[pl.BlockSpec((tb, d), row), pl.BlockSpec((tb, LANES), row),
                      pl.BlockSpec((tb, pd), row),
                      _resident((d, d)), _resident((pd, d)), _resident((1, d)), _resident((1, d)),
                      pl.BlockSpec(memory_space=pl.ANY)],
            out_specs=pl.BlockSpec((tb, d), row),
            scratch_shapes=[pltpu.VMEM((2, LOCAL_ROWS, d), BF16), pltpu.VMEM((d, d), BF16),
                            pltpu.VMEM((pd, d), BF16), pltpu.SemaphoreType.DMA((2,))]),
        out_shape=jax.ShapeDtypeStruct((t, d), F32),
        compiler_params=_params(("arbitrary",), blocks + temps),
        name="combine",
    )(plan["seg_len"], plan["seg_src"], plan["seg_dst"], plan["block_chunks"], x1, route, p2,
      w_pg, w_pp, ln_g, ln_b, y)


def kernel(x, p, w_in, w_pool, pool_scale, w_pu, w_au, w_o, ln1_g, ln1_b, w_rg, b_rg, w_re, b_re,
           w_eg, w_eu, w_ed, w_pg, w_pp, ln2_g, ln2_b):
    bsz, seq, d = x.shape
    depth = w_in.shape[0]
    t = bsz * seq
    de = w_eg.shape[-1]
    alpha = (2.0 * depth) ** 0.25
    assert w_rg.shape[2] == N_GROUPS and w_re.shape[1:] == (N_GROUPS, d, EXPERTS_PER_GROUP)
    assert w_in.shape[2] == 4 * d and w_pool.shape[1] == len(POOL_WINDOWS)
    assert t % ROUTE_BLOCK == 0 and N_EXPERTS + N_GROUPS <= LANES
    n_blocks = t // ROUTE_BLOCK
    n_tiles = -(-(2 * t + n_blocks * N_EXPERTS * (CHUNK - 1) + N_EXPERTS * (EXPERT_TILE - CHUNK))
                // EXPERT_TILE) + MACRO_TILES - 1

    x2 = x.reshape(t, d)
    for i in range(depth):
        pool_out, q, k, v = _in_hbm(
            *_proj(x2, w_in[i], w_pool[i], pool_scale[i][None, :], seq, tm=1024))
        attn_out, = _in_hbm(_attention(q, k, v, seq, tb=128, blocks_per_step=8))

        w_r = jnp.concatenate(
            [w_re[i].transpose(1, 0, 2).reshape(d, N_EXPERTS), w_rg[i]], axis=1)
        w_r = jnp.pad(w_r, ((0, 0), (0, LANES - w_r.shape[1])))
        b_r = jnp.pad(jnp.concatenate([b_re[i].reshape(-1), b_rg[i]]),
                      (0, LANES - N_GROUPS - N_EXPERTS))[None, :]

        x1, x1b, comb, sel, counts = _merge(
            x2, pool_out, attn_out, w_in[i], *_in_hbm(w_pu[i], w_au[i], w_o[i]), ln1_g[i][None, :],
            ln1_b[i][None, :], w_r, b_r, alpha, tm=ROUTE_BLOCK)
        x1, x1b, comb, sel = _in_hbm(x1, x1b, comb, sel)

        plan = _routing_plan(counts)
        sorted_x, route = _in_hbm(*_scatter(x1b, sel, comb, plan, n_tiles))
        y, = _in_hbm(_experts(sorted_x, plan, w_eg[i].reshape(N_EXPERTS, d, de),
                              w_eu[i].reshape(N_EXPERTS, d, de), w_ed[i].reshape(N_EXPERTS, de, d)))
        x2 = _combine(y, plan, x1, route, p[i].reshape(t, -1), *_in_hbm(w_pg[i], w_pp[i]),
                      ln2_g[i][None, :], ln2_b[i][None, :], alpha)
    return x2.reshape(bsz, seq, d)
```
